```python
import math
import jax
import jax.numpy as jnp
from jax import lax
import numpy as np

D_MODEL = 1024
BATCH = 32
SEQ = 256
DEPTH = 2
DEC_BATCH = 8
DEC_SEQ = 2048
PAST_LEN = 512

GRID_W = 64
D_HY = 256
HY_ORDER = 2
HY_SHORT = 3
HY_EMB = 33
HY_BANDS = (HY_EMB - 1) // 2
HY_FFN = 64
HY_MIN_DECAY = math.log(1e-2) / 1.5
HY_MAX_DECAY = math.log(1e-2) / 0.3
D_RG = 256
N_RG_HEADS = 4
RG_HEAD = D_RG // N_RG_HEADS
RG_CONV = 4
RG_C = 8.0
N_DA_HEADS = 4
DA_HEAD = 64
DA_VDIM = 2 * DA_HEAD
D_DA = N_DA_HEADS * DA_VDIM
D_MIX = D_HY + D_RG + D_DA
D_IN = 3 * D_HY + 2 * D_RG + 3 * D_DA
SPLITS = (3 * D_HY, 3 * D_HY + D_RG, 3 * D_HY + 2 * D_RG, 3 * D_HY + 2 * D_RG + D_DA, 3 * D_HY + 2 * D_RG + 2 * D_DA)
ROPE_PAIRS = DA_HEAD // 4
ROPE_THETA = 10000.0
Q_BLOCK = 128
N_EXPERTS = 16
N_GROUPS = 4
EXP_PER_GROUP = N_EXPERTS // N_GROUPS
TOP_K = 2
D_EXPERT = 512
EPS = 1e-6
F32 = jnp.float32

kernel_name = 'hymba_hyena_rglru_diffattn_moe_dit_step'


def _rmsnorm(x, g):
    xf = x.astype(F32)
    y = xf * lax.rsqrt(jnp.mean(xf * xf, axis=-1, keepdims=True) + EPS)
    return (y * g.astype(F32)).astype(x.dtype)


def _dwconv(x, w, b, pad_left):
    k = w.shape[0]
    L = x.shape[1]
    xp = jnp.pad(x, ((0, 0), (pad_left, k - 1 - pad_left), (0, 0)))
    y = b
    for j in range(k):
        y = y + xp[:, j:j + L] * w[j]
    return y


def _hyena_filters(L, w1, b1, w2, b2, w3, freq):
    t = jnp.linspace(0.0, 1.0, L, dtype=F32)[:, None]
    ang = (2.0 * math.pi / L) * jnp.arange(L, dtype=F32)[:, None]
    bands = jnp.linspace(1e-4, HY_BANDS - 1, HY_BANDS, dtype=F32)[None, :]
    z = jnp.concatenate([t, jnp.cos(bands * ang), -jnp.sin(bands * ang)], axis=-1)
    fr = freq.astype(F32)
    h = jnp.sin(fr[0] * (z @ w1.astype(F32) + b1.astype(F32)))
    h = jnp.sin(fr[1] * (h @ w2.astype(F32) + b2.astype(F32)))
    h = (h @ w3.astype(F32)).reshape(L, HY_ORDER, 2, D_HY)
    deltas = jnp.linspace(HY_MIN_DECAY, HY_MAX_DECAY, D_HY, dtype=F32)
    window = jnp.exp(-t * jnp.abs(deltas))
    return h * window[:, None, None, :]


def _bidir_fftconv(u, h_fwd, h_bwd, skip):
    L, C = h_fwd.shape
    k = jnp.concatenate([h_fwd, jnp.zeros((1, C), F32), h_bwd[:0:-1]], axis=0)
    kf = jnp.fft.rfft(k, n=2 * L, axis=0)
    uf32 = u.astype(F32)
    uf = jnp.fft.rfft(uf32, n=2 * L, axis=1)
    y = jnp.fft.irfft(uf * kf[None], n=2 * L, axis=1)[:, :L]
    return y + uf32 * skip.astype(F32)


def _hyena(p_hy, lp):
    u = _dwconv(p_hy, lp['hy_short_w'], lp['hy_short_b'], HY_SHORT // 2)
    v, x1, x2 = jnp.split(u, 3, axis=-1)
    L = u.shape[1]
    h = _hyena_filters(L, lp['hy_w1'], lp['hy_b1'], lp['hy_w2'], lp['hy_b2'], lp['hy_w3'], lp['hy_freq'])
    bias = lp['hy_bias']
    z = x1.astype(F32) * _bidir_fftconv(v, h[:, 0, 0], h[:, 0, 1], bias[0])
    z = x2.astype(F32) * _bidir_fftconv(z, h[:, 1, 0], h[:, 1, 1], bias[1])
    return z.astype(p_hy.dtype)


def _rglru_gates(x, wa, ba, wx, bx, lam):
    B, L, _ = x.shape
    xh = x.reshape(B, L, N_RG_HEADS, RG_HEAD)
    r = jax.nn.sigmoid(jnp.einsum('blhi,hij->blhj', xh, wa.astype(F32)).reshape(B, L, D_RG) + ba.astype(F32))
    i = jax.nn.sigmoid(jnp.einsum('blhi,hij->blhj', xh, wx.astype(F32)).reshape(B, L, D_RG) + bx.astype(F32))
    log_a = -RG_C * r * jax.nn.softplus(-lam.astype(F32))
    a = jnp.exp(log_a)
    b = jnp.sqrt(-jnp.expm1(2.0 * log_a)) * (i * x)
    return a, b


def _linear_scan(a, b, h0):
    def comb(l, r):
        return l[0] * r[0], r[0] * l[1] + r[1]
    a_cum, b_cum = lax.associative_scan(comb, (a, b), axis=1)
    return a_cum * h0[:, None, :] + b_cum


def _recurrent(p_g, p_x, lp, h0_f, h0_b):
    xr = _dwconv(p_x, lp['rg_conv_w'], lp['rg_conv_b'], RG_CONV // 2).astype(F32)
    a_f, b_f = _rglru_gates(xr, lp['rg_wa'][0], lp['rg_ba'][0], lp['rg_wx'][0], lp['rg_bx'][0], lp['rg_lambda'][0])
    a_b, b_b = _rglru_gates(xr, lp['rg_wa'][1], lp['rg_ba'][1], lp['rg_wx'][1], lp['rg_bx'][1], lp['rg_lambda'][1])
    h_f = _linear_scan(a_f, b_f, h0_f)
    h_b = jnp.flip(_linear_scan(jnp.flip(a_b, 1), jnp.flip(b_b, 1), h0_b), 1)
    y = (h_f + h_b) * jax.nn.gelu(p_g.astype(F32))
    return y.astype(p_x.dtype), h_f[:, -1], h_b[:, 0]


def _rope2d(x):
    L = x.shape[3]
    rows = L // GRID_W
    row, col = jnp.meshgrid(jnp.arange(rows, dtype=F32), jnp.arange(GRID_W, dtype=F32), indexing='ij')
    row = row.reshape(-1)
    col = col.reshape(-1)
    inv = ROPE_THETA ** (-jnp.arange(ROPE_PAIRS, dtype=F32) / ROPE_PAIRS)

    def rot(xc, pos):
        ang = pos[:, None] * inv[None, :]
        cos, sin = jnp.cos(ang), jnp.sin(ang)
        x1, x2 = xc[..., :ROPE_PAIRS], xc[..., ROPE_PAIRS:]
        return jnp.concatenate([x1 * cos - x2 * sin, x1 * sin + x2 * cos], axis=-1)

    half = DA_HEAD // 2
    xf = x.astype(F32)
    return jnp.concatenate([rot(xf[..., :half], row), rot(xf[..., half:], col)], axis=-1).astype(x.dtype)


def _diff_attention(q, k, v, lam):
    B, H, _, Lq, d = q.shape
    nb = Lq // Q_BLOCK
    kf = k.astype(F32)
    vf = v.astype(F32)
    qb = jnp.moveaxis(q.astype(F32).reshape(B, H, 2, nb, Q_BLOCK, d), 3, 0)
    scale = d ** -0.5

    def block(qi):
        s = jnp.einsum('bhmqd,bhmkd->bhmqk', qi, kf) * scale
        p = jax.nn.softmax(s, axis=-1)
        w = p[:, :, 0] - lam * p[:, :, 1]
        return jnp.einsum('bhqk,bhkd->bhqd', w, vf)

    o = lax.map(block, qb)
    return jnp.moveaxis(o, 0, 2).reshape(B, H, Lq, DA_VDIM)


def _cache_to_heads(ck):
    B, H, T, _ = ck.shape
    return ck.reshape(B, H, T, 2, DA_HEAD).transpose(0, 1, 3, 2, 4)


def _heads_to_cache(k):
    B, H, _, T, d = k.shape
    return k.transpose(0, 1, 3, 2, 4).reshape(B, H, T, 2 * d)


def _token_mixers(h, lp, ctx):
    B, L, _ = h.shape
    proj = jnp.dot(h, lp['w_in'])
    p_hy, p_g, p_x, p_q, p_k, p_v = jnp.split(proj, SPLITS, axis=-1)
    y_hy = _hyena(p_hy, lp)
    q = p_q.reshape(B, L, N_DA_HEADS, 2, DA_HEAD).transpose(0, 2, 3, 1, 4)
    k = p_k.reshape(B, L, N_DA_HEADS, 2, DA_HEAD).transpose(0, 2, 3, 1, 4)
    v = p_v.reshape(B, L, N_DA_HEADS, DA_VDIM).transpose(0, 2, 1, 3)
    if ctx is None:
        h0 = jnp.zeros((B, D_RG), F32)
        y_rg, s_f, s_b = _recurrent(p_g, p_x, lp, h0, h0)
        o = _diff_attention(q, k, v, lp['lam'])
    else:
        ck, cv, cs = ctx
        y_rg, _, _ = _recurrent(p_g, p_x, lp, cs[:, 0].astype(F32), cs[:, 1].astype(F32))
        keys = jnp.concatenate([_cache_to_heads(ck).astype(k.dtype), _rope2d(k)], axis=3)
        vals = jnp.concatenate([cv.astype(v.dtype), v], axis=2)
        o = _diff_attention(_rope2d(q), keys, vals, lp['lam'])
    o = _rmsnorm(o, lp['da_subln']) * (1.0 - lp['lam_init'])
    o = o.transpose(0, 2, 1, 3).reshape(B, L, D_DA).astype(h.dtype)
    y = jnp.dot(jnp.concatenate([y_hy, y_rg, o], axis=-1), lp['w_out'])
    if ctx is None:
        return y, (_heads_to_cache(k), v, jnp.stack([s_f, s_b], axis=1))
    return y, None


def _moe(h, w_router, b_router, wg, wu, wd):
    B, L, D = h.shape
    t = h.reshape(B * L, D)
    logits = jnp.dot(t, w_router).astype(F32) + b_router.astype(F32)
    probs = jax.nn.softmax(logits, axis=-1)
    pg = probs.reshape(-1, N_GROUPS, EXP_PER_GROUP)
    gscore = jnp.sum(lax.top_k(pg, TOP_K)[0], axis=-1)
    gsel = jnp.argmax(gscore, axis=-1)
    in_group = jnp.arange(N_GROUPS)[None, :] == gsel[:, None]
    masked = jnp.where(in_group[:, :, None], pg, -1.0).reshape(-1, N_EXPERTS)
    top_p, top_i = lax.top_k(masked, TOP_K)
    top_w = top_p / jnp.sum(top_p, axis=-1, keepdims=True)
    gates = jnp.sum(jax.nn.one_hot(top_i, N_EXPERTS, dtype=F32) * top_w[..., None], axis=1).astype(t.dtype)
    out = jnp.zeros_like(t)
    for e in range(N_EXPERTS):
        he = jax.nn.silu(jnp.dot(t, wg[e])) * jnp.dot(t, wu[e])
        out = out + gates[:, e:e + 1] * jnp.dot(he, wd[e])
    return out.reshape(B, L, D)


def _layer(x, cond, lp, ctx):
    m = jnp.dot(jax.nn.silu(cond), lp['w_ada']) + lp['b_ada']
    sh1, sc1, g1, sh2, sc2, g2 = [t[:, None, :] for t in jnp.split(m, 6, axis=-1)]
    h = _rmsnorm(x, lp['norm1_g']) * (1.0 + sc1) + sh1
    y, new_ctx = _token_mixers(h, lp, ctx)
    x = x + g1 * y
    h = _rmsnorm(x, lp['norm2_g']) * (1.0 + sc2) + sh2
    x = x + g2 * _moe(h, lp['w_router'], lp['b_router'], lp['moe_wg'], lp['moe_wu'], lp['moe_wd'])
    return x, new_ctx


def setup_inputs(seed: int = 0) -> dict:
    key = jax.random.key(seed)
    ks = iter(jax.random.split(key, 48))

    def nrm(shape, scale):
        return jax.random.normal(next(ks), shape, F32) * scale

    D = D_MODEL
    u = jax.random.uniform(next(ks), (DEPTH, 2, D_RG), F32, 0.9, 0.999)
    a_root = u ** (1.0 / RG_C)
    rg_lambda = jnp.log(a_root) - jnp.log1p(-a_root)
    return {
        'x_prompt': nrm((BATCH, SEQ, D), 1.0),
        'x_sample': nrm((DEC_BATCH, DEC_SEQ, D), 1.0),
        'cache_k': nrm((DEC_BATCH, DEPTH, N_DA_HEADS, PAST_LEN, 2 * DA_HEAD), 1.0),
        'cache_v': nrm((DEC_BATCH, DEPTH, N_DA_HEADS, PAST_LEN, DA_VDIM), 1.0),
        'state_rglru': nrm((DEC_BATCH, DEPTH, 2, D_RG), 0.5),
        'c': nrm((DEC_BATCH, D), 1.0),
        'c_ctx': nrm((D,), 1.0),
        'w_ada': nrm((DEPTH, D, 6 * D), 0.5 * D ** -0.5),
        'b_ada': nrm((DEPTH, 6 * D), 0.01),
        'norm1_g': 1.0 + nrm((DEPTH, D), 0.02),
        'norm2_g': 1.0 + nrm((DEPTH, D), 0.02),
        'w_in': nrm((DEPTH, D, D_IN), D ** -0.5),
        'w_out': nrm((DEPTH, D_MIX, D), D_MIX ** -0.5),
        'hy_short_w': nrm((DEPTH, HY_SHORT, 3 * D_HY), HY_SHORT ** -0.5),
        'hy_short_b': nrm((DEPTH, 3 * D_HY), 0.01),
        'hy_w1': nrm((DEPTH, HY_EMB, HY_FFN), HY_EMB ** -0.5),
        'hy_b1': nrm((DEPTH, HY_FFN), 0.1),
        'hy_w2': nrm((DEPTH, HY_FFN, HY_FFN), HY_FFN ** -0.5),
        'hy_b2': nrm((DEPTH, HY_FFN), 0.1),
        'hy_w3': nrm((DEPTH, HY_FFN, HY_ORDER * 2 * D_HY), 0.05 * HY_FFN ** -0.5),
        'hy_freq': 1.0 + nrm((DEPTH, 2, HY_FFN), 0.1),
        'hy_bias': nrm((DEPTH, HY_ORDER, D_HY), 0.5),
        'rg_conv_w': nrm((DEPTH, RG_CONV, D_RG), RG_CONV ** -0.5),
        'rg_conv_b': nrm((DEPTH, D_RG), 0.01),
        'rg_wa': nrm((DEPTH, 2, N_RG_HEADS, RG_HEAD, RG_HEAD), RG_HEAD ** -0.5),
        'rg_ba': nrm((DEPTH, 2, D_RG), 0.1),
        'rg_wx': nrm((DEPTH, 2, N_RG_HEADS, RG_HEAD, RG_HEAD), RG_HEAD ** -0.5),
        'rg_bx': nrm((DEPTH, 2, D_RG), 0.1),
        'rg_lambda': rg_lambda,
        'da_lambda': nrm((DEPTH, 4, DA_HEAD), 0.1),
        'da_subln': 1.0 + nrm((DEPTH, DA_VDIM), 0.02),
        'w_router': nrm((D, N_EXPERTS), D ** -0.5),
        'b_router': nrm((N_EXPERTS,), 0.01),
        'moe_wg': nrm((DEPTH, N_EXPERTS, D, D_EXPERT), D ** -0.5),
        'moe_wu': nrm((DEPTH, N_EXPERTS, D, D_EXPERT), D ** -0.5),
        'moe_wd': nrm((DEPTH, N_EXPERTS, D_EXPERT, D), D_EXPERT ** -0.5),
        'final_g': 1.0 + nrm((D,), 0.02),
    }


def reference(x_prompt, x_sample, cache_k, cache_v, state_rglru, c, c_ctx, w_ada, b_ada, norm1_g, norm2_g, w_in, w_out, hy_short_w, hy_short_b, hy_w1, hy_b1, hy_w2, hy_b2, hy_w3, hy_freq, hy_bias, rg_conv_w, rg_conv_b, rg_wa, rg_ba, rg_wx, rg_bx, rg_lambda, da_lambda, da_subln, w_router, b_router, moe_wg, moe_wu, moe_wd, final_g):
    xp = x_prompt
    xs = x_sample
    ks, vs, ss = [], [], []
    for l in range(DEPTH):
        lam_init = 0.8 - 0.6 * math.exp(-0.3 * l)
        lv = da_lambda[l].astype(F32)
        lam = jnp.exp(jnp.sum(lv[0] * lv[1])) - jnp.exp(jnp.sum(lv[2] * lv[3])) + lam_init
        lp = {
            'w_ada': w_ada[l], 'b_ada': b_ada[l], 'norm1_g': norm1_g[l], 'norm2_g': norm2_g[l],
            'w_in': w_in[l], 'w_out': w_out[l],
            'hy_short_w': hy_short_w[l], 'hy_short_b': hy_short_b[l],
            'hy_w1': hy_w1[l], 'hy_b1': hy_b1[l], 'hy_w2': hy_w2[l], 'hy_b2': hy_b2[l],
            'hy_w3': hy_w3[l], 'hy_freq': hy_freq[l], 'hy_bias': hy_bias[l],
            'rg_conv_w': rg_conv_w[l], 'rg_conv_b': rg_conv_b[l],
            'rg_wa': rg_wa[l], 'rg_ba': rg_ba[l], 'rg_wx': rg_wx[l], 'rg_bx': rg_bx[l],
            'rg_lambda': rg_lambda[l],
            'lam': lam, 'lam_init': lam_init, 'da_subln': da_subln[l],
            'w_router': w_router, 'b_router': b_router,
            'moe_wg': moe_wg[l], 'moe_wu': moe_wu[l], 'moe_wd': moe_wd[l],
        }
        xp, (k_l, v_l, s_l) = _layer(xp, c_ctx[None, :], lp, None)
        xs, _ = _layer(xs, c, lp, (cache_k[:, l], cache_v[:, l], state_rglru[:, l]))
        ks.append(k_l)
        vs.append(v_l)
        ss.append(s_l)
    y_prompt = _rmsnorm(xp, final_g)
    y_sample = _rmsnorm(xs, final_g)
    new_cache_k = jnp.stack(ks, axis=1)
    new_cache_v = jnp.stack(vs, axis=1)
    new_state_rglru = jnp.stack(ss, axis=1)
    return (y_prompt, y_sample, new_cache_k, new_cache_v, new_state_rglru)
```

```python
import functools
import math

import numpy as np
import jax
import jax.numpy as jnp
from jax import lax
from jax.experimental import pallas as pl
from jax.experimental.pallas import tpu as pltpu
from jax.experimental.pallas import tpu_sc as plsc

F32 = jnp.float32
BF16 = jnp.bfloat16

D_MODEL = 1024
DEPTH = 2
GRID_W = 64
D_HY = 256
HY_EMB = 33
HY_BANDS = (HY_EMB - 1) // 2
HY_FFN = 64
HY_MIN_DECAY = math.log(1e-2) / 1.5
HY_MAX_DECAY = math.log(1e-2) / 0.3
D_RG = 256
N_RG_HEADS = 4
RG_C = 8.0
N_DA_HEADS = 4
DA_HEAD = 64
DA_VDIM = 2 * DA_HEAD
D_DA = N_DA_HEADS * DA_VDIM
D_MIX = D_HY + D_RG + D_DA
D_IN = 3 * D_HY + 2 * D_RG + 3 * D_DA
ROPE_PAIRS = DA_HEAD // 4
ROPE_THETA = 10000.0
N_EXPERTS = 16
N_GROUPS = 4
EXP_PER_GROUP = N_EXPERTS // N_GROUPS
D_EXPERT = 512
PAIRS_PER_GROUP = EXP_PER_GROUP * (EXP_PER_GROUP - 1) // 2
N_CLASSES = N_GROUPS * PAIRS_PER_GROUP
EPS = 1e-6
N_COND = 16
CTX_ROW = 8
LANES = 128
VMEM_LIMIT = 56 * 1024 * 1024


def _cparams(*sem):
    return pltpu.CompilerParams(dimension_semantics=sem, vmem_limit_bytes=VMEM_LIMIT)


def _split(x):
    hi = x.astype(BF16)
    lo = (x - hi.astype(F32)).astype(BF16)
    return hi, lo


def _dot(a, b):
    return jnp.dot(a, b, preferred_element_type=F32)


def _dot3(a, b):
    ah, al = _split(a)
    bh, bl = _split(b)
    return _dot(ah, bh) + _dot(al, bh) + _dot(ah, bl)


def _dot_nt(a, b):
    return lax.dot_general(a, b, (((1,), (1,)), ((), ())), preferred_element_type=F32)


def _sigmoid(x):
    return 1.0 / (1.0 + jnp.exp(-x))


def _const_spec(shape):
    n = len(shape)
    return pl.BlockSpec(shape, lambda *_: (0,) * n)


def _ada_kernel(c_ref, w_ref, b_ref, o_ref):
    c = c_ref[...]
    s = c * _sigmoid(c)
    o_ref[...] = _dot3(s, w_ref[...]) + b_ref[...]


def _ada_table(cond, w_ada, b_ada):
    D = D_MODEL
    out = pl.pallas_call(
        _ada_kernel,
        grid=(DEPTH, 6),
        in_specs=[
            pl.BlockSpec((N_COND, D), lambda l, j: (0, 0)),
            pl.BlockSpec((None, D, D), lambda l, j: (l, 0, j)),
            pl.BlockSpec((None, None, 1, D), lambda l, j: (l, j, 0, 0)),
        ],
        out_specs=pl.BlockSpec((None, None, N_COND, D), lambda l, j: (l, j, 0, 0)),
        out_shape=jax.ShapeDtypeStruct((DEPTH, 6, N_COND, D), F32),
        compiler_params=_cparams("parallel", "parallel"),
        name="ada_table",
    )(cond, w_ada, b_ada.reshape(DEPTH, 6, 1, D))
    return out.transpose(0, 2, 1, 3)


def _rope_tables(L):
    t = np.arange(L)
    j = np.arange(LANES)
    jj = j % DA_HEAD
    is_col = (jj // (DA_HEAD // 2)) == 1
    pair = jj % ROPE_PAIRS
    second = (jj % (DA_HEAD // 2)) >= ROPE_PAIRS
    inv = ROPE_THETA ** (-np.arange(ROPE_PAIRS, dtype=np.float64) / ROPE_PAIRS)
    pos = np.where(is_col[None, :], (t % GRID_W)[:, None], (t // GRID_W)[:, None]).astype(np.float64)
    ang = pos * inv[pair][None, :]
    cos = np.cos(ang).astype(np.float32)
    sin = np.sin(ang).astype(np.float32)
    sin_a = np.where(second[None, :], 0.0, -sin).astype(np.float32)
    sin_b = np.where(second[None, :], sin, 0.0).astype(np.float32)
    return cos, sin_a, sin_b


def _rope(x, cos, sin_a, sin_b):
    nxt = pltpu.roll(x, LANES - ROPE_PAIRS, axis=1)
    prv = pltpu.roll(x, ROPE_PAIRS, axis=1)
    return x * cos + nxt * sin_a + prv * sin_b


def _norm_proj_kernel(rope, kv_dtype, x_ref, mod_ref, g_ref, w_ref, *rest):
    if rope:
        cos_ref, sa_ref, sb_ref = rest[:3]
        rest = rest[3:]
    phy_ref, pg_ref, px_ref, q_ref, k_ref, v_ref = rest
    x = x_ref[...]
    ms = jnp.mean(x * x, axis=-1, keepdims=True)
    y = x * lax.rsqrt(ms + EPS) * g_ref[...]
    h = (y * (1.0 + mod_ref[1:2, :]) + mod_ref[0:1, :]).astype(BF16)
    o = 3 * D_HY
    phy_ref[...] = _dot(h, w_ref[:, 0:o]).astype(BF16)
    pg_ref[...] = _dot(h, w_ref[:, o:o + D_RG])
    px_ref[...] = _dot(h, w_ref[:, o + D_RG:o + 2 * D_RG])
    o += 2 * D_RG
    q = _dot(h, w_ref[:, o:o + D_DA]) * (DA_HEAD ** -0.5)
    k = _dot(h, w_ref[:, o + D_DA:o + 2 * D_DA])
    v = _dot(h, w_ref[:, o + 2 * D_DA:o + 3 * D_DA])
    if rope:
        cos, sa, sb = cos_ref[...], sa_ref[...], sb_ref[...]
    for hd in range(N_DA_HEADS):
        sl = slice(hd * DA_VDIM, (hd + 1) * DA_VDIM)
        qh, kh = q[:, sl], k[:, sl]
        if rope:
            qh = _rope(qh, cos, sa, sb)
            kh = _rope(kh, cos, sa, sb)
        q_ref[hd] = qh.astype(BF16)
        k_ref[hd] = kh.astype(kv_dtype)
        v_ref[hd] = v[:, sl].astype(kv_dtype)


def _norm_proj(x, mod, g, w_in, B, L, rope, kv_dtype, ctx_rows, tm=512):
    T = B * L
    tm = min(tm, L)
    nl = L // tm
    row = (lambda i: CTX_ROW) if ctx_rows else (lambda i: i // nl)
    in_specs = [
        pl.BlockSpec((tm, D_MODEL), lambda i: (i, 0)),
        pl.BlockSpec((None, 6, D_MODEL), lambda i: (row(i), 0, 0)),
        _const_spec((1, D_MODEL)),
        _const_spec((D_MODEL, D_IN)),
    ]
    args = [x, mod, g.reshape(1, D_MODEL), w_in]
    if rope:
        tabs = _rope_tables(L)
        in_specs += [pl.BlockSpec((tm, LANES), lambda i: (i % nl, 0))] * 3
        args += [jnp.asarray(t) for t in tabs]
    head_spec = pl.BlockSpec((None, N_DA_HEADS, tm, DA_VDIM), lambda i: (i // nl, 0, i % nl, 0))
    head_shape = (B, N_DA_HEADS, L, DA_VDIM)
    return pl.pallas_call(
        functools.partial(_norm_proj_kernel, rope, kv_dtype),
        grid=(T // tm,),
        in_specs=in_specs,
        out_specs=[
            pl.BlockSpec((tm, 3 * D_HY), lambda i: (i, 0)),
            pl.BlockSpec((tm, D_RG), lambda i: (i, 0)),
            pl.BlockSpec((tm, D_RG), lambda i: (i, 0)),
            head_spec, head_spec, head_spec,
        ],
        out_shape=[
            jax.ShapeDtypeStruct((T, 3 * D_HY), BF16),
            jax.ShapeDtypeStruct((T, D_RG), F32),
            jax.ShapeDtypeStruct((T, D_RG), F32),
            jax.ShapeDtypeStruct(head_shape, BF16),
            jax.ShapeDtypeStruct(head_shape, kv_dtype),
            jax.ShapeDtypeStruct(head_shape, kv_dtype),
        ],
        compiler_params=_cparams("parallel"),
        name="norm_proj_rope" if rope else "norm_proj",
    )(*args)


def _dft_mats(L):
    n = 2 * L - 1
    fs = (np.arange(L, dtype=np.int64)[:, None] * np.arange(L, dtype=np.int64)[None, :]) % n
    ang = fs.astype(np.float64) * (2.0 * np.pi / n)
    return np.cos(ang).astype(np.float32), np.sin(ang).astype(np.float32)


def _hy_features(L):
    t = np.linspace(0.0, 1.0, L, dtype=np.float64)[:, None]
    ang = ((2.0 * math.pi / L) * np.arange(L, dtype=np.float64))[:, None]
    bands = np.linspace(1e-4, HY_BANDS - 1, HY_BANDS, dtype=np.float64)[None, :]
    ba = bands * ang
    z = np.concatenate([t, np.cos(ba), -np.sin(ba)], axis=-1).astype(np.float32)
    return np.pad(z, ((0, 0), (0, LANES - HY_EMB)))


def _hy_filter_kernel(L, z_ref, w1_ref, b1_ref, w2_ref, b2_ref, w3_ref, fr_ref, rc_ref, rs_ref):
    z = z_ref[...]
    h = jnp.sin(fr_ref[0:1, :] * (_dot3(z, w1_ref[...]) + b1_ref[...]))
    h = jnp.sin(fr_ref[1:2, :] * (_dot3(h, w2_ref[...]) + b2_ref[...]))
    h = _dot3(h, w3_ref[...])
    t = z[:, 0:1]
    step = (HY_MAX_DECAY - HY_MIN_DECAY) / (D_HY - 1)
    deltas = HY_MIN_DECAY + step * lax.broadcasted_iota(jnp.int32, (1, D_HY), 1).astype(F32)
    window = jnp.exp(-t * jnp.abs(deltas))
    not_first = lax.broadcasted_iota(jnp.int32, (L, 1), 0) > 0
    for o in range(2):
        hf = h[:, (2 * o) * D_HY:(2 * o + 1) * D_HY] * window
        hb = jnp.where(not_first, h[:, (2 * o + 1) * D_HY:(2 * o + 2) * D_HY] * window, 0.0)
        rc_ref[:, o * D_HY:(o + 1) * D_HY] = hf + hb
        rs_ref[:, o * D_HY:(o + 1) * D_HY] = hb - hf


def _hy_spectrum_kernel(c_ref, s_ref, rc_ref, rs_ref, w_ref, kre_ref, kim_ref):
    rch, rcl = _split(rc_ref[...])
    rsh, rsl = _split(rs_ref[...])
    c, s, w = c_ref[...], s_ref[...], w_ref[...]
    kre_ref[...] = (_dot(c, rch) + _dot(c, rcl)) * w
    kim_ref[...] = (_dot(s, rsh) + _dot(s, rsl)) * w


def _hy_spectra(L, cmat, smat, w1, b1, w2, b2, w3, freq):
    z = jnp.asarray(_hy_features(L))
    w1p = jnp.pad(w1, ((0, LANES - HY_EMB), (0, 0)))
    nw = 2 * D_HY
    rc, rs = pl.pallas_call(
        functools.partial(_hy_filter_kernel, L),
        out_shape=[jax.ShapeDtypeStruct((L, nw), F32)] * 2,
        compiler_params=pltpu.CompilerParams(vmem_limit_bytes=VMEM_LIMIT),
        name="hy_filter",
    )(z, w1p, b1.reshape(1, HY_FFN), w2, b2.reshape(1, HY_FFN), w3, freq)
    n = 2 * L - 1
    wsc = np.full((L, 1), 2.0 / n, np.float32)
    wsc[0, 0] = 1.0 / n
    tr = min(L, 256)
    return pl.pallas_call(
        _hy_spectrum_kernel,
        grid=(L // tr,),
        in_specs=[
            pl.BlockSpec((tr, L), lambda i: (i, 0)),
            pl.BlockSpec((tr, L), lambda i: (i, 0)),
            _const_spec((L, nw)),
            _const_spec((L, nw)),
            pl.BlockSpec((tr, 1), lambda i: (i, 0)),
        ],
        out_specs=[pl.BlockSpec((tr, nw), lambda i: (i, 0))] * 2,
        out_shape=[jax.ShapeDtypeStruct((L, nw), F32)] * 2,
        compiler_params=_cparams("parallel"),
        name="hy_spectrum",
    )(cmat, smat, rc, rs, jnp.asarray(wsc))


def _hyena_kernel(L, tr, p_ref, sw_ref, sb_ref, bias_ref, c_ref, s_ref, kre_ref, kim_ref, o_ref,
                  pad_ref, u_ref, sig_ref, sig16_ref, zre_ref, zim_ref):
    C3 = 3 * D_HY
    zeros = jnp.zeros((8, C3), F32)
    pad_ref[0:8, :] = zeros
    pad_ref[8 + L:16 + L, :] = zeros
    chunks = [slice(r0, r0 + tr) for r0 in range(0, L, tr)]
    for c in chunks:
        pad_ref[8 + c.start:8 + c.stop, :] = p_ref[c, :].astype(F32)
    for c in chunks:
        u = sb_ref[...]
        for j in range(3):
            u = u + pad_ref[7 + j + c.start:7 + j + c.stop, :] * sw_ref[j:j + 1, :]
        u_ref[c, :] = u[:, D_HY:C3]
        sig_ref[c, :] = u[:, 0:D_HY]
        sig16_ref[c, :] = u[:, 0:D_HY].astype(BF16)

    for o in range(2):
        ko = slice(o * D_HY, (o + 1) * D_HY)
        for c in chunks:
            ure = _dot(c_ref[c, :], sig16_ref[...])
            us = _dot(s_ref[c, :], sig16_ref[...])
            kre, kim = kre_ref[c, ko], kim_ref[c, ko]
            zre_ref[c, :] = (ure * kre + us * kim).astype(BF16)
            zim_ref[c, :] = (ure * kim - us * kre).astype(BF16)
        gate = slice(o * D_HY, (o + 1) * D_HY)
        for c in chunks:
            y = _dot(c_ref[c, :], zre_ref[...]) - _dot(s_ref[c, :], zim_ref[...])
            z = u_ref[c, gate] * (y + sig_ref[c, :] * bias_ref[o:o + 1, :])
            if o == 0:
                sig_ref[c, :] = z
                sig16_ref[c, :] = z.astype(BF16)
            else:
                o_ref[c, :] = z.astype(o_ref.dtype)


def _hyena(p_hy, B, L, cmat, smat, kre, kim, short_w, short_b, bias, tr=512):
    C3 = 3 * D_HY
    tr = min(tr, L)
    once = pl.Buffered(1)
    return pl.pallas_call(
        functools.partial(_hyena_kernel, L, tr),
        grid=(B,),
        in_specs=[
            pl.BlockSpec((L, C3), lambda b: (b, 0)),
            _const_spec((3, C3)),
            _const_spec((1, C3)),
            _const_spec((2, D_HY)),
            pl.BlockSpec((L, L), lambda b: (0, 0), pipeline_mode=once),
            pl.BlockSpec((L, L), lambda b: (0, 0), pipeline_mode=once),
            pl.BlockSpec((L, 2 * D_HY), lambda b: (0, 0), pipeline_mode=once),
            pl.BlockSpec((L, 2 * D_HY), lambda b: (0, 0), pipeline_mode=once),
        ],
        out_specs=pl.BlockSpec((L, D_HY), lambda b: (b, 0)),
        out_shape=jax.ShapeDtypeStruct((B * L, D_HY), BF16),
        scratch_shapes=[
            pltpu.VMEM((L + 16, C3), F32),
            pltpu.VMEM((L, 2 * D_HY), F32),
            pltpu.VMEM((L, D_HY), F32),
            pltpu.VMEM((L, D_HY), BF16),
            pltpu.VMEM((L, D_HY), BF16),
            pltpu.VMEM((L, D_HY), BF16),
        ],
        compiler_params=_cparams("parallel"),
        name="hyena",
    )(p_hy, short_w, short_b.reshape(1, C3), bias, cmat, smat, kre, kim)


def _softplus(z):
    return jnp.maximum(z, 0.0) + jnp.log1p(jnp.exp(-jnp.abs(z)))


def _expm1(x):
    u = jnp.exp(x)
    near = (u - 1.0) * x / jnp.where(u == 1.0, 1.0, jnp.log(u))
    return jnp.where(x < -0.5, u - 1.0, jnp.where(u == 1.0, x, near))


def _gelu_tanh(x):
    return 0.5 * x * (1.0 + jnp.tanh(math.sqrt(2.0 / math.pi) * (x + 0.044715 * x * x * x)))


def _rglru_kernel(L, has_state, pg_ref, px_ref, cw_ref, cb_ref, wh_ref, wl_ref, gb_ref, lam_ref, *rest):
    if has_state:
        st_ref, y_ref, pad_ref, a_ref, b_ref, h_ref = rest
    else:
        y_ref, st_out_ref, pad_ref, a_ref, b_ref, h_ref = rest
    C = D_RG
    zeros = jnp.zeros((8, C), F32)
    pad_ref[0:8, :] = zeros
    pad_ref[8 + L:16 + L, :] = zeros
    pad_ref[8:8 + L, :] = px_ref[...]
    sp = _softplus(-lam_ref[...])
    tr = min(L, 256)
    for r0 in range(0, L, tr):
        xr = cb_ref[...]
        for j in range(4):
            xr = xr + pad_ref[6 + j + r0:6 + j + r0 + tr, :] * cw_ref[j:j + 1, :]
        xh, xl = _split(xr)
        for d in range(2):
            g = []
            for m in range(2):
                cols = slice((2 * d + m) * C, (2 * d + m + 1) * C)
                wh = wh_ref[:, cols]
                g.append(_sigmoid(_dot(xh, wh) + _dot(xl, wh) + _dot(xh, wl_ref[:, cols]) + gb_ref[:, cols]))
            log_a = -RG_C * g[0] * sp[d:d + 1, :]
            a_ref[d, r0:r0 + tr, :] = jnp.exp(log_a)
            b_ref[d, r0:r0 + tr, :] = jnp.sqrt(-_expm1(2.0 * log_a)) * (g[1] * xr)

    if has_state:
        h0f, h0b = st_ref[0:1, :], st_ref[1:2, :]
    else:
        h0f = h0b = jnp.zeros((1, C), F32)

    def step(t, carry):
        hf, hb = carry
        tb = L - 1 - t
        hf = a_ref[0, pl.ds(t, 1), :] * hf + b_ref[0, pl.ds(t, 1), :]
        hb = a_ref[1, pl.ds(tb, 1), :] * hb + b_ref[1, pl.ds(tb, 1), :]
        h_ref[0, pl.ds(t, 1), :] = hf
        h_ref[1, pl.ds(tb, 1), :] = hb
        return hf, hb

    lax.fori_loop(0, L, step, (h0f, h0b))
    y_ref[...] = ((h_ref[0] + h_ref[1]) * _gelu_tanh(pg_ref[...])).astype(y_ref.dtype)
    if not has_state:
        st_out_ref[0:1, :] = h_ref[0, L - 1:L, :]
        st_out_ref[1:2, :] = h_ref[1, 0:1, :]


def _block_diag(w):
    H, d, _ = w.shape
    eye = jnp.eye(H, dtype=w.dtype)
    return (eye[:, None, :, None] * w[:, :, None, :]).reshape(H * d, H * d)


def _rglru(p_g, p_x, B, L, conv_w, conv_b, wa, ba, wx, bx, lam, state):
    C = D_RG
    wcat = jnp.concatenate([_block_diag(wa[0]), _block_diag(wx[0]), _block_diag(wa[1]), _block_diag(wx[1])], axis=1)
    wh = wcat.astype(BF16)
    wl = (wcat - wh.astype(F32)).astype(BF16)
    gb = jnp.concatenate([ba[0], bx[0], ba[1], bx[1]]).reshape(1, 4 * C)
    has_state = state is not None
    in_specs = [
        pl.BlockSpec((L, C), lambda b: (b, 0)),
        pl.BlockSpec((L, C), lambda b: (b, 0)),
        _const_spec((4, C)),
        _const_spec((1, C)),
        _const_spec((C, 4 * C)),
        _const_spec((C, 4 * C)),
        _const_spec((1, 4 * C)),
        _const_spec((2, C)),
    ]
    args = [p_g, p_x, conv_w, conv_b.reshape(1, C), wh, wl, gb, lam]
    y_spec = pl.BlockSpec((L, C), lambda b: (b, 0))
    y_shape = jax.ShapeDtypeStruct((B * L, C), BF16)
    if has_state:
        in_specs.append(pl.BlockSpec((None, 2, C), lambda b: (b, 0, 0)))
        args.append(state)
        out_specs, out_shape = y_spec, y_shape
    else:
        out_specs = [y_spec, pl.BlockSpec((None, 2, C), lambda b: (b, 0, 0))]
        out_shape = [y_shape, jax.ShapeDtypeStruct((B, 2, C), F32)]
    return pl.pallas_call(
        functools.partial(_rglru_kernel, L, has_state),
        grid=(B,),
        in_specs=in_specs,
        out_specs=out_specs,
        out_shape=out_shape,
        scratch_shapes=[
            pltpu.VMEM((L + 16, C), F32),
            pltpu.VMEM((2, L, C), F32),
            pltpu.VMEM((2, L, C), F32),
            pltpu.VMEM((2, L, C), F32),
        ],
        compiler_params=_cparams("parallel"),
        name="rglru_state" if has_state else "rglru",
    )(*args)


def _attn_kernel(L, P, tq, lam_init, q_ref, k_ref, v_ref, *rest):
    if P:
        ck_ref, cv_ref, dal_ref, sub_ref, o_ref, kk_ref, vv_ref = rest
    else:
        dal_ref, sub_ref, o_ref, kk_ref, vv_ref = rest
    lv = dal_ref[...]
    s01 = jnp.sum(lv[0:1, :] * lv[1:2, :], axis=-1, keepdims=True)
    s23 = jnp.sum(lv[2:3, :] * lv[3:4, :], axis=-1, keepdims=True)
    lam = jnp.exp(s01) - jnp.exp(s23) + lam_init
    first_half = lax.broadcasted_iota(jnp.int32, (1, DA_VDIM), 1) < DA_HEAD
    sub = sub_ref[...] * (1.0 - lam_init)
    for hd in range(N_DA_HEADS):
        if P:
            kk_ref[0:P, :] = ck_ref[hd].astype(BF16)
            vv_ref[0:P, :] = cv_ref[hd].astype(BF16)
        kk_ref[P:P + L, :] = k_ref[hd].astype(BF16)
        vv_ref[P:P + L, :] = v_ref[hd].astype(BF16)

        def qblock(i, carry):
            r0 = pl.multiple_of(i * tq, tq)
            q = q_ref[hd, pl.ds(r0, tq), :]
            zero = jnp.zeros_like(q)
            qs = jnp.concatenate([jnp.where(first_half, q, zero), jnp.where(first_half, zero, q)], axis=0)
            s = _dot_nt(qs, kk_ref[...])
            p = jnp.exp(s - jnp.max(s, axis=-1, keepdims=True))
            rinv = 1.0 / jnp.sum(p, axis=-1, keepdims=True)
            w = p[0:tq] * rinv[0:tq] - p[tq:2 * tq] * (lam * rinv[tq:2 * tq])
            o = _dot(w.astype(BF16), vv_ref[...])
            o = o * lax.rsqrt(jnp.mean(o * o, axis=-1, keepdims=True) + EPS) * sub
            o_ref[pl.ds(r0, tq), hd * DA_VDIM:(hd + 1) * DA_VDIM] = o.astype(o_ref.dtype)
            return carry

        lax.fori_loop(0, L // tq, qblock, 0)


def _attention(q, k, v, cache, dal, subln, lam_init, B, L, tq=128):
    H, dv = N_DA_HEADS, DA_VDIM
    hspec = pl.BlockSpec((None, H, L, dv), lambda b: (b, 0, 0, 0))
    in_specs = [hspec, hspec, hspec]
    args = [q, k, v]
    P = 0
    if cache is not None:
        ck, cv, layer = cache
        P = ck.shape[3]
        cspec = pl.BlockSpec((None, None, H, P, dv), lambda b: (b, layer, 0, 0, 0))
        in_specs += [cspec, cspec]
        args += [ck, cv]
    in_specs += [_const_spec((4, DA_HEAD)), _const_spec((1, dv))]
    args += [dal, subln.reshape(1, dv)]
    return pl.pallas_call(
        functools.partial(_attn_kernel, L, P, tq, lam_init),
        grid=(B,),
        in_specs=in_specs,
        out_specs=pl.BlockSpec((L, H * dv), lambda b: (b, 0)),
        out_shape=jax.ShapeDtypeStruct((B * L, H * dv), BF16),
        scratch_shapes=[pltpu.VMEM((P + L, dv), BF16), pltpu.VMEM((P + L, dv), BF16)],
        compiler_params=_cparams("parallel"),
        name="diff_attn_cache" if P else "diff_attn",
    )(*args)


def _route(logits):
    m = logits[0]
    for e in range(1, N_EXPERTS):
        m = jnp.maximum(m, logits[e])
    ex = [jnp.exp(l - m) for l in logits]
    tot = ex[0]
    for e in range(1, N_EXPERTS):
        tot = tot + ex[e]
    inv = 1.0 / tot
    p = [e_ * inv for e_ in ex]
    G = EXP_PER_GROUP
    best, gsel = None, None
    for g in range(N_GROUPS):
        a = p[g * G:(g + 1) * G]
        sc = None
        for i in range(G):
            for j in range(i + 1, G):
                pair = a[i] + a[j]
                sc = pair if sc is None else jnp.maximum(sc, pair)
        if g == 0:
            best, gsel = sc, jnp.zeros_like(sc, dtype=jnp.int32)
        else:
            upd = sc > best
            best = jnp.where(upd, sc, best)
            gsel = jnp.where(upd, g, gsel)
    vals = []
    for j in range(G):
        vj = p[j]
        for g in range(1, N_GROUPS):
            vj = jnp.where(gsel == g, p[g * G + j], vj)
        vals.append(vj)
    p1, i1 = vals[0], jnp.zeros_like(gsel)
    for j in range(1, G):
        upd = vals[j] > p1
        p1 = jnp.where(upd, vals[j], p1)
        i1 = jnp.where(upd, j, i1)
    p2, i2 = None, None
    for j in range(G):
        cand = jnp.where(i1 == j, -1.0, vals[j])
        if p2 is None:
            p2, i2 = cand, jnp.zeros_like(gsel)
        else:
            upd = cand > p2
            p2 = jnp.where(upd, cand, p2)
            i2 = jnp.where(upd, j, i2)
    den = 1.0 / (p1 + p2)
    w1, w2 = p1 * den, p2 * den
    swap = i2 < i1
    a, b = jnp.where(swap, i2, i1), jnp.where(swap, i1, i2)
    w_lo, w_hi = jnp.where(swap, w2, w1), jnp.where(swap, w1, w2)
    pair = jnp.where(a == 0, b - 1, jnp.where(a == 1, b + 1, 5))
    cls = gsel * PAIRS_PER_GROUP + pair
    return cls.astype(F32), w_lo, w_hi


def _pack_pairs(x):
    n = x.shape[1] // 2
    b = pltpu.bitcast(x, jnp.uint32)
    w = (b[:, :n] >> 16) | (b[:, n:] & jnp.uint32(0xFFFF0000))
    return pltpu.bitcast(w, jnp.int32)


def _unpack_pairs(w):
    u = pltpu.bitcast(w, jnp.uint32)
    lo = pltpu.bitcast(u << 16, F32)
    hi = pltpu.bitcast(u & jnp.uint32(0xFFFF0000), F32)
    return jnp.concatenate([lo, hi], axis=1)


def _out_proj_kernel(yh_ref, yr_ref, o_ref, w_ref, x_ref, mod_ref, g_ref, wrh_ref, wrl_ref, br_ref,
                     xo_ref, h_ref, route_ref):
    y = (_dot(yh_ref[...], w_ref[0:D_HY, :]) + _dot(yr_ref[...], w_ref[D_HY:D_HY + D_RG, :])
         + _dot(o_ref[...], w_ref[D_HY + D_RG:D_MIX, :]))
    x = x_ref[...] + mod_ref[2:3, :] * y
    xo_ref[...] = x
    ms = jnp.mean(x * x, axis=-1, keepdims=True)
    h = (x * lax.rsqrt(ms + EPS) * g_ref[...]) * (1.0 + mod_ref[4:5, :]) + mod_ref[3:4, :]
    hh, hl = _split(h)
    h_ref[...] = _pack_pairs(hh.astype(F32))
    lg = _dot_nt(wrh_ref[...], hh) + _dot_nt(wrh_ref[...], hl) + _dot_nt(wrl_ref[...], hh) + br_ref[...]
    info = _route([lg[e:e + 1, :] for e in range(N_EXPERTS)])
    rt = jnp.concatenate(list(info) + [jnp.zeros((LANES - len(info), lg.shape[1]), F32)], axis=0)
    route_ref[...] = rt.T


def _out_proj(y_hy, y_rg, o, w_out, x, mod, g2, w_router, b_router, B, L, ctx_rows, tm=512):
    T = B * L
    tm = min(tm, L)
    nl = L // tm
    row = (lambda i: CTX_ROW) if ctx_rows else (lambda i: i // nl)
    wrt = w_router.T
    wrh = wrt.astype(BF16)
    wrl = (wrt - wrh.astype(F32)).astype(BF16)
    rows = lambda w: pl.BlockSpec((tm, w), lambda i: (i, 0))
    return pl.pallas_call(
        _out_proj_kernel,
        grid=(T // tm,),
        in_specs=[
            rows(D_HY), rows(D_RG), rows(D_DA),
            _const_spec((D_MIX, D_MODEL)),
            rows(D_MODEL),
            pl.BlockSpec((None, 6, D_MODEL), lambda i: (row(i), 0, 0)),
            _const_spec((1, D_MODEL)),
            _const_spec((N_EXPERTS, D_MODEL)),
            _const_spec((N_EXPERTS, D_MODEL)),
            _const_spec((N_EXPERTS, 1)),
        ],
        out_specs=[rows(D_MODEL), rows(D_MODEL // 2), rows(LANES)],
        out_shape=[
            jax.ShapeDtypeStruct((T, D_MODEL), F32),
            jax.ShapeDtypeStruct((T, D_MODEL // 2), jnp.int32),
            jax.ShapeDtypeStruct((T, LANES), F32),
        ],
        compiler_params=_cparams("parallel"),
        name="out_proj_route",
    )(y_hy, y_rg, o, w_out, x, mod, g2.reshape(1, D_MODEL), wrh, wrl, b_router.reshape(N_EXPERTS, 1))


def _gather_rows(table, idx, rows_per_step=32):
    info = plsc.get_sparse_core_info()
    n_workers = info.num_cores * info.num_subcores
    n, width = idx.shape[0], table.shape[1]
    per_worker = n // n_workers
    assert per_worker * n_workers == n and per_worker % rows_per_step == 0
    mesh = plsc.VectorSubcoreMesh(core_axis_name="c", subcore_axis_name="s")

    @functools.partial(
        pl.kernel, mesh=mesh,
        out_type=jax.ShapeDtypeStruct((n, width), table.dtype),
        scratch_types=[
            pltpu.VMEM((rows_per_step,), jnp.int32),
            pltpu.VMEM((rows_per_step, width), table.dtype),
            pltpu.SemaphoreType.DMA,
        ],
    )
    def gather(table_hbm, idx_hbm, out_hbm, idx_v, rows_v, sem):
        worker = lax.axis_index("s") * info.num_cores + lax.axis_index("c")
        base = worker * per_worker

        @pl.loop(0, per_worker // rows_per_step)
        def _(j):
            off = pl.multiple_of(base + j * rows_per_step, rows_per_step)
            pltpu.sync_copy(idx_hbm.at[pl.ds(off, rows_per_step)], idx_v)
            pltpu.async_copy(table_hbm.at[idx_v], rows_v, sem).wait()
            pltpu.sync_copy(rows_v, out_hbm.at[pl.ds(off, rows_per_step)])

    return gather(table, idx)


def _dispatch_plan(route, tm):
    T = route.shape[0]
    n_slots = T + N_CLASSES * tm
    cls = route[:, 0].astype(jnp.int32)
    onehot = (cls[:, None] == jnp.arange(N_CLASSES, dtype=jnp.int32)[None, :]).astype(jnp.int32)
    csum = jnp.cumsum(onehot, axis=0)
    rank = jnp.sum(onehot * csum, axis=1) - 1
    counts = csum[-1]
    padded = ((counts + tm - 1) // tm) * tm
    ends = jnp.cumsum(padded)
    pos = jnp.sum(onehot * (ends - padded)[None, :], axis=1) + rank
    inv = jnp.zeros((n_slots,), jnp.int32).at[pos].set(jnp.arange(T, dtype=jnp.int32), unique_indices=True)
    tile_start = jnp.arange(n_slots // tm, dtype=jnp.int32) * tm
    tile_cls = jnp.minimum(jnp.searchsorted(ends, tile_start, side="right"), N_CLASSES - 1).astype(jnp.int32)
    valid = (tile_start < ends[-1]).astype(jnp.int32)
    pairs = np.array([(a, b) for a in range(EXP_PER_GROUP) for b in range(a + 1, EXP_PER_GROUP)], np.int32)
    group, pair = tile_cls // PAIRS_PER_GROUP, tile_cls % PAIRS_PER_GROUP
    lo = group * EXP_PER_GROUP + jnp.asarray(pairs[:, 0])[pair]
    hi = group * EXP_PER_GROUP + jnp.asarray(pairs[:, 1])[pair]
    return pos, inv, lo, hi, valid


def _moe_sorted_kernel(lo_ref, hi_ref, valid_ref, xs_ref, ws_ref, wg_lo, wu_lo, wd_lo, wg_hi, wu_hi, wd_hi, o_ref):
    i = pl.program_id(0)

    @pl.when(valid_ref[i] == 1)
    def _():
        x = _unpack_pairs(xs_ref[...]).astype(BF16)
        y = None
        for wg, wu, wd, col in ((wg_lo, wu_lo, wd_lo, 1), (wg_hi, wu_hi, wd_hi, 2)):
            a = _dot(x, wg[...])
            he = (a * _sigmoid(a)) * _dot(x, wu[...]) * ws_ref[:, col:col + 1]
            part = _dot(he.astype(BF16), wd[...])
            y = part if y is None else y + part
        o_ref[...] = _pack_pairs(y.astype(BF16).astype(F32))

    @pl.when(valid_ref[i] == 0)
    def _():
        o_ref[...] = jnp.zeros_like(o_ref)


def _moe_sorted(xs, ws, lo, hi, valid, wg, wu, wd, tm):
    n_slots = xs.shape[0]
    half = D_MODEL // 2
    up = lambda sel: pl.BlockSpec((None, D_MODEL, D_EXPERT), lambda i, lo, hi, v: ((lo, hi)[sel][i], 0, 0))
    down = lambda sel: pl.BlockSpec((None, D_EXPERT, D_MODEL), lambda i, lo, hi, v: ((lo, hi)[sel][i], 0, 0))
    return pl.pallas_call(
        _moe_sorted_kernel,
        grid_spec=pltpu.PrefetchScalarGridSpec(
            num_scalar_prefetch=3,
            grid=(n_slots // tm,),
            in_specs=[
                pl.BlockSpec((tm, half), lambda i, lo, hi, v: (i, 0)),
                pl.BlockSpec((tm, LANES), lambda i, lo, hi, v: (i, 0)),
                up(0), up(0), down(0), up(1), up(1), down(1),
            ],
            out_specs=pl.BlockSpec((tm, half), lambda i, lo, hi, v: (i, 0)),
        ),
        out_shape=jax.ShapeDtypeStruct((n_slots, half), jnp.int32),
        compiler_params=_cparams("arbitrary"),
        name="moe_sorted",
    )(lo, hi, valid, xs, ws, wg, wu, wd, wg, wu, wd)


def _moe_residual_kernel(final, y_ref, x_ref, mod_ref, fg_ref, o_ref):
    x = x_ref[...] + mod_ref[5:6, :] * _unpack_pairs(y_ref[...])
    if final:
        x = x * lax.rsqrt(jnp.mean(x * x, axis=-1, keepdims=True) + EPS) * fg_ref[...]
    o_ref[...] = x


def _moe_residual(y, x, mod, final_g, final, B, L, ctx_rows, tm=512):
    T = B * L
    tm = min(tm, L)
    nl = L // tm
    row = (lambda i: CTX_ROW) if ctx_rows else (lambda i: i // nl)
    return pl.pallas_call(
        functools.partial(_moe_residual_kernel, final),
        grid=(T // tm,),
        in_specs=[
            pl.BlockSpec((tm, D_MODEL // 2), lambda i: (i, 0)),
            pl.BlockSpec((tm, D_MODEL), lambda i: (i, 0)),
            pl.BlockSpec((None, 6, D_MODEL), lambda i: (row(i), 0, 0)),
            _const_spec((1, D_MODEL)),
        ],
        out_specs=pl.BlockSpec((tm, D_MODEL), lambda i: (i, 0)),
        out_shape=jax.ShapeDtypeStruct((T, D_MODEL), F32),
        compiler_params=_cparams("parallel"),
        name="moe_residual_final" if final else "moe_residual",
    )(y, x, mod, final_g.reshape(1, D_MODEL))


def _moe(h, route, wg, wu, wd, x, mod, final_g, final, B, L, ctx_rows, tm=256):
    pos, inv, lo, hi, valid = _dispatch_plan(route, tm)
    xs = _gather_rows(h, inv)
    ws = _gather_rows(route, inv)
    ys = _moe_sorted(xs, ws, lo, hi, valid, wg, wu, wd, tm)
    y = _gather_rows(ys, pos)
    return _moe_residual(y, x, mod, final_g, final, B, L, ctx_rows)


def kernel(x_prompt, x_sample, cache_k, cache_v, state_rglru, c, c_ctx, w_ada, b_ada, norm1_g, norm2_g, w_in, w_out, hy_short_w, hy_short_b, hy_w1, hy_b1, hy_w2, hy_b2, hy_w3, hy_freq, hy_bias, rg_conv_w, rg_conv_b, rg_wa, rg_ba, rg_wx, rg_bx, rg_lambda, da_lambda, da_subln, w_router, b_router, moe_wg, moe_wu, moe_wd, final_g):
    Bp, Lp, D = x_prompt.shape
    Bs, Ls, _ = x_sample.shape
    assert Bs <= CTX_ROW
    cond = jnp.zeros((N_COND, D), F32).at[:Bs].set(c).at[CTX_ROW].set(c_ctx)
    mods = _ada_table(cond, w_ada, b_ada)

    dft = {L: tuple(jnp.asarray(m).astype(BF16) for m in _dft_mats(L)) for L in (Lp, Ls)}
    streams = [
        dict(B=Bp, L=Lp, ctx=True, x=x_prompt.reshape(Bp * Lp, D)),
        dict(B=Bs, L=Ls, ctx=False, x=x_sample.reshape(Bs * Ls, D)),
    ]
    ks, vs, ss = [], [], []
    for l in range(DEPTH):
        lam_init = 0.8 - 0.6 * math.exp(-0.3 * l)
        w_in_l = w_in[l].astype(BF16)
        w_out_l = w_out[l].astype(BF16)
        wg, wu, wd = moe_wg[l].astype(BF16), moe_wu[l].astype(BF16), moe_wd[l].astype(BF16)
        final = l == DEPTH - 1
        for st in streams:
            B, L, ctx = st["B"], st["L"], st["ctx"]
            cmat, smat = dft[L]
            p_hy, p_g, p_x, q, k, v = _norm_proj(
                st["x"], mods[l], norm1_g[l], w_in_l, B, L, rope=not ctx,
                kv_dtype=F32 if ctx else BF16, ctx_rows=ctx)
            kre, kim = _hy_spectra(L, cmat, smat, hy_w1[l], hy_b1[l], hy_w2[l], hy_b2[l], hy_w3[l], hy_freq[l])
            y_hy = _hyena(p_hy, B, L, cmat, smat, kre, kim, hy_short_w[l], hy_short_b[l], hy_bias[l])
            rg_args = (rg_conv_w[l], rg_conv_b[l], rg_wa[l], rg_ba[l], rg_wx[l], rg_bx[l], rg_lambda[l])
            if ctx:
                y_rg, s_l = _rglru(p_g, p_x, B, L, *rg_args, None)
                o = _attention(q, k, v, None, da_lambda[l], da_subln[l], lam_init, B, L)
                ks.append(k)
                vs.append(v)
                ss.append(s_l)
            else:
                y_rg = _rglru(p_g, p_x, B, L, *rg_args, state_rglru[:, l])
                o = _attention(q, k, v, (cache_k, cache_v, l), da_lambda[l], da_subln[l], lam_init, B, L)
            x_mid, h2, route = _out_proj(y_hy, y_rg, o, w_out_l, st["x"], mods[l], norm2_g[l],
                                         w_router, b_router, B, L, ctx)
            st["x"] = _moe(h2, route, wg, wu, wd, x_mid, mods[l], final_g, final, B, L, ctx)
    y_prompt = streams[0]["x"].reshape(Bp, Lp, D)
    y_sample = streams[1]["x"].reshape(Bs, Ls, D)
    return (y_prompt, y_sample, jnp.stack(ks, axis=1), jnp.stack(vs, axis=1), jnp.stack(ss, axis=1))
```

```python
import functools
import math

import numpy as np
import jax
import jax.numpy as jnp
from jax import lax
from jax.experimental import pallas as pl
from jax.experimental.pallas import tpu as pltpu
from jax.experimental.pallas import tpu_sc as plsc

F32 = jnp.float32
BF16 = jnp.bfloat16

D_MODEL = 1024
DEPTH = 2
GRID_W = 64
D_HY = 256
HY_EMB = 33
HY_BANDS = (HY_EMB - 1) // 2
HY_FFN = 64
HY_MIN_DECAY = math.log(1e-2) / 1.5
HY_MAX_DECAY = math.log(1e-2) / 0.3
D_RG = 256
N_RG_HEADS = 4
RG_C = 8.0
N_DA_HEADS = 4
DA_HEAD = 64
DA_VDIM = 2 * DA_HEAD
D_DA = N_DA_HEADS * DA_VDIM
D_MIX = D_HY + D_RG + D_DA
D_IN = 3 * D_HY + 2 * D_RG + 3 * D_DA
ROPE_PAIRS = DA_HEAD // 4
ROPE_THETA = 10000.0
N_EXPERTS = 16
N_GROUPS = 4
EXP_PER_GROUP = N_EXPERTS // N_GROUPS
D_EXPERT = 512
PAIRS_PER_GROUP = EXP_PER_GROUP * (EXP_PER_GROUP - 1) // 2
N_CLASSES = N_GROUPS * PAIRS_PER_GROUP
EPS = 1e-6
N_COND = 16
CTX_ROW = 8
LANES = 128
VMEM_LIMIT = 56 * 1024 * 1024


def _cparams(*sem):
    return pltpu.CompilerParams(dimension_semantics=sem, vmem_limit_bytes=VMEM_LIMIT)


def _split(x):
    hi = x.astype(BF16)
    lo = (x - hi.astype(F32)).astype(BF16)
    return hi, lo


def _dot(a, b):
    return jnp.dot(a, b, preferred_element_type=F32)


def _dot3(a, b):
    ah, al = _split(a)
    bh, bl = _split(b)
    return _dot(ah, bh) + _dot(al, bh) + _dot(ah, bl)


def _dot_nt(a, b):
    return lax.dot_general(a, b, (((1,), (1,)), ((), ())), preferred_element_type=F32)


def _sigmoid(x):
    return 1.0 / (1.0 + jnp.exp(-x))


def _const_spec(shape):
    n = len(shape)
    return pl.BlockSpec(shape, lambda *_: (0,) * n)


def _ada_kernel(c_ref, w_ref, b_ref, o_ref):
    c = c_ref[...]
    s = c * _sigmoid(c)
    o_ref[...] = _dot3(s, w_ref[...]) + b_ref[...]


def _ada_table(cond, w_ada, b_ada):
    D = D_MODEL
    out = pl.pallas_call(
        _ada_kernel,
        grid=(DEPTH, 6),
        in_specs=[
            pl.BlockSpec((N_COND, D), lambda l, j: (0, 0)),
            pl.BlockSpec((None, D, D), lambda l, j: (l, 0, j)),
            pl.BlockSpec((None, None, 1, D), lambda l, j: (l, j, 0, 0)),
        ],
        out_specs=pl.BlockSpec((None, None, N_COND, D), lambda l, j: (l, j, 0, 0)),
        out_shape=jax.ShapeDtypeStruct((DEPTH, 6, N_COND, D), F32),
        compiler_params=_cparams("parallel", "parallel"),
        name="ada_table",
    )(cond, w_ada, b_ada.reshape(DEPTH, 6, 1, D))
    return out.transpose(0, 2, 1, 3)


def _rope_tables(L):
    t = np.arange(L)
    j = np.arange(LANES)
    jj = j % DA_HEAD
    is_col = (jj // (DA_HEAD // 2)) == 1
    pair = jj % ROPE_PAIRS
    second = (jj % (DA_HEAD // 2)) >= ROPE_PAIRS
    inv = ROPE_THETA ** (-np.arange(ROPE_PAIRS, dtype=np.float64) / ROPE_PAIRS)
    pos = np.where(is_col[None, :], (t % GRID_W)[:, None], (t // GRID_W)[:, None]).astype(np.float64)
    ang = pos * inv[pair][None, :]
    cos = np.cos(ang).astype(np.float32)
    sin = np.sin(ang).astype(np.float32)
    sin_a = np.where(second[None, :], 0.0, -sin).astype(np.float32)
    sin_b = np.where(second[None, :], sin, 0.0).astype(np.float32)
    return cos, sin_a, sin_b


def _rope(x, cos, sin_a, sin_b):
    nxt = pltpu.roll(x, LANES - ROPE_PAIRS, axis=1)
    prv = pltpu.roll(x, ROPE_PAIRS, axis=1)
    return x * cos + nxt * sin_a + prv * sin_b


def _norm_proj_kernel(rope, kv_dtype, x_ref, mod_ref, g_ref, w_ref, *rest):
    if rope:
        cos_ref, sa_ref, sb_ref = rest[:3]
        rest = rest[3:]
    phy_ref, pg_ref, px_ref, q_ref, k_ref, v_ref = rest
    x = x_ref[...]
    ms = jnp.mean(x * x, axis=-1, keepdims=True)
    y = x * lax.rsqrt(ms + EPS) * g_ref[...]
    h = (y * (1.0 + mod_ref[1:2, :]) + mod_ref[0:1, :]).astype(BF16)
    o = 3 * D_HY
    phy_ref[...] = _dot(h, w_ref[:, 0:o]).astype(BF16)
    pg_ref[...] = _dot(h, w_ref[:, o:o + D_RG])
    px_ref[...] = _dot(h, w_ref[:, o + D_RG:o + 2 * D_RG])
    o += 2 * D_RG
    q = _dot(h, w_ref[:, o:o + D_DA]) * (DA_HEAD ** -0.5 * math.log2(math.e))
    k = _dot(h, w_ref[:, o + D_DA:o + 2 * D_DA])
    v = _dot(h, w_ref[:, o + 2 * D_DA:o + 3 * D_DA])
    if rope:
        cos, sa, sb = cos_ref[...], sa_ref[...], sb_ref[...]
    for hd in range(N_DA_HEADS):
        sl = slice(hd * DA_VDIM, (hd + 1) * DA_VDIM)
        qh, kh = q[:, sl], k[:, sl]
        if rope:
            qh = _rope(qh, cos, sa, sb)
            kh = _rope(kh, cos, sa, sb)
        q_ref[hd] = qh.astype(BF16)
        k_ref[hd] = kh.astype(kv_dtype)
        v_ref[hd] = v[:, sl].astype(kv_dtype)


def _norm_proj(x, mod, g, w_in, B, L, rope, kv_dtype, ctx_rows, tm=512):
    T = B * L
    tm = min(tm, L)
    nl = L // tm
    row = (lambda i: CTX_ROW) if ctx_rows else (lambda i: i // nl)
    in_specs = [
        pl.BlockSpec((tm, D_MODEL), lambda i: (i, 0)),
        pl.BlockSpec((None, 6, D_MODEL), lambda i: (row(i), 0, 0)),
        _const_spec((1, D_MODEL)),
        _const_spec((D_MODEL, D_IN)),
    ]
    args = [x, mod, g.reshape(1, D_MODEL), w_in]
    if rope:
        tabs = _rope_tables(L)
        in_specs += [pl.BlockSpec((tm, LANES), lambda i: (i % nl, 0))] * 3
        args += [jnp.asarray(t) for t in tabs]
    head_spec = pl.BlockSpec((None, N_DA_HEADS, tm, DA_VDIM), lambda i: (i // nl, 0, i % nl, 0))
    head_shape = (B, N_DA_HEADS, L, DA_VDIM)
    return pl.pallas_call(
        functools.partial(_norm_proj_kernel, rope, kv_dtype),
        grid=(T // tm,),
        in_specs=in_specs,
        out_specs=[
            pl.BlockSpec((tm, 3 * D_HY), lambda i: (i, 0)),
            pl.BlockSpec((tm, D_RG), lambda i: (i, 0)),
            pl.BlockSpec((tm, D_RG), lambda i: (i, 0)),
            head_spec, head_spec, head_spec,
        ],
        out_shape=[
            jax.ShapeDtypeStruct((T, 3 * D_HY), BF16),
            jax.ShapeDtypeStruct((T, D_RG), F32),
            jax.ShapeDtypeStruct((T, D_RG), F32),
            jax.ShapeDtypeStruct(head_shape, BF16),
            jax.ShapeDtypeStruct(head_shape, kv_dtype),
            jax.ShapeDtypeStruct(head_shape, kv_dtype),
        ],
        compiler_params=_cparams("parallel"),
        name="norm_proj_rope" if rope else "norm_proj",
    )(*args)


def _dft_mats(L):
    n = 2 * L - 1
    fs = (np.arange(L, dtype=np.int64)[:, None] * np.arange(L, dtype=np.int64)[None, :]) % n
    ang = fs.astype(np.float64) * (2.0 * np.pi / n)
    return np.cos(ang).astype(np.float32), np.sin(ang).astype(np.float32)


def _hy_features(L):
    t = np.linspace(0.0, 1.0, L, dtype=np.float64)[:, None]
    ang = ((2.0 * math.pi / L) * np.arange(L, dtype=np.float64))[:, None]
    bands = np.linspace(1e-4, HY_BANDS - 1, HY_BANDS, dtype=np.float64)[None, :]
    ba = bands * ang
    z = np.concatenate([t, np.cos(ba), -np.sin(ba)], axis=-1).astype(np.float32)
    return np.pad(z, ((0, 0), (0, LANES - HY_EMB)))


def _hy_filter_kernel(L, z_ref, w1_ref, b1_ref, w2_ref, b2_ref, w3_ref, fr_ref, rc_ref, rs_ref):
    z = z_ref[...]
    h = jnp.sin(fr_ref[0:1, :] * (_dot3(z, w1_ref[...]) + b1_ref[...]))
    h = jnp.sin(fr_ref[1:2, :] * (_dot3(h, w2_ref[...]) + b2_ref[...]))
    h = _dot3(h, w3_ref[...])
    t = z[:, 0:1]
    step = (HY_MAX_DECAY - HY_MIN_DECAY) / (D_HY - 1)
    deltas = HY_MIN_DECAY + step * lax.broadcasted_iota(jnp.int32, (1, D_HY), 1).astype(F32)
    window = jnp.exp(-t * jnp.abs(deltas))
    not_first = lax.broadcasted_iota(jnp.int32, (L, 1), 0) > 0
    for o in range(2):
        hf = h[:, (2 * o) * D_HY:(2 * o + 1) * D_HY] * window
        hb = jnp.where(not_first, h[:, (2 * o + 1) * D_HY:(2 * o + 2) * D_HY] * window, 0.0)
        rc_ref[:, o * D_HY:(o + 1) * D_HY] = hf + hb
        rs_ref[:, o * D_HY:(o + 1) * D_HY] = hb - hf


def _hy_spectrum_kernel(c_ref, s_ref, rc_ref, rs_ref, w_ref, kre_ref, kim_ref):
    rch, rcl = _split(rc_ref[...])
    rsh, rsl = _split(rs_ref[...])
    c, s, w = c_ref[...], s_ref[...], w_ref[...]
    kre_ref[...] = (_dot(c, rch) + _dot(c, rcl)) * w
    kim_ref[...] = (_dot(s, rsh) + _dot(s, rsl)) * w


def _hy_spectra(L, cmat, smat, w1, b1, w2, b2, w3, freq):
    z = jnp.asarray(_hy_features(L))
    w1p = jnp.pad(w1, ((0, LANES - HY_EMB), (0, 0)))
    nw = 2 * D_HY
    rc, rs = pl.pallas_call(
        functools.partial(_hy_filter_kernel, L),
        out_shape=[jax.ShapeDtypeStruct((L, nw), F32)] * 2,
        compiler_params=pltpu.CompilerParams(vmem_limit_bytes=VMEM_LIMIT),
        name="hy_filter",
    )(z, w1p, b1.reshape(1, HY_FFN), w2, b2.reshape(1, HY_FFN), w3, freq)
    n = 2 * L - 1
    wsc = np.full((L, 1), 2.0 / n, np.float32)
    wsc[0, 0] = 1.0 / n
    tr = min(L, 256)
    return pl.pallas_call(
        _hy_spectrum_kernel,
        grid=(L // tr,),
        in_specs=[
            pl.BlockSpec((tr, L), lambda i: (i, 0)),
            pl.BlockSpec((tr, L), lambda i: (i, 0)),
            _const_spec((L, nw)),
            _const_spec((L, nw)),
            pl.BlockSpec((tr, 1), lambda i: (i, 0)),
        ],
        out_specs=[pl.BlockSpec((tr, nw), lambda i: (i, 0))] * 2,
        out_shape=[jax.ShapeDtypeStruct((L, nw), F32)] * 2,
        compiler_params=_cparams("parallel"),
        name="hy_spectrum",
    )(cmat, smat, rc, rs, jnp.asarray(wsc))


def _hyena_kernel(L, tr, p_ref, sw_ref, sb_ref, bias_ref, c_ref, s_ref, kre_ref, kim_ref, o_ref,
                  pad_ref, u_ref, sig_ref, sig16_ref, zre_ref, zim_ref):
    C3 = 3 * D_HY
    zeros = jnp.zeros((8, C3), F32)
    pad_ref[0:8, :] = zeros
    pad_ref[8 + L:16 + L, :] = zeros
    chunks = [slice(r0, r0 + tr) for r0 in range(0, L, tr)]
    for c in chunks:
        pad_ref[8 + c.start:8 + c.stop, :] = p_ref[c, :].astype(F32)
    for c in chunks:
        u = sb_ref[...]
        for j in range(3):
            u = u + pad_ref[7 + j + c.start:7 + j + c.stop, :] * sw_ref[j:j + 1, :]
        u_ref[c, :] = u[:, D_HY:C3]
        sig_ref[c, :] = u[:, 0:D_HY]
        sig16_ref[c, :] = u[:, 0:D_HY].astype(BF16)

    for o in range(2):
        ko = slice(o * D_HY, (o + 1) * D_HY)
        for c in chunks:
            ure = _dot(c_ref[c, :], sig16_ref[...])
            us = _dot(s_ref[c, :], sig16_ref[...])
            kre, kim = kre_ref[c, ko], kim_ref[c, ko]
            zre_ref[c, :] = (ure * kre + us * kim).astype(BF16)
            zim_ref[c, :] = (ure * kim - us * kre).astype(BF16)
        gate = slice(o * D_HY, (o + 1) * D_HY)
        for c in chunks:
            y = _dot(c_ref[c, :], zre_ref[...]) - _dot(s_ref[c, :], zim_ref[...])
            z = u_ref[c, gate] * (y + sig_ref[c, :] * bias_ref[o:o + 1, :])
            if o == 0:
                sig_ref[c, :] = z
                sig16_ref[c, :] = z.astype(BF16)
            else:
                o_ref[c, :] = z.astype(o_ref.dtype)


def _hyena(p_hy, B, L, cmat, smat, kre, kim, short_w, short_b, bias, tr=512):
    C3 = 3 * D_HY
    tr = min(tr, L)
    once = pl.Buffered(1)
    return pl.pallas_call(
        functools.partial(_hyena_kernel, L, tr),
        grid=(B,),
        in_specs=[
            pl.BlockSpec((L, C3), lambda b: (b, 0)),
            _const_spec((3, C3)),
            _const_spec((1, C3)),
            _const_spec((2, D_HY)),
            pl.BlockSpec((L, L), lambda b: (0, 0), pipeline_mode=once),
            pl.BlockSpec((L, L), lambda b: (0, 0), pipeline_mode=once),
            pl.BlockSpec((L, 2 * D_HY), lambda b: (0, 0), pipeline_mode=once),
            pl.BlockSpec((L, 2 * D_HY), lambda b: (0, 0), pipeline_mode=once),
        ],
        out_specs=pl.BlockSpec((L, D_HY), lambda b: (b, 0)),
        out_shape=jax.ShapeDtypeStruct((B * L, D_HY), BF16),
        scratch_shapes=[
            pltpu.VMEM((L + 16, C3), F32),
            pltpu.VMEM((L, 2 * D_HY), F32),
            pltpu.VMEM((L, D_HY), F32),
            pltpu.VMEM((L, D_HY), BF16),
            pltpu.VMEM((L, D_HY), BF16),
            pltpu.VMEM((L, D_HY), BF16),
        ],
        compiler_params=_cparams("parallel"),
        name="hyena",
    )(p_hy, short_w, short_b.reshape(1, C3), bias, cmat, smat, kre, kim)


def _softplus(z):
    return jnp.maximum(z, 0.0) + jnp.log1p(jnp.exp(-jnp.abs(z)))


def _expm1(x):
    u = jnp.exp(x)
    near = (u - 1.0) * x / jnp.where(u == 1.0, 1.0, jnp.log(u))
    return jnp.where(x < -0.5, u - 1.0, jnp.where(u == 1.0, x, near))


def _gelu_tanh(x):
    return 0.5 * x * (1.0 + jnp.tanh(math.sqrt(2.0 / math.pi) * (x + 0.044715 * x * x * x)))


def _rglru_kernel(L, has_state, pg_ref, px_ref, cw_ref, cb_ref, wh_ref, wl_ref, gb_ref, lam_ref, *rest):
    if has_state:
        st_ref, y_ref, pad_ref, a_ref, b_ref, h_ref = rest
    else:
        y_ref, st_out_ref, pad_ref, a_ref, b_ref, h_ref = rest
    C = D_RG
    zeros = jnp.zeros((8, C), F32)
    pad_ref[0:8, :] = zeros
    pad_ref[8 + L:16 + L, :] = zeros
    pad_ref[8:8 + L, :] = px_ref[...]
    sp = _softplus(-lam_ref[...])
    tr = min(L, 256)
    for r0 in range(0, L, tr):
        xr = cb_ref[...]
        for j in range(4):
            xr = xr + pad_ref[6 + j + r0:6 + j + r0 + tr, :] * cw_ref[j:j + 1, :]
        xh, xl = _split(xr)
        for d in range(2):
            g = []
            for m in range(2):
                cols = slice((2 * d + m) * C, (2 * d + m + 1) * C)
                wh = wh_ref[:, cols]
                g.append(_sigmoid(_dot(xh, wh) + _dot(xl, wh) + _dot(xh, wl_ref[:, cols]) + gb_ref[:, cols]))
            log_a = -RG_C * g[0] * sp[d:d + 1, :]
            a_ref[d, r0:r0 + tr, :] = jnp.exp(log_a)
            b_ref[d, r0:r0 + tr, :] = jnp.sqrt(-_expm1(2.0 * log_a)) * (g[1] * xr)

    if has_state:
        h0f, h0b = st_ref[0:1, :], st_ref[1:2, :]
    else:
        h0f = h0b = jnp.zeros((1, C), F32)

    def step(t, carry):
        hf, hb = carry
        tb = L - 1 - t
        hf = a_ref[0, pl.ds(t, 1), :] * hf + b_ref[0, pl.ds(t, 1), :]
        hb = a_ref[1, pl.ds(tb, 1), :] * hb + b_ref[1, pl.ds(tb, 1), :]
        h_ref[0, pl.ds(t, 1), :] = hf
        h_ref[1, pl.ds(tb, 1), :] = hb
        return hf, hb

    lax.fori_loop(0, L, step, (h0f, h0b))
    y_ref[...] = ((h_ref[0] + h_ref[1]) * _gelu_tanh(pg_ref[...])).astype(y_ref.dtype)
    if not has_state:
        st_out_ref[0:1, :] = h_ref[0, L - 1:L, :]
        st_out_ref[1:2, :] = h_ref[1, 0:1, :]


def _block_diag(w):
    H, d, _ = w.shape
    eye = jnp.eye(H, dtype=w.dtype)
    return (eye[:, None, :, None] * w[:, :, None, :]).reshape(H * d, H * d)


def _rglru(p_g, p_x, B, L, conv_w, conv_b, wa, ba, wx, bx, lam, state):
    C = D_RG
    wcat = jnp.concatenate([_block_diag(wa[0]), _block_diag(wx[0]), _block_diag(wa[1]), _block_diag(wx[1])], axis=1)
    wh = wcat.astype(BF16)
    wl = (wcat - wh.astype(F32)).astype(BF16)
    gb = jnp.concatenate([ba[0], bx[0], ba[1], bx[1]]).reshape(1, 4 * C)
    has_state = state is not None
    in_specs = [
        pl.BlockSpec((L, C), lambda b: (b, 0)),
        pl.BlockSpec((L, C), lambda b: (b, 0)),
        _const_spec((4, C)),
        _const_spec((1, C)),
        _const_spec((C, 4 * C)),
        _const_spec((C, 4 * C)),
        _const_spec((1, 4 * C)),
        _const_spec((2, C)),
    ]
    args = [p_g, p_x, conv_w, conv_b.reshape(1, C), wh, wl, gb, lam]
    y_spec = pl.BlockSpec((L, C), lambda b: (b, 0))
    y_shape = jax.ShapeDtypeStruct((B * L, C), BF16)
    if has_state:
        in_specs.append(pl.BlockSpec((None, 2, C), lambda b: (b, 0, 0)))
        args.append(state)
        out_specs, out_shape = y_spec, y_shape
    else:
        out_specs = [y_spec, pl.BlockSpec((None, 2, C), lambda b: (b, 0, 0))]
        out_shape = [y_shape, jax.ShapeDtypeStruct((B, 2, C), F32)]
    return pl.pallas_call(
        functools.partial(_rglru_kernel, L, has_state),
        grid=(B,),
        in_specs=in_specs,
        out_specs=out_specs,
        out_shape=out_shape,
        scratch_shapes=[
            pltpu.VMEM((L + 16, C), F32),
            pltpu.VMEM((2, L, C), F32),
            pltpu.VMEM((2, L, C), F32),
            pltpu.VMEM((2, L, C), F32),
        ],
        compiler_params=_cparams("parallel"),
        name="rglru_state" if has_state else "rglru",
    )(*args)


def _attn_kernel(L, P, tq, lc, lam_init, q_ref, k_ref, v_ref, *rest):
    if P:
        ck_ref, cv_ref, dal_ref, sub_ref, o_ref, kk_ref, vv_ref = rest
    else:
        dal_ref, sub_ref, o_ref, kk_ref, vv_ref = rest
    lv = dal_ref[...]
    s01 = jnp.sum(lv[0:1, :] * lv[1:2, :], axis=-1, keepdims=True)
    s23 = jnp.sum(lv[2:3, :] * lv[3:4, :], axis=-1, keepdims=True)
    lam = jnp.exp(s01) - jnp.exp(s23) + lam_init
    first_half = lax.broadcasted_iota(jnp.int32, (1, DA_VDIM), 1) < DA_HEAD
    sub = sub_ref[...] * (1.0 - lam_init)
    for hd in range(N_DA_HEADS):
        if P:
            kk_ref[0:P, :] = ck_ref[hd].astype(BF16)
            vv_ref[0:P, :] = cv_ref[hd].astype(BF16)
        kk_ref[P:P + L, :] = k_ref[hd].astype(BF16)
        vv_ref[P:P + L, :] = v_ref[hd].astype(BF16)

        def qblock(i, carry):
            r0 = pl.multiple_of(i * tq, tq)
            q = q_ref[hd, pl.ds(r0, tq), :]
            zero = jnp.zeros_like(q)
            qs = jnp.concatenate([jnp.where(first_half, q, zero), jnp.where(first_half, zero, q)], axis=0)
            m = l = acc = None
            for c0 in range(0, P + L, lc):
                s = _dot_nt(qs, kk_ref[c0:c0 + lc, :])
                mc = jnp.max(s, axis=-1, keepdims=True)
                if m is None:
                    m = mc
                    p = jnp.exp2(s - m)
                    l = jnp.sum(p, axis=-1, keepdims=True)
                    acc = _dot(p.astype(BF16), vv_ref[c0:c0 + lc, :])
                else:
                    m_new = jnp.maximum(m, mc)
                    alpha = jnp.exp2(m - m_new)
                    p = jnp.exp2(s - m_new)
                    l = alpha * l + jnp.sum(p, axis=-1, keepdims=True)
                    acc = alpha * acc + _dot(p.astype(BF16), vv_ref[c0:c0 + lc, :])
                    m = m_new
            rinv = 1.0 / l
            o = acc[0:tq] * rinv[0:tq] - acc[tq:2 * tq] * (lam * rinv[tq:2 * tq])
            o = o * lax.rsqrt(jnp.mean(o * o, axis=-1, keepdims=True) + EPS) * sub
            o_ref[pl.ds(r0, tq), hd * DA_VDIM:(hd + 1) * DA_VDIM] = o.astype(o_ref.dtype)
            return carry

        lax.fori_loop(0, L // tq, qblock, 0)


def _attention(q, k, v, cache, dal, subln, lam_init, B, L, tq=128, lc=512):
    H, dv = N_DA_HEADS, DA_VDIM
    lc = min(lc, L)
    hspec = pl.BlockSpec((None, H, L, dv), lambda b: (b, 0, 0, 0))
    in_specs = [hspec, hspec, hspec]
    args = [q, k, v]
    P = 0
    if cache is not None:
        ck, cv, layer = cache
        P = ck.shape[3]
        cspec = pl.BlockSpec((None, None, H, P, dv), lambda b: (b, layer, 0, 0, 0))
        in_specs += [cspec, cspec]
        args += [ck, cv]
    assert (P + L) % lc == 0 and L % tq == 0
    in_specs += [_const_spec((4, DA_HEAD)), _const_spec((1, dv))]
    args += [dal, subln.reshape(1, dv)]
    return pl.pallas_call(
        functools.partial(_attn_kernel, L, P, tq, lc, lam_init),
        grid=(B,),
        in_specs=in_specs,
        out_specs=pl.BlockSpec((L, H * dv), lambda b: (b, 0)),
        out_shape=jax.ShapeDtypeStruct((B * L, H * dv), BF16),
        scratch_shapes=[pltpu.VMEM((P + L, dv), BF16), pltpu.VMEM((P + L, dv), BF16)],
        compiler_params=_cparams("parallel"),
        name="diff_attn_cache" if P else "diff_attn",
    )(*args)


def _route(logits):
    m = logits[0]
    for e in range(1, N_EXPERTS):
        m = jnp.maximum(m, logits[e])
    ex = [jnp.exp(l - m) for l in logits]
    tot = ex[0]
    for e in range(1, N_EXPERTS):
        tot = tot + ex[e]
    inv = 1.0 / tot
    p = [e_ * inv for e_ in ex]
    G = EXP_PER_GROUP
    best, gsel = None, None
    for g in range(N_GROUPS):
        a = p[g * G:(g + 1) * G]
        sc = None
        for i in range(G):
            for j in range(i + 1, G):
                pair = a[i] + a[j]
                sc = pair if sc is None else jnp.maximum(sc, pair)
        if g == 0:
            best, gsel = sc, jnp.zeros_like(sc, dtype=jnp.int32)
        else:
            upd = sc > best
            best = jnp.where(upd, sc, best)
            gsel = jnp.where(upd, g, gsel)
    vals = []
    for j in range(G):
        vj = p[j]
        for g in range(1, N_GROUPS):
            vj = jnp.where(gsel == g, p[g * G + j], vj)
        vals.append(vj)
    p1, i1 = vals[0], jnp.zeros_like(gsel)
    for j in range(1, G):
        upd = vals[j] > p1
        p1 = jnp.where(upd, vals[j], p1)
        i1 = jnp.where(upd, j, i1)
    p2, i2 = None, None
    for j in range(G):
        cand = jnp.where(i1 == j, -1.0, vals[j])
        if p2 is None:
            p2, i2 = cand, jnp.zeros_like(gsel)
        else:
            upd = cand > p2
            p2 = jnp.where(upd, cand, p2)
            i2 = jnp.where(upd, j, i2)
    den = 1.0 / (p1 + p2)
    w1, w2 = p1 * den, p2 * den
    swap = i2 < i1
    a, b = jnp.where(swap, i2, i1), jnp.where(swap, i1, i2)
    w_lo, w_hi = jnp.where(swap, w2, w1), jnp.where(swap, w1, w2)
    pair = jnp.where(a == 0, b - 1, jnp.where(a == 1, b + 1, 5))
    cls = gsel * PAIRS_PER_GROUP + pair
    return cls.astype(F32), w_lo, w_hi


def _pack_pairs(x):
    n = x.shape[1] // 2
    b = pltpu.bitcast(x, jnp.uint32)
    w = (b[:, :n] >> 16) | (b[:, n:] & jnp.uint32(0xFFFF0000))
    return pltpu.bitcast(w, jnp.int32)


def _unpack_pairs(w):
    u = pltpu.bitcast(w, jnp.uint32)
    lo = pltpu.bitcast(u << 16, F32)
    hi = pltpu.bitcast(u & jnp.uint32(0xFFFF0000), F32)
    return jnp.concatenate([lo, hi], axis=1)


def _out_proj_kernel(yh_ref, yr_ref, o_ref, w_ref, x_ref, mod_ref, g_ref, wrh_ref, wrl_ref, br_ref,
                     xo_ref, h_ref, route_ref):
    y = (_dot(yh_ref[...], w_ref[0:D_HY, :]) + _dot(yr_ref[...], w_ref[D_HY:D_HY + D_RG, :])
         + _dot(o_ref[...], w_ref[D_HY + D_RG:D_MIX, :]))
    x = x_ref[...] + mod_ref[2:3, :] * y
    xo_ref[...] = x
    ms = jnp.mean(x * x, axis=-1, keepdims=True)
    h = (x * lax.rsqrt(ms + EPS) * g_ref[...]) * (1.0 + mod_ref[4:5, :]) + mod_ref[3:4, :]
    hh, hl = _split(h)
    h_ref[...] = _pack_pairs(hh.astype(F32))
    lg = _dot_nt(wrh_ref[...], hh) + _dot_nt(wrh_ref[...], hl) + _dot_nt(wrl_ref[...], hh) + br_ref[...]
    info = _route([lg[e:e + 1, :] for e in range(N_EXPERTS)])
    rt = jnp.concatenate(list(info) + [jnp.zeros((LANES - len(info), lg.shape[1]), F32)], axis=0)
    route_ref[...] = rt.T


def _out_proj(y_hy, y_rg, o, w_out, x, mod, g2, w_router, b_router, B, L, ctx_rows, tm=512):
    T = B * L
    tm = min(tm, L)
    nl = L // tm
    row = (lambda i: CTX_ROW) if ctx_rows else (lambda i: i // nl)
    wrt = w_router.T
    wrh = wrt.astype(BF16)
    wrl = (wrt - wrh.astype(F32)).astype(BF16)
    rows = lambda w: pl.BlockSpec((tm, w), lambda i: (i, 0))
    return pl.pallas_call(
        _out_proj_kernel,
        grid=(T // tm,),
        in_specs=[
            rows(D_HY), rows(D_RG), rows(D_DA),
            _const_spec((D_MIX, D_MODEL)),
            rows(D_MODEL),
            pl.BlockSpec((None, 6, D_MODEL), lambda i: (row(i), 0, 0)),
            _const_spec((1, D_MODEL)),
            _const_spec((N_EXPERTS, D_MODEL)),
            _const_spec((N_EXPERTS, D_MODEL)),
            _const_spec((N_EXPERTS, 1)),
        ],
        out_specs=[rows(D_MODEL), rows(D_MODEL // 2), rows(LANES)],
        out_shape=[
            jax.ShapeDtypeStruct((T, D_MODEL), F32),
            jax.ShapeDtypeStruct((T, D_MODEL // 2), jnp.int32),
            jax.ShapeDtypeStruct((T, LANES), F32),
        ],
        compiler_params=_cparams("parallel"),
        name="out_proj_route",
    )(y_hy, y_rg, o, w_out, x, mod, g2.reshape(1, D_MODEL), wrh, wrl, b_router.reshape(N_EXPERTS, 1))


def _gather_rows(table, idx, rows_per_step=16, n_buf=4):
    info = plsc.get_sparse_core_info()
    n_workers = info.num_cores * info.num_subcores
    n, width = idx.shape[0], table.shape[1]
    per_worker = n // n_workers
    n_steps = per_worker // rows_per_step
    assert per_worker * n_workers == n and n_steps * rows_per_step == per_worker and n_steps % n_buf == 0
    mesh = plsc.VectorSubcoreMesh(core_axis_name="c", subcore_axis_name="s")

    @functools.partial(
        pl.kernel, mesh=mesh,
        out_type=jax.ShapeDtypeStruct((n, width), table.dtype),
        scratch_types=[
            pltpu.VMEM((per_worker,), jnp.int32),
            pltpu.VMEM((n_buf, rows_per_step, width), table.dtype),
            pltpu.SemaphoreType.DMA((n_buf,)),
            pltpu.SemaphoreType.DMA((n_buf,)),
        ],
    )
    def gather(table_hbm, idx_hbm, out_hbm, idx_v, rows_v, sem_in, sem_out):
        worker = lax.axis_index("s") * info.num_cores + lax.axis_index("c")
        base = pl.multiple_of(worker * per_worker, per_worker)
        pltpu.sync_copy(idx_hbm.at[pl.ds(base, per_worker)], idx_v)

        def read(b, step):
            off = pl.multiple_of(step * rows_per_step, rows_per_step)
            return pltpu.make_async_copy(table_hbm.at[idx_v.at[pl.ds(off, rows_per_step)]], rows_v.at[b], sem_in.at[b])

        def write(b, step):
            off = pl.multiple_of(base + step * rows_per_step, rows_per_step)
            return pltpu.make_async_copy(rows_v.at[b], out_hbm.at[pl.ds(off, rows_per_step)], sem_out.at[b])

        @pl.loop(0, n_steps, step=n_buf)
        def _(j):
            for b in range(n_buf):
                @pl.when(j > 0)
                def _():
                    write(b, j - n_buf + b).wait()
                read(b, j + b).start()
            for b in range(n_buf):
                read(b, j + b).wait()
                write(b, j + b).start()

        for b in range(n_buf):
            write(b, n_steps - n_buf + b).wait()

    return gather(table, idx)


def _dispatch_plan(route, tm):
    T = route.shape[0]
    n_slots = T + N_CLASSES * tm
    cls = route[:, 0].astype(jnp.int32)
    onehot = (cls[:, None] == jnp.arange(N_CLASSES, dtype=jnp.int32)[None, :]).astype(jnp.int32)
    csum = jnp.cumsum(onehot, axis=0)
    rank = jnp.sum(onehot * csum, axis=1) - 1
    counts = csum[-1]
    padded = ((counts + tm - 1) // tm) * tm
    ends = jnp.cumsum(padded)
    pos = jnp.sum(onehot * (ends - padded)[None, :], axis=1) + rank
    inv = jnp.zeros((n_slots,), jnp.int32).at[pos].set(jnp.arange(T, dtype=jnp.int32), unique_indices=True)
    tile_start = jnp.arange(n_slots // tm, dtype=jnp.int32) * tm
    tile_cls = jnp.minimum(jnp.searchsorted(ends, tile_start, side="right"), N_CLASSES - 1).astype(jnp.int32)
    valid = (tile_start < ends[-1]).astype(jnp.int32)
    pairs = np.array([(a, b) for a in range(EXP_PER_GROUP) for b in range(a + 1, EXP_PER_GROUP)], np.int32)
    group, pair = tile_cls // PAIRS_PER_GROUP, tile_cls % PAIRS_PER_GROUP
    lo = group * EXP_PER_GROUP + jnp.asarray(pairs[:, 0])[pair]
    hi = group * EXP_PER_GROUP + jnp.asarray(pairs[:, 1])[pair]
    return pos, inv, lo, hi, valid


def _moe_sorted_kernel(lo_ref, hi_ref, valid_ref, xs_ref, ws_ref, wg_lo, wu_lo, wd_lo, wg_hi, wu_hi, wd_hi, o_ref):
    i = pl.program_id(0)

    @pl.when(valid_ref[i] == 1)
    def _():
        x = _unpack_pairs(xs_ref[...]).astype(BF16)
        y = None
        for wg, wu, wd, col in ((wg_lo, wu_lo, wd_lo, 1), (wg_hi, wu_hi, wd_hi, 2)):
            a = _dot(x, wg[...])
            he = (a * _sigmoid(a)) * _dot(x, wu[...]) * ws_ref[:, col:col + 1]
            part = _dot(he.astype(BF16), wd[...])
            y = part if y is None else y + part
        o_ref[...] = _pack_pairs(y.astype(BF16).astype(F32))

    @pl.when(valid_ref[i] == 0)
    def _():
        o_ref[...] = jnp.zeros_like(o_ref)


def _moe_sorted(xs, ws, lo, hi, valid, wg, wu, wd, tm):
    n_slots = xs.shape[0]
    half = D_MODEL // 2
    up = lambda sel: pl.BlockSpec((None, D_MODEL, D_EXPERT), lambda i, lo, hi, v: ((lo, hi)[sel][i], 0, 0))
    down = lambda sel: pl.BlockSpec((None, D_EXPERT, D_MODEL), lambda i, lo, hi, v: ((lo, hi)[sel][i], 0, 0))
    return pl.pallas_call(
        _moe_sorted_kernel,
        grid_spec=pltpu.PrefetchScalarGridSpec(
            num_scalar_prefetch=3,
            grid=(n_slots // tm,),
            in_specs=[
                pl.BlockSpec((tm, half), lambda i, lo, hi, v: (i, 0)),
                pl.BlockSpec((tm, LANES), lambda i, lo, hi, v: (i, 0)),
                up(0), up(0), down(0), up(1), up(1), down(1),
            ],
            out_specs=pl.BlockSpec((tm, half), lambda i, lo, hi, v: (i, 0)),
        ),
        out_shape=jax.ShapeDtypeStruct((n_slots, half), jnp.int32),
        compiler_params=_cparams("arbitrary"),
        name="moe_sorted",
    )(lo, hi, valid, xs, ws, wg, wu, wd, wg, wu, wd)


def _moe_residual_kernel(final, y_ref, x_ref, mod_ref, fg_ref, o_ref):
    x = x_ref[...] + mod_ref[5:6, :] * _unpack_pairs(y_ref[...])
    if final:
        x = x * lax.rsqrt(jnp.mean(x * x, axis=-1, keepdims=True) + EPS) * fg_ref[...]
    o_ref[...] = x


def _moe_residual(y, x, mod, final_g, final, B, L, ctx_rows, tm=512):
    T = B * L
    tm = min(tm, L)
    nl = L // tm
    row = (lambda i: CTX_ROW) if ctx_rows else (lambda i: i // nl)
    return pl.pallas_call(
        functools.partial(_moe_residual_kernel, final),
        grid=(T // tm,),
        in_specs=[
            pl.BlockSpec((tm, D_MODEL // 2), lambda i: (i, 0)),
            pl.BlockSpec((tm, D_MODEL), lambda i: (i, 0)),
            pl.BlockSpec((None, 6, D_MODEL), lambda i: (row(i), 0, 0)),
            _const_spec((1, D_MODEL)),
        ],
        out_specs=pl.BlockSpec((tm, D_MODEL), lambda i: (i, 0)),
        out_shape=jax.ShapeDtypeStruct((T, D_MODEL), F32),
        compiler_params=_cparams("parallel"),
        name="moe_residual_final" if final else "moe_residual",
    )(y, x, mod, final_g.reshape(1, D_MODEL))


def _moe(h, route, wg, wu, wd, x, mod, final_g, final, B, L, ctx_rows, tm=256):
    pos, inv, lo, hi, valid = _dispatch_plan(route, tm)
    xs = _gather_rows(h, inv)
    ws = _gather_rows(route, inv)
    ys = _moe_sorted(xs, ws, lo, hi, valid, wg, wu, wd, tm)
    y = _gather_rows(ys, pos)
    return _moe_residual(y, x, mod, final_g, final, B, L, ctx_rows)


def kernel(x_prompt, x_sample, cache_k, cache_v, state_rglru, c, c_ctx, w_ada, b_ada, norm1_g, norm2_g, w_in, w_out, hy_short_w, hy_short_b, hy_w1, hy_b1, hy_w2, hy_b2, hy_w3, hy_freq, hy_bias, rg_conv_w, rg_conv_b, rg_wa, rg_ba, rg_wx, rg_bx, rg_lambda, da_lambda, da_subln, w_router, b_router, moe_wg, moe_wu, moe_wd, final_g):
    Bp, Lp, D = x_prompt.shape
    Bs, Ls, _ = x_sample.shape
    assert Bs <= CTX_ROW
    cond = jnp.zeros((N_COND, D), F32).at[:Bs].set(c).at[CTX_ROW].set(c_ctx)
    mods = _ada_table(cond, w_ada, b_ada)

    dft = {L: tuple(jnp.asarray(m).astype(BF16) for m in _dft_mats(L)) for L in (Lp, Ls)}
    streams = [
        dict(B=Bp, L=Lp, ctx=True, x=x_prompt.reshape(Bp * Lp, D)),
        dict(B=Bs, L=Ls, ctx=False, x=x_sample.reshape(Bs * Ls, D)),
    ]
    ks, vs, ss = [], [], []
    for l in range(DEPTH):
        lam_init = 0.8 - 0.6 * math.exp(-0.3 * l)
        w_in_l = w_in[l].astype(BF16)
        w_out_l = w_out[l].astype(BF16)
        wg, wu, wd = moe_wg[l].astype(BF16), moe_wu[l].astype(BF16), moe_wd[l].astype(BF16)
        final = l == DEPTH - 1
        for st in streams:
            B, L, ctx = st["B"], st["L"], st["ctx"]
            cmat, smat = dft[L]
            p_hy, p_g, p_x, q, k, v = _norm_proj(
                st["x"], mods[l], norm1_g[l], w_in_l, B, L, rope=not ctx,
                kv_dtype=F32 if ctx else BF16, ctx_rows=ctx)
            kre, kim = _hy_spectra(L, cmat, smat, hy_w1[l], hy_b1[l], hy_w2[l], hy_b2[l], hy_w3[l], hy_freq[l])
            y_hy = _hyena(p_hy, B, L, cmat, smat, kre, kim, hy_short_w[l], hy_short_b[l], hy_bias[l])
            rg_args = (rg_conv_w[l], rg_conv_b[l], rg_wa[l], rg_ba[l], rg_wx[l], rg_bx[l], rg_lambda[l])
            if ctx:
                y_rg, s_l = _rglru(p_g, p_x, B, L, *rg_args, None)
                o = _attention(q, k, v, None, da_lambda[l], da_subln[l], lam_init, B, L)
                ks.append(k)
                vs.append(v)
                ss.append(s_l)
            else:
                y_rg = _rglru(p_g, p_x, B, L, *rg_args, state_rglru[:, l])
                o = _attention(q, k, v, (cache_k, cache_v, l), da_lambda[l], da_subln[l], lam_init, B, L)
            x_mid, h2, route = _out_proj(y_hy, y_rg, o, w_out_l, st["x"], mods[l], norm2_g[l],
                                         w_router, b_router, B, L, ctx)
            st["x"] = _moe(h2, route, wg, wu, wd, x_mid, mods[l], final_g, final, B, L, ctx)
    y_prompt = streams[0]["x"].reshape(Bp, Lp, D)
    y_sample = streams[1]["x"].reshape(Bs, Ls, D)
    return (y_prompt, y_sample, jnp.stack(ks, axis=1), jnp.stack(vs, axis=1), jnp.stack(ss, axis=1))
```

```python
import functools
import math

import numpy as np
import jax
import jax.numpy as jnp
from jax import lax
from jax.experimental import pallas as pl
from jax.experimental.pallas import tpu as pltpu
from jax.experimental.pallas import tpu_sc as plsc

F32 = jnp.float32
BF16 = jnp.bfloat16

D_MODEL = 1024
DEPTH = 2
GRID_W = 64
D_HY = 256
HY_EMB = 33
HY_BANDS = (HY_EMB - 1) // 2
HY_FFN = 64
HY_MIN_DECAY = math.log(1e-2) / 1.5
HY_MAX_DECAY = math.log(1e-2) / 0.3
D_RG = 256
N_RG_HEADS = 4
RG_C = 8.0
N_DA_HEADS = 4
DA_HEAD = 64
DA_VDIM = 2 * DA_HEAD
D_DA = N_DA_HEADS * DA_VDIM
D_MIX = D_HY + D_RG + D_DA
D_IN = 3 * D_HY + 2 * D_RG + 3 * D_DA
ROPE_PAIRS = DA_HEAD // 4
ROPE_THETA = 10000.0
N_EXPERTS = 16
N_GROUPS = 4
EXP_PER_GROUP = N_EXPERTS // N_GROUPS
D_EXPERT = 512
PAIRS_PER_GROUP = EXP_PER_GROUP * (EXP_PER_GROUP - 1) // 2
N_CLASSES = N_GROUPS * PAIRS_PER_GROUP
EPS = 1e-6
N_COND = 16
CTX_ROW = 8
LANES = 128
VMEM_LIMIT = 56 * 1024 * 1024


def _cparams(*sem):
    return pltpu.CompilerParams(dimension_semantics=sem, vmem_limit_bytes=VMEM_LIMIT)


def _split(x):
    hi = x.astype(BF16)
    lo = (x - hi.astype(F32)).astype(BF16)
    return hi, lo


def _dot(a, b):
    return jnp.dot(a, b, preferred_element_type=F32)


def _dot3(a, b):
    ah, al = _split(a)
    bh, bl = _split(b)
    return _dot(ah, bh) + _dot(al, bh) + _dot(ah, bl)


def _dot_nt(a, b):
    return lax.dot_general(a, b, (((1,), (1,)), ((), ())), preferred_element_type=F32)


def _sigmoid(x):
    return 1.0 / (1.0 + jnp.exp(-x))


def _const_spec(shape):
    n = len(shape)
    return pl.BlockSpec(shape, lambda *_: (0,) * n)


def _ada_kernel(c_ref, w_ref, b_ref, o_ref):
    c = c_ref[...]
    s = c * _sigmoid(c)
    o_ref[...] = _dot3(s, w_ref[...]) + b_ref[...]


def _ada_table(cond, w_ada, b_ada):
    D = D_MODEL
    out = pl.pallas_call(
        _ada_kernel,
        grid=(DEPTH, 6),
        in_specs=[
            pl.BlockSpec((N_COND, D), lambda l, j: (0, 0)),
            pl.BlockSpec((None, D, D), lambda l, j: (l, 0, j)),
            pl.BlockSpec((None, None, 1, D), lambda l, j: (l, j, 0, 0)),
        ],
        out_specs=pl.BlockSpec((None, None, N_COND, D), lambda l, j: (l, j, 0, 0)),
        out_shape=jax.ShapeDtypeStruct((DEPTH, 6, N_COND, D), F32),
        compiler_params=_cparams("parallel", "parallel"),
        name="ada_table",
    )(cond, w_ada, b_ada.reshape(DEPTH, 6, 1, D))
    return out.transpose(0, 2, 1, 3)


def _rope_tables(L):
    t = np.arange(L)
    j = np.arange(LANES)
    jj = j % DA_HEAD
    is_col = (jj // (DA_HEAD // 2)) == 1
    pair = jj % ROPE_PAIRS
    second = (jj % (DA_HEAD // 2)) >= ROPE_PAIRS
    inv = ROPE_THETA ** (-np.arange(ROPE_PAIRS, dtype=np.float64) / ROPE_PAIRS)
    pos = np.where(is_col[None, :], (t % GRID_W)[:, None], (t // GRID_W)[:, None]).astype(np.float64)
    ang = pos * inv[pair][None, :]
    cos = np.cos(ang).astype(np.float32)
    sin = np.sin(ang).astype(np.float32)
    sin_a = np.where(second[None, :], 0.0, -sin).astype(np.float32)
    sin_b = np.where(second[None, :], sin, 0.0).astype(np.float32)
    return cos, sin_a, sin_b


def _rope(x, cos, sin_a, sin_b):
    nxt = pltpu.roll(x, LANES - ROPE_PAIRS, axis=1)
    prv = pltpu.roll(x, ROPE_PAIRS, axis=1)
    return x * cos + nxt * sin_a + prv * sin_b


def _norm_proj_kernel(rope, kv_dtype, x_ref, mod_ref, g_ref, w_ref, *rest):
    if rope:
        cos_ref, sa_ref, sb_ref = rest[:3]
        rest = rest[3:]
    phy_ref, pg_ref, px_ref, q_ref, k_ref, v_ref = rest
    x = x_ref[...]
    ms = jnp.mean(x * x, axis=-1, keepdims=True)
    y = x * lax.rsqrt(ms + EPS) * g_ref[...]
    h = (y * (1.0 + mod_ref[1:2, :]) + mod_ref[0:1, :]).astype(BF16)
    o = 3 * D_HY
    phy_ref[...] = _dot(h, w_ref[:, 0:o]).astype(BF16)
    pg_ref[...] = _dot(h, w_ref[:, o:o + D_RG])
    px_ref[...] = _dot(h, w_ref[:, o + D_RG:o + 2 * D_RG])
    o += 2 * D_RG
    q = _dot(h, w_ref[:, o:o + D_DA]) * (DA_HEAD ** -0.5 * math.log2(math.e))
    k = _dot(h, w_ref[:, o + D_DA:o + 2 * D_DA])
    v = _dot(h, w_ref[:, o + 2 * D_DA:o + 3 * D_DA])
    if rope:
        cos, sa, sb = cos_ref[...], sa_ref[...], sb_ref[...]
    for hd in range(N_DA_HEADS):
        sl = slice(hd * DA_VDIM, (hd + 1) * DA_VDIM)
        qh, kh = q[:, sl], k[:, sl]
        if rope:
            qh = _rope(qh, cos, sa, sb)
            kh = _rope(kh, cos, sa, sb)
        q_ref[hd] = qh.astype(BF16)
        k_ref[hd] = kh.astype(kv_dtype)
        v_ref[hd] = v[:, sl].astype(kv_dtype)


def _norm_proj(x, mod, g, w_in, layer, B, L, rope, kv_dtype, ctx_rows, tm=512):
    T = B * L
    tm = min(tm, L)
    nl = L // tm
    row = (lambda i: CTX_ROW) if ctx_rows else (lambda i: i // nl)
    in_specs = [
        pl.BlockSpec((tm, D_MODEL), lambda i: (i, 0)),
        pl.BlockSpec((None, 6, D_MODEL), lambda i: (row(i), 0, 0)),
        _const_spec((1, D_MODEL)),
        pl.BlockSpec((None, D_MODEL, D_IN), lambda i: (layer, 0, 0)),
    ]
    args = [x, mod, g.reshape(1, D_MODEL), w_in]
    if rope:
        tabs = _rope_tables(L)
        in_specs += [pl.BlockSpec((tm, LANES), lambda i: (i % nl, 0))] * 3
        args += [jnp.asarray(t) for t in tabs]
    head_spec = pl.BlockSpec((None, N_DA_HEADS, tm, DA_VDIM), lambda i: (i // nl, 0, i % nl, 0))
    head_shape = (B, N_DA_HEADS, L, DA_VDIM)
    return pl.pallas_call(
        functools.partial(_norm_proj_kernel, rope, kv_dtype),
        grid=(T // tm,),
        in_specs=in_specs,
        out_specs=[
            pl.BlockSpec((tm, 3 * D_HY), lambda i: (i, 0)),
            pl.BlockSpec((tm, D_RG), lambda i: (i, 0)),
            pl.BlockSpec((tm, D_RG), lambda i: (i, 0)),
            head_spec, head_spec, head_spec,
        ],
        out_shape=[
            jax.ShapeDtypeStruct((T, 3 * D_HY), BF16),
            jax.ShapeDtypeStruct((T, D_RG), F32),
            jax.ShapeDtypeStruct((T, D_RG), F32),
            jax.ShapeDtypeStruct(head_shape, BF16),
            jax.ShapeDtypeStruct(head_shape, kv_dtype),
            jax.ShapeDtypeStruct(head_shape, kv_dtype),
        ],
        compiler_params=_cparams("parallel"),
        name="norm_proj_rope" if rope else "norm_proj",
    )(*args)


def _dft_mats(L):
    n = 2 * L - 1
    fs = (np.arange(L, dtype=np.int64)[:, None] * np.arange(L, dtype=np.int64)[None, :]) % n
    ang = fs.astype(np.float64) * (2.0 * np.pi / n)
    return np.cos(ang).astype(np.float32), np.sin(ang).astype(np.float32)


def _hy_features(L):
    t = np.linspace(0.0, 1.0, L, dtype=np.float64)[:, None]
    ang = ((2.0 * math.pi / L) * np.arange(L, dtype=np.float64))[:, None]
    bands = np.linspace(1e-4, HY_BANDS - 1, HY_BANDS, dtype=np.float64)[None, :]
    ba = bands * ang
    z = np.concatenate([t, np.cos(ba), -np.sin(ba)], axis=-1).astype(np.float32)
    return np.pad(z, ((0, 0), (0, LANES - HY_EMB)))


def _hy_filter_kernel(L, z_ref, w1_ref, b1_ref, w2_ref, b2_ref, w3_ref, fr_ref, rc_ref, rs_ref):
    z = z_ref[...]
    h = jnp.sin(fr_ref[0:1, :] * (_dot3(z, w1_ref[...]) + b1_ref[...]))
    h = jnp.sin(fr_ref[1:2, :] * (_dot3(h, w2_ref[...]) + b2_ref[...]))
    h = _dot3(h, w3_ref[...])
    t = z[:, 0:1]
    step = (HY_MAX_DECAY - HY_MIN_DECAY) / (D_HY - 1)
    deltas = HY_MIN_DECAY + step * lax.broadcasted_iota(jnp.int32, (1, D_HY), 1).astype(F32)
    window = jnp.exp(-t * jnp.abs(deltas))
    not_first = lax.broadcasted_iota(jnp.int32, (L, 1), 0) > 0
    for o in range(2):
        hf = h[:, (2 * o) * D_HY:(2 * o + 1) * D_HY] * window
        hb = jnp.where(not_first, h[:, (2 * o + 1) * D_HY:(2 * o + 2) * D_HY] * window, 0.0)
        rc_ref[:, o * D_HY:(o + 1) * D_HY] = hf + hb
        rs_ref[:, o * D_HY:(o + 1) * D_HY] = hb - hf


def _hy_spectrum_kernel(c_ref, s_ref, rc_ref, rs_ref, w_ref, kre_ref, kim_ref):
    rch, rcl = _split(rc_ref[...])
    rsh, rsl = _split(rs_ref[...])
    c, s, w = c_ref[...], s_ref[...], w_ref[...]
    kre_ref[...] = (_dot(c, rch) + _dot(c, rcl)) * w
    kim_ref[...] = (_dot(s, rsh) + _dot(s, rsl)) * w


def _hy_spectra(L, cmat, smat, w1, b1, w2, b2, w3, freq):
    z = jnp.asarray(_hy_features(L))
    w1p = jnp.pad(w1, ((0, LANES - HY_EMB), (0, 0)))
    nw = 2 * D_HY
    rc, rs = pl.pallas_call(
        functools.partial(_hy_filter_kernel, L),
        out_shape=[jax.ShapeDtypeStruct((L, nw), F32)] * 2,
        compiler_params=pltpu.CompilerParams(vmem_limit_bytes=VMEM_LIMIT),
        name="hy_filter",
    )(z, w1p, b1.reshape(1, HY_FFN), w2, b2.reshape(1, HY_FFN), w3, freq)
    n = 2 * L - 1
    wsc = np.full((L, 1), 2.0 / n, np.float32)
    wsc[0, 0] = 1.0 / n
    tr = min(L, 256)
    return pl.pallas_call(
        _hy_spectrum_kernel,
        grid=(L // tr,),
        in_specs=[
            pl.BlockSpec((tr, L), lambda i: (i, 0)),
            pl.BlockSpec((tr, L), lambda i: (i, 0)),
            _const_spec((L, nw)),
            _const_spec((L, nw)),
            pl.BlockSpec((tr, 1), lambda i: (i, 0)),
        ],
        out_specs=[pl.BlockSpec((tr, nw), lambda i: (i, 0))] * 2,
        out_shape=[jax.ShapeDtypeStruct((L, nw), F32)] * 2,
        compiler_params=_cparams("parallel"),
        name="hy_spectrum",
    )(cmat, smat, rc, rs, jnp.asarray(wsc))


def _hyena_kernel(L, tr, p_ref, sw_ref, sb_ref, bias_ref, c_ref, s_ref, kre_ref, kim_ref, o_ref,
                  pad_ref, u_ref, sig_ref, sig16_ref, zre_ref, zim_ref):
    C3 = 3 * D_HY
    zeros = jnp.zeros((8, C3), F32)
    pad_ref[0:8, :] = zeros
    pad_ref[8 + L:16 + L, :] = zeros
    chunks = [slice(r0, r0 + tr) for r0 in range(0, L, tr)]
    for c in chunks:
        pad_ref[8 + c.start:8 + c.stop, :] = p_ref[c, :].astype(F32)
    for c in chunks:
        u = sb_ref[...]
        for j in range(3):
            u = u + pad_ref[7 + j + c.start:7 + j + c.stop, :] * sw_ref[j:j + 1, :]
        u_ref[c, :] = u[:, D_HY:C3]
        sig_ref[c, :] = u[:, 0:D_HY]
        sig16_ref[c, :] = u[:, 0:D_HY].astype(BF16)

    for o in range(2):
        ko = slice(o * D_HY, (o + 1) * D_HY)
        for c in chunks:
            ure = _dot(c_ref[c, :], sig16_ref[...])
            us = _dot(s_ref[c, :], sig16_ref[...])
            kre, kim = kre_ref[c, ko], kim_ref[c, ko]
            zre_ref[c, :] = (ure * kre + us * kim).astype(BF16)
            zim_ref[c, :] = (ure * kim - us * kre).astype(BF16)
        gate = slice(o * D_HY, (o + 1) * D_HY)
        for c in chunks:
            y = _dot(c_ref[c, :], zre_ref[...]) - _dot(s_ref[c, :], zim_ref[...])
            z = u_ref[c, gate] * (y + sig_ref[c, :] * bias_ref[o:o + 1, :])
            if o == 0:
                sig_ref[c, :] = z
                sig16_ref[c, :] = z.astype(BF16)
            else:
                o_ref[c, :] = z.astype(o_ref.dtype)


def _hyena(p_hy, B, L, cmat, smat, kre, kim, short_w, short_b, bias, tr=512):
    C3 = 3 * D_HY
    tr = min(tr, L)
    once = pl.Buffered(1)
    return pl.pallas_call(
        functools.partial(_hyena_kernel, L, tr),
        grid=(B,),
        in_specs=[
            pl.BlockSpec((L, C3), lambda b: (b, 0)),
            _const_spec((3, C3)),
            _const_spec((1, C3)),
            _const_spec((2, D_HY)),
            pl.BlockSpec((L, L), lambda b: (0, 0), pipeline_mode=once),
            pl.BlockSpec((L, L), lambda b: (0, 0), pipeline_mode=once),
            pl.BlockSpec((L, 2 * D_HY), lambda b: (0, 0), pipeline_mode=once),
            pl.BlockSpec((L, 2 * D_HY), lambda b: (0, 0), pipeline_mode=once),
        ],
        out_specs=pl.BlockSpec((L, D_HY), lambda b: (b, 0)),
        out_shape=jax.ShapeDtypeStruct((B * L, D_HY), BF16),
        scratch_shapes=[
            pltpu.VMEM((L + 16, C3), F32),
            pltpu.VMEM((L, 2 * D_HY), F32),
            pltpu.VMEM((L, D_HY), F32),
            pltpu.VMEM((L, D_HY), BF16),
            pltpu.VMEM((L, D_HY), BF16),
            pltpu.VMEM((L, D_HY), BF16),
        ],
        compiler_params=_cparams("parallel"),
        name="hyena",
    )(p_hy, short_w, short_b.reshape(1, C3), bias, cmat, smat, kre, kim)


def _softplus(z):
    return jnp.maximum(z, 0.0) + jnp.log1p(jnp.exp(-jnp.abs(z)))


def _expm1(x):
    u = jnp.exp(x)
    near = (u - 1.0) * x / jnp.where(u == 1.0, 1.0, jnp.log(u))
    return jnp.where(x < -0.5, u - 1.0, jnp.where(u == 1.0, x, near))


def _gelu_tanh(x):
    return 0.5 * x * (1.0 + jnp.tanh(math.sqrt(2.0 / math.pi) * (x + 0.044715 * x * x * x)))


def _rglru_kernel(L, has_state, pg_ref, px_ref, cw_ref, cb_ref, wh_ref, wl_ref, gb_ref, lam_ref, *rest):
    if has_state:
        st_ref, y_ref, pad_ref, a_ref, b_ref, h_ref = rest
    else:
        y_ref, st_out_ref, pad_ref, a_ref, b_ref, h_ref = rest
    C = D_RG
    zeros = jnp.zeros((8, C), F32)
    pad_ref[0:8, :] = zeros
    pad_ref[8 + L:16 + L, :] = zeros
    pad_ref[8:8 + L, :] = px_ref[...]
    sp = _softplus(-lam_ref[...])
    tr = min(L, 256)
    for r0 in range(0, L, tr):
        xr = cb_ref[...]
        for j in range(4):
            xr = xr + pad_ref[6 + j + r0:6 + j + r0 + tr, :] * cw_ref[j:j + 1, :]
        xh, xl = _split(xr)
        for d in range(2):
            g = []
            for m in range(2):
                cols = slice((2 * d + m) * C, (2 * d + m + 1) * C)
                wh = wh_ref[:, cols]
                g.append(_sigmoid(_dot(xh, wh) + _dot(xl, wh) + _dot(xh, wl_ref[:, cols]) + gb_ref[:, cols]))
            log_a = -RG_C * g[0] * sp[d:d + 1, :]
            a_ref[d, r0:r0 + tr, :] = jnp.exp(log_a)
            b_ref[d, r0:r0 + tr, :] = jnp.sqrt(-_expm1(2.0 * log_a)) * (g[1] * xr)

    if has_state:
        h0f, h0b = st_ref[0:1, :], st_ref[1:2, :]
    else:
        h0f = h0b = jnp.zeros((1, C), F32)

    def step(t, carry):
        hf, hb = carry
        tb = L - 1 - t
        hf = a_ref[0, pl.ds(t, 1), :] * hf + b_ref[0, pl.ds(t, 1), :]
        hb = a_ref[1, pl.ds(tb, 1), :] * hb + b_ref[1, pl.ds(tb, 1), :]
        h_ref[0, pl.ds(t, 1), :] = hf
        h_ref[1, pl.ds(tb, 1), :] = hb
        return hf, hb

    lax.fori_loop(0, L, step, (h0f, h0b))
    y_ref[...] = ((h_ref[0] + h_ref[1]) * _gelu_tanh(pg_ref[...])).astype(y_ref.dtype)
    if not has_state:
        st_out_ref[0:1, :] = h_ref[0, L - 1:L, :]
        st_out_ref[1:2, :] = h_ref[1, 0:1, :]


def _block_diag(w):
    H, d, _ = w.shape
    eye = jnp.eye(H, dtype=w.dtype)
    return (eye[:, None, :, None] * w[:, :, None, :]).reshape(H * d, H * d)


def _rglru(p_g, p_x, B, L, conv_w, conv_b, wa, ba, wx, bx, lam, state):
    C = D_RG
    wcat = jnp.concatenate([_block_diag(wa[0]), _block_diag(wx[0]), _block_diag(wa[1]), _block_diag(wx[1])], axis=1)
    wh = wcat.astype(BF16)
    wl = (wcat - wh.astype(F32)).astype(BF16)
    gb = jnp.concatenate([ba[0], bx[0], ba[1], bx[1]]).reshape(1, 4 * C)
    has_state = state is not None
    in_specs = [
        pl.BlockSpec((L, C), lambda b: (b, 0)),
        pl.BlockSpec((L, C), lambda b: (b, 0)),
        _const_spec((4, C)),
        _const_spec((1, C)),
        _const_spec((C, 4 * C)),
        _const_spec((C, 4 * C)),
        _const_spec((1, 4 * C)),
        _const_spec((2, C)),
    ]
    args = [p_g, p_x, conv_w, conv_b.reshape(1, C), wh, wl, gb, lam]
    y_spec = pl.BlockSpec((L, C), lambda b: (b, 0))
    y_shape = jax.ShapeDtypeStruct((B * L, C), BF16)
    if has_state:
        in_specs.append(pl.BlockSpec((None, 2, C), lambda b: (b, 0, 0)))
        args.append(state)
        out_specs, out_shape = y_spec, y_shape
    else:
        out_specs = [y_spec, pl.BlockSpec((None, 2, C), lambda b: (b, 0, 0))]
        out_shape = [y_shape, jax.ShapeDtypeStruct((B, 2, C), F32)]
    return pl.pallas_call(
        functools.partial(_rglru_kernel, L, has_state),
        grid=(B,),
        in_specs=in_specs,
        out_specs=out_specs,
        out_shape=out_shape,
        scratch_shapes=[
            pltpu.VMEM((L + 16, C), F32),
            pltpu.VMEM((2, L, C), F32),
            pltpu.VMEM((2, L, C), F32),
            pltpu.VMEM((2, L, C), F32),
        ],
        compiler_params=_cparams("parallel"),
        name="rglru_state" if has_state else "rglru",
    )(*args)


def _attn_kernel(L, P, tq, lam_init, q_ref, k_ref, v_ref, *rest):
    if P:
        ck_ref, cv_ref, dal_ref, sub_ref, o_ref, kk_ref, vv_ref = rest
    else:
        dal_ref, sub_ref, o_ref, kk_ref, vv_ref = rest
    lv = dal_ref[...]
    s01 = jnp.sum(lv[0:1, :] * lv[1:2, :], axis=-1, keepdims=True)
    s23 = jnp.sum(lv[2:3, :] * lv[3:4, :], axis=-1, keepdims=True)
    lam = jnp.exp(s01) - jnp.exp(s23) + lam_init
    first_half = lax.broadcasted_iota(jnp.int32, (1, DA_VDIM), 1) < DA_HEAD
    sub = sub_ref[...] * (1.0 - lam_init)
    for hd in range(N_DA_HEADS):
        if P:
            kk_ref[0:P, :] = ck_ref[hd].astype(BF16)
            vv_ref[0:P, :] = cv_ref[hd].astype(BF16)
        kk_ref[P:P + L, :] = k_ref[hd].astype(BF16)
        vv_ref[P:P + L, :] = v_ref[hd].astype(BF16)

        def qblock(i, carry):
            r0 = pl.multiple_of(i * tq, tq)
            q = q_ref[hd, pl.ds(r0, tq), :]
            zero = jnp.zeros_like(q)
            qs = jnp.concatenate([jnp.where(first_half, q, zero), jnp.where(first_half, zero, q)], axis=0)
            s = _dot_nt(qs, kk_ref[...])
            p = jnp.exp2(s - jnp.max(s, axis=-1, keepdims=True))
            rinv = 1.0 / jnp.sum(p, axis=-1, keepdims=True)
            acc = _dot(p.astype(BF16), vv_ref[...])
            o = acc[0:tq] * rinv[0:tq] - acc[tq:2 * tq] * (lam * rinv[tq:2 * tq])
            o = o * lax.rsqrt(jnp.mean(o * o, axis=-1, keepdims=True) + EPS) * sub
            o_ref[pl.ds(r0, tq), hd * DA_VDIM:(hd + 1) * DA_VDIM] = o.astype(o_ref.dtype)
            return carry

        lax.fori_loop(0, L // tq, qblock, 0)


def _attention(q, k, v, cache, dal, subln, lam_init, B, L, tq=128):
    H, dv = N_DA_HEADS, DA_VDIM
    hspec = pl.BlockSpec((None, H, L, dv), lambda b: (b, 0, 0, 0))
    in_specs = [hspec, hspec, hspec]
    args = [q, k, v]
    P = 0
    if cache is not None:
        ck, cv, layer = cache
        P = ck.shape[3]
        cspec = pl.BlockSpec((None, None, H, P, dv), lambda b: (b, layer, 0, 0, 0))
        in_specs += [cspec, cspec]
        args += [ck, cv]
    assert L % tq == 0
    in_specs += [_const_spec((4, DA_HEAD)), _const_spec((1, dv))]
    args += [dal, subln.reshape(1, dv)]
    return pl.pallas_call(
        functools.partial(_attn_kernel, L, P, tq, lam_init),
        grid=(B,),
        in_specs=in_specs,
        out_specs=pl.BlockSpec((L, H * dv), lambda b: (b, 0)),
        out_shape=jax.ShapeDtypeStruct((B * L, H * dv), BF16),
        scratch_shapes=[pltpu.VMEM((P + L, dv), BF16), pltpu.VMEM((P + L, dv), BF16)],
        compiler_params=_cparams("parallel"),
        name="diff_attn_cache" if P else "diff_attn",
    )(*args)


def _route(logits):
    m = logits[0]
    for e in range(1, N_EXPERTS):
        m = jnp.maximum(m, logits[e])
    ex = [jnp.exp(l - m) for l in logits]
    tot = ex[0]
    for e in range(1, N_EXPERTS):
        tot = tot + ex[e]
    inv = 1.0 / tot
    p = [e_ * inv for e_ in ex]
    G = EXP_PER_GROUP
    best, gsel = None, None
    for g in range(N_GROUPS):
        a = p[g * G:(g + 1) * G]
        sc = None
        for i in range(G):
            for j in range(i + 1, G):
                pair = a[i] + a[j]
                sc = pair if sc is None else jnp.maximum(sc, pair)
        if g == 0:
            best, gsel = sc, jnp.zeros_like(sc, dtype=jnp.int32)
        else:
            upd = sc > best
            best = jnp.where(upd, sc, best)
            gsel = jnp.where(upd, g, gsel)
    vals = []
    for j in range(G):
        vj = p[j]
        for g in range(1, N_GROUPS):
            vj = jnp.where(gsel == g, p[g * G + j], vj)
        vals.append(vj)
    p1, i1 = vals[0], jnp.zeros_like(gsel)
    for j in range(1, G):
        upd = vals[j] > p1
        p1 = jnp.where(upd, vals[j], p1)
        i1 = jnp.where(upd, j, i1)
    p2, i2 = None, None
    for j in range(G):
        cand = jnp.where(i1 == j, -1.0, vals[j])
        if p2 is None:
            p2, i2 = cand, jnp.zeros_like(gsel)
        else:
            upd = cand > p2
            p2 = jnp.where(upd, cand, p2)
            i2 = jnp.where(upd, j, i2)
    den = 1.0 / (p1 + p2)
    w1, w2 = p1 * den, p2 * den
    swap = i2 < i1
    a, b = jnp.where(swap, i2, i1), jnp.where(swap, i1, i2)
    w_lo, w_hi = jnp.where(swap, w2, w1), jnp.where(swap, w1, w2)
    pair = jnp.where(a == 0, b - 1, jnp.where(a == 1, b + 1, 5))
    cls = gsel * PAIRS_PER_GROUP + pair
    return cls.astype(F32), w_lo, w_hi


def _pack_pairs(x):
    n = x.shape[1] // 2
    b = pltpu.bitcast(x, jnp.uint32)
    w = (b[:, :n] >> 16) | (b[:, n:] & jnp.uint32(0xFFFF0000))
    return pltpu.bitcast(w, jnp.int32)


def _unpack_pairs(w):
    u = pltpu.bitcast(w, jnp.uint32)
    lo = pltpu.bitcast(u << 16, F32)
    hi = pltpu.bitcast(u & jnp.uint32(0xFFFF0000), F32)
    return jnp.concatenate([lo, hi], axis=1)


def _out_proj_kernel(yh_ref, yr_ref, o_ref, w_ref, x_ref, mod_ref, g_ref, wrh_ref, wrl_ref, br_ref,
                     xo_ref, h_ref, route_ref):
    y = (_dot(yh_ref[...], w_ref[0:D_HY, :]) + _dot(yr_ref[...], w_ref[D_HY:D_HY + D_RG, :])
         + _dot(o_ref[...], w_ref[D_HY + D_RG:D_MIX, :]))
    x = x_ref[...] + mod_ref[2:3, :] * y
    xo_ref[...] = x
    ms = jnp.mean(x * x, axis=-1, keepdims=True)
    h = (x * lax.rsqrt(ms + EPS) * g_ref[...]) * (1.0 + mod_ref[4:5, :]) + mod_ref[3:4, :]
    hh, hl = _split(h)
    h_ref[...] = _pack_pairs(hh.astype(F32))
    lg = _dot_nt(wrh_ref[...], hh) + _dot_nt(wrh_ref[...], hl) + _dot_nt(wrl_ref[...], hh) + br_ref[...]
    info = _route([lg[e:e + 1, :] for e in range(N_EXPERTS)])
    rt = jnp.concatenate(list(info) + [jnp.zeros((LANES - len(info), lg.shape[1]), F32)], axis=0)
    route_ref[...] = rt.T


def _out_proj(y_hy, y_rg, o, w_out, layer, x, mod, g2, w_router, b_router, B, L, ctx_rows, tm=512):
    T = B * L
    tm = min(tm, L)
    nl = L // tm
    row = (lambda i: CTX_ROW) if ctx_rows else (lambda i: i // nl)
    wrt = w_router.T
    wrh = wrt.astype(BF16)
    wrl = (wrt - wrh.astype(F32)).astype(BF16)
    rows = lambda w: pl.BlockSpec((tm, w), lambda i: (i, 0))
    return pl.pallas_call(
        _out_proj_kernel,
        grid=(T // tm,),
        in_specs=[
            rows(D_HY), rows(D_RG), rows(D_DA),
            pl.BlockSpec((None, D_MIX, D_MODEL), lambda i: (layer, 0, 0)),
            rows(D_MODEL),
            pl.BlockSpec((None, 6, D_MODEL), lambda i: (row(i), 0, 0)),
            _const_spec((1, D_MODEL)),
            _const_spec((N_EXPERTS, D_MODEL)),
            _const_spec((N_EXPERTS, D_MODEL)),
            _const_spec((N_EXPERTS, 1)),
        ],
        out_specs=[rows(D_MODEL), rows(D_MODEL // 2), rows(LANES)],
        out_shape=[
            jax.ShapeDtypeStruct((T, D_MODEL), F32),
            jax.ShapeDtypeStruct((T, D_MODEL // 2), jnp.int32),
            jax.ShapeDtypeStruct((T, LANES), F32),
        ],
        compiler_params=_cparams("parallel"),
        name="out_proj_route",
    )(y_hy, y_rg, o, w_out, x, mod, g2.reshape(1, D_MODEL), wrh, wrl, b_router.reshape(N_EXPERTS, 1))


def _gather_rows(table, idx, rows_per_step=64, n_buf=2):
    info = plsc.get_sparse_core_info()
    n_workers = info.num_cores * info.num_subcores
    n, width = idx.shape[0], table.shape[1]
    per_worker = n // n_workers
    n_steps = per_worker // rows_per_step
    assert per_worker * n_workers == n and n_steps * rows_per_step == per_worker and n_steps >= n_buf
    mesh = plsc.VectorSubcoreMesh(core_axis_name="c", subcore_axis_name="s")

    @functools.partial(
        pl.kernel, mesh=mesh,
        out_type=jax.ShapeDtypeStruct((n, width), table.dtype),
        scratch_types=[
            pltpu.VMEM((per_worker,), jnp.int32),
            pltpu.VMEM((n_buf, rows_per_step, width), table.dtype),
            pltpu.SemaphoreType.DMA((n_buf,)),
            pltpu.SemaphoreType.DMA((n_buf,)),
        ],
    )
    def gather(table_hbm, idx_hbm, out_hbm, idx_v, rows_v, sem_in, sem_out):
        worker = lax.axis_index("s") * info.num_cores + lax.axis_index("c")
        base = pl.multiple_of(worker * per_worker, per_worker)
        pltpu.sync_copy(idx_hbm.at[pl.ds(base, per_worker)], idx_v)

        def read(b, step):
            rows = idx_v.at[pl.ds(step * rows_per_step, rows_per_step)]
            return pltpu.make_async_copy(table_hbm.at[rows], rows_v.at[b], sem_in.at[b])

        def write(b, step):
            off = pl.multiple_of(base + step * rows_per_step, rows_per_step)
            return pltpu.make_async_copy(rows_v.at[b], out_hbm.at[pl.ds(off, rows_per_step)], sem_out.at[b])

        for step in range(n_steps + 1):
            if step < n_steps:
                if step >= n_buf:
                    write(step % n_buf, step - n_buf).wait()
                read(step % n_buf, step).start()
            if step >= 1:
                read((step - 1) % n_buf, step - 1).wait()
                write((step - 1) % n_buf, step - 1).start()
        for step in range(n_steps - n_buf, n_steps):
            write(step % n_buf, step).wait()

    return gather(table, idx)


def _dispatch_plan(route, tm):
    T = route.shape[0]
    n_slots = T + N_CLASSES * tm
    cls = route[:, 0].astype(jnp.int32)
    onehot = (cls[:, None] == jnp.arange(N_CLASSES, dtype=jnp.int32)[None, :]).astype(jnp.int32)
    csum = jnp.cumsum(onehot, axis=0)
    rank = jnp.sum(onehot * csum, axis=1) - 1
    counts = csum[-1]
    padded = ((counts + tm - 1) // tm) * tm
    ends = jnp.cumsum(padded)
    pos = jnp.sum(onehot * (ends - padded)[None, :], axis=1) + rank
    inv = jnp.zeros((n_slots,), jnp.int32).at[pos].set(jnp.arange(T, dtype=jnp.int32), unique_indices=True)
    tile_start = jnp.arange(n_slots // tm, dtype=jnp.int32) * tm
    tile_cls = jnp.minimum(jnp.searchsorted(ends, tile_start, side="right"), N_CLASSES - 1).astype(jnp.int32)
    valid = (tile_start < ends[-1]).astype(jnp.int32)
    pairs = np.array([(a, b) for a in range(EXP_PER_GROUP) for b in range(a + 1, EXP_PER_GROUP)], np.int32)
    group, pair = tile_cls // PAIRS_PER_GROUP, tile_cls % PAIRS_PER_GROUP
    lo = group * EXP_PER_GROUP + jnp.asarray(pairs[:, 0])[pair]
    hi = group * EXP_PER_GROUP + jnp.asarray(pairs[:, 1])[pair]
    return pos, inv, lo, hi, valid


def _moe_sorted_kernel(lo_ref, hi_ref, valid_ref, xs_ref, ws_ref, wg_lo, wu_lo, wd_lo, wg_hi, wu_hi, wd_hi, o_ref):
    i = pl.program_id(0)

    @pl.when(valid_ref[i] == 1)
    def _():
        x = _unpack_pairs(xs_ref[...]).astype(BF16)
        y = None
        for wg, wu, wd, col in ((wg_lo, wu_lo, wd_lo, 1), (wg_hi, wu_hi, wd_hi, 2)):
            a = _dot(x, wg[...])
            he = (a * _sigmoid(a)) * _dot(x, wu[...]) * ws_ref[:, col:col + 1]
            part = _dot(he.astype(BF16), wd[...])
            y = part if y is None else y + part
        o_ref[...] = _pack_pairs(y.astype(BF16).astype(F32))

    @pl.when(valid_ref[i] == 0)
    def _():
        o_ref[...] = jnp.zeros_like(o_ref)


def _moe_sorted(xs, ws, lo, hi, valid, wg, wu, wd, tm):
    n_slots = xs.shape[0]
    half = D_MODEL // 2
    up = lambda sel: pl.BlockSpec((None, D_MODEL, D_EXPERT), lambda i, lo, hi, v: ((lo, hi)[sel][i], 0, 0))
    down = lambda sel: pl.BlockSpec((None, D_EXPERT, D_MODEL), lambda i, lo, hi, v: ((lo, hi)[sel][i], 0, 0))
    return pl.pallas_call(
        _moe_sorted_kernel,
        grid_spec=pltpu.PrefetchScalarGridSpec(
            num_scalar_prefetch=3,
            grid=(n_slots // tm,),
            in_specs=[
                pl.BlockSpec((tm, half), lambda i, lo, hi, v: (i, 0)),
                pl.BlockSpec((tm, LANES), lambda i, lo, hi, v: (i, 0)),
                up(0), up(0), down(0), up(1), up(1), down(1),
            ],
            out_specs=pl.BlockSpec((tm, half), lambda i, lo, hi, v: (i, 0)),
        ),
        out_shape=jax.ShapeDtypeStruct((n_slots, half), jnp.int32),
        compiler_params=_cparams("arbitrary"),
        name="moe_sorted",
    )(lo, hi, valid, xs, ws, wg, wu, wd, wg, wu, wd)


def _moe_residual_kernel(final, y_ref, x_ref, mod_ref, fg_ref, o_ref):
    x = x_ref[...] + mod_ref[5:6, :] * _unpack_pairs(y_ref[...])
    if final:
        x = x * lax.rsqrt(jnp.mean(x * x, axis=-1, keepdims=True) + EPS) * fg_ref[...]
    o_ref[...] = x


def _moe_residual(y, x, mod, final_g, final, B, L, ctx_rows, tm=512):
    T = B * L
    tm = min(tm, L)
    nl = L // tm
    row = (lambda i: CTX_ROW) if ctx_rows else (lambda i: i // nl)
    return pl.pallas_call(
        functools.partial(_moe_residual_kernel, final),
        grid=(T // tm,),
        in_specs=[
            pl.BlockSpec((tm, D_MODEL // 2), lambda i: (i, 0)),
            pl.BlockSpec((tm, D_MODEL), lambda i: (i, 0)),
            pl.BlockSpec((None, 6, D_MODEL), lambda i: (row(i), 0, 0)),
            _const_spec((1, D_MODEL)),
        ],
        out_specs=pl.BlockSpec((tm, D_MODEL), lambda i: (i, 0)),
        out_shape=jax.ShapeDtypeStruct((T, D_MODEL), F32),
        compiler_params=_cparams("parallel"),
        name="moe_residual_final" if final else "moe_residual",
    )(y, x, mod, final_g.reshape(1, D_MODEL))


def _moe(h, route, wg, wu, wd, layer, x, mod, final_g, final, B, L, ctx_rows, tm=256):
    pos, inv, lo, hi, valid = _dispatch_plan(route, tm)
    xs = _gather_rows(h, inv)
    ws = _gather_rows(route, inv)
    ys = _moe_sorted(xs, ws, lo + layer * N_EXPERTS, hi + layer * N_EXPERTS, valid, wg, wu, wd, tm)
    y = _gather_rows(ys, pos)
    return _moe_residual(y, x, mod, final_g, final, B, L, ctx_rows)


def kernel(x_prompt, x_sample, cache_k, cache_v, state_rglru, c, c_ctx, w_ada, b_ada, norm1_g, norm2_g, w_in, w_out, hy_short_w, hy_short_b, hy_w1, hy_b1, hy_w2, hy_b2, hy_w3, hy_freq, hy_bias, rg_conv_w, rg_conv_b, rg_wa, rg_ba, rg_wx, rg_bx, rg_lambda, da_lambda, da_subln, w_router, b_router, moe_wg, moe_wu, moe_wd, final_g):
    Bp, Lp, D = x_prompt.shape
    Bs, Ls, _ = x_sample.shape
    assert Bs <= CTX_ROW
    cond = jnp.zeros((N_COND, D), F32).at[:Bs].set(c).at[CTX_ROW].set(c_ctx)
    mods = _ada_table(cond, w_ada, b_ada)

    dft = {L: tuple(jnp.asarray(m).astype(BF16) for m in _dft_mats(L)) for L in (Lp, Ls)}
    streams = [
        dict(B=Bp, L=Lp, ctx=True, x=x_prompt.reshape(Bp * Lp, D)),
        dict(B=Bs, L=Ls, ctx=False, x=x_sample.reshape(Bs * Ls, D)),
    ]
    w_in_b, w_out_b = w_in.astype(BF16), w_out.astype(BF16)
    wg, wu, wd = (w.astype(BF16).reshape((DEPTH * N_EXPERTS,) + w.shape[2:]) for w in (moe_wg, moe_wu, moe_wd))
    ks, vs, ss = [], [], []
    for l in range(DEPTH):
        lam_init = 0.8 - 0.6 * math.exp(-0.3 * l)
        final = l == DEPTH - 1
        for st in streams:
            B, L, ctx = st["B"], st["L"], st["ctx"]
            cmat, smat = dft[L]
            p_hy, p_g, p_x, q, k, v = _norm_proj(
                st["x"], mods[l], norm1_g[l], w_in_b, l, B, L, rope=not ctx,
                kv_dtype=F32 if ctx else BF16, ctx_rows=ctx)
            kre, kim = _hy_spectra(L, cmat, smat, hy_w1[l], hy_b1[l], hy_w2[l], hy_b2[l], hy_w3[l], hy_freq[l])
            y_hy = _hyena(p_hy, B, L, cmat, smat, kre, kim, hy_short_w[l], hy_short_b[l], hy_bias[l])
            rg_args = (rg_conv_w[l], rg_conv_b[l], rg_wa[l], rg_ba[l], rg_wx[l], rg_bx[l], rg_lambda[l])
            if ctx:
                y_rg, s_l = _rglru(p_g, p_x, B, L, *rg_args, None)
                o = _attention(q, k, v, None, da_lambda[l], da_subln[l], lam_init, B, L)
                ks.append(k)
                vs.append(v)
                ss.append(s_l)
            else:
                y_rg = _rglru(p_g, p_x, B, L, *rg_args, state_rglru[:, l])
                o = _attention(q, k, v, (cache_k, cache_v, l), da_lambda[l], da_subln[l], lam_init, B, L)
            x_mid, h2, route = _out_proj(y_hy, y_rg, o, w_out_b, l, st["x"], mods[l], norm2_g[l],
                                         w_router, b_router, B, L, ctx)
            st["x"] = _moe(h2, route, wg, wu, wd, l, x_mid, mods[l], final_g, final, B, L, ctx)
    y_prompt = streams[0]["x"].reshape(Bp, Lp, D)
    y_sample = streams[1]["x"].reshape(Bs, Ls, D)
    return (y_prompt, y_sample, jnp.stack(ks, axis=1), jnp.stack(vs, axis=1), jnp.stack(ss, axis=1))
```

```python
import functools
import math

import numpy as np
import jax
import jax.numpy as jnp
from jax import lax
from jax.experimental import pallas as pl
from jax.experimental.pallas import tpu as pltpu
from jax.experimental.pallas import tpu_sc as plsc

F32 = jnp.float32
BF16 = jnp.bfloat16

D_MODEL = 1024
DEPTH = 2
GRID_W = 64
D_HY = 256
HY_EMB = 33
HY_BANDS = (HY_EMB - 1) // 2
HY_FFN = 64
HY_MIN_DECAY = math.log(1e-2) / 1.5
HY_MAX_DECAY = math.log(1e-2) / 0.3
D_RG = 256
N_RG_HEADS = 4
RG_C = 8.0
N_DA_HEADS = 4
DA_HEAD = 64
DA_VDIM = 2 * DA_HEAD
D_DA = N_DA_HEADS * DA_VDIM
D_MIX = D_HY + D_RG + D_DA
D_IN = 3 * D_HY + 2 * D_RG + 3 * D_DA
ROPE_PAIRS = DA_HEAD // 4
ROPE_THETA = 10000.0
N_EXPERTS = 16
N_GROUPS = 4
EXP_PER_GROUP = N_EXPERTS // N_GROUPS
D_EXPERT = 512
PAIRS_PER_GROUP = EXP_PER_GROUP * (EXP_PER_GROUP - 1) // 2
N_CLASSES = N_GROUPS * PAIRS_PER_GROUP
EPS = 1e-6
N_COND = 16
CTX_ROW = 8
LANES = 128
VMEM_LIMIT = 56 * 1024 * 1024


def _cparams(*sem):
    return pltpu.CompilerParams(dimension_semantics=sem, vmem_limit_bytes=VMEM_LIMIT)


def _split(x):
    hi = x.astype(BF16)
    lo = (x - hi.astype(F32)).astype(BF16)
    return hi, lo


def _dot(a, b):
    return jnp.dot(a, b, preferred_element_type=F32)


def _dot3(a, b):
    ah, al = _split(a)
    bh, bl = _split(b)
    return _dot(ah, bh) + _dot(al, bh) + _dot(ah, bl)


def _dot_nt(a, b):
    return lax.dot_general(a, b, (((1,), (1,)), ((), ())), preferred_element_type=F32)


def _sigmoid(x):
    return 1.0 / (1.0 + jnp.exp(-x))


def _const_spec(shape):
    n = len(shape)
    return pl.BlockSpec(shape, lambda *_: (0,) * n)


def _ada_kernel(c_ref, w_ref, b_ref, o_ref):
    c = c_ref[...]
    s = c * _sigmoid(c)
    o_ref[...] = _dot3(s, w_ref[...]) + b_ref[...]


def _ada_table(cond, w_ada, b_ada):
    D = D_MODEL
    out = pl.pallas_call(
        _ada_kernel,
        grid=(DEPTH, 6),
        in_specs=[
            pl.BlockSpec((N_COND, D), lambda l, j: (0, 0)),
            pl.BlockSpec((None, D, D), lambda l, j: (l, 0, j)),
            pl.BlockSpec((None, None, 1, D), lambda l, j: (l, j, 0, 0)),
        ],
        out_specs=pl.BlockSpec((None, None, N_COND, D), lambda l, j: (l, j, 0, 0)),
        out_shape=jax.ShapeDtypeStruct((DEPTH, 6, N_COND, D), F32),
        compiler_params=_cparams("parallel", "parallel"),
        name="ada_table",
    )(cond, w_ada, b_ada.reshape(DEPTH, 6, 1, D))
    return out.transpose(0, 2, 1, 3)


def _rope_tables(L):
    t = np.arange(L)
    j = np.arange(LANES)
    jj = j % DA_HEAD
    is_col = (jj // (DA_HEAD // 2)) == 1
    pair = jj % ROPE_PAIRS
    second = (jj % (DA_HEAD // 2)) >= ROPE_PAIRS
    inv = ROPE_THETA ** (-np.arange(ROPE_PAIRS, dtype=np.float64) / ROPE_PAIRS)
    pos = np.where(is_col[None, :], (t % GRID_W)[:, None], (t // GRID_W)[:, None]).astype(np.float64)
    ang = pos * inv[pair][None, :]
    cos = np.cos(ang).astype(np.float32)
    sin = np.sin(ang).astype(np.float32)
    sin_a = np.where(second[None, :], 0.0, -sin).astype(np.float32)
    sin_b = np.where(second[None, :], sin, 0.0).astype(np.float32)
    return cos, sin_a, sin_b


def _rope(x, cos, sin_a, sin_b):
    nxt = pltpu.roll(x, LANES - ROPE_PAIRS, axis=1)
    prv = pltpu.roll(x, ROPE_PAIRS, axis=1)
    return x * cos + nxt * sin_a + prv * sin_b


def _norm_proj_kernel(rope, kv_dtype, x_ref, mod_ref, g_ref, w_ref, *rest):
    if rope:
        cos_ref, sa_ref, sb_ref = rest[:3]
        rest = rest[3:]
    phy_ref, pg_ref, px_ref, q_ref, k_ref, v_ref = rest
    x = x_ref[...]
    ms = jnp.mean(x * x, axis=-1, keepdims=True)
    y = x * lax.rsqrt(ms + EPS) * g_ref[...]
    h = (y * (1.0 + mod_ref[1:2, :]) + mod_ref[0:1, :]).astype(BF16)
    o = 3 * D_HY
    phy_ref[...] = _dot(h, w_ref[:, 0:o]).astype(BF16)
    pg_ref[...] = _dot(h, w_ref[:, o:o + D_RG])
    px_ref[...] = _dot(h, w_ref[:, o + D_RG:o + 2 * D_RG])
    o += 2 * D_RG
    q = _dot(h, w_ref[:, o:o + D_DA]) * (DA_HEAD ** -0.5 * math.log2(math.e))
    k = _dot(h, w_ref[:, o + D_DA:o + 2 * D_DA])
    v = _dot(h, w_ref[:, o + 2 * D_DA:o + 3 * D_DA])
    if rope:
        cos, sa, sb = cos_ref[...], sa_ref[...], sb_ref[...]
    for hd in range(N_DA_HEADS):
        sl = slice(hd * DA_VDIM, (hd + 1) * DA_VDIM)
        qh, kh = q[:, sl], k[:, sl]
        if rope:
            qh = _rope(qh, cos, sa, sb)
            kh = _rope(kh, cos, sa, sb)
        q_ref[hd] = qh.astype(BF16)
        k_ref[hd] = kh.astype(kv_dtype)
        v_ref[hd] = v[:, sl].astype(kv_dtype)


def _norm_proj(x, mod, g, w_in, layer, B, L, rope, kv_dtype, ctx_rows, tm=512):
    T = B * L
    tm = min(tm, L)
    nl = L // tm
    row = (lambda i: CTX_ROW) if ctx_rows else (lambda i: i // nl)
    in_specs = [
        pl.BlockSpec((tm, D_MODEL), lambda i: (i, 0)),
        pl.BlockSpec((None, 6, D_MODEL), lambda i: (row(i), 0, 0)),
        _const_spec((1, D_MODEL)),
        pl.BlockSpec((None, D_MODEL, D_IN), lambda i: (layer, 0, 0)),
    ]
    args = [x, mod, g.reshape(1, D_MODEL), w_in]
    if rope:
        tabs = _rope_tables(L)
        in_specs += [pl.BlockSpec((tm, LANES), lambda i: (i % nl, 0))] * 3
        args += [jnp.asarray(t) for t in tabs]
    head_spec = pl.BlockSpec((None, N_DA_HEADS, tm, DA_VDIM), lambda i: (i // nl, 0, i % nl, 0))
    head_shape = (B, N_DA_HEADS, L, DA_VDIM)
    return pl.pallas_call(
        functools.partial(_norm_proj_kernel, rope, kv_dtype),
        grid=(T // tm,),
        in_specs=in_specs,
        out_specs=[
            pl.BlockSpec((tm, 3 * D_HY), lambda i: (i, 0)),
            pl.BlockSpec((tm, D_RG), lambda i: (i, 0)),
            pl.BlockSpec((tm, D_RG), lambda i: (i, 0)),
            head_spec, head_spec, head_spec,
        ],
        out_shape=[
            jax.ShapeDtypeStruct((T, 3 * D_HY), BF16),
            jax.ShapeDtypeStruct((T, D_RG), F32),
            jax.ShapeDtypeStruct((T, D_RG), F32),
            jax.ShapeDtypeStruct(head_shape, BF16),
            jax.ShapeDtypeStruct(head_shape, kv_dtype),
            jax.ShapeDtypeStruct(head_shape, kv_dtype),
        ],
        compiler_params=_cparams("parallel"),
        name="norm_proj_rope" if rope else "norm_proj",
    )(*args)


def _dft_mats(L):
    n = 2 * L - 1
    fs = (np.arange(L, dtype=np.int64)[:, None] * np.arange(L, dtype=np.int64)[None, :]) % n
    ang = fs.astype(np.float64) * (2.0 * np.pi / n)
    return np.cos(ang).astype(np.float32), np.sin(ang).astype(np.float32)


def _hy_features(L):
    t = np.linspace(0.0, 1.0, L, dtype=np.float64)[:, None]
    ang = ((2.0 * math.pi / L) * np.arange(L, dtype=np.float64))[:, None]
    bands = np.linspace(1e-4, HY_BANDS - 1, HY_BANDS, dtype=np.float64)[None, :]
    ba = bands * ang
    z = np.concatenate([t, np.cos(ba), -np.sin(ba)], axis=-1).astype(np.float32)
    return np.pad(z, ((0, 0), (0, LANES - HY_EMB)))


def _hy_filter_kernel(L, z_ref, w1_ref, b1_ref, w2_ref, b2_ref, w3_ref, fr_ref, rc_ref, rs_ref):
    z = z_ref[...]
    h = jnp.sin(fr_ref[0:1, :] * (_dot3(z, w1_ref[...]) + b1_ref[...]))
    h = jnp.sin(fr_ref[1:2, :] * (_dot3(h, w2_ref[...]) + b2_ref[...]))
    h = _dot3(h, w3_ref[...])
    t = z[:, 0:1]
    step = (HY_MAX_DECAY - HY_MIN_DECAY) / (D_HY - 1)
    deltas = HY_MIN_DECAY + step * lax.broadcasted_iota(jnp.int32, (1, D_HY), 1).astype(F32)
    window = jnp.exp(-t * jnp.abs(deltas))
    not_first = lax.broadcasted_iota(jnp.int32, (L, 1), 0) > 0
    for o in range(2):
        hf = h[:, (2 * o) * D_HY:(2 * o + 1) * D_HY] * window
        hb = jnp.where(not_first, h[:, (2 * o + 1) * D_HY:(2 * o + 2) * D_HY] * window, 0.0)
        rc_ref[:, o * D_HY:(o + 1) * D_HY] = hf + hb
        rs_ref[:, o * D_HY:(o + 1) * D_HY] = hb - hf


def _hy_spectrum_kernel(c_ref, s_ref, rc_ref, rs_ref, w_ref, kre_ref, kim_ref):
    rch, rcl = _split(rc_ref[...])
    rsh, rsl = _split(rs_ref[...])
    c, s, w = c_ref[...], s_ref[...], w_ref[...]
    kre_ref[...] = (_dot(c, rch) + _dot(c, rcl)) * w
    kim_ref[...] = (_dot(s, rsh) + _dot(s, rsl)) * w


def _hy_spectra(L, cmat, smat, w1, b1, w2, b2, w3, freq):
    z = jnp.asarray(_hy_features(L))
    w1p = jnp.pad(w1, ((0, LANES - HY_EMB), (0, 0)))
    nw = 2 * D_HY
    rc, rs = pl.pallas_call(
        functools.partial(_hy_filter_kernel, L),
        out_shape=[jax.ShapeDtypeStruct((L, nw), F32)] * 2,
        compiler_params=pltpu.CompilerParams(vmem_limit_bytes=VMEM_LIMIT),
        name="hy_filter",
    )(z, w1p, b1.reshape(1, HY_FFN), w2, b2.reshape(1, HY_FFN), w3, freq)
    n = 2 * L - 1
    wsc = np.full((L, 1), 2.0 / n, np.float32)
    wsc[0, 0] = 1.0 / n
    tr = min(L, 256)
    return pl.pallas_call(
        _hy_spectrum_kernel,
        grid=(L // tr,),
        in_specs=[
            pl.BlockSpec((tr, L), lambda i: (i, 0)),
            pl.BlockSpec((tr, L), lambda i: (i, 0)),
            _const_spec((L, nw)),
            _const_spec((L, nw)),
            pl.BlockSpec((tr, 1), lambda i: (i, 0)),
        ],
        out_specs=[pl.BlockSpec((tr, nw), lambda i: (i, 0))] * 2,
        out_shape=[jax.ShapeDtypeStruct((L, nw), F32)] * 2,
        compiler_params=_cparams("parallel"),
        name="hy_spectrum",
    )(cmat, smat, rc, rs, jnp.asarray(wsc))


def _hyena_kernel(L, tr, p_ref, sw_ref, sb_ref, bias_ref, c_ref, s_ref, kre_ref, kim_ref, o_ref,
                  pad_ref, u_ref, sig_ref, sig16_ref, zre_ref, zim_ref):
    C3 = 3 * D_HY
    zeros = jnp.zeros((8, C3), F32)
    pad_ref[0:8, :] = zeros
    pad_ref[8 + L:16 + L, :] = zeros
    chunks = [slice(r0, r0 + tr) for r0 in range(0, L, tr)]
    for c in chunks:
        pad_ref[8 + c.start:8 + c.stop, :] = p_ref[c, :].astype(F32)
    for c in chunks:
        u = sb_ref[...]
        for j in range(3):
            u = u + pad_ref[7 + j + c.start:7 + j + c.stop, :] * sw_ref[j:j + 1, :]
        u_ref[c, :] = u[:, D_HY:C3]
        sig_ref[c, :] = u[:, 0:D_HY]
        sig16_ref[c, :] = u[:, 0:D_HY].astype(BF16)

    for o in range(2):
        ko = slice(o * D_HY, (o + 1) * D_HY)
        for c in chunks:
            ure = _dot(c_ref[c, :], sig16_ref[...])
            us = _dot(s_ref[c, :], sig16_ref[...])
            kre, kim = kre_ref[c, ko], kim_ref[c, ko]
            zre_ref[c, :] = (ure * kre + us * kim).astype(BF16)
            zim_ref[c, :] = (ure * kim - us * kre).astype(BF16)
        gate = slice(o * D_HY, (o + 1) * D_HY)
        for c in chunks:
            y = _dot(c_ref[c, :], zre_ref[...]) - _dot(s_ref[c, :], zim_ref[...])
            z = u_ref[c, gate] * (y + sig_ref[c, :] * bias_ref[o:o + 1, :])
            if o == 0:
                sig_ref[c, :] = z
                sig16_ref[c, :] = z.astype(BF16)
            else:
                o_ref[c, :] = z.astype(o_ref.dtype)


def _hyena(p_hy, B, L, cmat, smat, kre, kim, short_w, short_b, bias, tr=512):
    C3 = 3 * D_HY
    tr = min(tr, L)
    once = pl.Buffered(1)
    return pl.pallas_call(
        functools.partial(_hyena_kernel, L, tr),
        grid=(B,),
        in_specs=[
            pl.BlockSpec((L, C3), lambda b: (b, 0)),
            _const_spec((3, C3)),
            _const_spec((1, C3)),
            _const_spec((2, D_HY)),
            pl.BlockSpec((L, L), lambda b: (0, 0), pipeline_mode=once),
            pl.BlockSpec((L, L), lambda b: (0, 0), pipeline_mode=once),
            pl.BlockSpec((L, 2 * D_HY), lambda b: (0, 0), pipeline_mode=once),
            pl.BlockSpec((L, 2 * D_HY), lambda b: (0, 0), pipeline_mode=once),
        ],
        out_specs=pl.BlockSpec((L, D_HY), lambda b: (b, 0)),
        out_shape=jax.ShapeDtypeStruct((B * L, D_HY), BF16),
        scratch_shapes=[
            pltpu.VMEM((L + 16, C3), F32),
            pltpu.VMEM((L, 2 * D_HY), F32),
            pltpu.VMEM((L, D_HY), F32),
            pltpu.VMEM((L, D_HY), BF16),
            pltpu.VMEM((L, D_HY), BF16),
            pltpu.VMEM((L, D_HY), BF16),
        ],
        compiler_params=_cparams("parallel"),
        name="hyena",
    )(p_hy, short_w, short_b.reshape(1, C3), bias, cmat, smat, kre, kim)


def _softplus(z):
    return jnp.maximum(z, 0.0) + jnp.log1p(jnp.exp(-jnp.abs(z)))


def _expm1(x):
    u = jnp.exp(x)
    near = (u - 1.0) * x / jnp.where(u == 1.0, 1.0, jnp.log(u))
    return jnp.where(x < -0.5, u - 1.0, jnp.where(u == 1.0, x, near))


def _gelu_tanh(x):
    return 0.5 * x * (1.0 + jnp.tanh(math.sqrt(2.0 / math.pi) * (x + 0.044715 * x * x * x)))


def _rglru_kernel(L, has_state, pg_ref, px_ref, cw_ref, cb_ref, wh_ref, wl_ref, gb_ref, lam_ref, *rest):
    if has_state:
        st_ref, y_ref, pad_ref, a_ref, b_ref, h_ref = rest
    else:
        y_ref, st_out_ref, pad_ref, a_ref, b_ref, h_ref = rest
    C = D_RG
    zeros = jnp.zeros((8, C), F32)
    pad_ref[0:8, :] = zeros
    pad_ref[8 + L:16 + L, :] = zeros
    pad_ref[8:8 + L, :] = px_ref[...]
    sp = _softplus(-lam_ref[...])
    tr = min(L, 256)
    for r0 in range(0, L, tr):
        xr = cb_ref[...]
        for j in range(4):
            xr = xr + pad_ref[6 + j + r0:6 + j + r0 + tr, :] * cw_ref[j:j + 1, :]
        xh, xl = _split(xr)
        for d in range(2):
            g = []
            for m in range(2):
                cols = slice((2 * d + m) * C, (2 * d + m + 1) * C)
                wh = wh_ref[:, cols]
                g.append(_sigmoid(_dot(xh, wh) + _dot(xl, wh) + _dot(xh, wl_ref[:, cols]) + gb_ref[:, cols]))
            log_a = -RG_C * g[0] * sp[d:d + 1, :]
            a_ref[d, r0:r0 + tr, :] = jnp.exp(log_a)
            b_ref[d, r0:r0 + tr, :] = jnp.sqrt(-_expm1(2.0 * log_a)) * (g[1] * xr)

    if has_state:
        h0f, h0b = st_ref[0:1, :], st_ref[1:2, :]
    else:
        h0f = h0b = jnp.zeros((1, C), F32)

    def step(t, carry):
        hf, hb = carry
        tb = L - 1 - t
        hf = a_ref[0, pl.ds(t, 1), :] * hf + b_ref[0, pl.ds(t, 1), :]
        hb = a_ref[1, pl.ds(tb, 1), :] * hb + b_ref[1, pl.ds(tb, 1), :]
        h_ref[0, pl.ds(t, 1), :] = hf
        h_ref[1, pl.ds(tb, 1), :] = hb
        return hf, hb

    lax.fori_loop(0, L, step, (h0f, h0b))
    y_ref[...] = ((h_ref[0] + h_ref[1]) * _gelu_tanh(pg_ref[...])).astype(y_ref.dtype)
    if not has_state:
        st_out_ref[0:1, :] = h_ref[0, L - 1:L, :]
        st_out_ref[1:2, :] = h_ref[1, 0:1, :]


def _block_diag(w):
    H, d, _ = w.shape
    eye = jnp.eye(H, dtype=w.dtype)
    return (eye[:, None, :, None] * w[:, :, None, :]).reshape(H * d, H * d)


def _rglru(p_g, p_x, B, L, conv_w, conv_b, wa, ba, wx, bx, lam, state):
    C = D_RG
    wcat = jnp.concatenate([_block_diag(wa[0]), _block_diag(wx[0]), _block_diag(wa[1]), _block_diag(wx[1])], axis=1)
    wh = wcat.astype(BF16)
    wl = (wcat - wh.astype(F32)).astype(BF16)
    gb = jnp.concatenate([ba[0], bx[0], ba[1], bx[1]]).reshape(1, 4 * C)
    has_state = state is not None
    in_specs = [
        pl.BlockSpec((L, C), lambda b: (b, 0)),
        pl.BlockSpec((L, C), lambda b: (b, 0)),
        _const_spec((4, C)),
        _const_spec((1, C)),
        _const_spec((C, 4 * C)),
        _const_spec((C, 4 * C)),
        _const_spec((1, 4 * C)),
        _const_spec((2, C)),
    ]
    args = [p_g, p_x, conv_w, conv_b.reshape(1, C), wh, wl, gb, lam]
    y_spec = pl.BlockSpec((L, C), lambda b: (b, 0))
    y_shape = jax.ShapeDtypeStruct((B * L, C), BF16)
    if has_state:
        in_specs.append(pl.BlockSpec((None, 2, C), lambda b: (b, 0, 0)))
        args.append(state)
        out_specs, out_shape = y_spec, y_shape
    else:
        out_specs = [y_spec, pl.BlockSpec((None, 2, C), lambda b: (b, 0, 0))]
        out_shape = [y_shape, jax.ShapeDtypeStruct((B, 2, C), F32)]
    return pl.pallas_call(
        functools.partial(_rglru_kernel, L, has_state),
        grid=(B,),
        in_specs=in_specs,
        out_specs=out_specs,
        out_shape=out_shape,
        scratch_shapes=[
            pltpu.VMEM((L + 16, C), F32),
            pltpu.VMEM((2, L, C), F32),
            pltpu.VMEM((2, L, C), F32),
            pltpu.VMEM((2, L, C), F32),
        ],
        compiler_params=_cparams("parallel"),
        name="rglru_state" if has_state else "rglru",
    )(*args)


def _attn_kernel(L, P, tq, lam_init, q_ref, k_ref, v_ref, *rest):
    if P:
        ck_ref, cv_ref, dal_ref, sub_ref, o_ref, kk_ref, vv_ref = rest
    else:
        dal_ref, sub_ref, o_ref, kk_ref, vv_ref = rest
    lv = dal_ref[...]
    s01 = jnp.sum(lv[0:1, :] * lv[1:2, :], axis=-1, keepdims=True)
    s23 = jnp.sum(lv[2:3, :] * lv[3:4, :], axis=-1, keepdims=True)
    lam = jnp.exp(s01) - jnp.exp(s23) + lam_init
    first_half = lax.broadcasted_iota(jnp.int32, (1, DA_VDIM), 1) < DA_HEAD
    sub = sub_ref[...] * (1.0 - lam_init)
    for hd in range(N_DA_HEADS):
        if P:
            kk_ref[0:P, :] = ck_ref[hd].astype(BF16)
            vv_ref[0:P, :] = cv_ref[hd].astype(BF16)
        kk_ref[P:P + L, :] = k_ref[hd].astype(BF16)
        vv_ref[P:P + L, :] = v_ref[hd].astype(BF16)

        def qblock(i, carry):
            r0 = pl.multiple_of(i * tq, tq)
            q = q_ref[hd, pl.ds(r0, tq), :]
            zero = jnp.zeros_like(q)
            qs = jnp.concatenate([jnp.where(first_half, q, zero), jnp.where(first_half, zero, q)], axis=0)
            s = _dot_nt(qs, kk_ref[...])
            p = jnp.exp2(s - jnp.max(s, axis=-1, keepdims=True))
            rinv = 1.0 / jnp.sum(p, axis=-1, keepdims=True)
            acc = _dot(p.astype(BF16), vv_ref[...])
            o = acc[0:tq] * rinv[0:tq] - acc[tq:2 * tq] * (lam * rinv[tq:2 * tq])
            o = o * lax.rsqrt(jnp.mean(o * o, axis=-1, keepdims=True) + EPS) * sub
            o_ref[pl.ds(r0, tq), hd * DA_VDIM:(hd + 1) * DA_VDIM] = o.astype(o_ref.dtype)
            return carry

        lax.fori_loop(0, L // tq, qblock, 0, unroll=2)


def _attention(q, k, v, cache, dal, subln, lam_init, B, L, tq=128):
    H, dv = N_DA_HEADS, DA_VDIM
    hspec = pl.BlockSpec((None, H, L, dv), lambda b: (b, 0, 0, 0))
    in_specs = [hspec, hspec, hspec]
    args = [q, k, v]
    P = 0
    if cache is not None:
        ck, cv, layer = cache
        P = ck.shape[3]
        cspec = pl.BlockSpec((None, None, H, P, dv), lambda b: (b, layer, 0, 0, 0))
        in_specs += [cspec, cspec]
        args += [ck, cv]
    assert L % tq == 0
    in_specs += [_const_spec((4, DA_HEAD)), _const_spec((1, dv))]
    args += [dal, subln.reshape(1, dv)]
    return pl.pallas_call(
        functools.partial(_attn_kernel, L, P, tq, lam_init),
        grid=(B,),
        in_specs=in_specs,
        out_specs=pl.BlockSpec((L, H * dv), lambda b: (b, 0)),
        out_shape=jax.ShapeDtypeStruct((B * L, H * dv), BF16),
        scratch_shapes=[pltpu.VMEM((P + L, dv), BF16), pltpu.VMEM((P + L, dv), BF16)],
        compiler_params=_cparams("parallel"),
        name="diff_attn_cache" if P else "diff_attn",
    )(*args)


def _route(logits):
    m = logits[0]
    for e in range(1, N_EXPERTS):
        m = jnp.maximum(m, logits[e])
    ex = [jnp.exp(l - m) for l in logits]
    tot = ex[0]
    for e in range(1, N_EXPERTS):
        tot = tot + ex[e]
    inv = 1.0 / tot
    p = [e_ * inv for e_ in ex]
    G = EXP_PER_GROUP
    best, gsel = None, None
    for g in range(N_GROUPS):
        a = p[g * G:(g + 1) * G]
        sc = None
        for i in range(G):
            for j in range(i + 1, G):
                pair = a[i] + a[j]
                sc = pair if sc is None else jnp.maximum(sc, pair)
        if g == 0:
            best, gsel = sc, jnp.zeros_like(sc, dtype=jnp.int32)
        else:
            upd = sc > best
            best = jnp.where(upd, sc, best)
            gsel = jnp.where(upd, g, gsel)
    vals = []
    for j in range(G):
        vj = p[j]
        for g in range(1, N_GROUPS):
            vj = jnp.where(gsel == g, p[g * G + j], vj)
        vals.append(vj)
    p1, i1 = vals[0], jnp.zeros_like(gsel)
    for j in range(1, G):
        upd = vals[j] > p1
        p1 = jnp.where(upd, vals[j], p1)
        i1 = jnp.where(upd, j, i1)
    p2, i2 = None, None
    for j in range(G):
        cand = jnp.where(i1 == j, -1.0, vals[j])
        if p2 is None:
            p2, i2 = cand, jnp.zeros_like(gsel)
        else:
            upd = cand > p2
            p2 = jnp.where(upd, cand, p2)
            i2 = jnp.where(upd, j, i2)
    den = 1.0 / (p1 + p2)
    w1, w2 = p1 * den, p2 * den
    swap = i2 < i1
    a, b = jnp.where(swap, i2, i1), jnp.where(swap, i1, i2)
    w_lo, w_hi = jnp.where(swap, w2, w1), jnp.where(swap, w1, w2)
    pair = jnp.where(a == 0, b - 1, jnp.where(a == 1, b + 1, 5))
    cls = gsel * PAIRS_PER_GROUP + pair
    return cls.astype(F32), w_lo, w_hi


def _pack_pairs(x):
    n = x.shape[1] // 2
    b = pltpu.bitcast(x, jnp.uint32)
    w = (b[:, :n] >> 16) | (b[:, n:] & jnp.uint32(0xFFFF0000))
    return pltpu.bitcast(w, jnp.int32)


def _unpack_pairs(w):
    u = pltpu.bitcast(w, jnp.uint32)
    lo = pltpu.bitcast(u << 16, F32)
    hi = pltpu.bitcast(u & jnp.uint32(0xFFFF0000), F32)
    return jnp.concatenate([lo, hi], axis=1)


def _out_proj_kernel(yh_ref, yr_ref, o_ref, w_ref, x_ref, mod_ref, g_ref, wrh_ref, wrl_ref, br_ref,
                     xo_ref, h_ref, route_ref):
    y = (_dot(yh_ref[...], w_ref[0:D_HY, :]) + _dot(yr_ref[...], w_ref[D_HY:D_HY + D_RG, :])
         + _dot(o_ref[...], w_ref[D_HY + D_RG:D_MIX, :]))
    x = x_ref[...] + mod_ref[2:3, :] * y
    xo_ref[...] = x
    ms = jnp.mean(x * x, axis=-1, keepdims=True)
    h = (x * lax.rsqrt(ms + EPS) * g_ref[...]) * (1.0 + mod_ref[4:5, :]) + mod_ref[3:4, :]
    hh, hl = _split(h)
    h_ref[...] = _pack_pairs(hh.astype(F32))
    lg = _dot_nt(wrh_ref[...], hh) + _dot_nt(wrh_ref[...], hl) + _dot_nt(wrl_ref[...], hh) + br_ref[...]
    info = _route([lg[e:e + 1, :] for e in range(N_EXPERTS)])
    rt = jnp.concatenate(list(info) + [jnp.zeros((LANES - len(info), lg.shape[1]), F32)], axis=0)
    route_ref[...] = rt.T


def _out_proj(y_hy, y_rg, o, w_out, layer, x, mod, g2, w_router, b_router, B, L, ctx_rows, tm=512):
    T = B * L
    tm = min(tm, L)
    nl = L // tm
    row = (lambda i: CTX_ROW) if ctx_rows else (lambda i: i // nl)
    wrt = w_router.T
    wrh = wrt.astype(BF16)
    wrl = (wrt - wrh.astype(F32)).astype(BF16)
    rows = lambda w: pl.BlockSpec((tm, w), lambda i: (i, 0))
    return pl.pallas_call(
        _out_proj_kernel,
        grid=(T // tm,),
        in_specs=[
            rows(D_HY), rows(D_RG), rows(D_DA),
            pl.BlockSpec((None, D_MIX, D_MODEL), lambda i: (layer, 0, 0)),
            rows(D_MODEL),
            pl.BlockSpec((None, 6, D_MODEL), lambda i: (row(i), 0, 0)),
            _const_spec((1, D_MODEL)),
            _const_spec((N_EXPERTS, D_MODEL)),
            _const_spec((N_EXPERTS, D_MODEL)),
            _const_spec((N_EXPERTS, 1)),
        ],
        out_specs=[rows(D_MODEL), rows(D_MODEL // 2), rows(LANES)],
        out_shape=[
            jax.ShapeDtypeStruct((T, D_MODEL), F32),
            jax.ShapeDtypeStruct((T, D_MODEL // 2), jnp.int32),
            jax.ShapeDtypeStruct((T, LANES), F32),
        ],
        compiler_params=_cparams("parallel"),
        name="out_proj_route",
    )(y_hy, y_rg, o, w_out, x, mod, g2.reshape(1, D_MODEL), wrh, wrl, b_router.reshape(N_EXPERTS, 1))


def _gather_rows(table, idx, rows_per_step=64, n_buf=2):
    info = plsc.get_sparse_core_info()
    n_workers = info.num_cores * info.num_subcores
    n, width = idx.shape[0], table.shape[1]
    per_worker = n // n_workers
    n_steps = per_worker // rows_per_step
    assert per_worker * n_workers == n and n_steps * rows_per_step == per_worker and n_steps >= n_buf
    mesh = plsc.VectorSubcoreMesh(core_axis_name="c", subcore_axis_name="s")

    @functools.partial(
        pl.kernel, mesh=mesh,
        out_type=jax.ShapeDtypeStruct((n, width), table.dtype),
        scratch_types=[
            pltpu.VMEM((per_worker,), jnp.int32),
            pltpu.VMEM((n_buf, rows_per_step, width), table.dtype),
            pltpu.SemaphoreType.DMA((n_buf,)),
            pltpu.SemaphoreType.DMA((n_buf,)),
        ],
    )
    def gather(table_hbm, idx_hbm, out_hbm, idx_v, rows_v, sem_in, sem_out):
        worker = lax.axis_index("s") * info.num_cores + lax.axis_index("c")
        base = pl.multiple_of(worker * per_worker, per_worker)
        pltpu.sync_copy(idx_hbm.at[pl.ds(base, per_worker)], idx_v)

        def read(b, step):
            rows = idx_v.at[pl.ds(step * rows_per_step, rows_per_step)]
            return pltpu.make_async_copy(table_hbm.at[rows], rows_v.at[b], sem_in.at[b])

        def write(b, step):
            off = pl.multiple_of(base + step * rows_per_step, rows_per_step)
            return pltpu.make_async_copy(rows_v.at[b], out_hbm.at[pl.ds(off, rows_per_step)], sem_out.at[b])

        for step in range(n_steps + 1):
            if step < n_steps:
                if step >= n_buf:
                    write(step % n_buf, step - n_buf).wait()
                read(step % n_buf, step).start()
            if step >= 1:
                read((step - 1) % n_buf, step - 1).wait()
                write((step - 1) % n_buf, step - 1).start()
        for step in range(n_steps - n_buf, n_steps):
            write(step % n_buf, step).wait()

    return gather(table, idx)


def _dispatch_plan(route, tm):
    T = route.shape[0]
    n_slots = T + N_CLASSES * tm
    cls = route[:, 0].astype(jnp.int32)
    onehot = (cls[:, None] == jnp.arange(N_CLASSES, dtype=jnp.int32)[None, :]).astype(jnp.int32)
    csum = jnp.cumsum(onehot, axis=0)
    rank = jnp.sum(onehot * csum, axis=1) - 1
    counts = csum[-1]
    padded = ((counts + tm - 1) // tm) * tm
    ends = jnp.cumsum(padded)
    pos = jnp.sum(onehot * (ends - padded)[None, :], axis=1) + rank
    inv = (jnp.arange(n_slots, dtype=jnp.int32) % T).at[pos].set(jnp.arange(T, dtype=jnp.int32), unique_indices=True)
    tile_start = jnp.arange(n_slots // tm, dtype=jnp.int32) * tm
    tile_cls = jnp.minimum(jnp.searchsorted(ends, tile_start, side="right"), N_CLASSES - 1).astype(jnp.int32)
    valid = (tile_start < ends[-1]).astype(jnp.int32)
    pairs = np.array([(a, b) for a in range(EXP_PER_GROUP) for b in range(a + 1, EXP_PER_GROUP)], np.int32)
    group, pair = tile_cls // PAIRS_PER_GROUP, tile_cls % PAIRS_PER_GROUP
    lo = group * EXP_PER_GROUP + jnp.asarray(pairs[:, 0])[pair]
    hi = group * EXP_PER_GROUP + jnp.asarray(pairs[:, 1])[pair]
    return pos, inv, lo, hi, valid


def _moe_sorted_kernel(lo_ref, hi_ref, valid_ref, xs_ref, ws_ref, wg_lo, wu_lo, wd_lo, wg_hi, wu_hi, wd_hi, o_ref):
    i = pl.program_id(0)

    @pl.when(valid_ref[i] == 1)
    def _():
        x = _unpack_pairs(xs_ref[...]).astype(BF16)
        y = None
        for wg, wu, wd, col in ((wg_lo, wu_lo, wd_lo, 1), (wg_hi, wu_hi, wd_hi, 2)):
            a = _dot(x, wg[...])
            he = (a * _sigmoid(a)) * _dot(x, wu[...]) * ws_ref[:, col:col + 1]
            part = _dot(he.astype(BF16), wd[...])
            y = part if y is None else y + part
        o_ref[...] = _pack_pairs(y.astype(BF16).astype(F32))

    @pl.when(valid_ref[i] == 0)
    def _():
        o_ref[...] = jnp.zeros_like(o_ref)


def _moe_sorted(xs, ws, lo, hi, valid, wg, wu, wd, tm):
    n_slots = xs.shape[0]
    half = D_MODEL // 2
    up = lambda sel: pl.BlockSpec((None, D_MODEL, D_EXPERT), lambda i, lo, hi, v: ((lo, hi)[sel][i], 0, 0))
    down = lambda sel: pl.BlockSpec((None, D_EXPERT, D_MODEL), lambda i, lo, hi, v: ((lo, hi)[sel][i], 0, 0))
    return pl.pallas_call(
        _moe_sorted_kernel,
        grid_spec=pltpu.PrefetchScalarGridSpec(
            num_scalar_prefetch=3,
            grid=(n_slots // tm,),
            in_specs=[
                pl.BlockSpec((tm, half), lambda i, lo, hi, v: (i, 0)),
                pl.BlockSpec((tm, LANES), lambda i, lo, hi, v: (i, 0)),
                up(0), up(0), down(0), up(1), up(1), down(1),
            ],
            out_specs=pl.BlockSpec((tm, half), lambda i, lo, hi, v: (i, 0)),
        ),
        out_shape=jax.ShapeDtypeStruct((n_slots, half), jnp.int32),
        compiler_params=_cparams("arbitrary"),
        name="moe_sorted",
    )(lo, hi, valid, xs, ws, wg, wu, wd, wg, wu, wd)


def _moe_residual_kernel(final, y_ref, x_ref, mod_ref, fg_ref, o_ref):
    x = x_ref[...] + mod_ref[5:6, :] * _unpack_pairs(y_ref[...])
    if final:
        x = x * lax.rsqrt(jnp.mean(x * x, axis=-1, keepdims=True) + EPS) * fg_ref[...]
    o_ref[...] = x


def _moe_residual(y, x, mod, final_g, final, B, L, ctx_rows, tm=512):
    T = B * L
    tm = min(tm, L)
    nl = L // tm
    row = (lambda i: CTX_ROW) if ctx_rows else (lambda i: i // nl)
    return pl.pallas_call(
        functools.partial(_moe_residual_kernel, final),
        grid=(T // tm,),
        in_specs=[
            pl.BlockSpec((tm, D_MODEL // 2), lambda i: (i, 0)),
            pl.BlockSpec((tm, D_MODEL), lambda i: (i, 0)),
            pl.BlockSpec((None, 6, D_MODEL), lambda i: (row(i), 0, 0)),
            _const_spec((1, D_MODEL)),
        ],
        out_specs=pl.BlockSpec((tm, D_MODEL), lambda i: (i, 0)),
        out_shape=jax.ShapeDtypeStruct((T, D_MODEL), F32),
        compiler_params=_cparams("parallel"),
        name="moe_residual_final" if final else "moe_residual",
    )(y, x, mod, final_g.reshape(1, D_MODEL))


def _moe(h, route, wg, wu, wd, layer, x, mod, final_g, final, B, L, ctx_rows, tm=256):
    pos, inv, lo, hi, valid = _dispatch_plan(route, tm)
    xs = _gather_rows(h, inv)
    ws = _gather_rows(route, inv)
    ys = _moe_sorted(xs, ws, lo + layer * N_EXPERTS, hi + layer * N_EXPERTS, valid, wg, wu, wd, tm)
    y = _gather_rows(ys, pos)
    return _moe_residual(y, x, mod, final_g, final, B, L, ctx_rows)


def kernel(x_prompt, x_sample, cache_k, cache_v, state_rglru, c, c_ctx, w_ada, b_ada, norm1_g, norm2_g, w_in, w_out, hy_short_w, hy_short_b, hy_w1, hy_b1, hy_w2, hy_b2, hy_w3, hy_freq, hy_bias, rg_conv_w, rg_conv_b, rg_wa, rg_ba, rg_wx, rg_bx, rg_lambda, da_lambda, da_subln, w_router, b_router, moe_wg, moe_wu, moe_wd, final_g):
    Bp, Lp, D = x_prompt.shape
    Bs, Ls, _ = x_sample.shape
    assert Bs <= CTX_ROW
    cond = jnp.zeros((N_COND, D), F32).at[:Bs].set(c).at[CTX_ROW].set(c_ctx)
    mods = _ada_table(cond, w_ada, b_ada)

    dft = {L: tuple(jnp.asarray(m).astype(BF16) for m in _dft_mats(L)) for L in (Lp, Ls)}
    streams = [
        dict(B=Bp, L=Lp, ctx=True, x=x_prompt.reshape(Bp * Lp, D)),
        dict(B=Bs, L=Ls, ctx=False, x=x_sample.reshape(Bs * Ls, D)),
    ]
    w_in_b, w_out_b = w_in.astype(BF16), w_out.astype(BF16)
    wg, wu, wd = (w.astype(BF16).reshape((DEPTH * N_EXPERTS,) + w.shape[2:]) for w in (moe_wg, moe_wu, moe_wd))
    ks, vs, ss = [], [], []
    for l in range(DEPTH):
        lam_init = 0.8 - 0.6 * math.exp(-0.3 * l)
        final = l == DEPTH - 1
        for st in streams:
            B, L, ctx = st["B"], st["L"], st["ctx"]
            cmat, smat = dft[L]
            p_hy, p_g, p_x, q, k, v = _norm_proj(
                st["x"], mods[l], norm1_g[l], w_in_b, l, B, L, rope=not ctx,
                kv_dtype=F32 if ctx else BF16, ctx_rows=ctx)
            kre, kim = _hy_spectra(L, cmat, smat, hy_w1[l], hy_b1[l], hy_w2[l], hy_b2[l], hy_w3[l], hy_freq[l])
            y_hy = _hyena(p_hy, B, L, cmat, smat, kre, kim, hy_short_w[l], hy_short_b[l], hy_bias[l])
            rg_args = (rg_conv_w[l], rg_conv_b[l], rg_wa[l], rg_ba[l], rg_wx[l], rg_bx[l], rg_lambda[l])
            if ctx:
                y_rg, s_l = _rglru(p_g, p_x, B, L, *rg_args, None)
                o = _attention(q, k, v, None, da_lambda[l], da_subln[l], lam_init, B, L)
                ks.append(k)
                vs.append(v)
                ss.append(s_l)
            else:
                y_rg = _rglru(p_g, p_x, B, L, *rg_args, state_rglru[:, l])
                o = _attention(q, k, v, (cache_k, cache_v, l), da_lambda[l], da_subln[l], lam_init, B, L)
            x_mid, h2, route = _out_proj(y_hy, y_rg, o, w_out_b, l, st["x"], mods[l], norm2_g[l],
                                         w_router, b_router, B, L, ctx)
            st["x"] = _moe(h2, route, wg, wu, wd, l, x_mid, mods[l], final_g, final, B, L, ctx)
    y_prompt = streams[0]["x"].reshape(Bp, Lp, D)
    y_sample = streams[1]["x"].reshape(Bs, Ls, D)
    return (y_prompt, y_sample, jnp.stack(ks, axis=1), jnp.stack(vs, axis=1), jnp.stack(ss, axis=1))
```

```python
import functools
import math

import numpy as np
import jax
import jax.numpy as jnp
from jax import lax
from jax.experimental import pallas as pl
from jax.experimental.pallas import tpu as pltpu
from jax.experimental.pallas import tpu_sc as plsc

F32 = jnp.float32
BF16 = jnp.bfloat16

D_MODEL = 1024
DEPTH = 2
GRID_W = 64
D_HY = 256
HY_EMB = 33
HY_BANDS = (HY_EMB - 1) // 2
HY_FFN = 64
HY_MIN_DECAY = math.log(1e-2) / 1.5
HY_MAX_DECAY = math.log(1e-2) / 0.3
D_RG = 256
N_RG_HEADS = 4
RG_C = 8.0
N_DA_HEADS = 4
DA_HEAD = 64
DA_VDIM = 2 * DA_HEAD
D_DA = N_DA_HEADS * DA_VDIM
D_MIX = D_HY + D_RG + D_DA
D_IN = 3 * D_HY + 2 * D_RG + 3 * D_DA
ROPE_PAIRS = DA_HEAD // 4
ROPE_THETA = 10000.0
N_EXPERTS = 16
N_GROUPS = 4
EXP_PER_GROUP = N_EXPERTS // N_GROUPS
D_EXPERT = 512
PAIRS_PER_GROUP = EXP_PER_GROUP * (EXP_PER_GROUP - 1) // 2
N_CLASSES = N_GROUPS * PAIRS_PER_GROUP
EPS = 1e-6
N_COND = 16
CTX_ROW = 8
LANES = 128
VMEM_LIMIT = 56 * 1024 * 1024


def _cparams(*sem):
    return pltpu.CompilerParams(dimension_semantics=sem, vmem_limit_bytes=VMEM_LIMIT)


def _split(x):
    hi = x.astype(BF16)
    lo = (x - hi.astype(F32)).astype(BF16)
    return hi, lo


def _dot(a, b):
    return jnp.dot(a, b, preferred_element_type=F32)


def _dot3(a, b):
    ah, al = _split(a)
    bh, bl = _split(b)
    return _dot(ah, bh) + _dot(al, bh) + _dot(ah, bl)


def _dot_nt(a, b):
    return lax.dot_general(a, b, (((1,), (1,)), ((), ())), preferred_element_type=F32)


def _sigmoid(x):
    return 1.0 / (1.0 + jnp.exp(-x))


def _const_spec(shape):
    n = len(shape)
    return pl.BlockSpec(shape, lambda *_: (0,) * n)


def _ada_kernel(c_ref, w_ref, b_ref, o_ref):
    c = c_ref[...]
    s = c * _sigmoid(c)
    o_ref[...] = _dot3(s, w_ref[...]) + b_ref[...]


def _ada_table(cond, w_ada, b_ada):
    D = D_MODEL
    out = pl.pallas_call(
        _ada_kernel,
        grid=(DEPTH, 6),
        in_specs=[
            pl.BlockSpec((N_COND, D), lambda l, j: (0, 0)),
            pl.BlockSpec((None, D, D), lambda l, j: (l, 0, j)),
            pl.BlockSpec((None, None, 1, D), lambda l, j: (l, j, 0, 0)),
        ],
        out_specs=pl.BlockSpec((None, None, N_COND, D), lambda l, j: (l, j, 0, 0)),
        out_shape=jax.ShapeDtypeStruct((DEPTH, 6, N_COND, D), F32),
        compiler_params=_cparams("parallel", "parallel"),
        name="ada_table",
    )(cond, w_ada, b_ada.reshape(DEPTH, 6, 1, D))
    return out.transpose(0, 2, 1, 3)


def _rope_tables(L):
    t = np.arange(L)
    j = np.arange(LANES)
    jj = j % DA_HEAD
    is_col = (jj // (DA_HEAD // 2)) == 1
    pair = jj % ROPE_PAIRS
    second = (jj % (DA_HEAD // 2)) >= ROPE_PAIRS
    inv = ROPE_THETA ** (-np.arange(ROPE_PAIRS, dtype=np.float64) / ROPE_PAIRS)
    pos = np.where(is_col[None, :], (t % GRID_W)[:, None], (t // GRID_W)[:, None]).astype(np.float64)
    ang = pos * inv[pair][None, :]
    cos = np.cos(ang).astype(np.float32)
    sin = np.sin(ang).astype(np.float32)
    sin_a = np.where(second[None, :], 0.0, -sin).astype(np.float32)
    sin_b = np.where(second[None, :], sin, 0.0).astype(np.float32)
    return cos, sin_a, sin_b


def _rope(x, cos, sin_a, sin_b):
    nxt = pltpu.roll(x, LANES - ROPE_PAIRS, axis=1)
    prv = pltpu.roll(x, ROPE_PAIRS, axis=1)
    return x * cos + nxt * sin_a + prv * sin_b


def _norm_proj_kernel(rope, kv_dtype, x_ref, mod_ref, g_ref, w_ref, *rest):
    if rope:
        cos_ref, sa_ref, sb_ref = rest[:3]
    phy_ref, pg_ref, px_ref, q_ref, k_ref, v_ref = rest[-6:]
    x = x_ref[...]
    ms = jnp.mean(x * x, axis=-1, keepdims=True)
    y = x * lax.rsqrt(ms + EPS) * g_ref[...]
    h = (y * (1.0 + mod_ref[1:2, :]) + mod_ref[0:1, :]).astype(BF16)
    o = 3 * D_HY
    phy_ref[...] = _dot(h, w_ref[:, 0:o]).astype(BF16)
    pg_ref[...] = _dot(h, w_ref[:, o:o + D_RG])
    px_ref[...] = _dot(h, w_ref[:, o + D_RG:o + 2 * D_RG])
    o += 2 * D_RG
    q = _dot(h, w_ref[:, o:o + D_DA]) * (DA_HEAD ** -0.5 * math.log2(math.e))
    k = _dot(h, w_ref[:, o + D_DA:o + 2 * D_DA])
    v = _dot(h, w_ref[:, o + 2 * D_DA:o + 3 * D_DA])
    if rope:
        cos, sa, sb = cos_ref[...], sa_ref[...], sb_ref[...]
    for hd in range(N_DA_HEADS):
        sl = slice(hd * DA_VDIM, (hd + 1) * DA_VDIM)
        qh, kh = q[:, sl], k[:, sl]
        if rope:
            qh = _rope(qh, cos, sa, sb)
            kh = _rope(kh, cos, sa, sb)
        q_ref[hd] = qh.astype(BF16)
        k_ref[hd] = kh.astype(kv_dtype)
        v_ref[hd] = v[:, sl].astype(kv_dtype)


def _norm_proj(x, mod, g, w_in, layer, B, L, ctx, kv_prev=None, tm=512):
    T = B * L
    tm = min(tm, L)
    nl = L // tm
    rope, kv_dtype = not ctx, (F32 if ctx else BF16)
    row = (lambda i: CTX_ROW) if ctx else (lambda i: i // nl)
    in_specs = [
        pl.BlockSpec((tm, D_MODEL), lambda i: (i, 0)),
        pl.BlockSpec((None, 6, D_MODEL), lambda i: (row(i), 0, 0)),
        _const_spec((1, D_MODEL)),
        pl.BlockSpec((None, D_MODEL, D_IN), lambda i: (layer, 0, 0)),
    ]
    args = [x, mod, g.reshape(1, D_MODEL), w_in]
    if rope:
        tabs = _rope_tables(L)
        in_specs += [pl.BlockSpec((tm, LANES), lambda i: (i % nl, 0))] * 3
        args += [jnp.asarray(t) for t in tabs]
    head_spec = pl.BlockSpec((None, N_DA_HEADS, tm, DA_VDIM), lambda i: (i // nl, 0, i % nl, 0))
    head_shape = (B, N_DA_HEADS, L, DA_VDIM)
    kv_spec, kv_shape, aliases = head_spec, head_shape, {}
    if ctx:
        kv_spec = pl.BlockSpec((None, None, N_DA_HEADS, tm, DA_VDIM), lambda i: (i // nl, layer, 0, i % nl, 0))
        kv_shape = (B, DEPTH, N_DA_HEADS, L, DA_VDIM)
        if kv_prev is not None:
            aliases = {len(args): 4, len(args) + 1: 5}
            in_specs += [pl.BlockSpec(memory_space=pl.ANY)] * 2
            args += list(kv_prev)
    return pl.pallas_call(
        functools.partial(_norm_proj_kernel, rope, kv_dtype),
        grid=(T // tm,),
        in_specs=in_specs,
        out_specs=[
            pl.BlockSpec((tm, 3 * D_HY), lambda i: (i, 0)),
            pl.BlockSpec((tm, D_RG), lambda i: (i, 0)),
            pl.BlockSpec((tm, D_RG), lambda i: (i, 0)),
            head_spec, kv_spec, kv_spec,
        ],
        out_shape=[
            jax.ShapeDtypeStruct((T, 3 * D_HY), BF16),
            jax.ShapeDtypeStruct((T, D_RG), F32),
            jax.ShapeDtypeStruct((T, D_RG), F32),
            jax.ShapeDtypeStruct(head_shape, BF16),
            jax.ShapeDtypeStruct(kv_shape, kv_dtype),
            jax.ShapeDtypeStruct(kv_shape, kv_dtype),
        ],
        input_output_aliases=aliases,
        compiler_params=_cparams("parallel"),
        name="norm_proj_rope" if rope else "norm_proj",
    )(*args)


def _dft_mats(L):
    n = 2 * L - 1
    fs = (np.arange(L, dtype=np.int64)[:, None] * np.arange(L, dtype=np.int64)[None, :]) % n
    ang = fs.astype(np.float64) * (2.0 * np.pi / n)
    return np.cos(ang).astype(np.float32), np.sin(ang).astype(np.float32)


def _hy_features(L):
    t = np.linspace(0.0, 1.0, L, dtype=np.float64)[:, None]
    ang = ((2.0 * math.pi / L) * np.arange(L, dtype=np.float64))[:, None]
    bands = np.linspace(1e-4, HY_BANDS - 1, HY_BANDS, dtype=np.float64)[None, :]
    ba = bands * ang
    z = np.concatenate([t, np.cos(ba), -np.sin(ba)], axis=-1).astype(np.float32)
    return np.pad(z, ((0, 0), (0, LANES - HY_EMB)))


def _hy_filter_kernel(L, z_ref, w1_ref, b1_ref, w2_ref, b2_ref, w3_ref, fr_ref, rc_ref, rs_ref):
    z = z_ref[...]
    h = jnp.sin(fr_ref[0:1, :] * (_dot3(z, w1_ref[...]) + b1_ref[...]))
    h = jnp.sin(fr_ref[1:2, :] * (_dot3(h, w2_ref[...]) + b2_ref[...]))
    h = _dot3(h, w3_ref[...])
    t = z[:, 0:1]
    step = (HY_MAX_DECAY - HY_MIN_DECAY) / (D_HY - 1)
    deltas = HY_MIN_DECAY + step * lax.broadcasted_iota(jnp.int32, (1, D_HY), 1).astype(F32)
    window = jnp.exp(-t * jnp.abs(deltas))
    not_first = lax.broadcasted_iota(jnp.int32, (L, 1), 0) > 0
    for o in range(2):
        hf = h[:, (2 * o) * D_HY:(2 * o + 1) * D_HY] * window
        hb = jnp.where(not_first, h[:, (2 * o + 1) * D_HY:(2 * o + 2) * D_HY] * window, 0.0)
        rc_ref[:, o * D_HY:(o + 1) * D_HY] = hf + hb
        rs_ref[:, o * D_HY:(o + 1) * D_HY] = hb - hf


def _hy_spectrum_kernel(c_ref, s_ref, rc_ref, rs_ref, w_ref, kre_ref, kim_ref):
    rch, rcl = _split(rc_ref[...])
    rsh, rsl = _split(rs_ref[...])
    c, s, w = c_ref[...], s_ref[...], w_ref[...]
    kre_ref[...] = (_dot(c, rch) + _dot(c, rcl)) * w
    kim_ref[...] = (_dot(s, rsh) + _dot(s, rsl)) * w


def _hy_spectra(L, cmat, smat, w1, b1, w2, b2, w3, freq):
    z = jnp.asarray(_hy_features(L))
    w1p = jnp.pad(w1, ((0, LANES - HY_EMB), (0, 0)))
    nw = 2 * D_HY
    rc, rs = pl.pallas_call(
        functools.partial(_hy_filter_kernel, L),
        out_shape=[jax.ShapeDtypeStruct((L, nw), F32)] * 2,
        compiler_params=pltpu.CompilerParams(vmem_limit_bytes=VMEM_LIMIT),
        name="hy_filter",
    )(z, w1p, b1.reshape(1, HY_FFN), w2, b2.reshape(1, HY_FFN), w3, freq)
    n = 2 * L - 1
    wsc = np.full((L, 1), 2.0 / n, np.float32)
    wsc[0, 0] = 1.0 / n
    tr = min(L, 256)
    return pl.pallas_call(
        _hy_spectrum_kernel,
        grid=(L // tr,),
        in_specs=[
            pl.BlockSpec((tr, L), lambda i: (i, 0)),
            pl.BlockSpec((tr, L), lambda i: (i, 0)),
            _const_spec((L, nw)),
            _const_spec((L, nw)),
            pl.BlockSpec((tr, 1), lambda i: (i, 0)),
        ],
        out_specs=[pl.BlockSpec((tr, nw), lambda i: (i, 0))] * 2,
        out_shape=[jax.ShapeDtypeStruct((L, nw), F32)] * 2,
        compiler_params=_cparams("parallel"),
        name="hy_spectrum",
    )(cmat, smat, rc, rs, jnp.asarray(wsc))


def _hyena_kernel(L, tr, p_ref, sw_ref, sb_ref, bias_ref, c_ref, s_ref, kre_ref, kim_ref, o_ref,
                  pad_ref, u_ref, sig_ref, sig16_ref, zre_ref, zim_ref):
    C3 = 3 * D_HY
    zeros = jnp.zeros((8, C3), F32)
    pad_ref[0:8, :] = zeros
    pad_ref[8 + L:16 + L, :] = zeros
    chunks = [slice(r0, r0 + tr) for r0 in range(0, L, tr)]
    for c in chunks:
        pad_ref[8 + c.start:8 + c.stop, :] = p_ref[c, :].astype(F32)
    for c in chunks:
        u = sb_ref[...]
        for j in range(3):
            u = u + pad_ref[7 + j + c.start:7 + j + c.stop, :] * sw_ref[j:j + 1, :]
        u_ref[c, :] = u[:, D_HY:C3]
        sig_ref[c, :] = u[:, 0:D_HY]
        sig16_ref[c, :] = u[:, 0:D_HY].astype(BF16)

    for o in range(2):
        ko = slice(o * D_HY, (o + 1) * D_HY)
        for c in chunks:
            ure = _dot(c_ref[c, :], sig16_ref[...])
            us = _dot(s_ref[c, :], sig16_ref[...])
            kre, kim = kre_ref[c, ko], kim_ref[c, ko]
            zre_ref[c, :] = (ure * kre + us * kim).astype(BF16)
            zim_ref[c, :] = (ure * kim - us * kre).astype(BF16)
        gate = slice(o * D_HY, (o + 1) * D_HY)
        for c in chunks:
            y = _dot(c_ref[c, :], zre_ref[...]) - _dot(s_ref[c, :], zim_ref[...])
            z = u_ref[c, gate] * (y + sig_ref[c, :] * bias_ref[o:o + 1, :])
            if o == 0:
                sig_ref[c, :] = z
                sig16_ref[c, :] = z.astype(BF16)
            else:
                o_ref[c, :] = z.astype(o_ref.dtype)


def _hyena(p_hy, B, L, cmat, smat, kre, kim, short_w, short_b, bias, tr=512):
    C3 = 3 * D_HY
    tr = min(tr, L)
    once = pl.Buffered(1)
    return pl.pallas_call(
        functools.partial(_hyena_kernel, L, tr),
        grid=(B,),
        in_specs=[
            pl.BlockSpec((L, C3), lambda b: (b, 0)),
            _const_spec((3, C3)),
            _const_spec((1, C3)),
            _const_spec((2, D_HY)),
            pl.BlockSpec((L, L), lambda b: (0, 0), pipeline_mode=once),
            pl.BlockSpec((L, L), lambda b: (0, 0), pipeline_mode=once),
            pl.BlockSpec((L, 2 * D_HY), lambda b: (0, 0), pipeline_mode=once),
            pl.BlockSpec((L, 2 * D_HY), lambda b: (0, 0), pipeline_mode=once),
        ],
        out_specs=pl.BlockSpec((L, D_HY), lambda b: (b, 0)),
        out_shape=jax.ShapeDtypeStruct((B * L, D_HY), BF16),
        scratch_shapes=[
            pltpu.VMEM((L + 16, C3), F32),
            pltpu.VMEM((L, 2 * D_HY), F32),
            pltpu.VMEM((L, D_HY), F32),
            pltpu.VMEM((L, D_HY), BF16),
            pltpu.VMEM((L, D_HY), BF16),
            pltpu.VMEM((L, D_HY), BF16),
        ],
        compiler_params=_cparams("parallel"),
        name="hyena",
    )(p_hy, short_w, short_b.reshape(1, C3), bias, cmat, smat, kre, kim)


def _softplus(z):
    return jnp.maximum(z, 0.0) + jnp.log1p(jnp.exp(-jnp.abs(z)))


def _sigmoid_tanh(x):
    return 0.5 + 0.5 * jnp.tanh(0.5 * x)


def _gelu_tanh(x):
    return 0.5 * x * (1.0 + jnp.tanh(math.sqrt(2.0 / math.pi) * (x + 0.044715 * x * x * x)))


def _rglru_kernel(L, has_state, pg_ref, px_ref, cw_ref, cb_ref, wh_ref, wl_ref, gb_ref, lam_ref, *rest):
    if has_state:
        st_ref, y_ref, pad_ref, a_ref, b_ref, h_ref = rest
    else:
        y_ref, st_out_ref, pad_ref, a_ref, b_ref, h_ref = rest
    C = D_RG
    zeros = jnp.zeros((8, C), F32)
    pad_ref[0:8, :] = zeros
    pad_ref[8 + L:16 + L, :] = zeros
    pad_ref[8:8 + L, :] = px_ref[...]
    sp = _softplus(-lam_ref[...])
    tr = min(L, 256)
    for r0 in range(0, L, tr):
        xr = cb_ref[...]
        for j in range(4):
            xr = xr + pad_ref[6 + j + r0:6 + j + r0 + tr, :] * cw_ref[j:j + 1, :]
        xh, xl = _split(xr)
        for d in range(2):
            g = []
            for m in range(2):
                cols = slice((2 * d + m) * C, (2 * d + m + 1) * C)
                wh = wh_ref[:, cols]
                g.append(_sigmoid_tanh(_dot(xh, wh) + _dot(xl, wh) + _dot(xh, wl_ref[:, cols]) + gb_ref[:, cols]))
            log_a = -RG_C * g[0] * sp[d:d + 1, :]
            a = jnp.exp(log_a)
            a_ref[d, r0:r0 + tr, :] = a
            b_ref[d, r0:r0 + tr, :] = jnp.sqrt(-jnp.tanh(log_a) * (1.0 + a * a)) * (g[1] * xr)

    if has_state:
        h0f, h0b = st_ref[0:1, :], st_ref[1:2, :]
    else:
        h0f = h0b = jnp.zeros((1, C), F32)

    def step(t, carry):
        hf, hb = carry
        tb = L - 1 - t
        hf = a_ref[0, pl.ds(t, 1), :] * hf + b_ref[0, pl.ds(t, 1), :]
        hb = a_ref[1, pl.ds(tb, 1), :] * hb + b_ref[1, pl.ds(tb, 1), :]
        h_ref[0, pl.ds(t, 1), :] = hf
        h_ref[1, pl.ds(tb, 1), :] = hb
        return hf, hb

    lax.fori_loop(0, L, step, (h0f, h0b), unroll=8)
    y_ref[...] = ((h_ref[0] + h_ref[1]) * _gelu_tanh(pg_ref[...])).astype(y_ref.dtype)
    if not has_state:
        st_out_ref[0:1, :] = h_ref[0, L - 1:L, :]
        st_out_ref[1:2, :] = h_ref[1, 0:1, :]


def _block_diag(w):
    H, d, _ = w.shape
    eye = jnp.eye(H, dtype=w.dtype)
    return (eye[:, None, :, None] * w[:, :, None, :]).reshape(H * d, H * d)


def _rglru(p_g, p_x, B, L, conv_w, conv_b, wa, ba, wx, bx, lam, state):
    C = D_RG
    wcat = jnp.concatenate([_block_diag(wa[0]), _block_diag(wx[0]), _block_diag(wa[1]), _block_diag(wx[1])], axis=1)
    wh = wcat.astype(BF16)
    wl = (wcat - wh.astype(F32)).astype(BF16)
    gb = jnp.concatenate([ba[0], bx[0], ba[1], bx[1]]).reshape(1, 4 * C)
    has_state = state is not None
    in_specs = [
        pl.BlockSpec((L, C), lambda b: (b, 0)),
        pl.BlockSpec((L, C), lambda b: (b, 0)),
        _const_spec((4, C)),
        _const_spec((1, C)),
        _const_spec((C, 4 * C)),
        _const_spec((C, 4 * C)),
        _const_spec((1, 4 * C)),
        _const_spec((2, C)),
    ]
    args = [p_g, p_x, conv_w, conv_b.reshape(1, C), wh, wl, gb, lam]
    y_spec = pl.BlockSpec((L, C), lambda b: (b, 0))
    y_shape = jax.ShapeDtypeStruct((B * L, C), BF16)
    if has_state:
        in_specs.append(pl.BlockSpec((None, 2, C), lambda b: (b, 0, 0)))
        args.append(state)
        out_specs, out_shape = y_spec, y_shape
    else:
        out_specs = [y_spec, pl.BlockSpec((None, 2, C), lambda b: (b, 0, 0))]
        out_shape = [y_shape, jax.ShapeDtypeStruct((B, 2, C), F32)]
    return pl.pallas_call(
        functools.partial(_rglru_kernel, L, has_state),
        grid=(B,),
        in_specs=in_specs,
        out_specs=out_specs,
        out_shape=out_shape,
        scratch_shapes=[
            pltpu.VMEM((L + 16, C), F32),
            pltpu.VMEM((2, L, C), F32),
            pltpu.VMEM((2, L, C), F32),
            pltpu.VMEM((2, L, C), F32),
        ],
        compiler_params=_cparams("parallel"),
        name="rglru_state" if has_state else "rglru",
    )(*args)


def _attn_kernel(L, P, tq, unroll, lam_init, q_ref, k_ref, v_ref, *rest):
    if P:
        ck_ref, cv_ref, dal_ref, sub_ref, o_ref, kk_ref, vv_ref = rest
    else:
        dal_ref, sub_ref, o_ref, kk_ref, vv_ref = rest
    lv = dal_ref[...]
    s01 = jnp.sum(lv[0:1, :] * lv[1:2, :], axis=-1, keepdims=True)
    s23 = jnp.sum(lv[2:3, :] * lv[3:4, :], axis=-1, keepdims=True)
    lam = jnp.exp(s01) - jnp.exp(s23) + lam_init
    first_half = lax.broadcasted_iota(jnp.int32, (1, DA_VDIM), 1) < DA_HEAD
    sub = sub_ref[...] * (1.0 - lam_init)
    for hd in range(N_DA_HEADS):
        if P:
            kk_ref[0:P, :] = ck_ref[hd].astype(BF16)
            vv_ref[0:P, :] = cv_ref[hd].astype(BF16)
        kk_ref[P:P + L, :] = k_ref[hd].astype(BF16)
        vv_ref[P:P + L, :] = v_ref[hd].astype(BF16)

        def qblock(i, carry):
            r0 = pl.multiple_of(i * tq, tq)
            q = q_ref[hd, pl.ds(r0, tq), :]
            zero = jnp.zeros_like(q)
            qs = jnp.concatenate([jnp.where(first_half, q, zero), jnp.where(first_half, zero, q)], axis=0)
            s = _dot_nt(qs, kk_ref[...])
            p = jnp.exp2(s - jnp.max(s, axis=-1, keepdims=True))
            rinv = 1.0 / jnp.sum(p, axis=-1, keepdims=True)
            acc = _dot(p.astype(BF16), vv_ref[...])
            o = acc[0:tq] * rinv[0:tq] - acc[tq:2 * tq] * (lam * rinv[tq:2 * tq])
            o = o * lax.rsqrt(jnp.mean(o * o, axis=-1, keepdims=True) + EPS) * sub
            o_ref[pl.ds(r0, tq), hd * DA_VDIM:(hd + 1) * DA_VDIM] = o.astype(o_ref.dtype)
            return carry

        lax.fori_loop(0, L // tq, qblock, 0, unroll=unroll)


def _attention(q, k, v, layer, cache, dal, subln, lam_init, B, L, tq=128, unroll=4):
    H, dv = N_DA_HEADS, DA_VDIM
    hspec = pl.BlockSpec((None, H, L, dv), lambda b: (b, 0, 0, 0))
    kvspec = hspec if k.ndim == 4 else pl.BlockSpec((None, None, H, L, dv), lambda b: (b, layer, 0, 0, 0))
    in_specs = [hspec, kvspec, kvspec]
    args = [q, k, v]
    P = 0
    if cache is not None:
        ck, cv = cache
        P = ck.shape[3]
        cspec = pl.BlockSpec((None, None, H, P, dv), lambda b: (b, layer, 0, 0, 0))
        in_specs += [cspec, cspec]
        args += [ck, cv]
    assert L % tq == 0
    in_specs += [_const_spec((4, DA_HEAD)), _const_spec((1, dv))]
    args += [dal, subln.reshape(1, dv)]
    return pl.pallas_call(
        functools.partial(_attn_kernel, L, P, tq, min(unroll, L // tq), lam_init),
        grid=(B,),
        in_specs=in_specs,
        out_specs=pl.BlockSpec((L, H * dv), lambda b: (b, 0)),
        out_shape=jax.ShapeDtypeStruct((B * L, H * dv), BF16),
        scratch_shapes=[pltpu.VMEM((P + L, dv), BF16), pltpu.VMEM((P + L, dv), BF16)],
        compiler_params=_cparams("parallel"),
        name="diff_attn_cache" if P else "diff_attn",
    )(*args)


def _route(logits):
    m = logits[0]
    for e in range(1, N_EXPERTS):
        m = jnp.maximum(m, logits[e])
    ex = [jnp.exp(l - m) for l in logits]
    tot = ex[0]
    for e in range(1, N_EXPERTS):
        tot = tot + ex[e]
    inv = 1.0 / tot
    p = [e_ * inv for e_ in ex]
    G = EXP_PER_GROUP
    best, gsel = None, None
    for g in range(N_GROUPS):
        a = p[g * G:(g + 1) * G]
        sc = None
        for i in range(G):
            for j in range(i + 1, G):
                pair = a[i] + a[j]
                sc = pair if sc is None else jnp.maximum(sc, pair)
        if g == 0:
            best, gsel = sc, jnp.zeros_like(sc, dtype=jnp.int32)
        else:
            upd = sc > best
            best = jnp.where(upd, sc, best)
            gsel = jnp.where(upd, g, gsel)
    vals = []
    for j in range(G):
        vj = p[j]
        for g in range(1, N_GROUPS):
            vj = jnp.where(gsel == g, p[g * G + j], vj)
        vals.append(vj)
    p1, i1 = vals[0], jnp.zeros_like(gsel)
    for j in range(1, G):
        upd = vals[j] > p1
        p1 = jnp.where(upd, vals[j], p1)
        i1 = jnp.where(upd, j, i1)
    p2, i2 = None, None
    for j in range(G):
        cand = jnp.where(i1 == j, -1.0, vals[j])
        if p2 is None:
            p2, i2 = cand, jnp.zeros_like(gsel)
        else:
            upd = cand > p2
            p2 = jnp.where(upd, cand, p2)
            i2 = jnp.where(upd, j, i2)
    den = 1.0 / (p1 + p2)
    w1, w2 = p1 * den, p2 * den
    swap = i2 < i1
    a, b = jnp.where(swap, i2, i1), jnp.where(swap, i1, i2)
    w_lo, w_hi = jnp.where(swap, w2, w1), jnp.where(swap, w1, w2)
    pair = jnp.where(a == 0, b - 1, jnp.where(a == 1, b + 1, 5))
    cls = gsel * PAIRS_PER_GROUP + pair
    return cls.astype(F32), w_lo, w_hi


def _pack_pairs(x):
    n = x.shape[1] // 2
    b = pltpu.bitcast(x, jnp.uint32)
    w = (b[:, :n] >> 16) | (b[:, n:] & jnp.uint32(0xFFFF0000))
    return pltpu.bitcast(w, jnp.int32)


def _unpack_pairs(w):
    u = pltpu.bitcast(w, jnp.uint32)
    lo = pltpu.bitcast(u << 16, F32)
    hi = pltpu.bitcast(u & jnp.uint32(0xFFFF0000), F32)
    return jnp.concatenate([lo, hi], axis=1)


def _out_proj_kernel(yh_ref, yr_ref, o_ref, w_ref, x_ref, mod_ref, g_ref, wrh_ref, wrl_ref, br_ref,
                     xo_ref, h_ref, route_ref):
    y = (_dot(yh_ref[...], w_ref[0:D_HY, :]) + _dot(yr_ref[...], w_ref[D_HY:D_HY + D_RG, :])
         + _dot(o_ref[...], w_ref[D_HY + D_RG:D_MIX, :]))
    x = x_ref[...] + mod_ref[2:3, :] * y
    xo_ref[...] = x
    ms = jnp.mean(x * x, axis=-1, keepdims=True)
    h = (x * lax.rsqrt(ms + EPS) * g_ref[...]) * (1.0 + mod_ref[4:5, :]) + mod_ref[3:4, :]
    hh, hl = _split(h)
    h_ref[...] = _pack_pairs(hh.astype(F32))
    lg = _dot_nt(wrh_ref[...], hh) + _dot_nt(wrh_ref[...], hl) + _dot_nt(wrl_ref[...], hh) + br_ref[...]
    info = _route([lg[e:e + 1, :] for e in range(N_EXPERTS)])
    rt = jnp.concatenate(list(info) + [jnp.zeros((LANES - len(info), lg.shape[1]), F32)], axis=0)
    route_ref[...] = rt.T


def _out_proj(y_hy, y_rg, o, w_out, layer, x, mod, g2, w_router, b_router, B, L, ctx_rows, tm=512):
    T = B * L
    tm = min(tm, L)
    nl = L // tm
    row = (lambda i: CTX_ROW) if ctx_rows else (lambda i: i // nl)
    wrt = w_router.T
    wrh = wrt.astype(BF16)
    wrl = (wrt - wrh.astype(F32)).astype(BF16)
    rows = lambda w: pl.BlockSpec((tm, w), lambda i: (i, 0))
    return pl.pallas_call(
        _out_proj_kernel,
        grid=(T // tm,),
        in_specs=[
            rows(D_HY), rows(D_RG), rows(D_DA),
            pl.BlockSpec((None, D_MIX, D_MODEL), lambda i: (layer, 0, 0)),
            rows(D_MODEL),
            pl.BlockSpec((None, 6, D_MODEL), lambda i: (row(i), 0, 0)),
            _const_spec((1, D_MODEL)),
            _const_spec((N_EXPERTS, D_MODEL)),
            _const_spec((N_EXPERTS, D_MODEL)),
            _const_spec((N_EXPERTS, 1)),
        ],
        out_specs=[rows(D_MODEL), rows(D_MODEL // 2), rows(LANES)],
        out_shape=[
            jax.ShapeDtypeStruct((T, D_MODEL), F32),
            jax.ShapeDtypeStruct((T, D_MODEL // 2), jnp.int32),
            jax.ShapeDtypeStruct((T, LANES), F32),
        ],
        compiler_params=_cparams("parallel"),
        name="out_proj_route",
    )(y_hy, y_rg, o, w_out, x, mod, g2.reshape(1, D_MODEL), wrh, wrl, b_router.reshape(N_EXPERTS, 1))


def _gather_rows(table, idx, rows_per_step=64, n_buf=2):
    info = plsc.get_sparse_core_info()
    n_workers = info.num_cores * info.num_subcores
    n, width = idx.shape[0], table.shape[1]
    per_worker = n // n_workers
    n_steps = per_worker // rows_per_step
    assert per_worker * n_workers == n and n_steps * rows_per_step == per_worker and n_steps >= n_buf
    mesh = plsc.VectorSubcoreMesh(core_axis_name="c", subcore_axis_name="s")

    @functools.partial(
        pl.kernel, mesh=mesh,
        out_type=jax.ShapeDtypeStruct((n, width), table.dtype),
        scratch_types=[
            pltpu.VMEM((per_worker,), jnp.int32),
            pltpu.VMEM((n_buf, rows_per_step, width), table.dtype),
            pltpu.SemaphoreType.DMA((n_buf,)),
            pltpu.SemaphoreType.DMA((n_buf,)),
        ],
    )
    def gather(table_hbm, idx_hbm, out_hbm, idx_v, rows_v, sem_in, sem_out):
        worker = lax.axis_index("s") * info.num_cores + lax.axis_index("c")
        base = pl.multiple_of(worker * per_worker, per_worker)
        pltpu.sync_copy(idx_hbm.at[pl.ds(base, per_worker)], idx_v)

        def read(b, step):
            rows = idx_v.at[pl.ds(step * rows_per_step, rows_per_step)]
            return pltpu.make_async_copy(table_hbm.at[rows], rows_v.at[b], sem_in.at[b])

        def write(b, step):
            off = pl.multiple_of(base + step * rows_per_step, rows_per_step)
            return pltpu.make_async_copy(rows_v.at[b], out_hbm.at[pl.ds(off, rows_per_step)], sem_out.at[b])

        for step in range(n_steps + 1):
            if step < n_steps:
                if step >= n_buf:
                    write(step % n_buf, step - n_buf).wait()
                read(step % n_buf, step).start()
            if step >= 1:
                read((step - 1) % n_buf, step - 1).wait()
                write((step - 1) % n_buf, step - 1).start()
        for step in range(n_steps - n_buf, n_steps):
            write(step % n_buf, step).wait()

    return gather(table, idx)


def _dispatch_plan(route, tm):
    T = route.shape[0]
    n_slots = T + N_CLASSES * tm
    cls = route[:, 0].astype(jnp.int32)
    onehot = (cls[:, None] == jnp.arange(N_CLASSES, dtype=jnp.int32)[None, :]).astype(jnp.int32)
    csum = jnp.cumsum(onehot, axis=0)
    rank = jnp.sum(onehot * csum, axis=1) - 1
    counts = csum[-1]
    padded = ((counts + tm - 1) // tm) * tm
    ends = jnp.cumsum(padded)
    pos = jnp.sum(onehot * (ends - padded)[None, :], axis=1) + rank
    inv = (jnp.arange(n_slots, dtype=jnp.int32) % T).at[pos].set(jnp.arange(T, dtype=jnp.int32), unique_indices=True)
    tile_start = jnp.arange(n_slots // tm, dtype=jnp.int32) * tm
    tile_cls = jnp.minimum(jnp.searchsorted(ends, tile_start, side="right"), N_CLASSES - 1).astype(jnp.int32)
    valid = (tile_start < ends[-1]).astype(jnp.int32)
    pairs = np.array([(a, b) for a in range(EXP_PER_GROUP) for b in range(a + 1, EXP_PER_GROUP)], np.int32)
    group, pair = tile_cls // PAIRS_PER_GROUP, tile_cls % PAIRS_PER_GROUP
    lo = group * EXP_PER_GROUP + jnp.asarray(pairs[:, 0])[pair]
    hi = group * EXP_PER_GROUP + jnp.asarray(pairs[:, 1])[pair]
    return pos, inv, lo, hi, valid


def _moe_sorted_kernel(lo_ref, hi_ref, valid_ref, xs_ref, ws_ref, wg_lo, wu_lo, wd_lo, wg_hi, wu_hi, wd_hi, o_ref):
    i = pl.program_id(0)

    @pl.when(valid_ref[i] == 1)
    def _():
        x = _unpack_pairs(xs_ref[...]).astype(BF16)
        y = None
        for wg, wu, wd, col in ((wg_lo, wu_lo, wd_lo, 1), (wg_hi, wu_hi, wd_hi, 2)):
            a = _dot(x, wg[...])
            he = (a * _sigmoid(a)) * _dot(x, wu[...]) * ws_ref[:, col:col + 1]
            part = _dot(he.astype(BF16), wd[...])
            y = part if y is None else y + part
        o_ref[...] = _pack_pairs(y.astype(BF16).astype(F32))

    @pl.when(valid_ref[i] == 0)
    def _():
        o_ref[...] = jnp.zeros_like(o_ref)


def _moe_sorted(xs, ws, lo, hi, valid, wg, wu, wd, tm):
    n_slots = xs.shape[0]
    half = D_MODEL // 2
    up = lambda sel: pl.BlockSpec((None, D_MODEL, D_EXPERT), lambda i, lo, hi, v: ((lo, hi)[sel][i], 0, 0))
    down = lambda sel: pl.BlockSpec((None, D_EXPERT, D_MODEL), lambda i, lo, hi, v: ((lo, hi)[sel][i], 0, 0))
    return pl.pallas_call(
        _moe_sorted_kernel,
        grid_spec=pltpu.PrefetchScalarGridSpec(
            num_scalar_prefetch=3,
            grid=(n_slots // tm,),
            in_specs=[
                pl.BlockSpec((tm, half), lambda i, lo, hi, v: (i, 0)),
                pl.BlockSpec((tm, LANES), lambda i, lo, hi, v: (i, 0)),
                up(0), up(0), down(0), up(1), up(1), down(1),
            ],
            out_specs=pl.BlockSpec((tm, half), lambda i, lo, hi, v: (i, 0)),
        ),
        out_shape=jax.ShapeDtypeStruct((n_slots, half), jnp.int32),
        compiler_params=_cparams("arbitrary"),
        name="moe_sorted",
    )(lo, hi, valid, xs, ws, wg, wu, wd, wg, wu, wd)


def _moe_residual_kernel(final, y_ref, x_ref, mod_ref, fg_ref, o_ref):
    x = x_ref[...] + mod_ref[5:6, :] * _unpack_pairs(y_ref[...])
    if final:
        x = x * lax.rsqrt(jnp.mean(x * x, axis=-1, keepdims=True) + EPS) * fg_ref[...]
    o_ref[...] = x


def _moe_residual(y, x, mod, final_g, final, B, L, ctx_rows, tm=512):
    T = B * L
    tm = min(tm, L)
    nl = L // tm
    row = (lambda i: CTX_ROW) if ctx_rows else (lambda i: i // nl)
    return pl.pallas_call(
        functools.partial(_moe_residual_kernel, final),
        grid=(T // tm,),
        in_specs=[
            pl.BlockSpec((tm, D_MODEL // 2), lambda i: (i, 0)),
            pl.BlockSpec((tm, D_MODEL), lambda i: (i, 0)),
            pl.BlockSpec((None, 6, D_MODEL), lambda i: (row(i), 0, 0)),
            _const_spec((1, D_MODEL)),
        ],
        out_specs=pl.BlockSpec((tm, D_MODEL), lambda i: (i, 0)),
        out_shape=jax.ShapeDtypeStruct((T, D_MODEL), F32),
        compiler_params=_cparams("parallel"),
        name="moe_residual_final" if final else "moe_residual",
    )(y, x, mod, final_g.reshape(1, D_MODEL))


def _moe(h, route, wg, wu, wd, layer, x, mod, final_g, final, B, L, ctx_rows, tm=256):
    pos, inv, lo, hi, valid = _dispatch_plan(route, tm)
    xs = _gather_rows(h, inv)
    ws = _gather_rows(route, inv)
    ys = _moe_sorted(xs, ws, lo + layer * N_EXPERTS, hi + layer * N_EXPERTS, valid, wg, wu, wd, tm)
    y = _gather_rows(ys, pos)
    return _moe_residual(y, x, mod, final_g, final, B, L, ctx_rows)


def kernel(x_prompt, x_sample, cache_k, cache_v, state_rglru, c, c_ctx, w_ada, b_ada, norm1_g, norm2_g, w_in, w_out, hy_short_w, hy_short_b, hy_w1, hy_b1, hy_w2, hy_b2, hy_w3, hy_freq, hy_bias, rg_conv_w, rg_conv_b, rg_wa, rg_ba, rg_wx, rg_bx, rg_lambda, da_lambda, da_subln, w_router, b_router, moe_wg, moe_wu, moe_wd, final_g):
    Bp, Lp, D = x_prompt.shape
    Bs, Ls, _ = x_sample.shape
    assert Bs <= CTX_ROW
    cond = jnp.zeros((N_COND, D), F32).at[:Bs].set(c).at[CTX_ROW].set(c_ctx)
    mods = _ada_table(cond, w_ada, b_ada)

    dft = {L: tuple(jnp.asarray(m).astype(BF16) for m in _dft_mats(L)) for L in (Lp, Ls)}
    streams = [
        dict(B=Bp, L=Lp, ctx=True, x=x_prompt.reshape(Bp * Lp, D)),
        dict(B=Bs, L=Ls, ctx=False, x=x_sample.reshape(Bs * Ls, D)),
    ]
    w_in_b, w_out_b = w_in.astype(BF16), w_out.astype(BF16)
    wg, wu, wd = (w.astype(BF16).reshape((DEPTH * N_EXPERTS,) + w.shape[2:]) for w in (moe_wg, moe_wu, moe_wd))
    new_kv, ss = None, []
    for l in range(DEPTH):
        lam_init = 0.8 - 0.6 * math.exp(-0.3 * l)
        final = l == DEPTH - 1
        for st in streams:
            B, L, ctx = st["B"], st["L"], st["ctx"]
            cmat, smat = dft[L]
            p_hy, p_g, p_x, q, k, v = _norm_proj(
                st["x"], mods[l], norm1_g[l], w_in_b, l, B, L, ctx, kv_prev=new_kv if ctx else None)
            kre, kim = _hy_spectra(L, cmat, smat, hy_w1[l], hy_b1[l], hy_w2[l], hy_b2[l], hy_w3[l], hy_freq[l])
            y_hy = _hyena(p_hy, B, L, cmat, smat, kre, kim, hy_short_w[l], hy_short_b[l], hy_bias[l])
            rg_args = (rg_conv_w[l], rg_conv_b[l], rg_wa[l], rg_ba[l], rg_wx[l], rg_bx[l], rg_lambda[l])
            if ctx:
                y_rg, s_l = _rglru(p_g, p_x, B, L, *rg_args, None)
                o = _attention(q, k, v, l, None, da_lambda[l], da_subln[l], lam_init, B, L)
                new_kv = (k, v)
                ss.append(s_l)
            else:
                y_rg = _rglru(p_g, p_x, B, L, *rg_args, state_rglru[:, l])
                o = _attention(q, k, v, l, (cache_k, cache_v), da_lambda[l], da_subln[l], lam_init, B, L)
            x_mid, h2, route = _out_proj(y_hy, y_rg, o, w_out_b, l, st["x"], mods[l], norm2_g[l],
                                         w_router, b_router, B, L, ctx)
            st["x"] = _moe(h2, route, wg, wu, wd, l, x_mid, mods[l], final_g, final, B, L, ctx)
    y_prompt = streams[0]["x"].reshape(Bp, Lp, D)
    y_sample = streams[1]["x"].reshape(Bs, Ls, D)
    return (y_prompt, y_sample, new_kv[0], new_kv[1], jnp.stack(ss, axis=1))
```

```python
import functools
import math

import numpy as np
import jax
import jax.numpy as jnp
from jax import lax
from jax.experimental import pallas as pl
from jax.experimental.pallas import tpu as pltpu
from jax.experimental.pallas import tpu_sc as plsc

F32 = jnp.float32
BF16 = jnp.bfloat16

D_MODEL = 1024
DEPTH = 2
GRID_W = 64
D_HY = 256
HY_EMB = 33
HY_BANDS = (HY_EMB - 1) // 2
HY_FFN = 64
HY_MIN_DECAY = math.log(1e-2) / 1.5
HY_MAX_DECAY = math.log(1e-2) / 0.3
D_RG = 256
N_RG_HEADS = 4
RG_C = 8.0
N_DA_HEADS = 4
DA_HEAD = 64
DA_VDIM = 2 * DA_HEAD
D_DA = N_DA_HEADS * DA_VDIM
D_MIX = D_HY + D_RG + D_DA
D_IN = 3 * D_HY + 2 * D_RG + 3 * D_DA
ROPE_PAIRS = DA_HEAD // 4
ROPE_THETA = 10000.0
N_EXPERTS = 16
N_GROUPS = 4
EXP_PER_GROUP = N_EXPERTS // N_GROUPS
D_EXPERT = 512
PAIRS_PER_GROUP = EXP_PER_GROUP * (EXP_PER_GROUP - 1) // 2
N_CLASSES = N_GROUPS * PAIRS_PER_GROUP
EPS = 1e-6
N_COND = 16
CTX_ROW = 8
LANES = 128
SUBLANES = 8
VMEM_LIMIT = 56 * 1024 * 1024


def _cparams(*sem):
    return pltpu.CompilerParams(dimension_semantics=sem, vmem_limit_bytes=VMEM_LIMIT)


def _split(x):
    hi = x.astype(BF16)
    lo = (x - hi.astype(F32)).astype(BF16)
    return hi, lo


def _dot(a, b):
    return jnp.dot(a, b, preferred_element_type=F32)


def _dot3(a, b):
    ah, al = _split(a)
    bh, bl = _split(b)
    return _dot(ah, bh) + _dot(al, bh) + _dot(ah, bl)


def _dot_nt(a, b):
    return lax.dot_general(a, b, (((1,), (1,)), ((), ())), preferred_element_type=F32)


def _sigmoid(x):
    return 1.0 / (1.0 + jnp.exp(-x))


def _const_spec(shape):
    n = len(shape)
    return pl.BlockSpec(shape, lambda *_: (0,) * n)


def _ada_kernel(c_ref, w_ref, b_ref, o_ref):
    c = c_ref[...]
    s = c * _sigmoid(c)
    o_ref[...] = _dot3(s, w_ref[...]) + b_ref[...]


def _ada_table(cond, w_ada, b_ada):
    D = D_MODEL
    out = pl.pallas_call(
        _ada_kernel,
        grid=(DEPTH, 6),
        in_specs=[
            pl.BlockSpec((N_COND, D), lambda l, j: (0, 0)),
            pl.BlockSpec((None, D, D), lambda l, j: (l, 0, j)),
            pl.BlockSpec((None, None, 1, D), lambda l, j: (l, j, 0, 0)),
        ],
        out_specs=pl.BlockSpec((None, None, N_COND, D), lambda l, j: (l, j, 0, 0)),
        out_shape=jax.ShapeDtypeStruct((DEPTH, 6, N_COND, D), F32),
        compiler_params=_cparams("parallel", "parallel"),
        name="ada_table",
    )(cond, w_ada, b_ada.reshape(DEPTH, 6, 1, D))
    return out.transpose(0, 2, 1, 3)


def _rope_tables(L):
    t = np.arange(L)
    j = np.arange(LANES)
    jj = j % DA_HEAD
    is_col = (jj // (DA_HEAD // 2)) == 1
    pair = jj % ROPE_PAIRS
    second = (jj % (DA_HEAD // 2)) >= ROPE_PAIRS
    inv = ROPE_THETA ** (-np.arange(ROPE_PAIRS, dtype=np.float64) / ROPE_PAIRS)
    pos = np.where(is_col[None, :], (t % GRID_W)[:, None], (t // GRID_W)[:, None]).astype(np.float64)
    ang = pos * inv[pair][None, :]
    cos = np.cos(ang).astype(np.float32)
    sin = np.sin(ang).astype(np.float32)
    sin_a = np.where(second[None, :], 0.0, -sin).astype(np.float32)
    sin_b = np.where(second[None, :], sin, 0.0).astype(np.float32)
    return cos, sin_a, sin_b


def _rope(x, cos, sin_a, sin_b):
    nxt = pltpu.roll(x, LANES - ROPE_PAIRS, axis=1)
    prv = pltpu.roll(x, ROPE_PAIRS, axis=1)
    return x * cos + nxt * sin_a + prv * sin_b


def _norm_proj_kernel(rope, kv_dtype, x_ref, mod_ref, g_ref, w_ref, *rest):
    if rope:
        cos_ref, sa_ref, sb_ref = rest[:3]
    phy_ref, pg_ref, px_ref, q_ref, k_ref, v_ref = rest[-6:]
    x = x_ref[...]
    ms = jnp.mean(x * x, axis=-1, keepdims=True)
    y = x * lax.rsqrt(ms + EPS) * g_ref[...]
    h = (y * (1.0 + mod_ref[1:2, :]) + mod_ref[0:1, :]).astype(BF16)
    o = 3 * D_HY
    phy_ref[...] = _dot(h, w_ref[:, 0:o]).astype(BF16)
    pg_ref[...] = _dot(h, w_ref[:, o:o + D_RG])
    px_ref[...] = _dot(h, w_ref[:, o + D_RG:o + 2 * D_RG])
    o += 2 * D_RG
    q = _dot(h, w_ref[:, o:o + D_DA]) * (DA_HEAD ** -0.5 * math.log2(math.e))
    k = _dot(h, w_ref[:, o + D_DA:o + 2 * D_DA])
    v = _dot(h, w_ref[:, o + 2 * D_DA:o + 3 * D_DA])
    if rope:
        cos, sa, sb = cos_ref[...], sa_ref[...], sb_ref[...]
    for hd in range(N_DA_HEADS):
        sl = slice(hd * DA_VDIM, (hd + 1) * DA_VDIM)
        qh, kh = q[:, sl], k[:, sl]
        if rope:
            qh = _rope(qh, cos, sa, sb)
            kh = _rope(kh, cos, sa, sb)
        q_ref[hd] = qh.astype(BF16)
        k_ref[hd] = kh.astype(kv_dtype)
        v_ref[hd] = v[:, sl].astype(kv_dtype)


def _norm_proj(x, mod, g, w_in, layer, B, L, ctx, kv_prev=None, tm=512):
    T = B * L
    tm = min(tm, L)
    nl = L // tm
    rope, kv_dtype = not ctx, (F32 if ctx else BF16)
    row = (lambda i: CTX_ROW) if ctx else (lambda i: i // nl)
    in_specs = [
        pl.BlockSpec((tm, D_MODEL), lambda i: (i, 0)),
        pl.BlockSpec((None, 6, D_MODEL), lambda i: (row(i), 0, 0)),
        _const_spec((1, D_MODEL)),
        pl.BlockSpec((None, D_MODEL, D_IN), lambda i: (layer, 0, 0)),
    ]
    args = [x, mod, g.reshape(1, D_MODEL), w_in]
    if rope:
        tabs = _rope_tables(L)
        in_specs += [pl.BlockSpec((tm, LANES), lambda i: (i % nl, 0))] * 3
        args += [jnp.asarray(t) for t in tabs]
    head_spec = pl.BlockSpec((None, N_DA_HEADS, tm, DA_VDIM), lambda i: (i // nl, 0, i % nl, 0))
    head_shape = (B, N_DA_HEADS, L, DA_VDIM)
    kv_spec, kv_shape, aliases = head_spec, head_shape, {}
    if ctx:
        kv_spec = pl.BlockSpec((None, None, N_DA_HEADS, tm, DA_VDIM), lambda i: (i // nl, layer, 0, i % nl, 0))
        kv_shape = (B, DEPTH, N_DA_HEADS, L, DA_VDIM)
        if kv_prev is not None:
            aliases = {len(args): 4, len(args) + 1: 5}
            in_specs += [pl.BlockSpec(memory_space=pl.ANY)] * 2
            args += list(kv_prev)
    return pl.pallas_call(
        functools.partial(_norm_proj_kernel, rope, kv_dtype),
        grid=(T // tm,),
        in_specs=in_specs,
        out_specs=[
            pl.BlockSpec((tm, 3 * D_HY), lambda i: (i, 0)),
            pl.BlockSpec((tm, D_RG), lambda i: (i, 0)),
            pl.BlockSpec((tm, D_RG), lambda i: (i, 0)),
            head_spec, kv_spec, kv_spec,
        ],
        out_shape=[
            jax.ShapeDtypeStruct((T, 3 * D_HY), BF16),
            jax.ShapeDtypeStruct((T, D_RG), F32),
            jax.ShapeDtypeStruct((T, D_RG), F32),
            jax.ShapeDtypeStruct(head_shape, BF16),
            jax.ShapeDtypeStruct(kv_shape, kv_dtype),
            jax.ShapeDtypeStruct(kv_shape, kv_dtype),
        ],
        input_output_aliases=aliases,
        compiler_params=_cparams("parallel"),
        name="norm_proj_rope" if rope else "norm_proj",
    )(*args)


def _dft_mats(L):
    n = 2 * L - 1
    fs = (np.arange(L, dtype=np.int64)[:, None] * np.arange(L, dtype=np.int64)[None, :]) % n
    ang = fs.astype(np.float64) * (2.0 * np.pi / n)
    return np.cos(ang).astype(np.float32), np.sin(ang).astype(np.float32)


def _hy_features(L):
    t = np.linspace(0.0, 1.0, L, dtype=np.float64)[:, None]
    ang = ((2.0 * math.pi / L) * np.arange(L, dtype=np.float64))[:, None]
    bands = np.linspace(1e-4, HY_BANDS - 1, HY_BANDS, dtype=np.float64)[None, :]
    ba = bands * ang
    z = np.concatenate([t, np.cos(ba), -np.sin(ba)], axis=-1).astype(np.float32)
    return np.pad(z, ((0, 0), (0, LANES - HY_EMB)))


def _hy_filter_kernel(L, z_ref, w1_ref, b1_ref, w2_ref, b2_ref, w3_ref, fr_ref, rc_ref, rs_ref):
    z = z_ref[...]
    h = jnp.sin(fr_ref[0:1, :] * (_dot3(z, w1_ref[...]) + b1_ref[...]))
    h = jnp.sin(fr_ref[1:2, :] * (_dot3(h, w2_ref[...]) + b2_ref[...]))
    h = _dot3(h, w3_ref[...])
    t = z[:, 0:1]
    step = (HY_MAX_DECAY - HY_MIN_DECAY) / (D_HY - 1)
    deltas = HY_MIN_DECAY + step * lax.broadcasted_iota(jnp.int32, (1, D_HY), 1).astype(F32)
    window = jnp.exp(-t * jnp.abs(deltas))
    not_first = lax.broadcasted_iota(jnp.int32, (L, 1), 0) > 0
    for o in range(2):
        hf = h[:, (2 * o) * D_HY:(2 * o + 1) * D_HY] * window
        hb = jnp.where(not_first, h[:, (2 * o + 1) * D_HY:(2 * o + 2) * D_HY] * window, 0.0)
        rc_ref[:, o * D_HY:(o + 1) * D_HY] = hf + hb
        rs_ref[:, o * D_HY:(o + 1) * D_HY] = hb - hf


def _hy_spectrum_kernel(c_ref, s_ref, rc_ref, rs_ref, w_ref, kre_ref, kim_ref):
    rch, rcl = _split(rc_ref[...])
    rsh, rsl = _split(rs_ref[...])
    c, s, w = c_ref[...], s_ref[...], w_ref[...]
    kre_ref[...] = (_dot(c, rch) + _dot(c, rcl)) * w
    kim_ref[...] = (_dot(s, rsh) + _dot(s, rsl)) * w


def _hy_spectra(L, cmat, smat, w1, b1, w2, b2, w3, freq):
    z = jnp.asarray(_hy_features(L))
    w1p = jnp.pad(w1, ((0, LANES - HY_EMB), (0, 0)))
    nw = 2 * D_HY
    rc, rs = pl.pallas_call(
        functools.partial(_hy_filter_kernel, L),
        out_shape=[jax.ShapeDtypeStruct((L, nw), F32)] * 2,
        compiler_params=pltpu.CompilerParams(vmem_limit_bytes=VMEM_LIMIT),
        name="hy_filter",
    )(z, w1p, b1.reshape(1, HY_FFN), w2, b2.reshape(1, HY_FFN), w3, freq)
    n = 2 * L - 1
    wsc = np.full((L, 1), 2.0 / n, np.float32)
    wsc[0, 0] = 1.0 / n
    tr = min(L, 256)
    return pl.pallas_call(
        _hy_spectrum_kernel,
        grid=(L // tr,),
        in_specs=[
            pl.BlockSpec((tr, L), lambda i: (i, 0)),
            pl.BlockSpec((tr, L), lambda i: (i, 0)),
            _const_spec((L, nw)),
            _const_spec((L, nw)),
            pl.BlockSpec((tr, 1), lambda i: (i, 0)),
        ],
        out_specs=[pl.BlockSpec((tr, nw), lambda i: (i, 0))] * 2,
        out_shape=[jax.ShapeDtypeStruct((L, nw), F32)] * 2,
        compiler_params=_cparams("parallel"),
        name="hy_spectrum",
    )(cmat, smat, rc, rs, jnp.asarray(wsc))


def _hyena_kernel(L, tr, p_ref, sw_ref, sb_ref, bias_ref, c_ref, s_ref, kre_ref, kim_ref, o_ref,
                  pad_ref, u_ref, sig_ref, sig16_ref, zre_ref, zim_ref):
    C3 = 3 * D_HY
    zeros = jnp.zeros((8, C3), F32)
    pad_ref[0:8, :] = zeros
    pad_ref[8 + L:16 + L, :] = zeros
    chunks = [slice(r0, r0 + tr) for r0 in range(0, L, tr)]
    for c in chunks:
        pad_ref[8 + c.start:8 + c.stop, :] = p_ref[c, :].astype(F32)
    for c in chunks:
        u = sb_ref[...]
        for j in range(3):
            u = u + pad_ref[7 + j + c.start:7 + j + c.stop, :] * sw_ref[j:j + 1, :]
        u_ref[c, :] = u[:, D_HY:C3]
        sig_ref[c, :] = u[:, 0:D_HY]
        sig16_ref[c, :] = u[:, 0:D_HY].astype(BF16)

    for o in range(2):
        ko = slice(o * D_HY, (o + 1) * D_HY)
        for c in chunks:
            ure = _dot(c_ref[c, :], sig16_ref[...])
            us = _dot(s_ref[c, :], sig16_ref[...])
            kre, kim = kre_ref[c, ko], kim_ref[c, ko]
            zre_ref[c, :] = (ure * kre + us * kim).astype(BF16)
            zim_ref[c, :] = (ure * kim - us * kre).astype(BF16)
        gate = slice(o * D_HY, (o + 1) * D_HY)
        for c in chunks:
            y = _dot(c_ref[c, :], zre_ref[...]) - _dot(s_ref[c, :], zim_ref[...])
            z = u_ref[c, gate] * (y + sig_ref[c, :] * bias_ref[o:o + 1, :])
            if o == 0:
                sig_ref[c, :] = z
                sig16_ref[c, :] = z.astype(BF16)
            else:
                o_ref[c, :] = z.astype(o_ref.dtype)


def _hyena(p_hy, B, L, cmat, smat, kre, kim, short_w, short_b, bias, tr=512):
    C3 = 3 * D_HY
    tr = min(tr, L)
    once = pl.Buffered(1)
    return pl.pallas_call(
        functools.partial(_hyena_kernel, L, tr),
        grid=(B,),
        in_specs=[
            pl.BlockSpec((L, C3), lambda b: (b, 0)),
            _const_spec((3, C3)),
            _const_spec((1, C3)),
            _const_spec((2, D_HY)),
            pl.BlockSpec((L, L), lambda b: (0, 0), pipeline_mode=once),
            pl.BlockSpec((L, L), lambda b: (0, 0), pipeline_mode=once),
            pl.BlockSpec((L, 2 * D_HY), lambda b: (0, 0), pipeline_mode=once),
            pl.BlockSpec((L, 2 * D_HY), lambda b: (0, 0), pipeline_mode=once),
        ],
        out_specs=pl.BlockSpec((L, D_HY), lambda b: (b, 0)),
        out_shape=jax.ShapeDtypeStruct((B * L, D_HY), BF16),
        scratch_shapes=[
            pltpu.VMEM((L + 16, C3), F32),
            pltpu.VMEM((L, 2 * D_HY), F32),
            pltpu.VMEM((L, D_HY), F32),
            pltpu.VMEM((L, D_HY), BF16),
            pltpu.VMEM((L, D_HY), BF16),
            pltpu.VMEM((L, D_HY), BF16),
        ],
        compiler_params=_cparams("parallel"),
        name="hyena",
    )(p_hy, short_w, short_b.reshape(1, C3), bias, cmat, smat, kre, kim)


def _softplus(z):
    return jnp.maximum(z, 0.0) + jnp.log1p(jnp.exp(-jnp.abs(z)))


def _sigmoid_tanh(x):
    return 0.5 + 0.5 * jnp.tanh(0.5 * x)


def _gelu_tanh(x):
    return 0.5 * x * (1.0 + jnp.tanh(math.sqrt(2.0 / math.pi) * (x + 0.044715 * x * x * x)))


def _rglru_kernel(L, has_state, pg_ref, px_ref, cw_ref, cb_ref, w3_ref, gb_ref, lam_ref, *rest):
    if has_state:
        st_ref, y_ref, pad_ref, a_ref, b_ref, h_ref = rest
    else:
        y_ref, st_out_ref, pad_ref, a_ref, b_ref, h_ref = rest
    C = D_RG
    zeros = jnp.zeros((8, C), F32)
    pad_ref[0:8, :] = zeros
    pad_ref[8 + L:16 + L, :] = zeros
    pad_ref[8:8 + L, :] = px_ref[...]
    sp = _softplus(-lam_ref[...])
    tr = min(L, 256)
    for r0 in range(0, L, tr):
        xr = cb_ref[...]
        for j in range(4):
            xr = xr + pad_ref[6 + j + r0:6 + j + r0 + tr, :] * cw_ref[j:j + 1, :]
        xh, xl = _split(xr)
        x3 = jnp.concatenate([xh, xl, xh], axis=1)
        for d in range(2):
            g = []
            for m in range(2):
                cols = slice((2 * d + m) * C, (2 * d + m + 1) * C)
                g.append(_sigmoid_tanh(_dot(x3, w3_ref[:, cols]) + gb_ref[:, cols]))
            log_a = -RG_C * g[0] * sp[d:d + 1, :]
            a = jnp.exp(log_a)
            a_ref[d, r0:r0 + tr, :] = a
            b_ref[d, r0:r0 + tr, :] = jnp.sqrt(-jnp.tanh(log_a) * (1.0 + a * a)) * (g[1] * xr)

    if has_state:
        h0f, h0b = st_ref[0:1, :], st_ref[1:2, :]
    else:
        h0f = h0b = jnp.zeros((1, C), F32)

    row = lax.broadcasted_iota(jnp.int32, (SUBLANES, 1), 0)

    def tile_scan(a, b, reverse):
        for d in (1, 2, 4):
            shift = SUBLANES - d if reverse else d
            valid = (row < SUBLANES - d) if reverse else (row >= d)
            a_s, b_s = pltpu.roll(a, shift, axis=0), pltpu.roll(b, shift, axis=0)
            b = jnp.where(valid, a * b_s + b, b)
            a = jnp.where(valid, a * a_s, a)
        return a, b

    def step(i, carry):
        hf, hb = carry
        t0 = pl.multiple_of(i * SUBLANES, SUBLANES)
        tb0 = pl.multiple_of(L - SUBLANES - i * SUBLANES, SUBLANES)
        af, bf = tile_scan(a_ref[0, pl.ds(t0, SUBLANES), :], b_ref[0, pl.ds(t0, SUBLANES), :], False)
        ab, bb = tile_scan(a_ref[1, pl.ds(tb0, SUBLANES), :], b_ref[1, pl.ds(tb0, SUBLANES), :], True)
        hf_tile = af * hf + bf
        hb_tile = ab * hb + bb
        h_ref[0, pl.ds(t0, SUBLANES), :] = hf_tile
        h_ref[1, pl.ds(tb0, SUBLANES), :] = hb_tile
        return hf_tile[SUBLANES - 1:SUBLANES], hb_tile[0:1]

    lax.fori_loop(0, L // SUBLANES, step, (h0f, h0b), unroll=2)
    y_ref[...] = ((h_ref[0] + h_ref[1]) * _gelu_tanh(pg_ref[...])).astype(y_ref.dtype)
    if not has_state:
        st_out_ref[0:1, :] = h_ref[0, L - 1:L, :]
        st_out_ref[1:2, :] = h_ref[1, 0:1, :]


def _block_diag(w):
    H, d, _ = w.shape
    eye = jnp.eye(H, dtype=w.dtype)
    return (eye[:, None, :, None] * w[:, :, None, :]).reshape(H * d, H * d)


def _rglru(p_g, p_x, B, L, conv_w, conv_b, wa, ba, wx, bx, lam, state):
    C = D_RG
    wcat = jnp.concatenate([_block_diag(wa[0]), _block_diag(wx[0]), _block_diag(wa[1]), _block_diag(wx[1])], axis=1)
    wh = wcat.astype(BF16)
    wl = (wcat - wh.astype(F32)).astype(BF16)
    w3 = jnp.concatenate([wh, wh, wl], axis=0)
    gb = jnp.concatenate([ba[0], bx[0], ba[1], bx[1]]).reshape(1, 4 * C)
    has_state = state is not None
    in_specs = [
        pl.BlockSpec((L, C), lambda b: (b, 0)),
        pl.BlockSpec((L, C), lambda b: (b, 0)),
        _const_spec((4, C)),
        _const_spec((1, C)),
        _const_spec((3 * C, 4 * C)),
        _const_spec((1, 4 * C)),
        _const_spec((2, C)),
    ]
    args = [p_g, p_x, conv_w, conv_b.reshape(1, C), w3, gb, lam]
    y_spec = pl.BlockSpec((L, C), lambda b: (b, 0))
    y_shape = jax.ShapeDtypeStruct((B * L, C), BF16)
    if has_state:
        in_specs.append(pl.BlockSpec((None, 2, C), lambda b: (b, 0, 0)))
        args.append(state)
        out_specs, out_shape = y_spec, y_shape
    else:
        out_specs = [y_spec, pl.BlockSpec((None, 2, C), lambda b: (b, 0, 0))]
        out_shape = [y_shape, jax.ShapeDtypeStruct((B, 2, C), F32)]
    return pl.pallas_call(
        functools.partial(_rglru_kernel, L, has_state),
        grid=(B,),
        in_specs=in_specs,
        out_specs=out_specs,
        out_shape=out_shape,
        scratch_shapes=[
            pltpu.VMEM((L + 16, C), F32),
            pltpu.VMEM((2, L, C), F32),
            pltpu.VMEM((2, L, C), F32),
            pltpu.VMEM((2, L, C), F32),
        ],
        compiler_params=_cparams("parallel"),
        name="rglru_state" if has_state else "rglru",
    )(*args)


def _attn_kernel(L, P, tq, unroll, lam_init, q_ref, k_ref, v_ref, *rest):
    if P:
        ck_ref, cv_ref, dal_ref, sub_ref, o_ref, kk_ref, vv_ref = rest
    else:
        dal_ref, sub_ref, o_ref, kk_ref, vv_ref = rest
    lv = dal_ref[...]
    s01 = jnp.sum(lv[0:1, :] * lv[1:2, :], axis=-1, keepdims=True)
    s23 = jnp.sum(lv[2:3, :] * lv[3:4, :], axis=-1, keepdims=True)
    lam = jnp.exp(s01) - jnp.exp(s23) + lam_init
    first_half = lax.broadcasted_iota(jnp.int32, (1, DA_VDIM), 1) < DA_HEAD
    sub = sub_ref[...] * (1.0 - lam_init)
    for hd in range(N_DA_HEADS):
        if P:
            kk_ref[0:P, :] = ck_ref[hd].astype(BF16)
            vv_ref[0:P, :] = cv_ref[hd].astype(BF16)
        kk_ref[P:P + L, :] = k_ref[hd].astype(BF16)
        vv_ref[P:P + L, :] = v_ref[hd].astype(BF16)

        def qblock(i, carry):
            r0 = pl.multiple_of(i * tq, tq)
            q = q_ref[hd, pl.ds(r0, tq), :]
            zero = jnp.zeros_like(q)
            qs = jnp.concatenate([jnp.where(first_half, q, zero), jnp.where(first_half, zero, q)], axis=0)
            s = _dot_nt(qs, kk_ref[...])
            p = jnp.exp2(s - jnp.max(s, axis=-1, keepdims=True))
            rinv = 1.0 / jnp.sum(p, axis=-1, keepdims=True)
            acc = _dot(p.astype(BF16), vv_ref[...])
            o = acc[0:tq] * rinv[0:tq] - acc[tq:2 * tq] * (lam * rinv[tq:2 * tq])
            o = o * lax.rsqrt(jnp.mean(o * o, axis=-1, keepdims=True) + EPS) * sub
            o_ref[pl.ds(r0, tq), hd * DA_VDIM:(hd + 1) * DA_VDIM] = o.astype(o_ref.dtype)
            return carry

        lax.fori_loop(0, L // tq, qblock, 0, unroll=unroll)


def _attention(q, k, v, layer, cache, dal, subln, lam_init, B, L, tq=128, unroll=4):
    H, dv = N_DA_HEADS, DA_VDIM
    hspec = pl.BlockSpec((None, H, L, dv), lambda b: (b, 0, 0, 0))
    kvspec = hspec if k.ndim == 4 else pl.BlockSpec((None, None, H, L, dv), lambda b: (b, layer, 0, 0, 0))
    in_specs = [hspec, kvspec, kvspec]
    args = [q, k, v]
    P = 0
    if cache is not None:
        ck, cv = cache
        P = ck.shape[3]
        cspec = pl.BlockSpec((None, None, H, P, dv), lambda b: (b, layer, 0, 0, 0))
        in_specs += [cspec, cspec]
        args += [ck, cv]
    assert L % tq == 0
    in_specs += [_const_spec((4, DA_HEAD)), _const_spec((1, dv))]
    args += [dal, subln.reshape(1, dv)]
    return pl.pallas_call(
        functools.partial(_attn_kernel, L, P, tq, min(unroll, L // tq), lam_init),
        grid=(B,),
        in_specs=in_specs,
        out_specs=pl.BlockSpec((L, H * dv), lambda b: (b, 0)),
        out_shape=jax.ShapeDtypeStruct((B * L, H * dv), BF16),
        scratch_shapes=[pltpu.VMEM((P + L, dv), BF16), pltpu.VMEM((P + L, dv), BF16)],
        compiler_params=_cparams("parallel"),
        name="diff_attn_cache" if P else "diff_attn",
    )(*args)


def _route(logits):
    m = logits[0]
    for e in range(1, N_EXPERTS):
        m = jnp.maximum(m, logits[e])
    ex = [jnp.exp(l - m) for l in logits]
    tot = ex[0]
    for e in range(1, N_EXPERTS):
        tot = tot + ex[e]
    inv = 1.0 / tot
    p = [e_ * inv for e_ in ex]
    G = EXP_PER_GROUP
    best, gsel = None, None
    for g in range(N_GROUPS):
        a = p[g * G:(g + 1) * G]
        sc = None
        for i in range(G):
            for j in range(i + 1, G):
                pair = a[i] + a[j]
                sc = pair if sc is None else jnp.maximum(sc, pair)
        if g == 0:
            best, gsel = sc, jnp.zeros_like(sc, dtype=jnp.int32)
        else:
            upd = sc > best
            best = jnp.where(upd, sc, best)
            gsel = jnp.where(upd, g, gsel)
    vals = []
    for j in range(G):
        vj = p[j]
        for g in range(1, N_GROUPS):
            vj = jnp.where(gsel == g, p[g * G + j], vj)
        vals.append(vj)
    p1, i1 = vals[0], jnp.zeros_like(gsel)
    for j in range(1, G):
        upd = vals[j] > p1
        p1 = jnp.where(upd, vals[j], p1)
        i1 = jnp.where(upd, j, i1)
    p2, i2 = None, None
    for j in range(G):
        cand = jnp.where(i1 == j, -1.0, vals[j])
        if p2 is None:
            p2, i2 = cand, jnp.zeros_like(gsel)
        else:
            upd = cand > p2
            p2 = jnp.where(upd, cand, p2)
            i2 = jnp.where(upd, j, i2)
    den = 1.0 / (p1 + p2)
    w1, w2 = p1 * den, p2 * den
    swap = i2 < i1
    a, b = jnp.where(swap, i2, i1), jnp.where(swap, i1, i2)
    w_lo, w_hi = jnp.where(swap, w2, w1), jnp.where(swap, w1, w2)
    pair = jnp.where(a == 0, b - 1, jnp.where(a == 1, b + 1, 5))
    cls = gsel * PAIRS_PER_GROUP + pair
    return cls.astype(F32), w_lo, w_hi


def _pack_pairs(x):
    n = x.shape[1] // 2
    b = pltpu.bitcast(x, jnp.uint32)
    w = (b[:, :n] >> 16) | (b[:, n:] & jnp.uint32(0xFFFF0000))
    return pltpu.bitcast(w, jnp.int32)


def _unpack_pairs(w):
    u = pltpu.bitcast(w, jnp.uint32)
    lo = pltpu.bitcast(u << 16, F32)
    hi = pltpu.bitcast(u & jnp.uint32(0xFFFF0000), F32)
    return jnp.concatenate([lo, hi], axis=1)


def _out_proj_kernel(yh_ref, yr_ref, o_ref, w_ref, x_ref, mod_ref, g_ref, wrh_ref, wrl_ref, br_ref,
                     xo_ref, h_ref, route_ref):
    y = (_dot(yh_ref[...], w_ref[0:D_HY, :]) + _dot(yr_ref[...], w_ref[D_HY:D_HY + D_RG, :])
         + _dot(o_ref[...], w_ref[D_HY + D_RG:D_MIX, :]))
    x = x_ref[...] + mod_ref[2:3, :] * y
    xo_ref[...] = x
    ms = jnp.mean(x * x, axis=-1, keepdims=True)
    h = (x * lax.rsqrt(ms + EPS) * g_ref[...]) * (1.0 + mod_ref[4:5, :]) + mod_ref[3:4, :]
    hh, hl = _split(h)
    h_ref[...] = _pack_pairs(hh.astype(F32))
    lg = _dot_nt(wrh_ref[...], hh) + _dot_nt(wrh_ref[...], hl) + _dot_nt(wrl_ref[...], hh) + br_ref[...]
    info = _route([lg[e:e + 1, :] for e in range(N_EXPERTS)])
    rt = jnp.concatenate(list(info) + [jnp.zeros((LANES - len(info), lg.shape[1]), F32)], axis=0)
    route_ref[...] = rt.T


def _out_proj(y_hy, y_rg, o, w_out, layer, x, mod, g2, w_router, b_router, B, L, ctx_rows, tm=512):
    T = B * L
    tm = min(tm, L)
    nl = L // tm
    row = (lambda i: CTX_ROW) if ctx_rows else (lambda i: i // nl)
    wrt = w_router.T
    wrh = wrt.astype(BF16)
    wrl = (wrt - wrh.astype(F32)).astype(BF16)
    rows = lambda w: pl.BlockSpec((tm, w), lambda i: (i, 0))
    return pl.pallas_call(
        _out_proj_kernel,
        grid=(T // tm,),
        in_specs=[
            rows(D_HY), rows(D_RG), rows(D_DA),
            pl.BlockSpec((None, D_MIX, D_MODEL), lambda i: (layer, 0, 0)),
            rows(D_MODEL),
            pl.BlockSpec((None, 6, D_MODEL), lambda i: (row(i), 0, 0)),
            _const_spec((1, D_MODEL)),
            _const_spec((N_EXPERTS, D_MODEL)),
            _const_spec((N_EXPERTS, D_MODEL)),
            _const_spec((N_EXPERTS, 1)),
        ],
        out_specs=[rows(D_MODEL), rows(D_MODEL // 2), rows(LANES)],
        out_shape=[
            jax.ShapeDtypeStruct((T, D_MODEL), F32),
            jax.ShapeDtypeStruct((T, D_MODEL // 2), jnp.int32),
            jax.ShapeDtypeStruct((T, LANES), F32),
        ],
        compiler_params=_cparams("parallel"),
        name="out_proj_route",
    )(y_hy, y_rg, o, w_out, x, mod, g2.reshape(1, D_MODEL), wrh, wrl, b_router.reshape(N_EXPERTS, 1))


def _gather_rows(table, idx, rows_per_step=64, n_buf=2):
    info = plsc.get_sparse_core_info()
    n_workers = info.num_cores * info.num_subcores
    n, width = idx.shape[0], table.shape[1]
    per_worker = n // n_workers
    n_steps = per_worker // rows_per_step
    assert per_worker * n_workers == n and n_steps * rows_per_step == per_worker and n_steps >= n_buf
    mesh = plsc.VectorSubcoreMesh(core_axis_name="c", subcore_axis_name="s")

    @functools.partial(
        pl.kernel, mesh=mesh,
        out_type=jax.ShapeDtypeStruct((n, width), table.dtype),
        scratch_types=[
            pltpu.VMEM((per_worker,), jnp.int32),
            pltpu.VMEM((n_buf, rows_per_step, width), table.dtype),
            pltpu.SemaphoreType.DMA((n_buf,)),
            pltpu.SemaphoreType.DMA((n_buf,)),
        ],
    )
    def gather(table_hbm, idx_hbm, out_hbm, idx_v, rows_v, sem_in, sem_out):
        worker = lax.axis_index("s") * info.num_cores + lax.axis_index("c")
        base = pl.multiple_of(worker * per_worker, per_worker)
        pltpu.sync_copy(idx_hbm.at[pl.ds(base, per_worker)], idx_v)

        def read(b, step):
            rows = idx_v.at[pl.ds(step * rows_per_step, rows_per_step)]
            return pltpu.make_async_copy(table_hbm.at[rows], rows_v.at[b], sem_in.at[b])

        def write(b, step):
            off = pl.multiple_of(base + step * rows_per_step, rows_per_step)
            return pltpu.make_async_copy(rows_v.at[b], out_hbm.at[pl.ds(off, rows_per_step)], sem_out.at[b])

        for step in range(n_steps + 1):
            if step < n_steps:
                if step >= n_buf:
                    write(step % n_buf, step - n_buf).wait()
                read(step % n_buf, step).start()
            if step >= 1:
                read((step - 1) % n_buf, step - 1).wait()
                write((step - 1) % n_buf, step - 1).start()
        for step in range(n_steps - n_buf, n_steps):
            write(step % n_buf, step).wait()

    return gather(table, idx)


def _dispatch_plan(route, tm):
    T = route.shape[0]
    n_slots = T + N_CLASSES * tm
    cls = route[:, 0].astype(jnp.int32)
    onehot = (cls[:, None] == jnp.arange(N_CLASSES, dtype=jnp.int32)[None, :]).astype(jnp.int32)
    csum = jnp.cumsum(onehot, axis=0)
    rank = jnp.sum(onehot * csum, axis=1) - 1
    counts = csum[-1]
    padded = ((counts + tm - 1) // tm) * tm
    ends = jnp.cumsum(padded)
    pos = jnp.sum(onehot * (ends - padded)[None, :], axis=1) + rank
    inv = (jnp.arange(n_slots, dtype=jnp.int32) % T).at[pos].set(jnp.arange(T, dtype=jnp.int32), unique_indices=True)
    tile_start = jnp.arange(n_slots // tm, dtype=jnp.int32) * tm
    tile_cls = jnp.minimum(jnp.searchsorted(ends, tile_start, side="right"), N_CLASSES - 1).astype(jnp.int32)
    valid = (tile_start < ends[-1]).astype(jnp.int32)
    pairs = np.array([(a, b) for a in range(EXP_PER_GROUP) for b in range(a + 1, EXP_PER_GROUP)], np.int32)
    group, pair = tile_cls // PAIRS_PER_GROUP, tile_cls % PAIRS_PER_GROUP
    lo = group * EXP_PER_GROUP + jnp.asarray(pairs[:, 0])[pair]
    hi = group * EXP_PER_GROUP + jnp.asarray(pairs[:, 1])[pair]
    return pos, inv, lo, hi, valid


def _moe_sorted_kernel(lo_ref, hi_ref, valid_ref, xs_ref, ws_ref, wg_lo, wu_lo, wd_lo, wg_hi, wu_hi, wd_hi, o_ref):
    i = pl.program_id(0)

    @pl.when(valid_ref[i] == 1)
    def _():
        x = _unpack_pairs(xs_ref[...]).astype(BF16)
        y = None
        for wg, wu, wd, col in ((wg_lo, wu_lo, wd_lo, 1), (wg_hi, wu_hi, wd_hi, 2)):
            a = _dot(x, wg[...])
            he = (a * _sigmoid(a)) * _dot(x, wu[...]) * ws_ref[:, col:col + 1]
            part = _dot(he.astype(BF16), wd[...])
            y = part if y is None else y + part
        o_ref[...] = _pack_pairs(y.astype(BF16).astype(F32))

    @pl.when(valid_ref[i] == 0)
    def _():
        o_ref[...] = jnp.zeros_like(o_ref)


def _moe_sorted(xs, ws, lo, hi, valid, wg, wu, wd, tm):
    n_slots = xs.shape[0]
    half = D_MODEL // 2
    up = lambda sel: pl.BlockSpec((None, D_MODEL, D_EXPERT), lambda i, lo, hi, v: ((lo, hi)[sel][i], 0, 0))
    down = lambda sel: pl.BlockSpec((None, D_EXPERT, D_MODEL), lambda i, lo, hi, v: ((lo, hi)[sel][i], 0, 0))
    return pl.pallas_call(
        _moe_sorted_kernel,
        grid_spec=pltpu.PrefetchScalarGridSpec(
            num_scalar_prefetch=3,
            grid=(n_slots // tm,),
            in_specs=[
                pl.BlockSpec((tm, half), lambda i, lo, hi, v: (i, 0)),
                pl.BlockSpec((tm, LANES), lambda i, lo, hi, v: (i, 0)),
                up(0), up(0), down(0), up(1), up(1), down(1),
            ],
            out_specs=pl.BlockSpec((tm, half), lambda i, lo, hi, v: (i, 0)),
        ),
        out_shape=jax.ShapeDtypeStruct((n_slots, half), jnp.int32),
        compiler_params=_cparams("arbitrary"),
        name="moe_sorted",
    )(lo, hi, valid, xs, ws, wg, wu, wd, wg, wu, wd)


def _moe_residual_kernel(final, y_ref, x_ref, mod_ref, fg_ref, o_ref):
    x = x_ref[...] + mod_ref[5:6, :] * _unpack_pairs(y_ref[...])
    if final:
        x = x * lax.rsqrt(jnp.mean(x * x, axis=-1, keepdims=True) + EPS) * fg_ref[...]
    o_ref[...] = x


def _moe_residual(y, x, mod, final_g, final, B, L, ctx_rows, tm=512):
    T = B * L
    tm = min(tm, L)
    nl = L // tm
    row = (lambda i: CTX_ROW) if ctx_rows else (lambda i: i // nl)
    return pl.pallas_call(
        functools.partial(_moe_residual_kernel, final),
        grid=(T // tm,),
        in_specs=[
            pl.BlockSpec((tm, D_MODEL // 2), lambda i: (i, 0)),
            pl.BlockSpec((tm, D_MODEL), lambda i: (i, 0)),
            pl.BlockSpec((None, 6, D_MODEL), lambda i: (row(i), 0, 0)),
            _const_spec((1, D_MODEL)),
        ],
        out_specs=pl.BlockSpec((tm, D_MODEL), lambda i: (i, 0)),
        out_shape=jax.ShapeDtypeStruct((T, D_MODEL), F32),
        compiler_params=_cparams("parallel"),
        name="moe_residual_final" if final else "moe_residual",
    )(y, x, mod, final_g.reshape(1, D_MODEL))


def _moe(h, route, wg, wu, wd, layer, x, mod, final_g, final, B, L, ctx_rows, tm=256):
    pos, inv, lo, hi, valid = _dispatch_plan(route, tm)
    xs = _gather_rows(h, inv)
    ws = _gather_rows(route, inv)
    ys = _moe_sorted(xs, ws, lo + layer * N_EXPERTS, hi + layer * N_EXPERTS, valid, wg, wu, wd, tm)
    y = _gather_rows(ys, pos)
    return _moe_residual(y, x, mod, final_g, final, B, L, ctx_rows)


def kernel(x_prompt, x_sample, cache_k, cache_v, state_rglru, c, c_ctx, w_ada, b_ada, norm1_g, norm2_g, w_in, w_out, hy_short_w, hy_short_b, hy_w1, hy_b1, hy_w2, hy_b2, hy_w3, hy_freq, hy_bias, rg_conv_w, rg_conv_b, rg_wa, rg_ba, rg_wx, rg_bx, rg_lambda, da_lambda, da_subln, w_router, b_router, moe_wg, moe_wu, moe_wd, final_g):
    Bp, Lp, D = x_prompt.shape
    Bs, Ls, _ = x_sample.shape
    assert Bs <= CTX_ROW
    cond = jnp.zeros((N_COND, D), F32).at[:Bs].set(c).at[CTX_ROW].set(c_ctx)
    mods = _ada_table(cond, w_ada, b_ada)

    dft = {L: tuple(jnp.asarray(m).astype(BF16) for m in _dft_mats(L)) for L in (Lp, Ls)}
    streams = [
        dict(B=Bp, L=Lp, ctx=True, x=x_prompt.reshape(Bp * Lp, D)),
        dict(B=Bs, L=Ls, ctx=False, x=x_sample.reshape(Bs * Ls, D)),
    ]
    w_in_b, w_out_b = w_in.astype(BF16), w_out.astype(BF16)
    wg, wu, wd = (w.astype(BF16).reshape((DEPTH * N_EXPERTS,) + w.shape[2:]) for w in (moe_wg, moe_wu, moe_wd))
    kv_shape = (Bp, DEPTH, N_DA_HEADS, Lp, DA_VDIM)
    new_kv, ss = (jnp.zeros(kv_shape, F32), jnp.zeros(kv_shape, F32)), []
    for l in range(DEPTH):
        lam_init = 0.8 - 0.6 * math.exp(-0.3 * l)
        final = l == DEPTH - 1
        for st in streams:
            B, L, ctx = st["B"], st["L"], st["ctx"]
            cmat, smat = dft[L]
            p_hy, p_g, p_x, q, k, v = _norm_proj(
                st["x"], mods[l], norm1_g[l], w_in_b, l, B, L, ctx, kv_prev=new_kv if ctx else None)
            kre, kim = _hy_spectra(L, cmat, smat, hy_w1[l], hy_b1[l], hy_w2[l], hy_b2[l], hy_w3[l], hy_freq[l])
            y_hy = _hyena(p_hy, B, L, cmat, smat, kre, kim, hy_short_w[l], hy_short_b[l], hy_bias[l])
            rg_args = (rg_conv_w[l], rg_conv_b[l], rg_wa[l], rg_ba[l], rg_wx[l], rg_bx[l], rg_lambda[l])
            if ctx:
                y_rg, s_l = _rglru(p_g, p_x, B, L, *rg_args, None)
                o = _attention(q, k, v, l, None, da_lambda[l], da_subln[l], lam_init, B, L)
                new_kv = (k, v)
                ss.append(s_l)
            else:
                y_rg = _rglru(p_g, p_x, B, L, *rg_args, state_rglru[:, l])
                o = _attention(q, k, v, l, (cache_k, cache_v), da_lambda[l], da_subln[l], lam_init, B, L)
            x_mid, h2, route = _out_proj(y_hy, y_rg, o, w_out_b, l, st["x"], mods[l], norm2_g[l],
                                         w_router, b_router, B, L, ctx)
            st["x"] = _moe(h2, route, wg, wu, wd, l, x_mid, mods[l], final_g, final, B, L, ctx)
    y_prompt = streams[0]["x"].reshape(Bp, Lp, D)
    y_sample = streams[1]["x"].reshape(Bs, Ls, D)
    return (y_prompt, y_sample, new_kv[0], new_kv[1], jnp.stack(ss, axis=1))
```

```python
import functools
import math

import numpy as np
import jax
import jax.numpy as jnp
from jax import lax
from jax.experimental import pallas as pl
from jax.experimental.pallas import tpu as pltpu
from jax.experimental.pallas import tpu_sc as plsc

F32 = jnp.float32
BF16 = jnp.bfloat16

D_MODEL = 1024
DEPTH = 2
GRID_W = 64
D_HY = 256
HY_EMB = 33
HY_BANDS = (HY_EMB - 1) // 2
HY_FFN = 64
HY_MIN_DECAY = math.log(1e-2) / 1.5
HY_MAX_DECAY = math.log(1e-2) / 0.3
D_RG = 256
N_RG_HEADS = 4
RG_C = 8.0
N_DA_HEADS = 4
DA_HEAD = 64
DA_VDIM = 2 * DA_HEAD
D_DA = N_DA_HEADS * DA_VDIM
D_MIX = D_HY + D_RG + D_DA
D_IN = 3 * D_HY + 2 * D_RG + 3 * D_DA
ROPE_PAIRS = DA_HEAD // 4
ROPE_THETA = 10000.0
N_EXPERTS = 16
N_GROUPS = 4
EXP_PER_GROUP = N_EXPERTS // N_GROUPS
D_EXPERT = 512
PAIRS_PER_GROUP = EXP_PER_GROUP * (EXP_PER_GROUP - 1) // 2
N_CLASSES = N_GROUPS * PAIRS_PER_GROUP
EPS = 1e-6
N_COND = 16
CTX_ROW = 8
LANES = 128
SUBLANES = 8
VMEM_LIMIT = 56 * 1024 * 1024


def _cparams(*sem):
    return pltpu.CompilerParams(dimension_semantics=sem, vmem_limit_bytes=VMEM_LIMIT)


def _split(x):
    hi = x.astype(BF16)
    lo = (x - hi.astype(F32)).astype(BF16)
    return hi, lo


def _dot(a, b):
    return jnp.dot(a, b, preferred_element_type=F32)


def _dot3(a, b):
    ah, al = _split(a)
    bh, bl = _split(b)
    return _dot(ah, bh) + _dot(al, bh) + _dot(ah, bl)


def _dot_nt(a, b):
    return lax.dot_general(a, b, (((1,), (1,)), ((), ())), preferred_element_type=F32)


def _sigmoid(x):
    return 1.0 / (1.0 + jnp.exp(-x))


def _const_spec(shape):
    n = len(shape)
    return pl.BlockSpec(shape, lambda *_: (0,) * n)


def _ada_kernel(c_ref, w_ref, b_ref, o_ref):
    c = c_ref[...]
    s = c * _sigmoid(c)
    o_ref[...] = _dot3(s, w_ref[...]) + b_ref[...]


def _ada_table(cond, w_ada, b_ada):
    D = D_MODEL
    out = pl.pallas_call(
        _ada_kernel,
        grid=(DEPTH, 6),
        in_specs=[
            pl.BlockSpec((N_COND, D), lambda l, j: (0, 0)),
            pl.BlockSpec((None, D, D), lambda l, j: (l, 0, j)),
            pl.BlockSpec((None, None, 1, D), lambda l, j: (l, j, 0, 0)),
        ],
        out_specs=pl.BlockSpec((None, None, N_COND, D), lambda l, j: (l, j, 0, 0)),
        out_shape=jax.ShapeDtypeStruct((DEPTH, 6, N_COND, D), F32),
        compiler_params=_cparams("parallel", "parallel"),
        name="ada_table",
    )(cond, w_ada, b_ada.reshape(DEPTH, 6, 1, D))
    return out.transpose(0, 2, 1, 3)


def _rope_tables(L):
    t = np.arange(L)
    j = np.arange(LANES)
    jj = j % DA_HEAD
    is_col = (jj // (DA_HEAD // 2)) == 1
    pair = jj % ROPE_PAIRS
    second = (jj % (DA_HEAD // 2)) >= ROPE_PAIRS
    inv = ROPE_THETA ** (-np.arange(ROPE_PAIRS, dtype=np.float64) / ROPE_PAIRS)
    pos = np.where(is_col[None, :], (t % GRID_W)[:, None], (t // GRID_W)[:, None]).astype(np.float64)
    ang = pos * inv[pair][None, :]
    cos = np.cos(ang).astype(np.float32)
    sin = np.sin(ang).astype(np.float32)
    sin_a = np.where(second[None, :], 0.0, -sin).astype(np.float32)
    sin_b = np.where(second[None, :], sin, 0.0).astype(np.float32)
    return cos, sin_a, sin_b


def _rope(x, cos, sin_a, sin_b):
    nxt = pltpu.roll(x, LANES - ROPE_PAIRS, axis=1)
    prv = pltpu.roll(x, ROPE_PAIRS, axis=1)
    return x * cos + nxt * sin_a + prv * sin_b


def _norm_proj_kernel(rope, kv_dtype, pending, x_ref, mod_ref, g_ref, w_ref, *rest):
    x = x_ref[...]
    if pending:
        y_ref, modp_ref, xnew_ref = rest[0], rest[1], rest[-1]
        x = x + modp_ref[5:6, :] * _unpack_pairs(y_ref[...])
        xnew_ref[...] = x
        rest = rest[2:-1]
    if rope:
        cos_ref, sa_ref, sb_ref = rest[:3]
    phy_ref, pg_ref, px_ref, q_ref, k_ref, v_ref = rest[-6:]
    ms = jnp.mean(x * x, axis=-1, keepdims=True)
    y = x * lax.rsqrt(ms + EPS) * g_ref[...]
    h = (y * (1.0 + mod_ref[1:2, :]) + mod_ref[0:1, :]).astype(BF16)
    o = 3 * D_HY
    phy_ref[...] = _dot(h, w_ref[:, 0:o]).astype(BF16)
    pg_ref[...] = _dot(h, w_ref[:, o:o + D_RG])
    px_ref[...] = _dot(h, w_ref[:, o + D_RG:o + 2 * D_RG])
    o += 2 * D_RG
    q = _dot(h, w_ref[:, o:o + D_DA]) * (DA_HEAD ** -0.5 * math.log2(math.e))
    k = _dot(h, w_ref[:, o + D_DA:o + 2 * D_DA])
    v = _dot(h, w_ref[:, o + 2 * D_DA:o + 3 * D_DA])
    if rope:
        cos, sa, sb = cos_ref[...], sa_ref[...], sb_ref[...]
    for hd in range(N_DA_HEADS):
        sl = slice(hd * DA_VDIM, (hd + 1) * DA_VDIM)
        qh, kh = q[:, sl], k[:, sl]
        if rope:
            qh = _rope(qh, cos, sa, sb)
            kh = _rope(kh, cos, sa, sb)
        q_ref[hd] = qh.astype(BF16)
        k_ref[hd] = kh.astype(kv_dtype)
        v_ref[hd] = v[:, sl].astype(kv_dtype)


def _norm_proj(x, mod, g, w_in, layer, B, L, ctx, kv_prev=None, pending=None, tm=512):
    T = B * L
    tm = min(tm, L)
    nl = L // tm
    rope, kv_dtype = not ctx, (F32 if ctx else BF16)
    row = (lambda i: CTX_ROW) if ctx else (lambda i: i // nl)
    mod_spec = pl.BlockSpec((None, 6, D_MODEL), lambda i: (row(i), 0, 0))
    in_specs = [
        pl.BlockSpec((tm, D_MODEL), lambda i: (i, 0)),
        mod_spec,
        _const_spec((1, D_MODEL)),
        pl.BlockSpec((None, D_MODEL, D_IN), lambda i: (layer, 0, 0)),
    ]
    args = [x, mod, g.reshape(1, D_MODEL), w_in]
    if pending is not None:
        in_specs += [pl.BlockSpec((tm, D_MODEL // 2), lambda i: (i, 0)), mod_spec]
        args += list(pending)
    if rope:
        tabs = _rope_tables(L)
        in_specs += [pl.BlockSpec((tm, LANES), lambda i: (i % nl, 0))] * 3
        args += [jnp.asarray(t) for t in tabs]
    head_spec = pl.BlockSpec((None, N_DA_HEADS, tm, DA_VDIM), lambda i: (i // nl, 0, i % nl, 0))
    head_shape = (B, N_DA_HEADS, L, DA_VDIM)
    kv_spec, kv_shape, aliases = head_spec, head_shape, {}
    if ctx:
        kv_spec = pl.BlockSpec((None, None, N_DA_HEADS, tm, DA_VDIM), lambda i: (i // nl, layer, 0, i % nl, 0))
        kv_shape = (B, DEPTH, N_DA_HEADS, L, DA_VDIM)
        if kv_prev is not None:
            aliases = {len(args): 4, len(args) + 1: 5}
            in_specs += [pl.BlockSpec(memory_space=pl.ANY)] * 2
            args += list(kv_prev)
    out_specs = [
        pl.BlockSpec((tm, 3 * D_HY), lambda i: (i, 0)),
        pl.BlockSpec((tm, D_RG), lambda i: (i, 0)),
        pl.BlockSpec((tm, D_RG), lambda i: (i, 0)),
        head_spec, kv_spec, kv_spec,
    ]
    out_shape = [
        jax.ShapeDtypeStruct((T, 3 * D_HY), BF16),
        jax.ShapeDtypeStruct((T, D_RG), F32),
        jax.ShapeDtypeStruct((T, D_RG), F32),
        jax.ShapeDtypeStruct(head_shape, BF16),
        jax.ShapeDtypeStruct(kv_shape, kv_dtype),
        jax.ShapeDtypeStruct(kv_shape, kv_dtype),
    ]
    if pending is not None:
        out_specs.append(pl.BlockSpec((tm, D_MODEL), lambda i: (i, 0)))
        out_shape.append(jax.ShapeDtypeStruct((T, D_MODEL), F32))
    return pl.pallas_call(
        functools.partial(_norm_proj_kernel, rope, kv_dtype, pending is not None),
        grid=(T // tm,),
        in_specs=in_specs,
        out_specs=out_specs,
        out_shape=out_shape,
        input_output_aliases=aliases,
        compiler_params=_cparams("parallel"),
        name="norm_proj_rope" if rope else "norm_proj",
    )(*args)


def _dft_mats(L):
    n = 2 * L - 1
    fs = (np.arange(L, dtype=np.int64)[:, None] * np.arange(L, dtype=np.int64)[None, :]) % n
    ang = fs.astype(np.float64) * (2.0 * np.pi / n)
    return np.cos(ang).astype(np.float32), np.sin(ang).astype(np.float32)


def _hy_features(L):
    t = np.linspace(0.0, 1.0, L, dtype=np.float64)[:, None]
    ang = ((2.0 * math.pi / L) * np.arange(L, dtype=np.float64))[:, None]
    bands = np.linspace(1e-4, HY_BANDS - 1, HY_BANDS, dtype=np.float64)[None, :]
    ba = bands * ang
    z = np.concatenate([t, np.cos(ba), -np.sin(ba)], axis=-1).astype(np.float32)
    return np.pad(z, ((0, 0), (0, LANES - HY_EMB)))


def _hy_filter_kernel(L, z_ref, w1_ref, b1_ref, w2_ref, b2_ref, w3_ref, fr_ref, rc_ref, rs_ref):
    z = z_ref[...]
    h = jnp.sin(fr_ref[0:1, :] * (_dot3(z, w1_ref[...]) + b1_ref[...]))
    h = jnp.sin(fr_ref[1:2, :] * (_dot3(h, w2_ref[...]) + b2_ref[...]))
    h = _dot3(h, w3_ref[...])
    t = z[:, 0:1]
    step = (HY_MAX_DECAY - HY_MIN_DECAY) / (D_HY - 1)
    deltas = HY_MIN_DECAY + step * lax.broadcasted_iota(jnp.int32, (1, D_HY), 1).astype(F32)
    window = jnp.exp(-t * jnp.abs(deltas))
    not_first = lax.broadcasted_iota(jnp.int32, (L, 1), 0) > 0
    for o in range(2):
        hf = h[:, (2 * o) * D_HY:(2 * o + 1) * D_HY] * window
        hb = jnp.where(not_first, h[:, (2 * o + 1) * D_HY:(2 * o + 2) * D_HY] * window, 0.0)
        rc_ref[:, o * D_HY:(o + 1) * D_HY] = hf + hb
        rs_ref[:, o * D_HY:(o + 1) * D_HY] = hb - hf


def _hy_spectrum_kernel(c_ref, s_ref, rc_ref, rs_ref, w_ref, kre_ref, kim_ref):
    rch, rcl = _split(rc_ref[...])
    rsh, rsl = _split(rs_ref[...])
    c, s, w = c_ref[...], s_ref[...], w_ref[...]
    kre_ref[...] = (_dot(c, rch) + _dot(c, rcl)) * w
    kim_ref[...] = (_dot(s, rsh) + _dot(s, rsl)) * w


def _hy_spectra(L, cmat, smat, w1, b1, w2, b2, w3, freq):
    z = jnp.asarray(_hy_features(L))
    w1p = jnp.pad(w1, ((0, LANES - HY_EMB), (0, 0)))
    nw = 2 * D_HY
    rc, rs = pl.pallas_call(
        functools.partial(_hy_filter_kernel, L),
        out_shape=[jax.ShapeDtypeStruct((L, nw), F32)] * 2,
        compiler_params=pltpu.CompilerParams(vmem_limit_bytes=VMEM_LIMIT),
        name="hy_filter",
    )(z, w1p, b1.reshape(1, HY_FFN), w2, b2.reshape(1, HY_FFN), w3, freq)
    n = 2 * L - 1
    wsc = np.full((L, 1), 2.0 / n, np.float32)
    wsc[0, 0] = 1.0 / n
    tr = min(L, 256)
    return pl.pallas_call(
        _hy_spectrum_kernel,
        grid=(L // tr,),
        in_specs=[
            pl.BlockSpec((tr, L), lambda i: (i, 0)),
            pl.BlockSpec((tr, L), lambda i: (i, 0)),
            _const_spec((L, nw)),
            _const_spec((L, nw)),
            pl.BlockSpec((tr, 1), lambda i: (i, 0)),
        ],
        out_specs=[pl.BlockSpec((tr, nw), lambda i: (i, 0))] * 2,
        out_shape=[jax.ShapeDtypeStruct((L, nw), F32)] * 2,
        compiler_params=_cparams("parallel"),
        name="hy_spectrum",
    )(cmat, smat, rc, rs, jnp.asarray(wsc))


def _hyena_kernel(L, tr, p_ref, sw_ref, sb_ref, bias_ref, c_ref, s_ref, kre_ref, kim_ref, o_ref,
                  pad_ref, u_ref, sig_ref, sig16_ref, zre_ref, zim_ref):
    C3 = 3 * D_HY
    zeros = jnp.zeros((8, C3), F32)
    pad_ref[0:8, :] = zeros
    pad_ref[8 + L:16 + L, :] = zeros
    chunks = [slice(r0, r0 + tr) for r0 in range(0, L, tr)]
    for c in chunks:
        pad_ref[8 + c.start:8 + c.stop, :] = p_ref[c, :].astype(F32)
    for c in chunks:
        u = sb_ref[...]
        for j in range(3):
            u = u + pad_ref[7 + j + c.start:7 + j + c.stop, :] * sw_ref[j:j + 1, :]
        u_ref[c, :] = u[:, D_HY:C3]
        sig_ref[c, :] = u[:, 0:D_HY]
        sig16_ref[c, :] = u[:, 0:D_HY].astype(BF16)

    for o in range(2):
        ko = slice(o * D_HY, (o + 1) * D_HY)
        for c in chunks:
            ure = _dot(c_ref[c, :], sig16_ref[...])
            us = _dot(s_ref[c, :], sig16_ref[...])
            kre, kim = kre_ref[c, ko], kim_ref[c, ko]
            zre_ref[c, :] = (ure * kre + us * kim).astype(BF16)
            zim_ref[c, :] = (ure * kim - us * kre).astype(BF16)
        gate = slice(o * D_HY, (o + 1) * D_HY)
        for c in chunks:
            y = _dot(c_ref[c, :], zre_ref[...]) - _dot(s_ref[c, :], zim_ref[...])
            z = u_ref[c, gate] * (y + sig_ref[c, :] * bias_ref[o:o + 1, :])
            if o == 0:
                sig_ref[c, :] = z
                sig16_ref[c, :] = z.astype(BF16)
            else:
                o_ref[c, :] = z.astype(o_ref.dtype)


def _hyena(p_hy, B, L, cmat, smat, kre, kim, short_w, short_b, bias, tr=512):
    C3 = 3 * D_HY
    tr = min(tr, L)
    once = pl.Buffered(1)
    return pl.pallas_call(
        functools.partial(_hyena_kernel, L, tr),
        grid=(B,),
        in_specs=[
            pl.BlockSpec((L, C3), lambda b: (b, 0)),
            _const_spec((3, C3)),
            _const_spec((1, C3)),
            _const_spec((2, D_HY)),
            pl.BlockSpec((L, L), lambda b: (0, 0), pipeline_mode=once),
            pl.BlockSpec((L, L), lambda b: (0, 0), pipeline_mode=once),
            pl.BlockSpec((L, 2 * D_HY), lambda b: (0, 0), pipeline_mode=once),
            pl.BlockSpec((L, 2 * D_HY), lambda b: (0, 0), pipeline_mode=once),
        ],
        out_specs=pl.BlockSpec((L, D_HY), lambda b: (b, 0)),
        out_shape=jax.ShapeDtypeStruct((B * L, D_HY), BF16),
        scratch_shapes=[
            pltpu.VMEM((L + 16, C3), F32),
            pltpu.VMEM((L, 2 * D_HY), F32),
            pltpu.VMEM((L, D_HY), F32),
            pltpu.VMEM((L, D_HY), BF16),
            pltpu.VMEM((L, D_HY), BF16),
            pltpu.VMEM((L, D_HY), BF16),
        ],
        compiler_params=_cparams("parallel"),
        name="hyena",
    )(p_hy, short_w, short_b.reshape(1, C3), bias, cmat, smat, kre, kim)


def _softplus(z):
    return jnp.maximum(z, 0.0) + jnp.log1p(jnp.exp(-jnp.abs(z)))


def _sigmoid_tanh(x):
    return 0.5 + 0.5 * jnp.tanh(0.5 * x)


def _gelu_tanh(x):
    return 0.5 * x * (1.0 + jnp.tanh(math.sqrt(2.0 / math.pi) * (x + 0.044715 * x * x * x)))


def _rglru_kernel(L, has_state, pg_ref, px_ref, cw_ref, cb_ref, w3_ref, gb_ref, lam_ref, *rest):
    if has_state:
        st_ref, y_ref, pad_ref, a_ref, b_ref, h_ref = rest
    else:
        y_ref, st_out_ref, pad_ref, a_ref, b_ref, h_ref = rest
    C = D_RG
    zeros = jnp.zeros((8, C), F32)
    pad_ref[0:8, :] = zeros
    pad_ref[8 + L:16 + L, :] = zeros
    pad_ref[8:8 + L, :] = px_ref[...]
    sp = _softplus(-lam_ref[...])
    tr = min(L, 256)
    for r0 in range(0, L, tr):
        xr = cb_ref[...]
        for j in range(4):
            xr = xr + pad_ref[6 + j + r0:6 + j + r0 + tr, :] * cw_ref[j:j + 1, :]
        xh, xl = _split(xr)
        x3 = jnp.concatenate([xh, xl, xh], axis=1)
        for d in range(2):
            g = []
            for m in range(2):
                cols = slice((2 * d + m) * C, (2 * d + m + 1) * C)
                g.append(_sigmoid_tanh(_dot(x3, w3_ref[:, cols]) + gb_ref[:, cols]))
            log_a = -RG_C * g[0] * sp[d:d + 1, :]
            a = jnp.exp(log_a)
            a_ref[d, r0:r0 + tr, :] = a
            b_ref[d, r0:r0 + tr, :] = jnp.sqrt(-jnp.tanh(log_a) * (1.0 + a * a)) * (g[1] * xr)

    if has_state:
        h0f, h0b = st_ref[0:1, :], st_ref[1:2, :]
    else:
        h0f = h0b = jnp.zeros((1, C), F32)

    row = lax.broadcasted_iota(jnp.int32, (SUBLANES, 1), 0)

    def tile_scan(a, b, reverse):
        for d in (1, 2, 4):
            shift = SUBLANES - d if reverse else d
            valid = (row < SUBLANES - d) if reverse else (row >= d)
            a_s, b_s = pltpu.roll(a, shift, axis=0), pltpu.roll(b, shift, axis=0)
            b = jnp.where(valid, a * b_s + b, b)
            a = jnp.where(valid, a * a_s, a)
        return a, b

    def step(i, carry):
        hf, hb = carry
        t0 = pl.multiple_of(i * SUBLANES, SUBLANES)
        tb0 = pl.multiple_of(L - SUBLANES - i * SUBLANES, SUBLANES)
        af, bf = tile_scan(a_ref[0, pl.ds(t0, SUBLANES), :], b_ref[0, pl.ds(t0, SUBLANES), :], False)
        ab, bb = tile_scan(a_ref[1, pl.ds(tb0, SUBLANES), :], b_ref[1, pl.ds(tb0, SUBLANES), :], True)
        hf_tile = af * hf + bf
        hb_tile = ab * hb + bb
        h_ref[0, pl.ds(t0, SUBLANES), :] = hf_tile
        h_ref[1, pl.ds(tb0, SUBLANES), :] = hb_tile
        return hf_tile[SUBLANES - 1:SUBLANES], hb_tile[0:1]

    lax.fori_loop(0, L // SUBLANES, step, (h0f, h0b), unroll=2)
    y_ref[...] = ((h_ref[0] + h_ref[1]) * _gelu_tanh(pg_ref[...])).astype(y_ref.dtype)
    if not has_state:
        st_out_ref[0:1, :] = h_ref[0, L - 1:L, :]
        st_out_ref[1:2, :] = h_ref[1, 0:1, :]


def _block_diag(w):
    H, d, _ = w.shape
    eye = jnp.eye(H, dtype=w.dtype)
    return (eye[:, None, :, None] * w[:, :, None, :]).reshape(H * d, H * d)


def _rglru(p_g, p_x, B, L, conv_w, conv_b, wa, ba, wx, bx, lam, state):
    C = D_RG
    wcat = jnp.concatenate([_block_diag(wa[0]), _block_diag(wx[0]), _block_diag(wa[1]), _block_diag(wx[1])], axis=1)
    wh = wcat.astype(BF16)
    wl = (wcat - wh.astype(F32)).astype(BF16)
    w3 = jnp.concatenate([wh, wh, wl], axis=0)
    gb = jnp.concatenate([ba[0], bx[0], ba[1], bx[1]]).reshape(1, 4 * C)
    has_state = state is not None
    in_specs = [
        pl.BlockSpec((L, C), lambda b: (b, 0)),
        pl.BlockSpec((L, C), lambda b: (b, 0)),
        _const_spec((4, C)),
        _const_spec((1, C)),
        _const_spec((3 * C, 4 * C)),
        _const_spec((1, 4 * C)),
        _const_spec((2, C)),
    ]
    args = [p_g, p_x, conv_w, conv_b.reshape(1, C), w3, gb, lam]
    y_spec = pl.BlockSpec((L, C), lambda b: (b, 0))
    y_shape = jax.ShapeDtypeStruct((B * L, C), BF16)
    if has_state:
        in_specs.append(pl.BlockSpec((None, 2, C), lambda b: (b, 0, 0)))
        args.append(state)
        out_specs, out_shape = y_spec, y_shape
    else:
        out_specs = [y_spec, pl.BlockSpec((None, 2, C), lambda b: (b, 0, 0))]
        out_shape = [y_shape, jax.ShapeDtypeStruct((B, 2, C), F32)]
    return pl.pallas_call(
        functools.partial(_rglru_kernel, L, has_state),
        grid=(B,),
        in_specs=in_specs,
        out_specs=out_specs,
        out_shape=out_shape,
        scratch_shapes=[
            pltpu.VMEM((L + 16, C), F32),
            pltpu.VMEM((2, L, C), F32),
            pltpu.VMEM((2, L, C), F32),
            pltpu.VMEM((2, L, C), F32),
        ],
        compiler_params=_cparams("parallel"),
        name="rglru_state" if has_state else "rglru",
    )(*args)


def _attn_kernel(L, P, tq, unroll, lam_init, q_ref, k_ref, v_ref, *rest):
    if P:
        ck_ref, cv_ref, dal_ref, sub_ref, o_ref, kk_ref, vv_ref = rest
    else:
        dal_ref, sub_ref, o_ref, kk_ref, vv_ref = rest
    lv = dal_ref[...]
    s01 = jnp.sum(lv[0:1, :] * lv[1:2, :], axis=-1, keepdims=True)
    s23 = jnp.sum(lv[2:3, :] * lv[3:4, :], axis=-1, keepdims=True)
    lam = jnp.exp(s01) - jnp.exp(s23) + lam_init
    first_half = lax.broadcasted_iota(jnp.int32, (1, DA_VDIM), 1) < DA_HEAD
    sub = sub_ref[...] * (1.0 - lam_init)
    for hd in range(N_DA_HEADS):
        if P:
            kk_ref[0:P, :] = ck_ref[hd].astype(BF16)
            vv_ref[0:P, :] = cv_ref[hd].astype(BF16)
        kk_ref[P:P + L, :] = k_ref[hd].astype(BF16)
        vv_ref[P:P + L, :] = v_ref[hd].astype(BF16)

        def qblock(i, carry):
            r0 = pl.multiple_of(i * tq, tq)
            q = q_ref[hd, pl.ds(r0, tq), :]
            zero = jnp.zeros_like(q)
            qs = jnp.concatenate([jnp.where(first_half, q, zero), jnp.where(first_half, zero, q)], axis=0)
            s = _dot_nt(qs, kk_ref[...])
            p = jnp.exp2(s - jnp.max(s, axis=-1, keepdims=True))
            rinv = 1.0 / jnp.sum(p, axis=-1, keepdims=True)
            acc = _dot(p.astype(BF16), vv_ref[...])
            o = acc[0:tq] * rinv[0:tq] - acc[tq:2 * tq] * (lam * rinv[tq:2 * tq])
            o = o * lax.rsqrt(jnp.mean(o * o, axis=-1, keepdims=True) + EPS) * sub
            o_ref[pl.ds(r0, tq), hd * DA_VDIM:(hd + 1) * DA_VDIM] = o.astype(o_ref.dtype)
            return carry

        lax.fori_loop(0, L // tq, qblock, 0, unroll=unroll)


def _attention(q, k, v, layer, cache, dal, subln, lam_init, B, L, tq=128, unroll=4):
    H, dv = N_DA_HEADS, DA_VDIM
    hspec = pl.BlockSpec((None, H, L, dv), lambda b: (b, 0, 0, 0))
    kvspec = hspec if k.ndim == 4 else pl.BlockSpec((None, None, H, L, dv), lambda b: (b, layer, 0, 0, 0))
    in_specs = [hspec, kvspec, kvspec]
    args = [q, k, v]
    P = 0
    if cache is not None:
        ck, cv = cache
        P = ck.shape[3]
        cspec = pl.BlockSpec((None, None, H, P, dv), lambda b: (b, layer, 0, 0, 0))
        in_specs += [cspec, cspec]
        args += [ck, cv]
    assert L % tq == 0
    in_specs += [_const_spec((4, DA_HEAD)), _const_spec((1, dv))]
    args += [dal, subln.reshape(1, dv)]
    return pl.pallas_call(
        functools.partial(_attn_kernel, L, P, tq, min(unroll, L // tq), lam_init),
        grid=(B,),
        in_specs=in_specs,
        out_specs=pl.BlockSpec((L, H * dv), lambda b: (b, 0)),
        out_shape=jax.ShapeDtypeStruct((B * L, H * dv), BF16),
        scratch_shapes=[pltpu.VMEM((P + L, dv), BF16), pltpu.VMEM((P + L, dv), BF16)],
        compiler_params=_cparams("parallel"),
        name="diff_attn_cache" if P else "diff_attn",
    )(*args)


def _route(logits):
    m = logits[0]
    for e in range(1, N_EXPERTS):
        m = jnp.maximum(m, logits[e])
    ex = [jnp.exp(l - m) for l in logits]
    tot = ex[0]
    for e in range(1, N_EXPERTS):
        tot = tot + ex[e]
    inv = 1.0 / tot
    p = [e_ * inv for e_ in ex]
    G = EXP_PER_GROUP
    best, gsel = None, None
    for g in range(N_GROUPS):
        a = p[g * G:(g + 1) * G]
        sc = None
        for i in range(G):
            for j in range(i + 1, G):
                pair = a[i] + a[j]
                sc = pair if sc is None else jnp.maximum(sc, pair)
        if g == 0:
            best, gsel = sc, jnp.zeros_like(sc, dtype=jnp.int32)
        else:
            upd = sc > best
            best = jnp.where(upd, sc, best)
            gsel = jnp.where(upd, g, gsel)
    vals = []
    for j in range(G):
        vj = p[j]
        for g in range(1, N_GROUPS):
            vj = jnp.where(gsel == g, p[g * G + j], vj)
        vals.append(vj)
    p1, i1 = vals[0], jnp.zeros_like(gsel)
    for j in range(1, G):
        upd = vals[j] > p1
        p1 = jnp.where(upd, vals[j], p1)
        i1 = jnp.where(upd, j, i1)
    p2, i2 = None, None
    for j in range(G):
        cand = jnp.where(i1 == j, -1.0, vals[j])
        if p2 is None:
            p2, i2 = cand, jnp.zeros_like(gsel)
        else:
            upd = cand > p2
            p2 = jnp.where(upd, cand, p2)
            i2 = jnp.where(upd, j, i2)
    den = 1.0 / (p1 + p2)
    w1, w2 = p1 * den, p2 * den
    swap = i2 < i1
    a, b = jnp.where(swap, i2, i1), jnp.where(swap, i1, i2)
    w_lo, w_hi = jnp.where(swap, w2, w1), jnp.where(swap, w1, w2)
    pair = jnp.where(a == 0, b - 1, jnp.where(a == 1, b + 1, 5))
    cls = gsel * PAIRS_PER_GROUP + pair
    return cls.astype(F32), w_lo, w_hi


def _pack_pairs(x):
    n = x.shape[1] // 2
    b = pltpu.bitcast(x, jnp.uint32)
    w = (b[:, :n] >> 16) | (b[:, n:] & jnp.uint32(0xFFFF0000))
    return pltpu.bitcast(w, jnp.int32)


def _unpack_pairs(w):
    u = pltpu.bitcast(w, jnp.uint32)
    lo = pltpu.bitcast(u << 16, F32)
    hi = pltpu.bitcast(u & jnp.uint32(0xFFFF0000), F32)
    return jnp.concatenate([lo, hi], axis=1)


def _out_proj_kernel(yh_ref, yr_ref, o_ref, w_ref, x_ref, mod_ref, g_ref, wrh_ref, wrl_ref, br_ref,
                     xo_ref, h_ref, route_ref):
    y = (_dot(yh_ref[...], w_ref[0:D_HY, :]) + _dot(yr_ref[...], w_ref[D_HY:D_HY + D_RG, :])
         + _dot(o_ref[...], w_ref[D_HY + D_RG:D_MIX, :]))
    x = x_ref[...] + mod_ref[2:3, :] * y
    xo_ref[...] = x
    ms = jnp.mean(x * x, axis=-1, keepdims=True)
    h = (x * lax.rsqrt(ms + EPS) * g_ref[...]) * (1.0 + mod_ref[4:5, :]) + mod_ref[3:4, :]
    hh, hl = _split(h)
    h_ref[...] = _pack_pairs(hh.astype(F32))
    lg = _dot_nt(wrh_ref[...], hh) + _dot_nt(wrh_ref[...], hl) + _dot_nt(wrl_ref[...], hh) + br_ref[...]
    info = _route([lg[e:e + 1, :] for e in range(N_EXPERTS)])
    rt = jnp.concatenate(list(info) + [jnp.zeros((LANES - len(info), lg.shape[1]), F32)], axis=0)
    route_ref[...] = rt.T


def _out_proj(y_hy, y_rg, o, w_out, layer, x, mod, g2, w_router, b_router, B, L, ctx_rows, tm=512):
    T = B * L
    tm = min(tm, L)
    nl = L // tm
    row = (lambda i: CTX_ROW) if ctx_rows else (lambda i: i // nl)
    wrt = w_router.T
    wrh = wrt.astype(BF16)
    wrl = (wrt - wrh.astype(F32)).astype(BF16)
    rows = lambda w: pl.BlockSpec((tm, w), lambda i: (i, 0))
    return pl.pallas_call(
        _out_proj_kernel,
        grid=(T // tm,),
        in_specs=[
            rows(D_HY), rows(D_RG), rows(D_DA),
            pl.BlockSpec((None, D_MIX, D_MODEL), lambda i: (layer, 0, 0)),
            rows(D_MODEL),
            pl.BlockSpec((None, 6, D_MODEL), lambda i: (row(i), 0, 0)),
            _const_spec((1, D_MODEL)),
            _const_spec((N_EXPERTS, D_MODEL)),
            _const_spec((N_EXPERTS, D_MODEL)),
            _const_spec((N_EXPERTS, 1)),
        ],
        out_specs=[rows(D_MODEL), rows(D_MODEL // 2), rows(LANES)],
        out_shape=[
            jax.ShapeDtypeStruct((T, D_MODEL), F32),
            jax.ShapeDtypeStruct((T, D_MODEL // 2), jnp.int32),
            jax.ShapeDtypeStruct((T, LANES), F32),
        ],
        compiler_params=_cparams("parallel"),
        name="out_proj_route",
    )(y_hy, y_rg, o, w_out, x, mod, g2.reshape(1, D_MODEL), wrh, wrl, b_router.reshape(N_EXPERTS, 1))


def _gather_rows(table, idx, rows_per_step=64, n_buf=2):
    info = plsc.get_sparse_core_info()
    n_workers = info.num_cores * info.num_subcores
    n, width = idx.shape[0], table.shape[1]
    per_worker = n // n_workers
    n_steps = per_worker // rows_per_step
    assert per_worker * n_workers == n and n_steps * rows_per_step == per_worker and n_steps >= n_buf
    mesh = plsc.VectorSubcoreMesh(core_axis_name="c", subcore_axis_name="s")

    @functools.partial(
        pl.kernel, mesh=mesh,
        out_type=jax.ShapeDtypeStruct((n, width), table.dtype),
        scratch_types=[
            pltpu.VMEM((per_worker,), jnp.int32),
            pltpu.VMEM((n_buf, rows_per_step, width), table.dtype),
            pltpu.SemaphoreType.DMA((n_buf,)),
            pltpu.SemaphoreType.DMA((n_buf,)),
        ],
    )
    def gather(table_hbm, idx_hbm, out_hbm, idx_v, rows_v, sem_in, sem_out):
        worker = lax.axis_index("s") * info.num_cores + lax.axis_index("c")
        base = pl.multiple_of(worker * per_worker, per_worker)
        pltpu.sync_copy(idx_hbm.at[pl.ds(base, per_worker)], idx_v)

        def read(b, step):
            rows = idx_v.at[pl.ds(step * rows_per_step, rows_per_step)]
            return pltpu.make_async_copy(table_hbm.at[rows], rows_v.at[b], sem_in.at[b])

        def write(b, step):
            off = pl.multiple_of(base + step * rows_per_step, rows_per_step)
            return pltpu.make_async_copy(rows_v.at[b], out_hbm.at[pl.ds(off, rows_per_step)], sem_out.at[b])

        for step in range(n_steps + 1):
            if step < n_steps:
                if step >= n_buf:
                    write(step % n_buf, step - n_buf).wait()
                read(step % n_buf, step).start()
            if step >= 1:
                read((step - 1) % n_buf, step - 1).wait()
                write((step - 1) % n_buf, step - 1).start()
        for step in range(n_steps - n_buf, n_steps):
            write(step % n_buf, step).wait()

    return gather(table, idx)


def _dispatch_plan(route, tm):
    T = route.shape[0]
    n_slots = T + N_CLASSES * tm
    cls = route[:, 0].astype(jnp.int32)
    onehot = (cls[:, None] == jnp.arange(N_CLASSES, dtype=jnp.int32)[None, :]).astype(jnp.int32)
    csum = jnp.cumsum(onehot, axis=0)
    rank = jnp.sum(onehot * csum, axis=1) - 1
    counts = csum[-1]
    padded = ((counts + tm - 1) // tm) * tm
    ends = jnp.cumsum(padded)
    pos = jnp.sum(onehot * (ends - padded)[None, :], axis=1) + rank
    inv = (jnp.arange(n_slots, dtype=jnp.int32) % T).at[pos].set(jnp.arange(T, dtype=jnp.int32), unique_indices=True)
    tile_start = jnp.arange(n_slots // tm, dtype=jnp.int32) * tm
    tile_cls = jnp.minimum(jnp.searchsorted(ends, tile_start, side="right"), N_CLASSES - 1).astype(jnp.int32)
    valid = (tile_start < ends[-1]).astype(jnp.int32)
    pairs = np.array([(a, b) for a in range(EXP_PER_GROUP) for b in range(a + 1, EXP_PER_GROUP)], np.int32)
    group, pair = tile_cls // PAIRS_PER_GROUP, tile_cls % PAIRS_PER_GROUP
    lo = group * EXP_PER_GROUP + jnp.asarray(pairs[:, 0])[pair]
    hi = group * EXP_PER_GROUP + jnp.asarray(pairs[:, 1])[pair]
    return pos, inv, lo, hi, valid


def _moe_sorted_kernel(lo_ref, hi_ref, valid_ref, xs_ref, ws_ref, wg_lo, wu_lo, wd_lo, wg_hi, wu_hi, wd_hi, o_ref):
    i = pl.program_id(0)

    @pl.when(valid_ref[i] == 1)
    def _():
        x = _unpack_pairs(xs_ref[...]).astype(BF16)
        y = None
        for wg, wu, wd, col in ((wg_lo, wu_lo, wd_lo, 1), (wg_hi, wu_hi, wd_hi, 2)):
            a = _dot(x, wg[...])
            he = (a * _sigmoid(a)) * _dot(x, wu[...]) * ws_ref[:, col:col + 1]
            part = _dot(he.astype(BF16), wd[...])
            y = part if y is None else y + part
        o_ref[...] = _pack_pairs(y.astype(BF16).astype(F32))

    @pl.when(valid_ref[i] == 0)
    def _():
        o_ref[...] = jnp.zeros_like(o_ref)


def _moe_sorted(xs, ws, lo, hi, valid, wg, wu, wd, tm):
    n_slots = xs.shape[0]
    half = D_MODEL // 2
    up = lambda sel: pl.BlockSpec((None, D_MODEL, D_EXPERT), lambda i, lo, hi, v: ((lo, hi)[sel][i], 0, 0))
    down = lambda sel: pl.BlockSpec((None, D_EXPERT, D_MODEL), lambda i, lo, hi, v: ((lo, hi)[sel][i], 0, 0))
    return pl.pallas_call(
        _moe_sorted_kernel,
        grid_spec=pltpu.PrefetchScalarGridSpec(
            num_scalar_prefetch=3,
            grid=(n_slots // tm,),
            in_specs=[
                pl.BlockSpec((tm, half), lambda i, lo, hi, v: (i, 0)),
                pl.BlockSpec((tm, LANES), lambda i, lo, hi, v: (i, 0)),
                up(0), up(0), down(0), up(1), up(1), down(1),
            ],
            out_specs=pl.BlockSpec((tm, half), lambda i, lo, hi, v: (i, 0)),
        ),
        out_shape=jax.ShapeDtypeStruct((n_slots, half), jnp.int32),
        compiler_params=_cparams("arbitrary"),
        name="moe_sorted",
    )(lo, hi, valid, xs, ws, wg, wu, wd, wg, wu, wd)


def _final_residual_kernel(y_ref, x_ref, mod_ref, fg_ref, o_ref):
    x = x_ref[...] + mod_ref[5:6, :] * _unpack_pairs(y_ref[...])
    o_ref[...] = x * lax.rsqrt(jnp.mean(x * x, axis=-1, keepdims=True) + EPS) * fg_ref[...]


def _final_residual(y, x, mod, final_g, B, L, ctx_rows, tm=512):
    T = B * L
    tm = min(tm, L)
    nl = L // tm
    row = (lambda i: CTX_ROW) if ctx_rows else (lambda i: i // nl)
    return pl.pallas_call(
        _final_residual_kernel,
        grid=(T // tm,),
        in_specs=[
            pl.BlockSpec((tm, D_MODEL // 2), lambda i: (i, 0)),
            pl.BlockSpec((tm, D_MODEL), lambda i: (i, 0)),
            pl.BlockSpec((None, 6, D_MODEL), lambda i: (row(i), 0, 0)),
            _const_spec((1, D_MODEL)),
        ],
        out_specs=pl.BlockSpec((tm, D_MODEL), lambda i: (i, 0)),
        out_shape=jax.ShapeDtypeStruct((T, D_MODEL), F32),
        compiler_params=_cparams("parallel"),
        name="final_residual",
    )(y, x, mod, final_g.reshape(1, D_MODEL))


def _moe(h, route, wg, wu, wd, layer, tm=256):
    pos, inv, lo, hi, valid = _dispatch_plan(route, tm)
    xs = _gather_rows(h, inv)
    ws = _gather_rows(route, inv)
    ys = _moe_sorted(xs, ws, lo + layer * N_EXPERTS, hi + layer * N_EXPERTS, valid, wg, wu, wd, tm)
    return _gather_rows(ys, pos)


def kernel(x_prompt, x_sample, cache_k, cache_v, state_rglru, c, c_ctx, w_ada, b_ada, norm1_g, norm2_g, w_in, w_out, hy_short_w, hy_short_b, hy_w1, hy_b1, hy_w2, hy_b2, hy_w3, hy_freq, hy_bias, rg_conv_w, rg_conv_b, rg_wa, rg_ba, rg_wx, rg_bx, rg_lambda, da_lambda, da_subln, w_router, b_router, moe_wg, moe_wu, moe_wd, final_g):
    Bp, Lp, D = x_prompt.shape
    Bs, Ls, _ = x_sample.shape
    assert Bs <= CTX_ROW
    cond = jnp.zeros((N_COND, D), F32).at[:Bs].set(c).at[CTX_ROW].set(c_ctx)
    mods = _ada_table(cond, w_ada, b_ada)

    dft = {L: tuple(jnp.asarray(m).astype(BF16) for m in _dft_mats(L)) for L in (Lp, Ls)}
    streams = [
        dict(B=Bp, L=Lp, ctx=True, x=x_prompt.reshape(Bp * Lp, D)),
        dict(B=Bs, L=Ls, ctx=False, x=x_sample.reshape(Bs * Ls, D)),
    ]
    w_in_b, w_out_b = w_in.astype(BF16), w_out.astype(BF16)
    wg, wu, wd = (w.astype(BF16).reshape((DEPTH * N_EXPERTS,) + w.shape[2:]) for w in (moe_wg, moe_wu, moe_wd))
    kv_shape = (Bp, DEPTH, N_DA_HEADS, Lp, DA_VDIM)
    new_kv, ss = (jnp.zeros(kv_shape, F32), jnp.zeros(kv_shape, F32)), []
    for l in range(DEPTH):
        lam_init = 0.8 - 0.6 * math.exp(-0.3 * l)
        for st in streams:
            B, L, ctx = st["B"], st["L"], st["ctx"]
            cmat, smat = dft[L]
            outs = _norm_proj(st["x"], mods[l], norm1_g[l], w_in_b, l, B, L, ctx, kv_prev=new_kv if ctx else None,
                              pending=(st["y"], mods[l - 1]) if l else None)
            p_hy, p_g, p_x, q, k, v = outs[:6]
            if l:
                st["x"] = outs[6]
            kre, kim = _hy_spectra(L, cmat, smat, hy_w1[l], hy_b1[l], hy_w2[l], hy_b2[l], hy_w3[l], hy_freq[l])
            y_hy = _hyena(p_hy, B, L, cmat, smat, kre, kim, hy_short_w[l], hy_short_b[l], hy_bias[l])
            rg_args = (rg_conv_w[l], rg_conv_b[l], rg_wa[l], rg_ba[l], rg_wx[l], rg_bx[l], rg_lambda[l])
            if ctx:
                y_rg, s_l = _rglru(p_g, p_x, B, L, *rg_args, None)
                o = _attention(q, k, v, l, None, da_lambda[l], da_subln[l], lam_init, B, L)
                new_kv = (k, v)
                ss.append(s_l)
            else:
                y_rg = _rglru(p_g, p_x, B, L, *rg_args, state_rglru[:, l])
                o = _attention(q, k, v, l, (cache_k, cache_v), da_lambda[l], da_subln[l], lam_init, B, L)
            st["x"], h2, route = _out_proj(y_hy, y_rg, o, w_out_b, l, st["x"], mods[l], norm2_g[l],
                                           w_router, b_router, B, L, ctx)
            st["y"] = _moe(h2, route, wg, wu, wd, l)
    y_prompt, y_sample = (
        _final_residual(st["y"], st["x"], mods[DEPTH - 1], final_g, st["B"], st["L"], st["ctx"]).reshape(shape)
        for st, shape in zip(streams, (x_prompt.shape, x_sample.shape)))
    return (y_prompt, y_sample, new_kv[0], new_kv[1], jnp.stack(ss, axis=1))
```

```python
import functools
import math

import numpy as np
import jax
import jax.numpy as jnp
from jax import lax
from jax.experimental import pallas as pl
from jax.experimental.pallas import tpu as pltpu
from jax.experimental.pallas import tpu_sc as plsc

F32 = jnp.float32
BF16 = jnp.bfloat16

D_MODEL = 1024
DEPTH = 2
GRID_W = 64
D_HY = 256
HY_EMB = 33
HY_BANDS = (HY_EMB - 1) // 2
HY_FFN = 64
HY_MIN_DECAY = math.log(1e-2) / 1.5
HY_MAX_DECAY = math.log(1e-2) / 0.3
D_RG = 256
N_RG_HEADS = 4
RG_C = 8.0
N_DA_HEADS = 4
DA_HEAD = 64
DA_VDIM = 2 * DA_HEAD
D_DA = N_DA_HEADS * DA_VDIM
D_MIX = D_HY + D_RG + D_DA
D_IN = 3 * D_HY + 2 * D_RG + 3 * D_DA
ROPE_PAIRS = DA_HEAD // 4
ROPE_THETA = 10000.0
N_EXPERTS = 16
N_GROUPS = 4
EXP_PER_GROUP = N_EXPERTS // N_GROUPS
D_EXPERT = 512
PAIRS_PER_GROUP = EXP_PER_GROUP * (EXP_PER_GROUP - 1) // 2
N_CLASSES = N_GROUPS * PAIRS_PER_GROUP
EPS = 1e-6
N_COND = 16
CTX_ROW = 8
LANES = 128
SUBLANES = 8
VMEM_LIMIT = 56 * 1024 * 1024


def _cparams(*sem):
    return pltpu.CompilerParams(dimension_semantics=sem, vmem_limit_bytes=VMEM_LIMIT)


def _split(x):
    hi = x.astype(BF16)
    lo = (x - hi.astype(F32)).astype(BF16)
    return hi, lo


def _dot(a, b):
    return jnp.dot(a, b, preferred_element_type=F32)


def _dot3(a, b):
    ah, al = _split(a)
    bh, bl = _split(b)
    return _dot(ah, bh) + _dot(al, bh) + _dot(ah, bl)


def _dot_nt(a, b):
    return lax.dot_general(a, b, (((1,), (1,)), ((), ())), preferred_element_type=F32)


def _sigmoid(x):
    return 1.0 / (1.0 + jnp.exp(-x))


def _const_spec(shape):
    n = len(shape)
    return pl.BlockSpec(shape, lambda *_: (0,) * n)


def _ada_kernel(c_ref, w_ref, b_ref, o_ref):
    c = c_ref[...]
    s = c * _sigmoid(c)
    o_ref[...] = _dot3(s, w_ref[...]) + b_ref[...]


def _ada_table(cond, w_ada, b_ada):
    D = D_MODEL
    out = pl.pallas_call(
        _ada_kernel,
        grid=(DEPTH, 6),
        in_specs=[
            pl.BlockSpec((N_COND, D), lambda l, j: (0, 0)),
            pl.BlockSpec((None, D, D), lambda l, j: (l, 0, j)),
            pl.BlockSpec((None, None, 1, D), lambda l, j: (l, j, 0, 0)),
        ],
        out_specs=pl.BlockSpec((None, None, N_COND, D), lambda l, j: (l, j, 0, 0)),
        out_shape=jax.ShapeDtypeStruct((DEPTH, 6, N_COND, D), F32),
        compiler_params=_cparams("parallel", "parallel"),
        name="ada_table",
    )(cond, w_ada, b_ada.reshape(DEPTH, 6, 1, D))
    return out.transpose(0, 2, 1, 3)


def _rope_tables(L):
    t = np.arange(L)
    j = np.arange(LANES)
    jj = j % DA_HEAD
    is_col = (jj // (DA_HEAD // 2)) == 1
    pair = jj % ROPE_PAIRS
    second = (jj % (DA_HEAD // 2)) >= ROPE_PAIRS
    inv = ROPE_THETA ** (-np.arange(ROPE_PAIRS, dtype=np.float64) / ROPE_PAIRS)
    pos = np.where(is_col[None, :], (t % GRID_W)[:, None], (t // GRID_W)[:, None]).astype(np.float64)
    ang = pos * inv[pair][None, :]
    cos = np.cos(ang).astype(np.float32)
    sin = np.sin(ang).astype(np.float32)
    sin_a = np.where(second[None, :], 0.0, -sin).astype(np.float32)
    sin_b = np.where(second[None, :], sin, 0.0).astype(np.float32)
    return cos, sin_a, sin_b


def _rope(x, cos, sin_a, sin_b):
    nxt = pltpu.roll(x, LANES - ROPE_PAIRS, axis=1)
    prv = pltpu.roll(x, ROPE_PAIRS, axis=1)
    return x * cos + nxt * sin_a + prv * sin_b


def _norm_proj_kernel(rope, kv_dtype, pending, x_ref, mod_ref, g_ref, w_ref, *rest):
    x = x_ref[...]
    if pending:
        y_ref, modp_ref, xnew_ref = rest[0], rest[1], rest[-1]
        x = x + modp_ref[5:6, :] * _unpack_pairs(y_ref[...])
        xnew_ref[...] = x
        rest = rest[2:-1]
    if rope:
        cos_ref, sa_ref, sb_ref = rest[:3]
    phy_ref, pg_ref, px_ref, q_ref, k_ref, v_ref = rest[-6:]
    ms = jnp.mean(x * x, axis=-1, keepdims=True)
    y = x * lax.rsqrt(ms + EPS) * g_ref[...]
    h = (y * (1.0 + mod_ref[1:2, :]) + mod_ref[0:1, :]).astype(BF16)
    o = 3 * D_HY
    phy_ref[...] = _dot(h, w_ref[:, 0:o]).astype(BF16)
    pg_ref[...] = _dot(h, w_ref[:, o:o + D_RG])
    px_ref[...] = _dot(h, w_ref[:, o + D_RG:o + 2 * D_RG])
    o += 2 * D_RG
    q = _dot(h, w_ref[:, o:o + D_DA]) * (DA_HEAD ** -0.5 * math.log2(math.e))
    k = _dot(h, w_ref[:, o + D_DA:o + 2 * D_DA])
    v = _dot(h, w_ref[:, o + 2 * D_DA:o + 3 * D_DA])
    if rope:
        cos, sa, sb = cos_ref[...], sa_ref[...], sb_ref[...]
    for hd in range(N_DA_HEADS):
        sl = slice(hd * DA_VDIM, (hd + 1) * DA_VDIM)
        qh, kh = q[:, sl], k[:, sl]
        if rope:
            qh = _rope(qh, cos, sa, sb)
            kh = _rope(kh, cos, sa, sb)
        q_ref[hd] = qh.astype(BF16)
        k_ref[hd] = kh.astype(kv_dtype)
        v_ref[hd] = v[:, sl].astype(kv_dtype)


def _norm_proj(x, mod, g, w_in, layer, B, L, ctx, kv_prev=None, pending=None, tm=512):
    T = B * L
    tm = min(tm, L)
    nl = L // tm
    rope, kv_dtype = not ctx, (F32 if ctx else BF16)
    row = (lambda i: CTX_ROW) if ctx else (lambda i: i // nl)
    mod_spec = pl.BlockSpec((None, 6, D_MODEL), lambda i: (row(i), 0, 0))
    in_specs = [
        pl.BlockSpec((tm, D_MODEL), lambda i: (i, 0)),
        mod_spec,
        _const_spec((1, D_MODEL)),
        pl.BlockSpec((None, D_MODEL, D_IN), lambda i: (layer, 0, 0)),
    ]
    args = [x, mod, g.reshape(1, D_MODEL), w_in]
    if pending is not None:
        in_specs += [pl.BlockSpec((tm, D_MODEL // 2), lambda i: (i, 0)), mod_spec]
        args += list(pending)
    if rope:
        tabs = _rope_tables(L)
        in_specs += [pl.BlockSpec((tm, LANES), lambda i: (i % nl, 0))] * 3
        args += [jnp.asarray(t) for t in tabs]
    head_spec = pl.BlockSpec((None, N_DA_HEADS, tm, DA_VDIM), lambda i: (i // nl, 0, i % nl, 0))
    head_shape = (B, N_DA_HEADS, L, DA_VDIM)
    kv_spec, kv_shape, aliases = head_spec, head_shape, {}
    if ctx:
        kv_spec = pl.BlockSpec((None, None, N_DA_HEADS, tm, DA_VDIM), lambda i: (i // nl, layer, 0, i % nl, 0))
        kv_shape = (B, DEPTH, N_DA_HEADS, L, DA_VDIM)
        if kv_prev is not None:
            aliases = {len(args): 4, len(args) + 1: 5}
            in_specs += [pl.BlockSpec(memory_space=pl.ANY)] * 2
            args += list(kv_prev)
    out_specs = [
        pl.BlockSpec((tm, 3 * D_HY), lambda i: (i, 0)),
        pl.BlockSpec((tm, D_RG), lambda i: (i, 0)),
        pl.BlockSpec((tm, D_RG), lambda i: (i, 0)),
        head_spec, kv_spec, kv_spec,
    ]
    out_shape = [
        jax.ShapeDtypeStruct((T, 3 * D_HY), BF16),
        jax.ShapeDtypeStruct((T, D_RG), F32),
        jax.ShapeDtypeStruct((T, D_RG), F32),
        jax.ShapeDtypeStruct(head_shape, BF16),
        jax.ShapeDtypeStruct(kv_shape, kv_dtype),
        jax.ShapeDtypeStruct(kv_shape, kv_dtype),
    ]
    if pending is not None:
        out_specs.append(pl.BlockSpec((tm, D_MODEL), lambda i: (i, 0)))
        out_shape.append(jax.ShapeDtypeStruct((T, D_MODEL), F32))
    return pl.pallas_call(
        functools.partial(_norm_proj_kernel, rope, kv_dtype, pending is not None),
        grid=(T // tm,),
        in_specs=in_specs,
        out_specs=out_specs,
        out_shape=out_shape,
        input_output_aliases=aliases,
        compiler_params=_cparams("parallel"),
        name="norm_proj_rope" if rope else "norm_proj",
    )(*args)


def _dft_mats(L):
    n = 2 * L - 1
    fs = (np.arange(L, dtype=np.int64)[:, None] * np.arange(L, dtype=np.int64)[None, :]) % n
    ang = fs.astype(np.float64) * (2.0 * np.pi / n)
    return np.cos(ang).astype(np.float32), np.sin(ang).astype(np.float32)


def _hy_features(L):
    t = np.linspace(0.0, 1.0, L, dtype=np.float64)[:, None]
    ang = ((2.0 * math.pi / L) * np.arange(L, dtype=np.float64))[:, None]
    bands = np.linspace(1e-4, HY_BANDS - 1, HY_BANDS, dtype=np.float64)[None, :]
    ba = bands * ang
    z = np.concatenate([t, np.cos(ba), -np.sin(ba)], axis=-1).astype(np.float32)
    return np.pad(z, ((0, 0), (0, LANES - HY_EMB)))


def _hy_filter_kernel(L, z_ref, w1_ref, b1_ref, w2_ref, b2_ref, w3_ref, fr_ref, rc_ref, rs_ref):
    z = z_ref[...]
    h = jnp.sin(fr_ref[0:1, :] * (_dot3(z, w1_ref[...]) + b1_ref[...]))
    h = jnp.sin(fr_ref[1:2, :] * (_dot3(h, w2_ref[...]) + b2_ref[...]))
    h = _dot3(h, w3_ref[...])
    t = z[:, 0:1]
    step = (HY_MAX_DECAY - HY_MIN_DECAY) / (D_HY - 1)
    deltas = HY_MIN_DECAY + step * lax.broadcasted_iota(jnp.int32, (1, D_HY), 1).astype(F32)
    window = jnp.exp(-t * jnp.abs(deltas))
    not_first = lax.broadcasted_iota(jnp.int32, (L, 1), 0) > 0
    for o in range(2):
        hf = h[:, (2 * o) * D_HY:(2 * o + 1) * D_HY] * window
        hb = jnp.where(not_first, h[:, (2 * o + 1) * D_HY:(2 * o + 2) * D_HY] * window, 0.0)
        rc_ref[:, o * D_HY:(o + 1) * D_HY] = hf + hb
        rs_ref[:, o * D_HY:(o + 1) * D_HY] = hb - hf


def _hy_spectrum_kernel(c_ref, s_ref, rc_ref, rs_ref, w_ref, kre_ref, kim_ref):
    rch, rcl = _split(rc_ref[...])
    rsh, rsl = _split(rs_ref[...])
    c, s, w = c_ref[...], s_ref[...], w_ref[...]
    kre_ref[...] = (_dot(c, rch) + _dot(c, rcl)) * w
    kim_ref[...] = (_dot(s, rsh) + _dot(s, rsl)) * w


def _hy_spectra(L, cmat, smat, w1, b1, w2, b2, w3, freq):
    z = jnp.asarray(_hy_features(L))
    w1p = jnp.pad(w1, ((0, LANES - HY_EMB), (0, 0)))
    nw = 2 * D_HY
    rc, rs = pl.pallas_call(
        functools.partial(_hy_filter_kernel, L),
        out_shape=[jax.ShapeDtypeStruct((L, nw), F32)] * 2,
        compiler_params=pltpu.CompilerParams(vmem_limit_bytes=VMEM_LIMIT),
        name="hy_filter",
    )(z, w1p, b1.reshape(1, HY_FFN), w2, b2.reshape(1, HY_FFN), w3, freq)
    n = 2 * L - 1
    wsc = np.full((L, 1), 2.0 / n, np.float32)
    wsc[0, 0] = 1.0 / n
    tr = min(L, 256)
    return pl.pallas_call(
        _hy_spectrum_kernel,
        grid=(L // tr,),
        in_specs=[
            pl.BlockSpec((tr, L), lambda i: (i, 0)),
            pl.BlockSpec((tr, L), lambda i: (i, 0)),
            _const_spec((L, nw)),
            _const_spec((L, nw)),
            pl.BlockSpec((tr, 1), lambda i: (i, 0)),
        ],
        out_specs=[pl.BlockSpec((tr, nw), lambda i: (i, 0))] * 2,
        out_shape=[jax.ShapeDtypeStruct((L, nw), F32)] * 2,
        compiler_params=_cparams("parallel"),
        name="hy_spectrum",
    )(cmat, smat, rc, rs, jnp.asarray(wsc))


def _hyena_kernel(L, tr, p_ref, sw_ref, sb_ref, bias_ref, c_ref, s_ref, kre_ref, kim_ref, o_ref,
                  pad_ref, u_ref, sig_ref, sig16_ref, zre_ref, zim_ref):
    C3 = 3 * D_HY
    zeros = jnp.zeros((8, C3), F32)
    pad_ref[0:8, :] = zeros
    pad_ref[8 + L:16 + L, :] = zeros
    chunks = [slice(r0, r0 + tr) for r0 in range(0, L, tr)]
    for c in chunks:
        pad_ref[8 + c.start:8 + c.stop, :] = p_ref[c, :].astype(F32)
    for c in chunks:
        u = sb_ref[...]
        for j in range(3):
            u = u + pad_ref[7 + j + c.start:7 + j + c.stop, :] * sw_ref[j:j + 1, :]
        u_ref[c, :] = u[:, D_HY:C3]
        sig_ref[c, :] = u[:, 0:D_HY]
        sig16_ref[c, :] = u[:, 0:D_HY].astype(BF16)

    for o in range(2):
        ko = slice(o * D_HY, (o + 1) * D_HY)
        for c in chunks:
            ure = _dot(c_ref[c, :], sig16_ref[...])
            us = _dot(s_ref[c, :], sig16_ref[...])
            kre, kim = kre_ref[c, ko], kim_ref[c, ko]
            zre_ref[c, :] = (ure * kre + us * kim).astype(BF16)
            zim_ref[c, :] = (ure * kim - us * kre).astype(BF16)
        gate = slice(o * D_HY, (o + 1) * D_HY)
        for c in chunks:
            y = _dot(c_ref[c, :], zre_ref[...]) - _dot(s_ref[c, :], zim_ref[...])
            z = u_ref[c, gate] * (y + sig_ref[c, :] * bias_ref[o:o + 1, :])
            if o == 0:
                sig_ref[c, :] = z
                sig16_ref[c, :] = z.astype(BF16)
            else:
                o_ref[c, :] = z.astype(o_ref.dtype)


def _hyena(p_hy, B, L, cmat, smat, kre, kim, short_w, short_b, bias, tr=512):
    C3 = 3 * D_HY
    tr = min(tr, L)
    once = pl.Buffered(1)
    return pl.pallas_call(
        functools.partial(_hyena_kernel, L, tr),
        grid=(B,),
        in_specs=[
            pl.BlockSpec((L, C3), lambda b: (b, 0)),
            _const_spec((3, C3)),
            _const_spec((1, C3)),
            _const_spec((2, D_HY)),
            pl.BlockSpec((L, L), lambda b: (0, 0), pipeline_mode=once),
            pl.BlockSpec((L, L), lambda b: (0, 0), pipeline_mode=once),
            pl.BlockSpec((L, 2 * D_HY), lambda b: (0, 0), pipeline_mode=once),
            pl.BlockSpec((L, 2 * D_HY), lambda b: (0, 0), pipeline_mode=once),
        ],
        out_specs=pl.BlockSpec((L, D_HY), lambda b: (b, 0)),
        out_shape=jax.ShapeDtypeStruct((B * L, D_HY), BF16),
        scratch_shapes=[
            pltpu.VMEM((L + 16, C3), F32),
            pltpu.VMEM((L, 2 * D_HY), F32),
            pltpu.VMEM((L, D_HY), F32),
            pltpu.VMEM((L, D_HY), BF16),
            pltpu.VMEM((L, D_HY), BF16),
            pltpu.VMEM((L, D_HY), BF16),
        ],
        compiler_params=_cparams("parallel"),
        name="hyena",
    )(p_hy, short_w, short_b.reshape(1, C3), bias, cmat, smat, kre, kim)


def _softplus(z):
    return jnp.maximum(z, 0.0) + jnp.log1p(jnp.exp(-jnp.abs(z)))


def _sigmoid_tanh(x):
    return 0.5 + 0.5 * jnp.tanh(0.5 * x)


def _gelu_tanh(x):
    return 0.5 * x * (1.0 + jnp.tanh(math.sqrt(2.0 / math.pi) * (x + 0.044715 * x * x * x)))


def _rglru_kernel(L, has_state, pg_ref, px_ref, cw_ref, cb_ref, w3_ref, gb_ref, lam_ref, *rest):
    if has_state:
        st_ref, y_ref, pad_ref, a_ref, b_ref, h_ref = rest
    else:
        y_ref, st_out_ref, pad_ref, a_ref, b_ref, h_ref = rest
    C = D_RG
    zeros = jnp.zeros((8, C), F32)
    pad_ref[0:8, :] = zeros
    pad_ref[8 + L:16 + L, :] = zeros
    pad_ref[8:8 + L, :] = px_ref[...]
    sp = _softplus(-lam_ref[...])
    tr = min(L, 256)
    for r0 in range(0, L, tr):
        xr = cb_ref[...]
        for j in range(4):
            xr = xr + pad_ref[6 + j + r0:6 + j + r0 + tr, :] * cw_ref[j:j + 1, :]
        xh, xl = _split(xr)
        x3 = jnp.concatenate([xh, xl, xh], axis=1)
        for d in range(2):
            g = []
            for m in range(2):
                cols = slice((2 * d + m) * C, (2 * d + m + 1) * C)
                g.append(_sigmoid_tanh(_dot(x3, w3_ref[:, cols]) + gb_ref[:, cols]))
            log_a = -RG_C * g[0] * sp[d:d + 1, :]
            a = jnp.exp(log_a)
            a_ref[d, r0:r0 + tr, :] = a
            b_ref[d, r0:r0 + tr, :] = jnp.sqrt(-jnp.tanh(log_a) * (1.0 + a * a)) * (g[1] * xr)

    if has_state:
        h0f, h0b = st_ref[0:1, :], st_ref[1:2, :]
    else:
        h0f = h0b = jnp.zeros((1, C), F32)

    row = lax.broadcasted_iota(jnp.int32, (SUBLANES, 1), 0)

    def tile_scan(a, b, reverse):
        for d in (1, 2, 4):
            shift = SUBLANES - d if reverse else d
            valid = (row < SUBLANES - d) if reverse else (row >= d)
            a_s, b_s = pltpu.roll(a, shift, axis=0), pltpu.roll(b, shift, axis=0)
            b = jnp.where(valid, a * b_s + b, b)
            a = jnp.where(valid, a * a_s, a)
        return a, b

    def step(i, carry):
        hf, hb = carry
        t0 = pl.multiple_of(i * SUBLANES, SUBLANES)
        tb0 = pl.multiple_of(L - SUBLANES - i * SUBLANES, SUBLANES)
        af, bf = tile_scan(a_ref[0, pl.ds(t0, SUBLANES), :], b_ref[0, pl.ds(t0, SUBLANES), :], False)
        ab, bb = tile_scan(a_ref[1, pl.ds(tb0, SUBLANES), :], b_ref[1, pl.ds(tb0, SUBLANES), :], True)
        hf_tile = af * hf + bf
        hb_tile = ab * hb + bb
        h_ref[0, pl.ds(t0, SUBLANES), :] = hf_tile
        h_ref[1, pl.ds(tb0, SUBLANES), :] = hb_tile
        return hf_tile[SUBLANES - 1:SUBLANES], hb_tile[0:1]

    lax.fori_loop(0, L // SUBLANES, step, (h0f, h0b), unroll=2)
    y_ref[...] = ((h_ref[0] + h_ref[1]) * _gelu_tanh(pg_ref[...])).astype(y_ref.dtype)
    if not has_state:
        st_out_ref[0:1, :] = h_ref[0, L - 1:L, :]
        st_out_ref[1:2, :] = h_ref[1, 0:1, :]


def _block_diag(w):
    H, d, _ = w.shape
    eye = jnp.eye(H, dtype=w.dtype)
    return (eye[:, None, :, None] * w[:, :, None, :]).reshape(H * d, H * d)


def _rglru(p_g, p_x, B, L, conv_w, conv_b, wa, ba, wx, bx, lam, state):
    C = D_RG
    wcat = jnp.concatenate([_block_diag(wa[0]), _block_diag(wx[0]), _block_diag(wa[1]), _block_diag(wx[1])], axis=1)
    wh = wcat.astype(BF16)
    wl = (wcat - wh.astype(F32)).astype(BF16)
    w3 = jnp.concatenate([wh, wh, wl], axis=0)
    gb = jnp.concatenate([ba[0], bx[0], ba[1], bx[1]]).reshape(1, 4 * C)
    has_state = state is not None
    in_specs = [
        pl.BlockSpec((L, C), lambda b: (b, 0)),
        pl.BlockSpec((L, C), lambda b: (b, 0)),
        _const_spec((4, C)),
        _const_spec((1, C)),
        _const_spec((3 * C, 4 * C)),
        _const_spec((1, 4 * C)),
        _const_spec((2, C)),
    ]
    args = [p_g, p_x, conv_w, conv_b.reshape(1, C), w3, gb, lam]
    y_spec = pl.BlockSpec((L, C), lambda b: (b, 0))
    y_shape = jax.ShapeDtypeStruct((B * L, C), BF16)
    if has_state:
        in_specs.append(pl.BlockSpec((None, 2, C), lambda b: (b, 0, 0)))
        args.append(state)
        out_specs, out_shape = y_spec, y_shape
    else:
        out_specs = [y_spec, pl.BlockSpec((None, 2, C), lambda b: (b, 0, 0))]
        out_shape = [y_shape, jax.ShapeDtypeStruct((B, 2, C), F32)]
    return pl.pallas_call(
        functools.partial(_rglru_kernel, L, has_state),
        grid=(B,),
        in_specs=in_specs,
        out_specs=out_specs,
        out_shape=out_shape,
        scratch_shapes=[
            pltpu.VMEM((L + 16, C), F32),
            pltpu.VMEM((2, L, C), F32),
            pltpu.VMEM((2, L, C), F32),
            pltpu.VMEM((2, L, C), F32),
        ],
        compiler_params=_cparams("parallel"),
        name="rglru_state" if has_state else "rglru",
    )(*args)


def _attn_kernel(L, P, tq, unroll, lam_init, q_ref, k_ref, v_ref, *rest):
    if P:
        ck_ref, cv_ref, dal_ref, sub_ref, o_ref, kk_ref, vv_ref, s_ref = rest
    else:
        dal_ref, sub_ref, o_ref, kk_ref, vv_ref, s_ref = rest
    lv = dal_ref[...]
    s01 = jnp.sum(lv[0:1, :] * lv[1:2, :], axis=-1, keepdims=True)
    s23 = jnp.sum(lv[2:3, :] * lv[3:4, :], axis=-1, keepdims=True)
    lam = jnp.exp(s01) - jnp.exp(s23) + lam_init
    first_half = lax.broadcasted_iota(jnp.int32, (1, DA_VDIM), 1) < DA_HEAD
    sub = sub_ref[...] * (1.0 - lam_init)
    for hd in range(N_DA_HEADS):
        if P:
            kk_ref[0:P, :] = ck_ref[hd].astype(BF16)
            vv_ref[0:P, :] = cv_ref[hd].astype(BF16)
        kk_ref[P:P + L, :] = k_ref[hd].astype(BF16)
        vv_ref[P:P + L, :] = v_ref[hd].astype(BF16)

        def scores(i, buf):
            q = q_ref[hd, pl.ds(pl.multiple_of(i * tq, tq), tq), :]
            zero = jnp.zeros_like(q)
            qs = jnp.concatenate([jnp.where(first_half, q, zero), jnp.where(first_half, zero, q)], axis=0)
            s_ref[buf] = _dot_nt(qs, kk_ref[...])

        def finish(i, buf):
            s = s_ref[buf]
            p = jnp.exp2(s - jnp.max(s, axis=-1, keepdims=True))
            rinv = 1.0 / jnp.sum(p, axis=-1, keepdims=True)
            acc = _dot(p.astype(BF16), vv_ref[...])
            o = acc[0:tq] * rinv[0:tq] - acc[tq:2 * tq] * (lam * rinv[tq:2 * tq])
            o = o * lax.rsqrt(jnp.mean(o * o, axis=-1, keepdims=True) + EPS) * sub
            r0 = pl.multiple_of(i * tq, tq)
            o_ref[pl.ds(r0, tq), hd * DA_VDIM:(hd + 1) * DA_VDIM] = o.astype(o_ref.dtype)

        def pair(j, carry):
            i = 2 * j
            scores(i + 1, 1)
            finish(i, 0)
            scores(i + 2, 0)
            finish(i + 1, 1)
            return carry

        n = L // tq
        scores(0, 0)
        lax.fori_loop(0, n // 2 - 1, pair, 0, unroll=unroll)
        scores(n - 1, 1)
        finish(n - 2, 0)
        finish(n - 1, 1)


def _attention(q, k, v, layer, cache, dal, subln, lam_init, B, L, tq=128, unroll=2):
    H, dv = N_DA_HEADS, DA_VDIM
    hspec = pl.BlockSpec((None, H, L, dv), lambda b: (b, 0, 0, 0))
    kvspec = hspec if k.ndim == 4 else pl.BlockSpec((None, None, H, L, dv), lambda b: (b, layer, 0, 0, 0))
    in_specs = [hspec, kvspec, kvspec]
    args = [q, k, v]
    P = 0
    if cache is not None:
        ck, cv = cache
        P = ck.shape[3]
        cspec = pl.BlockSpec((None, None, H, P, dv), lambda b: (b, layer, 0, 0, 0))
        in_specs += [cspec, cspec]
        args += [ck, cv]
    assert L % tq == 0
    in_specs += [_const_spec((4, DA_HEAD)), _const_spec((1, dv))]
    args += [dal, subln.reshape(1, dv)]
    return pl.pallas_call(
        functools.partial(_attn_kernel, L, P, tq, min(unroll, max(1, L // tq // 2 - 1)), lam_init),
        grid=(B,),
        in_specs=in_specs,
        out_specs=pl.BlockSpec((L, H * dv), lambda b: (b, 0)),
        out_shape=jax.ShapeDtypeStruct((B * L, H * dv), BF16),
        scratch_shapes=[pltpu.VMEM((P + L, dv), BF16), pltpu.VMEM((P + L, dv), BF16),
                        pltpu.VMEM((2, 2 * tq, P + L), F32)],
        compiler_params=_cparams("parallel"),
        name="diff_attn_cache" if P else "diff_attn",
    )(*args)


def _route(logits):
    m = logits[0]
    for e in range(1, N_EXPERTS):
        m = jnp.maximum(m, logits[e])
    ex = [jnp.exp(l - m) for l in logits]
    tot = ex[0]
    for e in range(1, N_EXPERTS):
        tot = tot + ex[e]
    inv = 1.0 / tot
    p = [e_ * inv for e_ in ex]
    G = EXP_PER_GROUP
    best, gsel = None, None
    for g in range(N_GROUPS):
        a = p[g * G:(g + 1) * G]
        sc = None
        for i in range(G):
            for j in range(i + 1, G):
                pair = a[i] + a[j]
                sc = pair if sc is None else jnp.maximum(sc, pair)
        if g == 0:
            best, gsel = sc, jnp.zeros_like(sc, dtype=jnp.int32)
        else:
            upd = sc > best
            best = jnp.where(upd, sc, best)
            gsel = jnp.where(upd, g, gsel)
    vals = []
    for j in range(G):
        vj = p[j]
        for g in range(1, N_GROUPS):
            vj = jnp.where(gsel == g, p[g * G + j], vj)
        vals.append(vj)
    p1, i1 = vals[0], jnp.zeros_like(gsel)
    for j in range(1, G):
        upd = vals[j] > p1
        p1 = jnp.where(upd, vals[j], p1)
        i1 = jnp.where(upd, j, i1)
    p2, i2 = None, None
    for j in range(G):
        cand = jnp.where(i1 == j, -1.0, vals[j])
        if p2 is None:
            p2, i2 = cand, jnp.zeros_like(gsel)
        else:
            upd = cand > p2
            p2 = jnp.where(upd, cand, p2)
            i2 = jnp.where(upd, j, i2)
    den = 1.0 / (p1 + p2)
    w1, w2 = p1 * den, p2 * den
    swap = i2 < i1
    a, b = jnp.where(swap, i2, i1), jnp.where(swap, i1, i2)
    w_lo, w_hi = jnp.where(swap, w2, w1), jnp.where(swap, w1, w2)
    pair = jnp.where(a == 0, b - 1, jnp.where(a == 1, b + 1, 5))
    cls = gsel * PAIRS_PER_GROUP + pair
    return cls.astype(F32), w_lo, w_hi


def _pack_pairs(x):
    n = x.shape[1] // 2
    b = pltpu.bitcast(x, jnp.uint32)
    w = (b[:, :n] >> 16) | (b[:, n:] & jnp.uint32(0xFFFF0000))
    return pltpu.bitcast(w, jnp.int32)


def _unpack_pairs(w):
    u = pltpu.bitcast(w, jnp.uint32)
    lo = pltpu.bitcast(u << 16, F32)
    hi = pltpu.bitcast(u & jnp.uint32(0xFFFF0000), F32)
    return jnp.concatenate([lo, hi], axis=1)


def _out_proj_kernel(yh_ref, yr_ref, o_ref, w_ref, x_ref, mod_ref, g_ref, wrh_ref, wrl_ref, br_ref,
                     xo_ref, h_ref, route_ref):
    y = (_dot(yh_ref[...], w_ref[0:D_HY, :]) + _dot(yr_ref[...], w_ref[D_HY:D_HY + D_RG, :])
         + _dot(o_ref[...], w_ref[D_HY + D_RG:D_MIX, :]))
    x = x_ref[...] + mod_ref[2:3, :] * y
    xo_ref[...] = x
    ms = jnp.mean(x * x, axis=-1, keepdims=True)
    h = (x * lax.rsqrt(ms + EPS) * g_ref[...]) * (1.0 + mod_ref[4:5, :]) + mod_ref[3:4, :]
    hh, hl = _split(h)
    h_ref[...] = _pack_pairs(hh.astype(F32))
    lg = _dot_nt(wrh_ref[...], hh) + _dot_nt(wrh_ref[...], hl) + _dot_nt(wrl_ref[...], hh) + br_ref[...]
    info = _route([lg[e:e + 1, :] for e in range(N_EXPERTS)])
    rt = jnp.concatenate(list(info) + [jnp.zeros((LANES - len(info), lg.shape[1]), F32)], axis=0)
    route_ref[...] = rt.T


def _out_proj(y_hy, y_rg, o, w_out, layer, x, mod, g2, w_router, b_router, B, L, ctx_rows, tm=512):
    T = B * L
    tm = min(tm, L)
    nl = L // tm
    row = (lambda i: CTX_ROW) if ctx_rows else (lambda i: i // nl)
    wrt = w_router.T
    wrh = wrt.astype(BF16)
    wrl = (wrt - wrh.astype(F32)).astype(BF16)
    rows = lambda w: pl.BlockSpec((tm, w), lambda i: (i, 0))
    return pl.pallas_call(
        _out_proj_kernel,
        grid=(T // tm,),
        in_specs=[
            rows(D_HY), rows(D_RG), rows(D_DA),
            pl.BlockSpec((None, D_MIX, D_MODEL), lambda i: (layer, 0, 0)),
            rows(D_MODEL),
            pl.BlockSpec((None, 6, D_MODEL), lambda i: (row(i), 0, 0)),
            _const_spec((1, D_MODEL)),
            _const_spec((N_EXPERTS, D_MODEL)),
            _const_spec((N_EXPERTS, D_MODEL)),
            _const_spec((N_EXPERTS, 1)),
        ],
        out_specs=[rows(D_MODEL), rows(D_MODEL // 2), rows(LANES)],
        out_shape=[
            jax.ShapeDtypeStruct((T, D_MODEL), F32),
            jax.ShapeDtypeStruct((T, D_MODEL // 2), jnp.int32),
            jax.ShapeDtypeStruct((T, LANES), F32),
        ],
        compiler_params=_cparams("parallel"),
        name="out_proj_route",
    )(y_hy, y_rg, o, w_out, x, mod, g2.reshape(1, D_MODEL), wrh, wrl, b_router.reshape(N_EXPERTS, 1))


def _gather_rows(table, idx, rows_per_step=64, n_buf=2):
    info = plsc.get_sparse_core_info()
    n_workers = info.num_cores * info.num_subcores
    n, width = idx.shape[0], table.shape[1]
    per_worker = n // n_workers
    n_steps = per_worker // rows_per_step
    assert per_worker * n_workers == n and n_steps * rows_per_step == per_worker and n_steps >= n_buf
    mesh = plsc.VectorSubcoreMesh(core_axis_name="c", subcore_axis_name="s")

    @functools.partial(
        pl.kernel, mesh=mesh,
        out_type=jax.ShapeDtypeStruct((n, width), table.dtype),
        scratch_types=[
            pltpu.VMEM((per_worker,), jnp.int32),
            pltpu.VMEM((n_buf, rows_per_step, width), table.dtype),
            pltpu.SemaphoreType.DMA((n_buf,)),
            pltpu.SemaphoreType.DMA((n_buf,)),
        ],
    )
    def gather(table_hbm, idx_hbm, out_hbm, idx_v, rows_v, sem_in, sem_out):
        worker = lax.axis_index("s") * info.num_cores + lax.axis_index("c")
        base = pl.multiple_of(worker * per_worker, per_worker)
        pltpu.sync_copy(idx_hbm.at[pl.ds(base, per_worker)], idx_v)

        def read(b, step):
            rows = idx_v.at[pl.ds(step * rows_per_step, rows_per_step)]
            return pltpu.make_async_copy(table_hbm.at[rows], rows_v.at[b], sem_in.at[b])

        def write(b, step):
            off = pl.multiple_of(base + step * rows_per_step, rows_per_step)
            return pltpu.make_async_copy(rows_v.at[b], out_hbm.at[pl.ds(off, rows_per_step)], sem_out.at[b])

        for step in range(n_steps + 1):
            if step < n_steps:
                if step >= n_buf:
                    write(step % n_buf, step - n_buf).wait()
                read(step % n_buf, step).start()
            if step >= 1:
                read((step - 1) % n_buf, step - 1).wait()
                write((step - 1) % n_buf, step - 1).start()
        for step in range(n_steps - n_buf, n_steps):
            write(step % n_buf, step).wait()

    return gather(table, idx)


def _dispatch_plan(route, tm):
    T = route.shape[0]
    n_slots = T + N_CLASSES * tm
    cls = route[:, 0].astype(jnp.int32)
    onehot = (cls[:, None] == jnp.arange(N_CLASSES, dtype=jnp.int32)[None, :]).astype(jnp.int32)
    csum = jnp.cumsum(onehot, axis=0)
    rank = jnp.sum(onehot * csum, axis=1) - 1
    counts = csum[-1]
    padded = ((counts + tm - 1) // tm) * tm
    ends = jnp.cumsum(padded)
    pos = jnp.sum(onehot * (ends - padded)[None, :], axis=1) + rank
    inv = (jnp.arange(n_slots, dtype=jnp.int32) % T).at[pos].set(jnp.arange(T, dtype=jnp.int32), unique_indices=True)
    tile_start = jnp.arange(n_slots // tm, dtype=jnp.int32) * tm
    tile_cls = jnp.minimum(jnp.searchsorted(ends, tile_start, side="right"), N_CLASSES - 1).astype(jnp.int32)
    valid = (tile_start < ends[-1]).astype(jnp.int32)
    pairs = np.array([(a, b) for a in range(EXP_PER_GROUP) for b in range(a + 1, EXP_PER_GROUP)], np.int32)
    group, pair = tile_cls // PAIRS_PER_GROUP, tile_cls % PAIRS_PER_GROUP
    lo = group * EXP_PER_GROUP + jnp.asarray(pairs[:, 0])[pair]
    hi = group * EXP_PER_GROUP + jnp.asarray(pairs[:, 1])[pair]
    return pos, inv, lo, hi, valid


def _moe_sorted_kernel(lo_ref, hi_ref, valid_ref, xs_ref, ws_ref, wg_lo, wu_lo, wd_lo, wg_hi, wu_hi, wd_hi, o_ref):
    i = pl.program_id(0)

    @pl.when(valid_ref[i] == 1)
    def _():
        x = _unpack_pairs(xs_ref[...]).astype(BF16)
        y = None
        for wg, wu, wd, col in ((wg_lo, wu_lo, wd_lo, 1), (wg_hi, wu_hi, wd_hi, 2)):
            a = _dot(x, wg[...])
            he = (a * _sigmoid(a)) * _dot(x, wu[...]) * ws_ref[:, col:col + 1]
            part = _dot(he.astype(BF16), wd[...])
            y = part if y is None else y + part
        o_ref[...] = _pack_pairs(y.astype(BF16).astype(F32))

    @pl.when(valid_ref[i] == 0)
    def _():
        o_ref[...] = jnp.zeros_like(o_ref)


def _moe_sorted(xs, ws, lo, hi, valid, wg, wu, wd, tm):
    n_slots = xs.shape[0]
    half = D_MODEL // 2
    up = lambda sel: pl.BlockSpec((None, D_MODEL, D_EXPERT), lambda i, lo, hi, v: ((lo, hi)[sel][i], 0, 0))
    down = lambda sel: pl.BlockSpec((None, D_EXPERT, D_MODEL), lambda i, lo, hi, v: ((lo, hi)[sel][i], 0, 0))
    return pl.pallas_call(
        _moe_sorted_kernel,
        grid_spec=pltpu.PrefetchScalarGridSpec(
            num_scalar_prefetch=3,
            grid=(n_slots // tm,),
            in_specs=[
                pl.BlockSpec((tm, half), lambda i, lo, hi, v: (i, 0)),
                pl.BlockSpec((tm, LANES), lambda i, lo, hi, v: (i, 0)),
                up(0), up(0), down(0), up(1), up(1), down(1),
            ],
            out_specs=pl.BlockSpec((tm, half), lambda i, lo, hi, v: (i, 0)),
        ),
        out_shape=jax.ShapeDtypeStruct((n_slots, half), jnp.int32),
        compiler_params=_cparams("arbitrary"),
        name="moe_sorted",
    )(lo, hi, valid, xs, ws, wg, wu, wd, wg, wu, wd)


def _final_residual_kernel(y_ref, x_ref, mod_ref, fg_ref, o_ref):
    x = x_ref[...] + mod_ref[5:6, :] * _unpack_pairs(y_ref[...])
    o_ref[...] = x * lax.rsqrt(jnp.mean(x * x, axis=-1, keepdims=True) + EPS) * fg_ref[...]


def _final_residual(y, x, mod, final_g, B, L, ctx_rows, tm=512):
    T = B * L
    tm = min(tm, L)
    nl = L // tm
    row = (lambda i: CTX_ROW) if ctx_rows else (lambda i: i // nl)
    return pl.pallas_call(
        _final_residual_kernel,
        grid=(T // tm,),
        in_specs=[
            pl.BlockSpec((tm, D_MODEL // 2), lambda i: (i, 0)),
            pl.BlockSpec((tm, D_MODEL), lambda i: (i, 0)),
            pl.BlockSpec((None, 6, D_MODEL), lambda i: (row(i), 0, 0)),
            _const_spec((1, D_MODEL)),
        ],
        out_specs=pl.BlockSpec((tm, D_MODEL), lambda i: (i, 0)),
        out_shape=jax.ShapeDtypeStruct((T, D_MODEL), F32),
        compiler_params=_cparams("parallel"),
        name="final_residual",
    )(y, x, mod, final_g.reshape(1, D_MODEL))


def _moe(h, route, wg, wu, wd, layer, tm=256):
    pos, inv, lo, hi, valid = _dispatch_plan(route, tm)
    xs = _gather_rows(h, inv)
    ws = _gather_rows(route, inv)
    ys = _moe_sorted(xs, ws, lo + layer * N_EXPERTS, hi + layer * N_EXPERTS, valid, wg, wu, wd, tm)
    return _gather_rows(ys, pos)


def kernel(x_prompt, x_sample, cache_k, cache_v, state_rglru, c, c_ctx, w_ada, b_ada, norm1_g, norm2_g, w_in, w_out, hy_short_w, hy_short_b, hy_w1, hy_b1, hy_w2, hy_b2, hy_w3, hy_freq, hy_bias, rg_conv_w, rg_conv_b, rg_wa, rg_ba, rg_wx, rg_bx, rg_lambda, da_lambda, da_subln, w_router, b_router, moe_wg, moe_wu, moe_wd, final_g):
    Bp, Lp, D = x_prompt.shape
    Bs, Ls, _ = x_sample.shape
    assert Bs <= CTX_ROW
    cond = jnp.zeros((N_COND, D), F32).at[:Bs].set(c).at[CTX_ROW].set(c_ctx)
    mods = _ada_table(cond, w_ada, b_ada)

    dft = {L: tuple(jnp.asarray(m).astype(BF16) for m in _dft_mats(L)) for L in (Lp, Ls)}
    streams = [
        dict(B=Bp, L=Lp, ctx=True, x=x_prompt.reshape(Bp * Lp, D)),
        dict(B=Bs, L=Ls, ctx=False, x=x_sample.reshape(Bs * Ls, D)),
    ]
    w_in_b, w_out_b = w_in.astype(BF16), w_out.astype(BF16)
    wg, wu, wd = (w.astype(BF16).reshape((DEPTH * N_EXPERTS,) + w.shape[2:]) for w in (moe_wg, moe_wu, moe_wd))
    kv_shape = (Bp, DEPTH, N_DA_HEADS, Lp, DA_VDIM)
    new_kv, ss = (jnp.zeros(kv_shape, F32), jnp.zeros(kv_shape, F32)), []
    for l in range(DEPTH):
        lam_init = 0.8 - 0.6 * math.exp(-0.3 * l)
        for st in streams:
            B, L, ctx = st["B"], st["L"], st["ctx"]
            cmat, smat = dft[L]
            outs = _norm_proj(st["x"], mods[l], norm1_g[l], w_in_b, l, B, L, ctx, kv_prev=new_kv if ctx else None,
                              pending=(st["y"], mods[l - 1]) if l else None)
            p_hy, p_g, p_x, q, k, v = outs[:6]
            if l:
                st["x"] = outs[6]
            kre, kim = _hy_spectra(L, cmat, smat, hy_w1[l], hy_b1[l], hy_w2[l], hy_b2[l], hy_w3[l], hy_freq[l])
            y_hy = _hyena(p_hy, B, L, cmat, smat, kre, kim, hy_short_w[l], hy_short_b[l], hy_bias[l])
            rg_args = (rg_conv_w[l], rg_conv_b[l], rg_wa[l], rg_ba[l], rg_wx[l], rg_bx[l], rg_lambda[l])
            if ctx:
                y_rg, s_l = _rglru(p_g, p_x, B, L, *rg_args, None)
                o = _attention(q, k, v, l, None, da_lambda[l], da_subln[l], lam_init, B, L)
                new_kv = (k, v)
                ss.append(s_l)
            else:
                y_rg = _rglru(p_g, p_x, B, L, *rg_args, state_rglru[:, l])
                o = _attention(q, k, v, l, (cache_k, cache_v), da_lambda[l], da_subln[l], lam_init, B, L)
            st["x"], h2, route = _out_proj(y_hy, y_rg, o, w_out_b, l, st["x"], mods[l], norm2_g[l],
                                           w_router, b_router, B, L, ctx)
            st["y"] = _moe(h2, route, wg, wu, wd, l)
    y_prompt, y_sample = (
        _final_residual(st["y"], st["x"], mods[DEPTH - 1], final_g, st["B"], st["L"], st["ctx"]).reshape(shape)
        for st, shape in zip(streams, (x_prompt.shape, x_sample.shape)))
    return (y_prompt, y_sample, new_kv[0], new_kv[1], jnp.stack(ss, axis=1))
```

```python
import functools
import math

import numpy as np
import jax
import jax.numpy as jnp
from jax import lax
from jax.experimental import pallas as pl
from jax.experimental.pallas import tpu as pltpu
from jax.experimental.pallas import tpu_sc as plsc

F32 = jnp.float32
BF16 = jnp.bfloat16

D_MODEL = 1024
DEPTH = 2
GRID_W = 64
D_HY = 256
HY_EMB = 33
HY_BANDS = (HY_EMB - 1) // 2
HY_FFN = 64
HY_MIN_DECAY = math.log(1e-2) / 1.5
HY_MAX_DECAY = math.log(1e-2) / 0.3
D_RG = 256
N_RG_HEADS = 4
RG_C = 8.0
N_DA_HEADS = 4
DA_HEAD = 64
DA_VDIM = 2 * DA_HEAD
D_DA = N_DA_HEADS * DA_VDIM
D_MIX = D_HY + D_RG + D_DA
D_IN = 3 * D_HY + 2 * D_RG + 3 * D_DA
ROPE_PAIRS = DA_HEAD // 4
ROPE_THETA = 10000.0
N_EXPERTS = 16
N_GROUPS = 4
EXP_PER_GROUP = N_EXPERTS // N_GROUPS
D_EXPERT = 512
PAIRS_PER_GROUP = EXP_PER_GROUP * (EXP_PER_GROUP - 1) // 2
N_CLASSES = N_GROUPS * PAIRS_PER_GROUP
EPS = 1e-6
N_COND = 16
CTX_ROW = 8
LANES = 128
SUBLANES = 8
VMEM_LIMIT = 56 * 1024 * 1024


def _cparams(*sem):
    return pltpu.CompilerParams(dimension_semantics=sem, vmem_limit_bytes=VMEM_LIMIT)


def _split(x):
    hi = x.astype(BF16)
    lo = (x - hi.astype(F32)).astype(BF16)
    return hi, lo


def _dot(a, b):
    return jnp.dot(a, b, preferred_element_type=F32)


def _dot3(a, b):
    ah, al = _split(a)
    bh, bl = _split(b)
    return _dot(ah, bh) + _dot(al, bh) + _dot(ah, bl)


def _dot_nt(a, b):
    return lax.dot_general(a, b, (((1,), (1,)), ((), ())), preferred_element_type=F32)


def _sigmoid(x):
    return 1.0 / (1.0 + jnp.exp(-x))


def _const_spec(shape):
    n = len(shape)
    return pl.BlockSpec(shape, lambda *_: (0,) * n)


def _ada_kernel(c_ref, w_ref, b_ref, o_ref):
    c = c_ref[...]
    s = c * _sigmoid(c)
    o_ref[...] = _dot3(s, w_ref[...]) + b_ref[...]


def _ada_table(cond, w_ada, b_ada):
    D = D_MODEL
    out = pl.pallas_call(
        _ada_kernel,
        grid=(DEPTH, 6),
        in_specs=[
            pl.BlockSpec((N_COND, D), lambda l, j: (0, 0)),
            pl.BlockSpec((None, D, D), lambda l, j: (l, 0, j)),
            pl.BlockSpec((None, None, 1, D), lambda l, j: (l, j, 0, 0)),
        ],
        out_specs=pl.BlockSpec((None, None, N_COND, D), lambda l, j: (l, j, 0, 0)),
        out_shape=jax.ShapeDtypeStruct((DEPTH, 6, N_COND, D), F32),
        compiler_params=_cparams("parallel", "parallel"),
        name="ada_table",
    )(cond, w_ada, b_ada.reshape(DEPTH, 6, 1, D))
    return out.transpose(0, 2, 1, 3)


def _rope_tables(L):
    t = np.arange(L)
    j = np.arange(LANES)
    jj = j % DA_HEAD
    is_col = (jj // (DA_HEAD // 2)) == 1
    pair = jj % ROPE_PAIRS
    second = (jj % (DA_HEAD // 2)) >= ROPE_PAIRS
    inv = ROPE_THETA ** (-np.arange(ROPE_PAIRS, dtype=np.float64) / ROPE_PAIRS)
    pos = np.where(is_col[None, :], (t % GRID_W)[:, None], (t // GRID_W)[:, None]).astype(np.float64)
    ang = pos * inv[pair][None, :]
    cos = np.cos(ang).astype(np.float32)
    sin = np.sin(ang).astype(np.float32)
    sin_a = np.where(second[None, :], 0.0, -sin).astype(np.float32)
    sin_b = np.where(second[None, :], sin, 0.0).astype(np.float32)
    return cos, sin_a, sin_b


def _rope(x, cos, sin_a, sin_b):
    nxt = pltpu.roll(x, LANES - ROPE_PAIRS, axis=1)
    prv = pltpu.roll(x, ROPE_PAIRS, axis=1)
    return x * cos + nxt * sin_a + prv * sin_b


def _norm_proj_kernel(rope, kv_dtype, pending, x_ref, mod_ref, g_ref, w_ref, *rest):
    x = x_ref[...]
    if pending:
        y_ref, modp_ref, xnew_ref = rest[0], rest[1], rest[-1]
        x = x + modp_ref[5:6, :] * _unpack_pairs(y_ref[...])
        xnew_ref[...] = x
        rest = rest[2:-1]
    if rope:
        cos_ref, sa_ref, sb_ref = rest[:3]
    phy_ref, pg_ref, px_ref, q_ref, k_ref, v_ref = rest[-6:]
    ms = jnp.mean(x * x, axis=-1, keepdims=True)
    y = x * lax.rsqrt(ms + EPS) * g_ref[...]
    h = (y * (1.0 + mod_ref[1:2, :]) + mod_ref[0:1, :]).astype(BF16)
    o = 3 * D_HY
    phy_ref[...] = _dot(h, w_ref[:, 0:o]).astype(BF16)
    pg_ref[...] = _dot(h, w_ref[:, o:o + D_RG])
    px_ref[...] = _dot(h, w_ref[:, o + D_RG:o + 2 * D_RG])
    o += 2 * D_RG
    q = _dot(h, w_ref[:, o:o + D_DA]) * (DA_HEAD ** -0.5 * math.log2(math.e))
    k = _dot(h, w_ref[:, o + D_DA:o + 2 * D_DA])
    v = _dot(h, w_ref[:, o + 2 * D_DA:o + 3 * D_DA])
    if rope:
        cos, sa, sb = cos_ref[...], sa_ref[...], sb_ref[...]
    for hd in range(N_DA_HEADS):
        sl = slice(hd * DA_VDIM, (hd + 1) * DA_VDIM)
        qh, kh = q[:, sl], k[:, sl]
        if rope:
            qh = _rope(qh, cos, sa, sb)
            kh = _rope(kh, cos, sa, sb)
        q_ref[hd] = qh.astype(BF16)
        k_ref[hd] = kh.astype(kv_dtype)
        v_ref[hd] = v[:, sl].astype(kv_dtype)


def _norm_proj(x, mod, g, w_in, layer, B, L, ctx, kv_prev=None, pending=None, tm=512):
    T = B * L
    tm = min(tm, L)
    nl = L // tm
    rope, kv_dtype = not ctx, (F32 if ctx else BF16)
    row = (lambda i: CTX_ROW) if ctx else (lambda i: i // nl)
    mod_spec = pl.BlockSpec((None, 6, D_MODEL), lambda i: (row(i), 0, 0))
    in_specs = [
        pl.BlockSpec((tm, D_MODEL), lambda i: (i, 0)),
        mod_spec,
        _const_spec((1, D_MODEL)),
        pl.BlockSpec((None, D_MODEL, D_IN), lambda i: (layer, 0, 0)),
    ]
    args = [x, mod, g.reshape(1, D_MODEL), w_in]
    if pending is not None:
        in_specs += [pl.BlockSpec((tm, D_MODEL // 2), lambda i: (i, 0)), mod_spec]
        args += list(pending)
    if rope:
        tabs = _rope_tables(L)
        in_specs += [pl.BlockSpec((tm, LANES), lambda i: (i % nl, 0))] * 3
        args += [jnp.asarray(t) for t in tabs]
    head_spec = pl.BlockSpec((None, N_DA_HEADS, tm, DA_VDIM), lambda i: (i // nl, 0, i % nl, 0))
    head_shape = (B, N_DA_HEADS, L, DA_VDIM)
    kv_spec, kv_shape, aliases = head_spec, head_shape, {}
    if ctx:
        kv_spec = pl.BlockSpec((None, None, N_DA_HEADS, tm, DA_VDIM), lambda i: (i // nl, layer, 0, i % nl, 0))
        kv_shape = (B, DEPTH, N_DA_HEADS, L, DA_VDIM)
        if kv_prev is not None:
            aliases = {len(args): 4, len(args) + 1: 5}
            in_specs += [pl.BlockSpec(memory_space=pl.ANY)] * 2
            args += list(kv_prev)
    out_specs = [
        pl.BlockSpec((tm, 3 * D_HY), lambda i: (i, 0)),
        pl.BlockSpec((tm, D_RG), lambda i: (i, 0)),
        pl.BlockSpec((tm, D_RG), lambda i: (i, 0)),
        head_spec, kv_spec, kv_spec,
    ]
    out_shape = [
        jax.ShapeDtypeStruct((T, 3 * D_HY), BF16),
        jax.ShapeDtypeStruct((T, D_RG), F32),
        jax.ShapeDtypeStruct((T, D_RG), F32),
        jax.ShapeDtypeStruct(head_shape, BF16),
        jax.ShapeDtypeStruct(kv_shape, kv_dtype),
        jax.ShapeDtypeStruct(kv_shape, kv_dtype),
    ]
    if pending is not None:
        out_specs.append(pl.BlockSpec((tm, D_MODEL), lambda i: (i, 0)))
        out_shape.append(jax.ShapeDtypeStruct((T, D_MODEL), F32))
    return pl.pallas_call(
        functools.partial(_norm_proj_kernel, rope, kv_dtype, pending is not None),
        grid=(T // tm,),
        in_specs=in_specs,
        out_specs=out_specs,
        out_shape=out_shape,
        input_output_aliases=aliases,
        compiler_params=_cparams("parallel"),
        name="norm_proj_rope" if rope else "norm_proj",
    )(*args)


def _dft_mats(L):
    n = 2 * L - 1
    fs = (np.arange(L, dtype=np.int64)[:, None] * np.arange(L, dtype=np.int64)[None, :]) % n
    ang = fs.astype(np.float64) * (2.0 * np.pi / n)
    return np.cos(ang).astype(np.float32), np.sin(ang).astype(np.float32)


def _hy_features(L):
    t = np.linspace(0.0, 1.0, L, dtype=np.float64)[:, None]
    ang = ((2.0 * math.pi / L) * np.arange(L, dtype=np.float64))[:, None]
    bands = np.linspace(1e-4, HY_BANDS - 1, HY_BANDS, dtype=np.float64)[None, :]
    ba = bands * ang
    z = np.concatenate([t, np.cos(ba), -np.sin(ba)], axis=-1).astype(np.float32)
    return np.pad(z, ((0, 0), (0, LANES - HY_EMB)))


def _hy_filter_kernel(L, z_ref, w1_ref, b1_ref, w2_ref, b2_ref, w3_ref, fr_ref, rc_ref, rs_ref):
    z = z_ref[...]
    h = jnp.sin(fr_ref[0:1, :] * (_dot3(z, w1_ref[...]) + b1_ref[...]))
    h = jnp.sin(fr_ref[1:2, :] * (_dot3(h, w2_ref[...]) + b2_ref[...]))
    h = _dot3(h, w3_ref[...])
    t = z[:, 0:1]
    step = (HY_MAX_DECAY - HY_MIN_DECAY) / (D_HY - 1)
    deltas = HY_MIN_DECAY + step * lax.broadcasted_iota(jnp.int32, (1, D_HY), 1).astype(F32)
    window = jnp.exp(-t * jnp.abs(deltas))
    not_first = lax.broadcasted_iota(jnp.int32, (L, 1), 0) > 0
    for o in range(2):
        hf = h[:, (2 * o) * D_HY:(2 * o + 1) * D_HY] * window
        hb = jnp.where(not_first, h[:, (2 * o + 1) * D_HY:(2 * o + 2) * D_HY] * window, 0.0)
        rc_ref[:, o * D_HY:(o + 1) * D_HY] = hf + hb
        rs_ref[:, o * D_HY:(o + 1) * D_HY] = hb - hf


def _hy_spectrum_kernel(c_ref, s_ref, rc_ref, rs_ref, w_ref, kre_ref, kim_ref):
    rch, rcl = _split(rc_ref[...])
    rsh, rsl = _split(rs_ref[...])
    c, s, w = c_ref[...], s_ref[...], w_ref[...]
    kre_ref[...] = (_dot(c, rch) + _dot(c, rcl)) * w
    kim_ref[...] = (_dot(s, rsh) + _dot(s, rsl)) * w


def _hy_spectra(L, cmat, smat, w1, b1, w2, b2, w3, freq):
    z = jnp.asarray(_hy_features(L))
    w1p = jnp.pad(w1, ((0, LANES - HY_EMB), (0, 0)))
    nw = 2 * D_HY
    rc, rs = pl.pallas_call(
        functools.partial(_hy_filter_kernel, L),
        out_shape=[jax.ShapeDtypeStruct((L, nw), F32)] * 2,
        compiler_params=pltpu.CompilerParams(vmem_limit_bytes=VMEM_LIMIT),
        name="hy_filter",
    )(z, w1p, b1.reshape(1, HY_FFN), w2, b2.reshape(1, HY_FFN), w3, freq)
    n = 2 * L - 1
    wsc = np.full((L, 1), 2.0 / n, np.float32)
    wsc[0, 0] = 1.0 / n
    tr = min(L, 256)
    return pl.pallas_call(
        _hy_spectrum_kernel,
        grid=(L // tr,),
        in_specs=[
            pl.BlockSpec((tr, L), lambda i: (i, 0)),
            pl.BlockSpec((tr, L), lambda i: (i, 0)),
            _const_spec((L, nw)),
            _const_spec((L, nw)),
            pl.BlockSpec((tr, 1), lambda i: (i, 0)),
        ],
        out_specs=[pl.BlockSpec((tr, nw), lambda i: (i, 0))] * 2,
        out_shape=[jax.ShapeDtypeStruct((L, nw), F32)] * 2,
        compiler_params=_cparams("parallel"),
        name="hy_spectrum",
    )(cmat, smat, rc, rs, jnp.asarray(wsc))


def _hyena_kernel(L, tr, p_ref, sw_ref, sb_ref, bias_ref, c_ref, s_ref, kre_ref, kim_ref, o_ref,
                  pad_ref, u_ref, sig_ref, sig16_ref, zre_ref, zim_ref):
    C3 = 3 * D_HY
    zeros = jnp.zeros((8, C3), F32)
    pad_ref[0:8, :] = zeros
    pad_ref[8 + L:16 + L, :] = zeros
    chunks = [slice(r0, r0 + tr) for r0 in range(0, L, tr)]
    for c in chunks:
        pad_ref[8 + c.start:8 + c.stop, :] = p_ref[c, :].astype(F32)
    for c in chunks:
        u = sb_ref[...]
        for j in range(3):
            u = u + pad_ref[7 + j + c.start:7 + j + c.stop, :] * sw_ref[j:j + 1, :]
        u_ref[c, :] = u[:, D_HY:C3]
        sig_ref[c, :] = u[:, 0:D_HY]
        sig16_ref[c, :] = u[:, 0:D_HY].astype(BF16)

    for o in range(2):
        ko = slice(o * D_HY, (o + 1) * D_HY)
        for c in chunks:
            ure = _dot(c_ref[c, :], sig16_ref[...])
            us = _dot(s_ref[c, :], sig16_ref[...])
            kre, kim = kre_ref[c, ko], kim_ref[c, ko]
            zre_ref[c, :] = (ure * kre + us * kim).astype(BF16)
            zim_ref[c, :] = (ure * kim - us * kre).astype(BF16)
        gate = slice(o * D_HY, (o + 1) * D_HY)
        for c in chunks:
            y = _dot(c_ref[c, :], zre_ref[...]) - _dot(s_ref[c, :], zim_ref[...])
            z = u_ref[c, gate] * (y + sig_ref[c, :] * bias_ref[o:o + 1, :])
            if o == 0:
                sig_ref[c, :] = z
                sig16_ref[c, :] = z.astype(BF16)
            else:
                o_ref[c, :] = z.astype(o_ref.dtype)


def _hyena(p_hy, B, L, cmat, smat, kre, kim, short_w, short_b, bias, tr=512):
    C3 = 3 * D_HY
    tr = min(tr, L)
    once = pl.Buffered(1)
    return pl.pallas_call(
        functools.partial(_hyena_kernel, L, tr),
        grid=(B,),
        in_specs=[
            pl.BlockSpec((L, C3), lambda b: (b, 0)),
            _const_spec((3, C3)),
            _const_spec((1, C3)),
            _const_spec((2, D_HY)),
            pl.BlockSpec((L, L), lambda b: (0, 0), pipeline_mode=once),
            pl.BlockSpec((L, L), lambda b: (0, 0), pipeline_mode=once),
            pl.BlockSpec((L, 2 * D_HY), lambda b: (0, 0), pipeline_mode=once),
            pl.BlockSpec((L, 2 * D_HY), lambda b: (0, 0), pipeline_mode=once),
        ],
        out_specs=pl.BlockSpec((L, D_HY), lambda b: (b, 0)),
        out_shape=jax.ShapeDtypeStruct((B * L, D_HY), BF16),
        scratch_shapes=[
            pltpu.VMEM((L + 16, C3), F32),
            pltpu.VMEM((L, 2 * D_HY), F32),
            pltpu.VMEM((L, D_HY), F32),
            pltpu.VMEM((L, D_HY), BF16),
            pltpu.VMEM((L, D_HY), BF16),
            pltpu.VMEM((L, D_HY), BF16),
        ],
        compiler_params=_cparams("parallel"),
        name="hyena",
    )(p_hy, short_w, short_b.reshape(1, C3), bias, cmat, smat, kre, kim)


def _softplus(z):
    return jnp.maximum(z, 0.0) + jnp.log1p(jnp.exp(-jnp.abs(z)))


def _sigmoid_tanh(x):
    return 0.5 + 0.5 * jnp.tanh(0.5 * x)


def _gelu_tanh(x):
    return 0.5 * x * (1.0 + jnp.tanh(math.sqrt(2.0 / math.pi) * (x + 0.044715 * x * x * x)))


def _rglru_kernel(L, has_state, pg_ref, px_ref, cw_ref, cb_ref, w3_ref, gb_ref, lam_ref, *rest):
    if has_state:
        st_ref, y_ref, pad_ref, a_ref, b_ref, h_ref = rest
    else:
        y_ref, st_out_ref, pad_ref, a_ref, b_ref, h_ref = rest
    C = D_RG
    zeros = jnp.zeros((8, C), F32)
    pad_ref[0:8, :] = zeros
    pad_ref[8 + L:16 + L, :] = zeros
    pad_ref[8:8 + L, :] = px_ref[...]
    sp = _softplus(-lam_ref[...])
    tr = min(L, 256)
    for r0 in range(0, L, tr):
        xr = cb_ref[...]
        for j in range(4):
            xr = xr + pad_ref[6 + j + r0:6 + j + r0 + tr, :] * cw_ref[j:j + 1, :]
        xh, xl = _split(xr)
        x3 = jnp.concatenate([xh, xl, xh], axis=1)
        for d in range(2):
            g = []
            for m in range(2):
                cols = slice((2 * d + m) * C, (2 * d + m + 1) * C)
                g.append(_sigmoid_tanh(_dot(x3, w3_ref[:, cols]) + gb_ref[:, cols]))
            log_a = -RG_C * g[0] * sp[d:d + 1, :]
            a = jnp.exp(log_a)
            a_ref[d, r0:r0 + tr, :] = a
            b_ref[d, r0:r0 + tr, :] = jnp.sqrt(-jnp.tanh(log_a) * (1.0 + a * a)) * (g[1] * xr)

    if has_state:
        h0f, h0b = st_ref[0:1, :], st_ref[1:2, :]
    else:
        h0f = h0b = jnp.zeros((1, C), F32)

    row = lax.broadcasted_iota(jnp.int32, (SUBLANES, 1), 0)

    def tile_scan(a, b, reverse):
        for d in (1, 2, 4):
            shift = SUBLANES - d if reverse else d
            valid = (row < SUBLANES - d) if reverse else (row >= d)
            a_s, b_s = pltpu.roll(a, shift, axis=0), pltpu.roll(b, shift, axis=0)
            b = jnp.where(valid, a * b_s + b, b)
            a = jnp.where(valid, a * a_s, a)
        return a, b

    def step(i, carry):
        hf, hb = carry
        t0 = pl.multiple_of(i * SUBLANES, SUBLANES)
        tb0 = pl.multiple_of(L - SUBLANES - i * SUBLANES, SUBLANES)
        af, bf = tile_scan(a_ref[0, pl.ds(t0, SUBLANES), :], b_ref[0, pl.ds(t0, SUBLANES), :], False)
        ab, bb = tile_scan(a_ref[1, pl.ds(tb0, SUBLANES), :], b_ref[1, pl.ds(tb0, SUBLANES), :], True)
        hf_tile = af * hf + bf
        hb_tile = ab * hb + bb
        h_ref[0, pl.ds(t0, SUBLANES), :] = hf_tile
        h_ref[1, pl.ds(tb0, SUBLANES), :] = hb_tile
        return hf_tile[SUBLANES - 1:SUBLANES], hb_tile[0:1]

    lax.fori_loop(0, L // SUBLANES, step, (h0f, h0b), unroll=2)
    y_ref[...] = ((h_ref[0] + h_ref[1]) * _gelu_tanh(pg_ref[...])).astype(y_ref.dtype)
    if not has_state:
        st_out_ref[0:1, :] = h_ref[0, L - 1:L, :]
        st_out_ref[1:2, :] = h_ref[1, 0:1, :]


def _block_diag(w):
    H, d, _ = w.shape
    eye = jnp.eye(H, dtype=w.dtype)
    return (eye[:, None, :, None] * w[:, :, None, :]).reshape(H * d, H * d)


def _rglru(p_g, p_x, B, L, conv_w, conv_b, wa, ba, wx, bx, lam, state):
    C = D_RG
    wcat = jnp.concatenate([_block_diag(wa[0]), _block_diag(wx[0]), _block_diag(wa[1]), _block_diag(wx[1])], axis=1)
    wh = wcat.astype(BF16)
    wl = (wcat - wh.astype(F32)).astype(BF16)
    w3 = jnp.concatenate([wh, wh, wl], axis=0)
    gb = jnp.concatenate([ba[0], bx[0], ba[1], bx[1]]).reshape(1, 4 * C)
    has_state = state is not None
    in_specs = [
        pl.BlockSpec((L, C), lambda b: (b, 0)),
        pl.BlockSpec((L, C), lambda b: (b, 0)),
        _const_spec((4, C)),
        _const_spec((1, C)),
        _const_spec((3 * C, 4 * C)),
        _const_spec((1, 4 * C)),
        _const_spec((2, C)),
    ]
    args = [p_g, p_x, conv_w, conv_b.reshape(1, C), w3, gb, lam]
    y_spec = pl.BlockSpec((L, C), lambda b: (b, 0))
    y_shape = jax.ShapeDtypeStruct((B * L, C), BF16)
    if has_state:
        in_specs.append(pl.BlockSpec((None, 2, C), lambda b: (b, 0, 0)))
        args.append(state)
        out_specs, out_shape = y_spec, y_shape
    else:
        out_specs = [y_spec, pl.BlockSpec((None, 2, C), lambda b: (b, 0, 0))]
        out_shape = [y_shape, jax.ShapeDtypeStruct((B, 2, C), F32)]
    return pl.pallas_call(
        functools.partial(_rglru_kernel, L, has_state),
        grid=(B,),
        in_specs=in_specs,
        out_specs=out_specs,
        out_shape=out_shape,
        scratch_shapes=[
            pltpu.VMEM((L + 16, C), F32),
            pltpu.VMEM((2, L, C), F32),
            pltpu.VMEM((2, L, C), F32),
            pltpu.VMEM((2, L, C), F32),
        ],
        compiler_params=_cparams("parallel"),
        name="rglru_state" if has_state else "rglru",
    )(*args)


def _attn_kernel(L, P, tq, unroll, lam_init, q_ref, k_ref, v_ref, *rest):
    if P:
        ck_ref, cv_ref, dal_ref, sub_ref, o_ref, kk_ref, vv_ref, s_ref = rest
    else:
        dal_ref, sub_ref, o_ref, kk_ref, vv_ref, s_ref = rest
    lv = dal_ref[...]
    s01 = jnp.sum(lv[0:1, :] * lv[1:2, :], axis=-1, keepdims=True)
    s23 = jnp.sum(lv[2:3, :] * lv[3:4, :], axis=-1, keepdims=True)
    lam = jnp.exp(s01) - jnp.exp(s23) + lam_init
    first_half = lax.broadcasted_iota(jnp.int32, (1, DA_VDIM), 1) < DA_HEAD
    sub = sub_ref[...] * (1.0 - lam_init)
    for hd in range(N_DA_HEADS):
        if P:
            kk_ref[0:P, :] = ck_ref[hd].astype(BF16)
            vv_ref[0:P, :] = cv_ref[hd].astype(BF16)
        kk_ref[P:P + L, :] = k_ref[hd].astype(BF16)
        vv_ref[P:P + L, :] = v_ref[hd].astype(BF16)

        def scores(i, buf):
            q = q_ref[hd, pl.ds(pl.multiple_of(i * tq, tq), tq), :]
            zero = jnp.zeros_like(q)
            qs = jnp.concatenate([jnp.where(first_half, q, zero), jnp.where(first_half, zero, q)], axis=0)
            s_ref[buf] = _dot_nt(qs, kk_ref[...])

        def finish(i, buf):
            s = s_ref[buf]
            p = jnp.exp2(s - jnp.max(s, axis=-1, keepdims=True))
            rinv = 1.0 / jnp.sum(p, axis=-1, keepdims=True)
            acc = _dot(p.astype(BF16), vv_ref[...])
            o = acc[0:tq] * rinv[0:tq] - acc[tq:2 * tq] * (lam * rinv[tq:2 * tq])
            o = o * lax.rsqrt(jnp.mean(o * o, axis=-1, keepdims=True) + EPS) * sub
            r0 = pl.multiple_of(i * tq, tq)
            o_ref[pl.ds(r0, tq), hd * DA_VDIM:(hd + 1) * DA_VDIM] = o.astype(o_ref.dtype)

        def pair(j, carry):
            i = 2 * j
            scores(i + 1, 1)
            finish(i, 0)
            scores(i + 2, 0)
            finish(i + 1, 1)
            return carry

        n = L // tq
        scores(0, 0)
        lax.fori_loop(0, n // 2 - 1, pair, 0, unroll=unroll)
        scores(n - 1, 1)
        finish(n - 2, 0)
        finish(n - 1, 1)


def _attention(q, k, v, layer, cache, dal, subln, lam_init, B, L, tq=128, unroll=2):
    H, dv = N_DA_HEADS, DA_VDIM
    hspec = pl.BlockSpec((None, H, L, dv), lambda b: (b, 0, 0, 0))
    kvspec = hspec if k.ndim == 4 else pl.BlockSpec((None, None, H, L, dv), lambda b: (b, layer, 0, 0, 0))
    in_specs = [hspec, kvspec, kvspec]
    args = [q, k, v]
    P = 0
    if cache is not None:
        ck, cv = cache
        P = ck.shape[3]
        cspec = pl.BlockSpec((None, None, H, P, dv), lambda b: (b, layer, 0, 0, 0))
        in_specs += [cspec, cspec]
        args += [ck, cv]
    assert L % tq == 0
    in_specs += [_const_spec((4, DA_HEAD)), _const_spec((1, dv))]
    args += [dal, subln.reshape(1, dv)]
    return pl.pallas_call(
        functools.partial(_attn_kernel, L, P, tq, min(unroll, max(1, L // tq // 2 - 1)), lam_init),
        grid=(B,),
        in_specs=in_specs,
        out_specs=pl.BlockSpec((L, H * dv), lambda b: (b, 0)),
        out_shape=jax.ShapeDtypeStruct((B * L, H * dv), BF16),
        scratch_shapes=[pltpu.VMEM((P + L, dv), BF16), pltpu.VMEM((P + L, dv), BF16),
                        pltpu.VMEM((2, 2 * tq, P + L), F32)],
        compiler_params=_cparams("parallel"),
        name="diff_attn_cache" if P else "diff_attn",
    )(*args)


def _route(logits):
    m = logits[0]
    for e in range(1, N_EXPERTS):
        m = jnp.maximum(m, logits[e])
    ex = [jnp.exp(l - m) for l in logits]
    tot = ex[0]
    for e in range(1, N_EXPERTS):
        tot = tot + ex[e]
    inv = 1.0 / tot
    p = [e_ * inv for e_ in ex]
    G = EXP_PER_GROUP
    best, gsel = None, None
    for g in range(N_GROUPS):
        a = p[g * G:(g + 1) * G]
        sc = None
        for i in range(G):
            for j in range(i + 1, G):
                pair = a[i] + a[j]
                sc = pair if sc is None else jnp.maximum(sc, pair)
        if g == 0:
            best, gsel = sc, jnp.zeros_like(sc, dtype=jnp.int32)
        else:
            upd = sc > best
            best = jnp.where(upd, sc, best)
            gsel = jnp.where(upd, g, gsel)
    vals = []
    for j in range(G):
        vj = p[j]
        for g in range(1, N_GROUPS):
            vj = jnp.where(gsel == g, p[g * G + j], vj)
        vals.append(vj)
    p1, i1 = vals[0], jnp.zeros_like(gsel)
    for j in range(1, G):
        upd = vals[j] > p1
        p1 = jnp.where(upd, vals[j], p1)
        i1 = jnp.where(upd, j, i1)
    p2, i2 = None, None
    for j in range(G):
        cand = jnp.where(i1 == j, -1.0, vals[j])
        if p2 is None:
            p2, i2 = cand, jnp.zeros_like(gsel)
        else:
            upd = cand > p2
            p2 = jnp.where(upd, cand, p2)
            i2 = jnp.where(upd, j, i2)
    den = 1.0 / (p1 + p2)
    w1, w2 = p1 * den, p2 * den
    swap = i2 < i1
    a, b = jnp.where(swap, i2, i1), jnp.where(swap, i1, i2)
    w_lo, w_hi = jnp.where(swap, w2, w1), jnp.where(swap, w1, w2)
    pair = jnp.where(a == 0, b - 1, jnp.where(a == 1, b + 1, 5))
    cls = gsel * PAIRS_PER_GROUP + pair
    return cls.astype(F32), w_lo, w_hi


def _pack_pairs(x):
    n = x.shape[1] // 2
    b = pltpu.bitcast(x, jnp.uint32)
    w = (b[:, :n] >> 16) | (b[:, n:] & jnp.uint32(0xFFFF0000))
    return pltpu.bitcast(w, jnp.int32)


def _unpack_pairs(w):
    u = pltpu.bitcast(w, jnp.uint32)
    lo = pltpu.bitcast(u << 16, F32)
    hi = pltpu.bitcast(u & jnp.uint32(0xFFFF0000), F32)
    return jnp.concatenate([lo, hi], axis=1)


def _out_proj_kernel(yh_ref, yr_ref, o_ref, w_ref, x_ref, mod_ref, g_ref, wrh_ref, wrl_ref, br_ref,
                     xo_ref, h_ref, route_ref):
    y = (_dot(yh_ref[...], w_ref[0:D_HY, :]) + _dot(yr_ref[...], w_ref[D_HY:D_HY + D_RG, :])
         + _dot(o_ref[...], w_ref[D_HY + D_RG:D_MIX, :]))
    x = x_ref[...] + mod_ref[2:3, :] * y
    xo_ref[...] = x
    ms = jnp.mean(x * x, axis=-1, keepdims=True)
    h = (x * lax.rsqrt(ms + EPS) * g_ref[...]) * (1.0 + mod_ref[4:5, :]) + mod_ref[3:4, :]
    hh, hl = _split(h)
    h_ref[...] = _pack_pairs(hh.astype(F32))
    lg = _dot_nt(wrh_ref[...], hh) + _dot_nt(wrh_ref[...], hl) + _dot_nt(wrl_ref[...], hh) + br_ref[...]
    info = _route([lg[e:e + 1, :] for e in range(N_EXPERTS)])
    rt = jnp.concatenate(list(info) + [jnp.zeros((LANES - len(info), lg.shape[1]), F32)], axis=0)
    route_ref[...] = rt.T


def _out_proj(y_hy, y_rg, o, w_out, layer, x, mod, g2, w_router, b_router, B, L, ctx_rows, tm=512):
    T = B * L
    tm = min(tm, L)
    nl = L // tm
    row = (lambda i: CTX_ROW) if ctx_rows else (lambda i: i // nl)
    wrt = w_router.T
    wrh = wrt.astype(BF16)
    wrl = (wrt - wrh.astype(F32)).astype(BF16)
    rows = lambda w: pl.BlockSpec((tm, w), lambda i: (i, 0))
    return pl.pallas_call(
        _out_proj_kernel,
        grid=(T // tm,),
        in_specs=[
            rows(D_HY), rows(D_RG), rows(D_DA),
            pl.BlockSpec((None, D_MIX, D_MODEL), lambda i: (layer, 0, 0)),
            rows(D_MODEL),
            pl.BlockSpec((None, 6, D_MODEL), lambda i: (row(i), 0, 0)),
            _const_spec((1, D_MODEL)),
            _const_spec((N_EXPERTS, D_MODEL)),
            _const_spec((N_EXPERTS, D_MODEL)),
            _const_spec((N_EXPERTS, 1)),
        ],
        out_specs=[rows(D_MODEL), rows(D_MODEL // 2), rows(LANES)],
        out_shape=[
            jax.ShapeDtypeStruct((T, D_MODEL), F32),
            jax.ShapeDtypeStruct((T, D_MODEL // 2), jnp.int32),
            jax.ShapeDtypeStruct((T, LANES), F32),
        ],
        compiler_params=_cparams("parallel"),
        name="out_proj_route",
    )(y_hy, y_rg, o, w_out, x, mod, g2.reshape(1, D_MODEL), wrh, wrl, b_router.reshape(N_EXPERTS, 1))


def _gather_rows(table, idx, rows_per_step=64, n_buf=2):
    info = plsc.get_sparse_core_info()
    n_workers = info.num_cores * info.num_subcores
    n, width = idx.shape[0], table.shape[1]
    per_worker = n // n_workers
    n_steps = per_worker // rows_per_step
    assert per_worker * n_workers == n and n_steps * rows_per_step == per_worker and n_steps >= n_buf
    mesh = plsc.VectorSubcoreMesh(core_axis_name="c", subcore_axis_name="s")

    @functools.partial(
        pl.kernel, mesh=mesh,
        out_type=jax.ShapeDtypeStruct((n, width), table.dtype),
        scratch_types=[
            pltpu.VMEM((per_worker,), jnp.int32),
            pltpu.VMEM((n_buf, rows_per_step, width), table.dtype),
            pltpu.SemaphoreType.DMA((n_buf,)),
            pltpu.SemaphoreType.DMA((n_buf,)),
        ],
    )
    def gather(table_hbm, idx_hbm, out_hbm, idx_v, rows_v, sem_in, sem_out):
        worker = lax.axis_index("s") * info.num_cores + lax.axis_index("c")
        base = pl.multiple_of(worker * per_worker, per_worker)
        pltpu.sync_copy(idx_hbm.at[pl.ds(base, per_worker)], idx_v)

        def read(b, step):
            rows = idx_v.at[pl.ds(step * rows_per_step, rows_per_step)]
            return pltpu.make_async_copy(table_hbm.at[rows], rows_v.at[b], sem_in.at[b])

        def write(b, step):
            off = pl.multiple_of(base + step * rows_per_step, rows_per_step)
            return pltpu.make_async_copy(rows_v.at[b], out_hbm.at[pl.ds(off, rows_per_step)], sem_out.at[b])

        for step in range(n_steps + 1):
            if step < n_steps:
                if step >= n_buf:
                    write(step % n_buf, step - n_buf).wait()
                read(step % n_buf, step).start()
            if step >= 1:
                read((step - 1) % n_buf, step - 1).wait()
                write((step - 1) % n_buf, step - 1).start()
        for step in range(n_steps - n_buf, n_steps):
            write(step % n_buf, step).wait()

    return gather(table, idx)


def _scatter_rows(src, pos, n_slots, rows_per_step=64, n_buf=2):
    info = plsc.get_sparse_core_info()
    n_workers = info.num_cores * info.num_subcores
    n, width = src.shape
    per_worker = n // n_workers
    n_steps = per_worker // rows_per_step
    assert per_worker * n_workers == n and n_steps * rows_per_step == per_worker and n_steps >= n_buf
    mesh = plsc.VectorSubcoreMesh(core_axis_name="c", subcore_axis_name="s")

    @functools.partial(
        pl.kernel, mesh=mesh,
        out_type=jax.ShapeDtypeStruct((n_slots, width), src.dtype),
        scratch_types=[
            pltpu.VMEM((n_steps, rows_per_step), jnp.int32),
            pltpu.VMEM((n_buf, rows_per_step, width), src.dtype),
            pltpu.SemaphoreType.DMA((n_buf,)),
            pltpu.SemaphoreType.DMA((n_buf,)),
        ],
    )
    def scatter(src_hbm, idx_hbm, out_hbm, idx_v, rows_v, sem_in, sem_out):
        worker = lax.axis_index("s") * info.num_cores + lax.axis_index("c")
        base = pl.multiple_of(worker * per_worker, per_worker)
        pltpu.sync_copy(idx_hbm.at[worker], idx_v)

        def read(b, step):
            off = pl.multiple_of(base + step * rows_per_step, rows_per_step)
            return pltpu.make_async_copy(src_hbm.at[pl.ds(off, rows_per_step)], rows_v.at[b], sem_in.at[b])

        def write(b, step):
            return pltpu.make_async_copy(rows_v.at[b], out_hbm.at[idx_v.at[step]], sem_out.at[b])

        for step in range(n_steps + 1):
            if step < n_steps:
                if step >= n_buf:
                    write(step % n_buf, step - n_buf).wait()
                read(step % n_buf, step).start()
            if step >= 1:
                read((step - 1) % n_buf, step - 1).wait()
                write((step - 1) % n_buf, step - 1).start()
        for step in range(n_steps - n_buf, n_steps):
            write(step % n_buf, step).wait()

    return scatter(src, pos.reshape(n_workers, n_steps, rows_per_step))


def _dispatch_plan(route, tm):
    T = route.shape[0]
    n_slots = T + N_CLASSES * tm
    cls = route[:, 0].astype(jnp.int32)
    onehot = (cls[:, None] == jnp.arange(N_CLASSES, dtype=jnp.int32)[None, :]).astype(jnp.int32)
    csum = jnp.cumsum(onehot, axis=0)
    rank = jnp.sum(onehot * csum, axis=1) - 1
    counts = csum[-1]
    padded = ((counts + tm - 1) // tm) * tm
    ends = jnp.cumsum(padded)
    starts = ends - padded
    pos = jnp.sum(onehot * starts[None, :], axis=1) + rank
    tile_start = jnp.arange(n_slots // tm, dtype=jnp.int32) * tm
    tile_cls = jnp.minimum(jnp.searchsorted(ends, tile_start, side="right"), N_CLASSES - 1).astype(jnp.int32)
    n_rows = jnp.clip(counts[tile_cls] - (tile_start - starts[tile_cls]), 0, tm).astype(jnp.int32)
    pairs = np.array([(a, b) for a in range(EXP_PER_GROUP) for b in range(a + 1, EXP_PER_GROUP)], np.int32)
    group, pair = tile_cls // PAIRS_PER_GROUP, tile_cls % PAIRS_PER_GROUP
    lo = group * EXP_PER_GROUP + jnp.asarray(pairs[:, 0])[pair]
    hi = group * EXP_PER_GROUP + jnp.asarray(pairs[:, 1])[pair]
    return pos, n_slots, lo, hi, n_rows


def _moe_sorted_kernel(lo_ref, hi_ref, rows_ref, xs_ref, ws_ref, wg_lo, wu_lo, wd_lo, wg_hi, wu_hi, wd_hi, o_ref):
    i = pl.program_id(0)

    @pl.when(rows_ref[i] > 0)
    def _():
        real = lax.broadcasted_iota(jnp.int32, (xs_ref.shape[0], 1), 0) < rows_ref[i]
        x = jnp.where(real, _unpack_pairs(xs_ref[...]), 0.0).astype(BF16)
        y = None
        for wg, wu, wd, col in ((wg_lo, wu_lo, wd_lo, 1), (wg_hi, wu_hi, wd_hi, 2)):
            a = _dot(x, wg[...])
            he = (a * _sigmoid(a)) * _dot(x, wu[...]) * jnp.where(real, ws_ref[:, col:col + 1], 0.0)
            part = _dot(he.astype(BF16), wd[...])
            y = part if y is None else y + part
        o_ref[...] = _pack_pairs(y.astype(BF16).astype(F32))

    @pl.when(rows_ref[i] == 0)
    def _():
        o_ref[...] = jnp.zeros_like(o_ref)


def _moe_sorted(xs, ws, lo, hi, n_rows, wg, wu, wd, tm):
    n_slots = xs.shape[0]
    half = D_MODEL // 2
    up = lambda sel: pl.BlockSpec((None, D_MODEL, D_EXPERT), lambda i, lo, hi, v: ((lo, hi)[sel][i], 0, 0))
    down = lambda sel: pl.BlockSpec((None, D_EXPERT, D_MODEL), lambda i, lo, hi, v: ((lo, hi)[sel][i], 0, 0))
    return pl.pallas_call(
        _moe_sorted_kernel,
        grid_spec=pltpu.PrefetchScalarGridSpec(
            num_scalar_prefetch=3,
            grid=(n_slots // tm,),
            in_specs=[
                pl.BlockSpec((tm, half), lambda i, lo, hi, v: (i, 0)),
                pl.BlockSpec((tm, LANES), lambda i, lo, hi, v: (i, 0)),
                up(0), up(0), down(0), up(1), up(1), down(1),
            ],
            out_specs=pl.BlockSpec((tm, half), lambda i, lo, hi, v: (i, 0)),
        ),
        out_shape=jax.ShapeDtypeStruct((n_slots, half), jnp.int32),
        compiler_params=_cparams("arbitrary"),
        name="moe_sorted",
    )(lo, hi, n_rows, xs, ws, wg, wu, wd, wg, wu, wd)


def _final_residual_kernel(y_ref, x_ref, mod_ref, fg_ref, o_ref):
    x = x_ref[...] + mod_ref[5:6, :] * _unpack_pairs(y_ref[...])
    o_ref[...] = x * lax.rsqrt(jnp.mean(x * x, axis=-1, keepdims=True) + EPS) * fg_ref[...]


def _final_residual(y, x, mod, final_g, B, L, ctx_rows, tm=512):
    T = B * L
    tm = min(tm, L)
    nl = L // tm
    row = (lambda i: CTX_ROW) if ctx_rows else (lambda i: i // nl)
    return pl.pallas_call(
        _final_residual_kernel,
        grid=(T // tm,),
        in_specs=[
            pl.BlockSpec((tm, D_MODEL // 2), lambda i: (i, 0)),
            pl.BlockSpec((tm, D_MODEL), lambda i: (i, 0)),
            pl.BlockSpec((None, 6, D_MODEL), lambda i: (row(i), 0, 0)),
            _const_spec((1, D_MODEL)),
        ],
        out_specs=pl.BlockSpec((tm, D_MODEL), lambda i: (i, 0)),
        out_shape=jax.ShapeDtypeStruct((T, D_MODEL), F32),
        compiler_params=_cparams("parallel"),
        name="final_residual",
    )(y, x, mod, final_g.reshape(1, D_MODEL))


def _moe(h, route, wg, wu, wd, layer, tm=256):
    pos, n_slots, lo, hi, n_rows = _dispatch_plan(route, tm)
    xs = _scatter_rows(h, pos, n_slots)
    ws = _scatter_rows(route, pos, n_slots)
    ys = _moe_sorted(xs, ws, lo + layer * N_EXPERTS, hi + layer * N_EXPERTS, n_rows, wg, wu, wd, tm)
    return _gather_rows(ys, pos)


def kernel(x_prompt, x_sample, cache_k, cache_v, state_rglru, c, c_ctx, w_ada, b_ada, norm1_g, norm2_g, w_in, w_out, hy_short_w, hy_short_b, hy_w1, hy_b1, hy_w2, hy_b2, hy_w3, hy_freq, hy_bias, rg_conv_w, rg_conv_b, rg_wa, rg_ba, rg_wx, rg_bx, rg_lambda, da_lambda, da_subln, w_router, b_router, moe_wg, moe_wu, moe_wd, final_g):
    Bp, Lp, D = x_prompt.shape
    Bs, Ls, _ = x_sample.shape
    assert Bs <= CTX_ROW
    cond = jnp.zeros((N_COND, D), F32).at[:Bs].set(c).at[CTX_ROW].set(c_ctx)
    mods = _ada_table(cond, w_ada, b_ada)

    dft = {L: tuple(jnp.asarray(m).astype(BF16) for m in _dft_mats(L)) for L in (Lp, Ls)}
    streams = [
        dict(B=Bp, L=Lp, ctx=True, x=x_prompt.reshape(Bp * Lp, D)),
        dict(B=Bs, L=Ls, ctx=False, x=x_sample.reshape(Bs * Ls, D)),
    ]
    w_in_b, w_out_b = w_in.astype(BF16), w_out.astype(BF16)
    wg, wu, wd = (w.astype(BF16).reshape((DEPTH * N_EXPERTS,) + w.shape[2:]) for w in (moe_wg, moe_wu, moe_wd))
    kv_shape = (Bp, DEPTH, N_DA_HEADS, Lp, DA_VDIM)
    new_kv, ss = (jnp.zeros(kv_shape, F32), jnp.zeros(kv_shape, F32)), []
    for l in range(DEPTH):
        lam_init = 0.8 - 0.6 * math.exp(-0.3 * l)
        for st in streams:
            B, L, ctx = st["B"], st["L"], st["ctx"]
            cmat, smat = dft[L]
            outs = _norm_proj(st["x"], mods[l], norm1_g[l], w_in_b, l, B, L, ctx, kv_prev=new_kv if ctx else None,
                              pending=(st["y"], mods[l - 1]) if l else None)
            p_hy, p_g, p_x, q, k, v = outs[:6]
            if l:
                st["x"] = outs[6]
            kre, kim = _hy_spectra(L, cmat, smat, hy_w1[l], hy_b1[l], hy_w2[l], hy_b2[l], hy_w3[l], hy_freq[l])
            y_hy = _hyena(p_hy, B, L, cmat, smat, kre, kim, hy_short_w[l], hy_short_b[l], hy_bias[l])
            rg_args = (rg_conv_w[l], rg_conv_b[l], rg_wa[l], rg_ba[l], rg_wx[l], rg_bx[l], rg_lambda[l])
            if ctx:
                y_rg, s_l = _rglru(p_g, p_x, B, L, *rg_args, None)
                o = _attention(q, k, v, l, None, da_lambda[l], da_subln[l], lam_init, B, L)
                new_kv = (k, v)
                ss.append(s_l)
            else:
                y_rg = _rglru(p_g, p_x, B, L, *rg_args, state_rglru[:, l])
                o = _attention(q, k, v, l, (cache_k, cache_v), da_lambda[l], da_subln[l], lam_init, B, L)
            st["x"], h2, route = _out_proj(y_hy, y_rg, o, w_out_b, l, st["x"], mods[l], norm2_g[l],
                                           w_router, b_router, B, L, ctx)
            st["y"] = _moe(h2, route, wg, wu, wd, l)
    y_prompt, y_sample = (
        _final_residual(st["y"], st["x"], mods[DEPTH - 1], final_g, st["B"], st["L"], st["ctx"]).reshape(shape)
        for st, shape in zip(streams, (x_prompt.shape, x_sample.shape)))
    return (y_prompt, y_sample, new_kv[0], new_kv[1], jnp.stack(ss, axis=1))
```

```python
import functools
import math

import numpy as np
import jax
import jax.numpy as jnp
from jax import lax
from jax.experimental import pallas as pl
from jax.experimental.pallas import tpu as pltpu
from jax.experimental.pallas import tpu_sc as plsc

F32 = jnp.float32
BF16 = jnp.bfloat16

D_MODEL = 1024
DEPTH = 2
GRID_W = 64
D_HY = 256
HY_EMB = 33
HY_BANDS = (HY_EMB - 1) // 2
HY_FFN = 64
HY_MIN_DECAY = math.log(1e-2) / 1.5
HY_MAX_DECAY = math.log(1e-2) / 0.3
D_RG = 256
N_RG_HEADS = 4
RG_C = 8.0
N_DA_HEADS = 4
DA_HEAD = 64
DA_VDIM = 2 * DA_HEAD
D_DA = N_DA_HEADS * DA_VDIM
D_MIX = D_HY + D_RG + D_DA
D_IN = 3 * D_HY + 2 * D_RG + 3 * D_DA
ROPE_PAIRS = DA_HEAD // 4
ROPE_THETA = 10000.0
N_EXPERTS = 16
N_GROUPS = 4
EXP_PER_GROUP = N_EXPERTS // N_GROUPS
D_EXPERT = 512
PAIRS_PER_GROUP = EXP_PER_GROUP * (EXP_PER_GROUP - 1) // 2
N_CLASSES = N_GROUPS * PAIRS_PER_GROUP
EPS = 1e-6
N_COND = 16
CTX_ROW = 8
LANES = 128
SUBLANES = 8
VMEM_LIMIT = 56 * 1024 * 1024


def _cparams(*sem):
    return pltpu.CompilerParams(dimension_semantics=sem, vmem_limit_bytes=VMEM_LIMIT)


def _split(x):
    hi = x.astype(BF16)
    lo = (x - hi.astype(F32)).astype(BF16)
    return hi, lo


def _dot(a, b):
    return jnp.dot(a, b, preferred_element_type=F32)


def _dot3(a, b):
    ah, al = _split(a)
    bh, bl = _split(b)
    return _dot(ah, bh) + _dot(al, bh) + _dot(ah, bl)


def _dot_nt(a, b):
    return lax.dot_general(a, b, (((1,), (1,)), ((), ())), preferred_element_type=F32)


def _sigmoid(x):
    return 1.0 / (1.0 + jnp.exp(-x))


def _const_spec(shape):
    n = len(shape)
    return pl.BlockSpec(shape, lambda *_: (0,) * n)


def _ada_kernel(c_ref, w_ref, b_ref, o_ref):
    c = c_ref[...]
    s = c * _sigmoid(c)
    o_ref[...] = _dot3(s, w_ref[...]) + b_ref[...]


def _ada_table(cond, w_ada, b_ada):
    D = D_MODEL
    out = pl.pallas_call(
        _ada_kernel,
        grid=(DEPTH, 6),
        in_specs=[
            pl.BlockSpec((N_COND, D), lambda l, j: (0, 0)),
            pl.BlockSpec((None, D, D), lambda l, j: (l, 0, j)),
            pl.BlockSpec((None, None, 1, D), lambda l, j: (l, j, 0, 0)),
        ],
        out_specs=pl.BlockSpec((None, None, N_COND, D), lambda l, j: (l, j, 0, 0)),
        out_shape=jax.ShapeDtypeStruct((DEPTH, 6, N_COND, D), F32),
        compiler_params=_cparams("parallel", "parallel"),
        name="ada_table",
    )(cond, w_ada, b_ada.reshape(DEPTH, 6, 1, D))
    return out.transpose(0, 2, 1, 3)


def _rope_tables(L):
    t = np.arange(L)
    j = np.arange(LANES)
    jj = j % DA_HEAD
    is_col = (jj // (DA_HEAD // 2)) == 1
    pair = jj % ROPE_PAIRS
    second = (jj % (DA_HEAD // 2)) >= ROPE_PAIRS
    inv = ROPE_THETA ** (-np.arange(ROPE_PAIRS, dtype=np.float64) / ROPE_PAIRS)
    pos = np.where(is_col[None, :], (t % GRID_W)[:, None], (t // GRID_W)[:, None]).astype(np.float64)
    ang = pos * inv[pair][None, :]
    cos = np.cos(ang).astype(np.float32)
    sin = np.sin(ang).astype(np.float32)
    sin_a = np.where(second[None, :], 0.0, -sin).astype(np.float32)
    sin_b = np.where(second[None, :], sin, 0.0).astype(np.float32)
    return cos, sin_a, sin_b


def _rope(x, cos, sin_a, sin_b):
    nxt = pltpu.roll(x, LANES - ROPE_PAIRS, axis=1)
    prv = pltpu.roll(x, ROPE_PAIRS, axis=1)
    return x * cos + nxt * sin_a + prv * sin_b


def _norm_proj_kernel(rope, kv_dtype, pending, kv_layer, x_ref, mod_ref, g_ref, w_ref, *rest):
    x = x_ref[...]
    if pending:
        y_ref, modp_ref, xnew_ref = rest[0], rest[1], rest[-1]
        x = x + modp_ref[5:6, :] * _unpack_pairs(y_ref[...])
        xnew_ref[...] = x
        rest = rest[2:-1]
    if rope:
        cos_ref, sa_ref, sb_ref = rest[:3]
    phy_ref, pg_ref, px_ref, q_ref, k_ref, v_ref = rest[-6:]
    ms = jnp.mean(x * x, axis=-1, keepdims=True)
    y = x * lax.rsqrt(ms + EPS) * g_ref[...]
    h = (y * (1.0 + mod_ref[1:2, :]) + mod_ref[0:1, :]).astype(BF16)
    o = 3 * D_HY
    phy_ref[...] = _dot(h, w_ref[:, 0:o]).astype(BF16)
    pg_ref[...] = _dot(h, w_ref[:, o:o + D_RG])
    px_ref[...] = _dot(h, w_ref[:, o + D_RG:o + 2 * D_RG])
    o += 2 * D_RG
    q = _dot(h, w_ref[:, o:o + D_DA]) * (DA_HEAD ** -0.5 * math.log2(math.e))
    k = _dot(h, w_ref[:, o + D_DA:o + 2 * D_DA])
    v = _dot(h, w_ref[:, o + 2 * D_DA:o + 3 * D_DA])
    if rope:
        cos, sa, sb = cos_ref[...], sa_ref[...], sb_ref[...]
    for hd in range(N_DA_HEADS):
        sl = slice(hd * DA_VDIM, (hd + 1) * DA_VDIM)
        qh, kh = q[:, sl], k[:, sl]
        if rope:
            qh = _rope(qh, cos, sa, sb)
            kh = _rope(kh, cos, sa, sb)
        q_ref[hd] = qh.astype(BF16)
        if kv_layer is None:
            k_ref[hd] = kh.astype(kv_dtype)
            v_ref[hd] = v[:, sl].astype(kv_dtype)
        else:
            for l in range(DEPTH):
                k_ref[l, hd] = kh.astype(kv_dtype) if l == kv_layer else jnp.zeros_like(kh, dtype=kv_dtype)
                v_ref[l, hd] = v[:, sl].astype(kv_dtype) if l == kv_layer else jnp.zeros_like(kh, dtype=kv_dtype)


def _norm_proj(x, mod, g, w_in, layer, B, L, ctx, kv_prev=None, pending=None, tm=512):
    T = B * L
    tm = min(tm, L)
    nl = L // tm
    rope, kv_dtype = not ctx, (F32 if ctx else BF16)
    row = (lambda i: CTX_ROW) if ctx else (lambda i: i // nl)
    mod_spec = pl.BlockSpec((None, 6, D_MODEL), lambda i: (row(i), 0, 0))
    in_specs = [
        pl.BlockSpec((tm, D_MODEL), lambda i: (i, 0)),
        mod_spec,
        _const_spec((1, D_MODEL)),
        pl.BlockSpec((None, D_MODEL, D_IN), lambda i: (layer, 0, 0)),
    ]
    args = [x, mod, g.reshape(1, D_MODEL), w_in]
    if pending is not None:
        in_specs += [pl.BlockSpec((tm, D_MODEL // 2), lambda i: (i, 0)), mod_spec]
        args += list(pending)
    if rope:
        tabs = _rope_tables(L)
        in_specs += [pl.BlockSpec((tm, LANES), lambda i: (i % nl, 0))] * 3
        args += [jnp.asarray(t) for t in tabs]
    head_spec = pl.BlockSpec((None, N_DA_HEADS, tm, DA_VDIM), lambda i: (i // nl, 0, i % nl, 0))
    head_shape = (B, N_DA_HEADS, L, DA_VDIM)
    kv_spec, kv_shape, aliases, kv_layer = head_spec, head_shape, {}, None
    if ctx:
        kv_shape = (B, DEPTH, N_DA_HEADS, L, DA_VDIM)
        if kv_prev is not None:
            kv_spec = pl.BlockSpec((None, None, N_DA_HEADS, tm, DA_VDIM), lambda i: (i // nl, layer, 0, i % nl, 0))
            aliases = {len(args): 4, len(args) + 1: 5}
            in_specs += [pl.BlockSpec(memory_space=pl.ANY)] * 2
            args += list(kv_prev)
        else:
            kv_spec = pl.BlockSpec((None, DEPTH, N_DA_HEADS, tm, DA_VDIM), lambda i: (i // nl, 0, 0, i % nl, 0))
            kv_layer = layer
    out_specs = [
        pl.BlockSpec((tm, 3 * D_HY), lambda i: (i, 0)),
        pl.BlockSpec((tm, D_RG), lambda i: (i, 0)),
        pl.BlockSpec((tm, D_RG), lambda i: (i, 0)),
        head_spec, kv_spec, kv_spec,
    ]
    out_shape = [
        jax.ShapeDtypeStruct((T, 3 * D_HY), BF16),
        jax.ShapeDtypeStruct((T, D_RG), F32),
        jax.ShapeDtypeStruct((T, D_RG), F32),
        jax.ShapeDtypeStruct(head_shape, BF16),
        jax.ShapeDtypeStruct(kv_shape, kv_dtype),
        jax.ShapeDtypeStruct(kv_shape, kv_dtype),
    ]
    if pending is not None:
        out_specs.append(pl.BlockSpec((tm, D_MODEL), lambda i: (i, 0)))
        out_shape.append(jax.ShapeDtypeStruct((T, D_MODEL), F32))
    return pl.pallas_call(
        functools.partial(_norm_proj_kernel, rope, kv_dtype, pending is not None, kv_layer),
        grid=(T // tm,),
        in_specs=in_specs,
        out_specs=out_specs,
        out_shape=out_shape,
        input_output_aliases=aliases,
        compiler_params=_cparams("parallel"),
        name="norm_proj_rope" if rope else "norm_proj",
    )(*args)


def _dft_mats(L):
    n = 2 * L - 1
    fs = (np.arange(L, dtype=np.int64)[:, None] * np.arange(L, dtype=np.int64)[None, :]) % n
    ang = fs.astype(np.float64) * (2.0 * np.pi / n)
    return np.cos(ang).astype(np.float32), np.sin(ang).astype(np.float32)


def _hy_features(L):
    t = np.linspace(0.0, 1.0, L, dtype=np.float64)[:, None]
    ang = ((2.0 * math.pi / L) * np.arange(L, dtype=np.float64))[:, None]
    bands = np.linspace(1e-4, HY_BANDS - 1, HY_BANDS, dtype=np.float64)[None, :]
    ba = bands * ang
    z = np.concatenate([t, np.cos(ba), -np.sin(ba)], axis=-1).astype(np.float32)
    return np.pad(z, ((0, 0), (0, LANES - HY_EMB)))


def _hy_filter_kernel(L, z_ref, w1_ref, b1_ref, w2_ref, b2_ref, w3_ref, fr_ref, rc_ref, rs_ref):
    z = z_ref[...]
    h = jnp.sin(fr_ref[0:1, :] * (_dot3(z, w1_ref[...]) + b1_ref[...]))
    h = jnp.sin(fr_ref[1:2, :] * (_dot3(h, w2_ref[...]) + b2_ref[...]))
    h = _dot3(h, w3_ref[...])
    t = z[:, 0:1]
    step = (HY_MAX_DECAY - HY_MIN_DECAY) / (D_HY - 1)
    deltas = HY_MIN_DECAY + step * lax.broadcasted_iota(jnp.int32, (1, D_HY), 1).astype(F32)
    window = jnp.exp(-t * jnp.abs(deltas))
    not_first = lax.broadcasted_iota(jnp.int32, (L, 1), 0) > 0
    for o in range(2):
        hf = h[:, (2 * o) * D_HY:(2 * o + 1) * D_HY] * window
        hb = jnp.where(not_first, h[:, (2 * o + 1) * D_HY:(2 * o + 2) * D_HY] * window, 0.0)
        rc_ref[:, o * D_HY:(o + 1) * D_HY] = hf + hb
        rs_ref[:, o * D_HY:(o + 1) * D_HY] = hb - hf


def _hy_spectrum_kernel(c_ref, s_ref, rc_ref, rs_ref, w_ref, kre_ref, kim_ref):
    rch, rcl = _split(rc_ref[...])
    rsh, rsl = _split(rs_ref[...])
    c, s, w = c_ref[...], s_ref[...], w_ref[...]
    kre_ref[...] = (_dot(c, rch) + _dot(c, rcl)) * w
    kim_ref[...] = (_dot(s, rsh) + _dot(s, rsl)) * w


def _hy_spectra(L, cmat, smat, w1, b1, w2, b2, w3, freq):
    z = jnp.asarray(_hy_features(L))
    w1p = jnp.pad(w1, ((0, LANES - HY_EMB), (0, 0)))
    nw = 2 * D_HY
    rc, rs = pl.pallas_call(
        functools.partial(_hy_filter_kernel, L),
        out_shape=[jax.ShapeDtypeStruct((L, nw), F32)] * 2,
        compiler_params=pltpu.CompilerParams(vmem_limit_bytes=VMEM_LIMIT),
        name="hy_filter",
    )(z, w1p, b1.reshape(1, HY_FFN), w2, b2.reshape(1, HY_FFN), w3, freq)
    n = 2 * L - 1
    wsc = np.full((L, 1), 2.0 / n, np.float32)
    wsc[0, 0] = 1.0 / n
    tr = min(L, 256)
    return pl.pallas_call(
        _hy_spectrum_kernel,
        grid=(L // tr,),
        in_specs=[
            pl.BlockSpec((tr, L), lambda i: (i, 0)),
            pl.BlockSpec((tr, L), lambda i: (i, 0)),
            _const_spec((L, nw)),
            _const_spec((L, nw)),
            pl.BlockSpec((tr, 1), lambda i: (i, 0)),
        ],
        out_specs=[pl.BlockSpec((tr, nw), lambda i: (i, 0))] * 2,
        out_shape=[jax.ShapeDtypeStruct((L, nw), F32)] * 2,
        compiler_params=_cparams("parallel"),
        name="hy_spectrum",
    )(cmat, smat, rc, rs, jnp.asarray(wsc))


def _hyena_kernel(L, tr, p_ref, sw_ref, sb_ref, bias_ref, c_ref, s_ref, kre_ref, kim_ref, o_ref,
                  pad_ref, u_ref, sig_ref, sig16_ref, zre_ref, zim_ref):
    C3 = 3 * D_HY
    zeros = jnp.zeros((8, C3), F32)
    pad_ref[0:8, :] = zeros
    pad_ref[8 + L:16 + L, :] = zeros
    chunks = [slice(r0, r0 + tr) for r0 in range(0, L, tr)]
    for c in chunks:
        pad_ref[8 + c.start:8 + c.stop, :] = p_ref[c, :].astype(F32)
    for c in chunks:
        u = sb_ref[...]
        for j in range(3):
            u = u + pad_ref[7 + j + c.start:7 + j + c.stop, :] * sw_ref[j:j + 1, :]
        u_ref[c, :] = u[:, D_HY:C3]
        sig_ref[c, :] = u[:, 0:D_HY]
        sig16_ref[c, :] = u[:, 0:D_HY].astype(BF16)

    for o in range(2):
        ko = slice(o * D_HY, (o + 1) * D_HY)
        for c in chunks:
            ure = _dot(c_ref[c, :], sig16_ref[...])
            us = _dot(s_ref[c, :], sig16_ref[...])
            kre, kim = kre_ref[c, ko], kim_ref[c, ko]
            zre_ref[c, :] = (ure * kre + us * kim).astype(BF16)
            zim_ref[c, :] = (ure * kim - us * kre).astype(BF16)
        gate = slice(o * D_HY, (o + 1) * D_HY)
        for c in chunks:
            y = _dot(c_ref[c, :], zre_ref[...]) - _dot(s_ref[c, :], zim_ref[...])
            z = u_ref[c, gate] * (y + sig_ref[c, :] * bias_ref[o:o + 1, :])
            if o == 0:
                sig_ref[c, :] = z
                sig16_ref[c, :] = z.astype(BF16)
            else:
                o_ref[c, :] = z.astype(o_ref.dtype)


def _hyena(p_hy, B, L, cmat, smat, kre, kim, short_w, short_b, bias, tr=512):
    C3 = 3 * D_HY
    tr = min(tr, L)
    once = pl.Buffered(1)
    return pl.pallas_call(
        functools.partial(_hyena_kernel, L, tr),
        grid=(B,),
        in_specs=[
            pl.BlockSpec((L, C3), lambda b: (b, 0)),
            _const_spec((3, C3)),
            _const_spec((1, C3)),
            _const_spec((2, D_HY)),
            pl.BlockSpec((L, L), lambda b: (0, 0), pipeline_mode=once),
            pl.BlockSpec((L, L), lambda b: (0, 0), pipeline_mode=once),
            pl.BlockSpec((L, 2 * D_HY), lambda b: (0, 0), pipeline_mode=once),
            pl.BlockSpec((L, 2 * D_HY), lambda b: (0, 0), pipeline_mode=once),
        ],
        out_specs=pl.BlockSpec((L, D_HY), lambda b: (b, 0)),
        out_shape=jax.ShapeDtypeStruct((B * L, D_HY), BF16),
        scratch_shapes=[
            pltpu.VMEM((L + 16, C3), F32),
            pltpu.VMEM((L, 2 * D_HY), F32),
            pltpu.VMEM((L, D_HY), F32),
            pltpu.VMEM((L, D_HY), BF16),
            pltpu.VMEM((L, D_HY), BF16),
            pltpu.VMEM((L, D_HY), BF16),
        ],
        compiler_params=_cparams("parallel"),
        name="hyena",
    )(p_hy, short_w, short_b.reshape(1, C3), bias, cmat, smat, kre, kim)


def _softplus(z):
    return jnp.maximum(z, 0.0) + jnp.log1p(jnp.exp(-jnp.abs(z)))


def _sigmoid_tanh(x):
    return 0.5 + 0.5 * jnp.tanh(0.5 * x)


def _gelu_tanh(x):
    return 0.5 * x * (1.0 + jnp.tanh(math.sqrt(2.0 / math.pi) * (x + 0.044715 * x * x * x)))


def _rglru_kernel(L, has_state, pg_ref, px_ref, cw_ref, cb_ref, w3_ref, gb_ref, lam_ref, *rest):
    if has_state:
        st_ref, y_ref, pad_ref, a_ref, b_ref, h_ref = rest
    else:
        y_ref, st_out_ref, pad_ref, a_ref, b_ref, h_ref = rest
    C = D_RG
    zeros = jnp.zeros((8, C), F32)
    pad_ref[0:8, :] = zeros
    pad_ref[8 + L:16 + L, :] = zeros
    pad_ref[8:8 + L, :] = px_ref[...]
    sp = _softplus(-lam_ref[...])
    tr = min(L, 256)
    for r0 in range(0, L, tr):
        xr = cb_ref[...]
        for j in range(4):
            xr = xr + pad_ref[6 + j + r0:6 + j + r0 + tr, :] * cw_ref[j:j + 1, :]
        xh, xl = _split(xr)
        x3 = jnp.concatenate([xh, xl, xh], axis=1)
        for d in range(2):
            g = []
            for m in range(2):
                cols = slice((2 * d + m) * C, (2 * d + m + 1) * C)
                g.append(_sigmoid_tanh(_dot(x3, w3_ref[:, cols]) + gb_ref[:, cols]))
            log_a = -RG_C * g[0] * sp[d:d + 1, :]
            a = jnp.exp(log_a)
            a_ref[d, r0:r0 + tr, :] = a
            b_ref[d, r0:r0 + tr, :] = jnp.sqrt(-jnp.tanh(log_a) * (1.0 + a * a)) * (g[1] * xr)

    if has_state:
        h0f, h0b = st_ref[0:1, :], st_ref[1:2, :]
    else:
        h0f = h0b = jnp.zeros((1, C), F32)

    row = lax.broadcasted_iota(jnp.int32, (SUBLANES, 1), 0)

    def tile_scan(a, b, reverse):
        for d in (1, 2, 4):
            shift = SUBLANES - d if reverse else d
            valid = (row < SUBLANES - d) if reverse else (row >= d)
            a_s, b_s = pltpu.roll(a, shift, axis=0), pltpu.roll(b, shift, axis=0)
            b = jnp.where(valid, a * b_s + b, b)
            a = jnp.where(valid, a * a_s, a)
        return a, b

    def step(i, carry):
        hf, hb = carry
        t0 = pl.multiple_of(i * SUBLANES, SUBLANES)
        tb0 = pl.multiple_of(L - SUBLANES - i * SUBLANES, SUBLANES)
        af, bf = tile_scan(a_ref[0, pl.ds(t0, SUBLANES), :], b_ref[0, pl.ds(t0, SUBLANES), :], False)
        ab, bb = tile_scan(a_ref[1, pl.ds(tb0, SUBLANES), :], b_ref[1, pl.ds(tb0, SUBLANES), :], True)
        hf_tile = af * hf + bf
        hb_tile = ab * hb + bb
        h_ref[0, pl.ds(t0, SUBLANES), :] = hf_tile
        h_ref[1, pl.ds(tb0, SUBLANES), :] = hb_tile
        return hf_tile[SUBLANES - 1:SUBLANES], hb_tile[0:1]

    lax.fori_loop(0, L // SUBLANES, step, (h0f, h0b), unroll=2)
    y_ref[...] = ((h_ref[0] + h_ref[1]) * _gelu_tanh(pg_ref[...])).astype(y_ref.dtype)
    if not has_state:
        st_out_ref[0:1, :] = h_ref[0, L - 1:L, :]
        st_out_ref[1:2, :] = h_ref[1, 0:1, :]


def _block_diag(w):
    H, d, _ = w.shape
    eye = jnp.eye(H, dtype=w.dtype)
    return (eye[:, None, :, None] * w[:, :, None, :]).reshape(H * d, H * d)


def _rglru(p_g, p_x, B, L, conv_w, conv_b, wa, ba, wx, bx, lam, state):
    C = D_RG
    wcat = jnp.concatenate([_block_diag(wa[0]), _block_diag(wx[0]), _block_diag(wa[1]), _block_diag(wx[1])], axis=1)
    wh = wcat.astype(BF16)
    wl = (wcat - wh.astype(F32)).astype(BF16)
    w3 = jnp.concatenate([wh, wh, wl], axis=0)
    gb = jnp.concatenate([ba[0], bx[0], ba[1], bx[1]]).reshape(1, 4 * C)
    has_state = state is not None
    in_specs = [
        pl.BlockSpec((L, C), lambda b: (b, 0)),
        pl.BlockSpec((L, C), lambda b: (b, 0)),
        _const_spec((4, C)),
        _const_spec((1, C)),
        _const_spec((3 * C, 4 * C)),
        _const_spec((1, 4 * C)),
        _const_spec((2, C)),
    ]
    args = [p_g, p_x, conv_w, conv_b.reshape(1, C), w3, gb, lam]
    y_spec = pl.BlockSpec((L, C), lambda b: (b, 0))
    y_shape = jax.ShapeDtypeStruct((B * L, C), BF16)
    if has_state:
        in_specs.append(pl.BlockSpec((None, 2, C), lambda b: (b, 0, 0)))
        args.append(state)
        out_specs, out_shape = y_spec, y_shape
    else:
        out_specs = [y_spec, pl.BlockSpec((None, 2, C), lambda b: (b, 0, 0))]
        out_shape = [y_shape, jax.ShapeDtypeStruct((B, 2, C), F32)]
    return pl.pallas_call(
        functools.partial(_rglru_kernel, L, has_state),
        grid=(B,),
        in_specs=in_specs,
        out_specs=out_specs,
        out_shape=out_shape,
        scratch_shapes=[
            pltpu.VMEM((L + 16, C), F32),
            pltpu.VMEM((2, L, C), F32),
            pltpu.VMEM((2, L, C), F32),
            pltpu.VMEM((2, L, C), F32),
        ],
        compiler_params=_cparams("parallel"),
        name="rglru_state" if has_state else "rglru",
    )(*args)


def _attn_kernel(L, P, tq, unroll, lam_init, q_ref, k_ref, v_ref, *rest):
    if P:
        ck_ref, cv_ref, dal_ref, sub_ref, o_ref, kk_ref, vv_ref, s_ref = rest
    else:
        dal_ref, sub_ref, o_ref, kk_ref, vv_ref, s_ref = rest
    lv = dal_ref[...]
    s01 = jnp.sum(lv[0:1, :] * lv[1:2, :], axis=-1, keepdims=True)
    s23 = jnp.sum(lv[2:3, :] * lv[3:4, :], axis=-1, keepdims=True)
    lam = jnp.exp(s01) - jnp.exp(s23) + lam_init
    first_half = lax.broadcasted_iota(jnp.int32, (1, DA_VDIM), 1) < DA_HEAD
    sub = sub_ref[...] * (1.0 - lam_init)
    for hd in range(N_DA_HEADS):
        if P:
            kk_ref[0:P, :] = ck_ref[hd].astype(BF16)
            vv_ref[0:P, :] = cv_ref[hd].astype(BF16)
        kk_ref[P:P + L, :] = k_ref[hd].astype(BF16)
        vv_ref[P:P + L, :] = v_ref[hd].astype(BF16)

        def scores(i, buf):
            q = q_ref[hd, pl.ds(pl.multiple_of(i * tq, tq), tq), :]
            zero = jnp.zeros_like(q)
            qs = jnp.concatenate([jnp.where(first_half, q, zero), jnp.where(first_half, zero, q)], axis=0)
            s_ref[buf] = _dot_nt(qs, kk_ref[...])

        def finish(i, buf):
            s = s_ref[buf]
            p = jnp.exp2(s - jnp.max(s, axis=-1, keepdims=True))
            rinv = 1.0 / jnp.sum(p, axis=-1, keepdims=True)
            acc = _dot(p.astype(BF16), vv_ref[...])
            o = acc[0:tq] * rinv[0:tq] - acc[tq:2 * tq] * (lam * rinv[tq:2 * tq])
            o = o * lax.rsqrt(jnp.mean(o * o, axis=-1, keepdims=True) + EPS) * sub
            r0 = pl.multiple_of(i * tq, tq)
            o_ref[pl.ds(r0, tq), hd * DA_VDIM:(hd + 1) * DA_VDIM] = o.astype(o_ref.dtype)

        def pair(j, carry):
            i = 2 * j
            scores(i + 1, 1)
            finish(i, 0)
            scores(i + 2, 0)
            finish(i + 1, 1)
            return carry

        n = L // tq
        scores(0, 0)
        lax.fori_loop(0, n // 2 - 1, pair, 0, unroll=unroll)
        scores(n - 1, 1)
        finish(n - 2, 0)
        finish(n - 1, 1)


def _attention(q, k, v, layer, cache, dal, subln, lam_init, B, L, tq=128, unroll=2):
    H, dv = N_DA_HEADS, DA_VDIM
    hspec = pl.BlockSpec((None, H, L, dv), lambda b: (b, 0, 0, 0))
    kvspec = hspec if k.ndim == 4 else pl.BlockSpec((None, None, H, L, dv), lambda b: (b, layer, 0, 0, 0))
    in_specs = [hspec, kvspec, kvspec]
    args = [q, k, v]
    P = 0
    if cache is not None:
        ck, cv = cache
        P = ck.shape[3]
        cspec = pl.BlockSpec((None, None, H, P, dv), lambda b: (b, layer, 0, 0, 0))
        in_specs += [cspec, cspec]
        args += [ck, cv]
    assert L % tq == 0
    in_specs += [_const_spec((4, DA_HEAD)), _const_spec((1, dv))]
    args += [dal, subln.reshape(1, dv)]
    return pl.pallas_call(
        functools.partial(_attn_kernel, L, P, tq, min(unroll, max(1, L // tq // 2 - 1)), lam_init),
        grid=(B,),
        in_specs=in_specs,
        out_specs=pl.BlockSpec((L, H * dv), lambda b: (b, 0)),
        out_shape=jax.ShapeDtypeStruct((B * L, H * dv), BF16),
        scratch_shapes=[pltpu.VMEM((P + L, dv), BF16), pltpu.VMEM((P + L, dv), BF16),
                        pltpu.VMEM((2, 2 * tq, P + L), F32)],
        compiler_params=_cparams("parallel"),
        name="diff_attn_cache" if P else "diff_attn",
    )(*args)


def _route(logits):
    m = logits[0]
    for e in range(1, N_EXPERTS):
        m = jnp.maximum(m, logits[e])
    ex = [jnp.exp(l - m) for l in logits]
    tot = ex[0]
    for e in range(1, N_EXPERTS):
        tot = tot + ex[e]
    inv = 1.0 / tot
    p = [e_ * inv for e_ in ex]
    G = EXP_PER_GROUP
    best, gsel = None, None
    for g in range(N_GROUPS):
        a = p[g * G:(g + 1) * G]
        sc = None
        for i in range(G):
            for j in range(i + 1, G):
                pair = a[i] + a[j]
                sc = pair if sc is None else jnp.maximum(sc, pair)
        if g == 0:
            best, gsel = sc, jnp.zeros_like(sc, dtype=jnp.int32)
        else:
            upd = sc > best
            best = jnp.where(upd, sc, best)
            gsel = jnp.where(upd, g, gsel)
    vals = []
    for j in range(G):
        vj = p[j]
        for g in range(1, N_GROUPS):
            vj = jnp.where(gsel == g, p[g * G + j], vj)
        vals.append(vj)
    p1, i1 = vals[0], jnp.zeros_like(gsel)
    for j in range(1, G):
        upd = vals[j] > p1
        p1 = jnp.where(upd, vals[j], p1)
        i1 = jnp.where(upd, j, i1)
    p2, i2 = None, None
    for j in range(G):
        cand = jnp.where(i1 == j, -1.0, vals[j])
        if p2 is None:
            p2, i2 = cand, jnp.zeros_like(gsel)
        else:
            upd = cand > p2
            p2 = jnp.where(upd, cand, p2)
            i2 = jnp.where(upd, j, i2)
    den = 1.0 / (p1 + p2)
    w1, w2 = p1 * den, p2 * den
    swap = i2 < i1
    a, b = jnp.where(swap, i2, i1), jnp.where(swap, i1, i2)
    w_lo, w_hi = jnp.where(swap, w2, w1), jnp.where(swap, w1, w2)
    pair = jnp.where(a == 0, b - 1, jnp.where(a == 1, b + 1, 5))
    cls = gsel * PAIRS_PER_GROUP + pair
    return cls.astype(F32), w_lo, w_hi


def _pack_pairs(x):
    n = x.shape[1] // 2
    b = pltpu.bitcast(x, jnp.uint32)
    w = (b[:, :n] >> 16) | (b[:, n:] & jnp.uint32(0xFFFF0000))
    return pltpu.bitcast(w, jnp.int32)


def _unpack_pairs(w):
    u = pltpu.bitcast(w, jnp.uint32)
    lo = pltpu.bitcast(u << 16, F32)
    hi = pltpu.bitcast(u & jnp.uint32(0xFFFF0000), F32)
    return jnp.concatenate([lo, hi], axis=1)


def _out_proj_kernel(yh_ref, yr_ref, o_ref, w_ref, x_ref, mod_ref, g_ref, wrh_ref, wrl_ref, br_ref,
                     xo_ref, h_ref, route_ref, y_ref):
    n_sub = y_ref.shape[0]
    sub = x_ref.shape[0] // n_sub

    def project(j):
        r = slice(j * sub, (j + 1) * sub)
        y_ref[j] = (_dot(yh_ref[r, :], w_ref[0:D_HY, :]) + _dot(yr_ref[r, :], w_ref[D_HY:D_HY + D_RG, :])
                    + _dot(o_ref[r, :], w_ref[D_HY + D_RG:D_MIX, :]))

    def finish(j):
        r = slice(j * sub, (j + 1) * sub)
        x = x_ref[r, :] + mod_ref[2:3, :] * y_ref[j]
        xo_ref[r, :] = x
        ms = jnp.mean(x * x, axis=-1, keepdims=True)
        h = (x * lax.rsqrt(ms + EPS) * g_ref[...]) * (1.0 + mod_ref[4:5, :]) + mod_ref[3:4, :]
        hh, hl = _split(h)
        h_ref[r, :] = _pack_pairs(hh.astype(F32))
        lg = _dot_nt(wrh_ref[...], hh) + _dot_nt(wrh_ref[...], hl) + _dot_nt(wrl_ref[...], hh) + br_ref[...]
        info = _route([lg[e:e + 1, :] for e in range(N_EXPERTS)])
        rt = jnp.concatenate(list(info) + [jnp.zeros((LANES - len(info), sub), F32)], axis=0)
        route_ref[r, :] = rt.T

    project(0)
    for j in range(n_sub):
        if j + 1 < n_sub:
            project(j + 1)
        finish(j)


def _out_proj(y_hy, y_rg, o, w_out, layer, x, mod, g2, w_router, b_router, B, L, ctx_rows, tm=1024, n_sub=2):
    T = B * L
    tm = min(tm, T if ctx_rows else L)
    assert T % tm == 0 and tm % n_sub == 0
    nl = max(L // tm, 1)
    row = (lambda i: CTX_ROW) if ctx_rows else (lambda i: i // nl)
    wrt = w_router.T
    wrh = wrt.astype(BF16)
    wrl = (wrt - wrh.astype(F32)).astype(BF16)
    rows = lambda w: pl.BlockSpec((tm, w), lambda i: (i, 0))
    return pl.pallas_call(
        _out_proj_kernel,
        grid=(T // tm,),
        in_specs=[
            rows(D_HY), rows(D_RG), rows(D_DA),
            pl.BlockSpec((None, D_MIX, D_MODEL), lambda i: (layer, 0, 0)),
            rows(D_MODEL),
            pl.BlockSpec((None, 6, D_MODEL), lambda i: (row(i), 0, 0)),
            _const_spec((1, D_MODEL)),
            _const_spec((N_EXPERTS, D_MODEL)),
            _const_spec((N_EXPERTS, D_MODEL)),
            _const_spec((N_EXPERTS, 1)),
        ],
        out_specs=[rows(D_MODEL), rows(D_MODEL // 2), rows(LANES)],
        out_shape=[
            jax.ShapeDtypeStruct((T, D_MODEL), F32),
            jax.ShapeDtypeStruct((T, D_MODEL // 2), jnp.int32),
            jax.ShapeDtypeStruct((T, LANES), F32),
        ],
        scratch_shapes=[pltpu.VMEM((n_sub, tm // n_sub, D_MODEL), F32)],
        compiler_params=_cparams("parallel"),
        name="out_proj_route",
    )(y_hy, y_rg, o, w_out, x, mod, g2.reshape(1, D_MODEL), wrh, wrl, b_router.reshape(N_EXPERTS, 1))


def _gather_rows(table, idx, rows_per_step=64, n_buf=2):
    info = plsc.get_sparse_core_info()
    n_workers = info.num_cores * info.num_subcores
    n, width = idx.shape[0], table.shape[1]
    per_worker = n // n_workers
    n_steps = per_worker // rows_per_step
    assert per_worker * n_workers == n and n_steps * rows_per_step == per_worker and n_steps >= n_buf
    mesh = plsc.VectorSubcoreMesh(core_axis_name="c", subcore_axis_name="s")

    @functools.partial(
        pl.kernel, mesh=mesh,
        out_type=jax.ShapeDtypeStruct((n, width), table.dtype),
        scratch_types=[
            pltpu.VMEM((per_worker,), jnp.int32),
            pltpu.VMEM((n_buf, rows_per_step, width), table.dtype),
            pltpu.SemaphoreType.DMA((n_buf,)),
            pltpu.SemaphoreType.DMA((n_buf,)),
        ],
    )
    def gather(table_hbm, idx_hbm, out_hbm, idx_v, rows_v, sem_in, sem_out):
        worker = lax.axis_index("s") * info.num_cores + lax.axis_index("c")
        base = pl.multiple_of(worker * per_worker, per_worker)
        pltpu.sync_copy(idx_hbm.at[pl.ds(base, per_worker)], idx_v)

        def read(b, step):
            rows = idx_v.at[pl.ds(step * rows_per_step, rows_per_step)]
            return pltpu.make_async_copy(table_hbm.at[rows], rows_v.at[b], sem_in.at[b])

        def write(b, step):
            off = pl.multiple_of(base + step * rows_per_step, rows_per_step)
            return pltpu.make_async_copy(rows_v.at[b], out_hbm.at[pl.ds(off, rows_per_step)], sem_out.at[b])

        for step in range(n_steps + 1):
            if step < n_steps:
                if step >= n_buf:
                    write(step % n_buf, step - n_buf).wait()
                read(step % n_buf, step).start()
            if step >= 1:
                read((step - 1) % n_buf, step - 1).wait()
                write((step - 1) % n_buf, step - 1).start()
        for step in range(n_steps - n_buf, n_steps):
            write(step % n_buf, step).wait()

    return gather(table, idx)


def _scatter_rows(src, pos, n_slots, rows_per_step=64, n_buf=2):
    info = plsc.get_sparse_core_info()
    n_workers = info.num_cores * info.num_subcores
    n, width = src.shape
    per_worker = n // n_workers
    n_steps = per_worker // rows_per_step
    assert per_worker * n_workers == n and n_steps * rows_per_step == per_worker and n_steps >= n_buf
    mesh = plsc.VectorSubcoreMesh(core_axis_name="c", subcore_axis_name="s")

    @functools.partial(
        pl.kernel, mesh=mesh,
        out_type=jax.ShapeDtypeStruct((n_slots, width), src.dtype),
        scratch_types=[
            pltpu.VMEM((n_steps, rows_per_step), jnp.int32),
            pltpu.VMEM((n_buf, rows_per_step, width), src.dtype),
            pltpu.SemaphoreType.DMA((n_buf,)),
            pltpu.SemaphoreType.DMA((n_buf,)),
        ],
    )
    def scatter(src_hbm, idx_hbm, out_hbm, idx_v, rows_v, sem_in, sem_out):
        worker = lax.axis_index("s") * info.num_cores + lax.axis_index("c")
        base = pl.multiple_of(worker * per_worker, per_worker)
        pltpu.sync_copy(idx_hbm.at[worker], idx_v)

        def read(b, step):
            off = pl.multiple_of(base + step * rows_per_step, rows_per_step)
            return pltpu.make_async_copy(src_hbm.at[pl.ds(off, rows_per_step)], rows_v.at[b], sem_in.at[b])

        def write(b, step):
            return pltpu.make_async_copy(rows_v.at[b], out_hbm.at[idx_v.at[step]], sem_out.at[b])

        for step in range(n_steps + 1):
            if step < n_steps:
                if step >= n_buf:
                    write(step % n_buf, step - n_buf).wait()
                read(step % n_buf, step).start()
            if step >= 1:
                read((step - 1) % n_buf, step - 1).wait()
                write((step - 1) % n_buf, step - 1).start()
        for step in range(n_steps - n_buf, n_steps):
            write(step % n_buf, step).wait()

    return scatter(src, pos.reshape(n_workers, n_steps, rows_per_step))


def _dispatch_plan(route, tm):
    T = route.shape[0]
    n_slots = T + N_CLASSES * tm
    cls = route[:, 0].astype(jnp.int32)
    onehot = (cls[:, None] == jnp.arange(N_CLASSES, dtype=jnp.int32)[None, :]).astype(jnp.int32)
    csum = jnp.cumsum(onehot, axis=0)
    rank = jnp.sum(onehot * csum, axis=1) - 1
    counts = csum[-1]
    padded = ((counts + tm - 1) // tm) * tm
    ends = jnp.cumsum(padded)
    starts = ends - padded
    pos = jnp.sum(onehot * starts[None, :], axis=1) + rank
    tile_start = jnp.arange(n_slots // tm, dtype=jnp.int32) * tm
    tile_cls = jnp.minimum(jnp.searchsorted(ends, tile_start, side="right"), N_CLASSES - 1).astype(jnp.int32)
    n_rows = jnp.clip(counts[tile_cls] - (tile_start - starts[tile_cls]), 0, tm).astype(jnp.int32)
    pairs = np.array([(a, b) for a in range(EXP_PER_GROUP) for b in range(a + 1, EXP_PER_GROUP)], np.int32)
    group, pair = tile_cls // PAIRS_PER_GROUP, tile_cls % PAIRS_PER_GROUP
    lo = group * EXP_PER_GROUP + jnp.asarray(pairs[:, 0])[pair]
    hi = group * EXP_PER_GROUP + jnp.asarray(pairs[:, 1])[pair]
    return pos, n_slots, lo, hi, n_rows


def _moe_sorted_kernel(lo_ref, hi_ref, rows_ref, xs_ref, ws_ref, wg_lo, wu_lo, wd_lo, wg_hi, wu_hi, wd_hi, o_ref):
    i = pl.program_id(0)

    @pl.when(rows_ref[i] > 0)
    def _():
        real = lax.broadcasted_iota(jnp.int32, (xs_ref.shape[0], 1), 0) < rows_ref[i]
        x = jnp.where(real, _unpack_pairs(xs_ref[...]), 0.0).astype(BF16)
        y = None
        for wg, wu, wd, col in ((wg_lo, wu_lo, wd_lo, 1), (wg_hi, wu_hi, wd_hi, 2)):
            a = _dot(x, wg[...])
            he = (a * _sigmoid(a)) * _dot(x, wu[...]) * jnp.where(real, ws_ref[:, col:col + 1], 0.0)
            part = _dot(he.astype(BF16), wd[...])
            y = part if y is None else y + part
        o_ref[...] = _pack_pairs(y.astype(BF16).astype(F32))

    @pl.when(rows_ref[i] == 0)
    def _():
        o_ref[...] = jnp.zeros_like(o_ref)


def _moe_sorted(xs, ws, lo, hi, n_rows, wg, wu, wd, tm):
    n_slots = xs.shape[0]
    half = D_MODEL // 2
    up = lambda sel: pl.BlockSpec((None, D_MODEL, D_EXPERT), lambda i, lo, hi, v: ((lo, hi)[sel][i], 0, 0))
    down = lambda sel: pl.BlockSpec((None, D_EXPERT, D_MODEL), lambda i, lo, hi, v: ((lo, hi)[sel][i], 0, 0))
    return pl.pallas_call(
        _moe_sorted_kernel,
        grid_spec=pltpu.PrefetchScalarGridSpec(
            num_scalar_prefetch=3,
            grid=(n_slots // tm,),
            in_specs=[
                pl.BlockSpec((tm, half), lambda i, lo, hi, v: (i, 0)),
                pl.BlockSpec((tm, LANES), lambda i, lo, hi, v: (i, 0)),
                up(0), up(0), down(0), up(1), up(1), down(1),
            ],
            out_specs=pl.BlockSpec((tm, half), lambda i, lo, hi, v: (i, 0)),
        ),
        out_shape=jax.ShapeDtypeStruct((n_slots, half), jnp.int32),
        compiler_params=_cparams("arbitrary"),
        name="moe_sorted",
    )(lo, hi, n_rows, xs, ws, wg, wu, wd, wg, wu, wd)


def _final_residual_kernel(y_ref, x_ref, mod_ref, fg_ref, o_ref):
    x = x_ref[...] + mod_ref[5:6, :] * _unpack_pairs(y_ref[...])
    o_ref[...] = x * lax.rsqrt(jnp.mean(x * x, axis=-1, keepdims=True) + EPS) * fg_ref[...]


def _final_residual(y, x, mod, final_g, B, L, ctx_rows, tm=512):
    T = B * L
    tm = min(tm, L)
    nl = L // tm
    row = (lambda i: CTX_ROW) if ctx_rows else (lambda i: i // nl)
    return pl.pallas_call(
        _final_residual_kernel,
        grid=(T // tm,),
        in_specs=[
            pl.BlockSpec((tm, D_MODEL // 2), lambda i: (i, 0)),
            pl.BlockSpec((tm, D_MODEL), lambda i: (i, 0)),
            pl.BlockSpec((None, 6, D_MODEL), lambda i: (row(i), 0, 0)),
            _const_spec((1, D_MODEL)),
        ],
        out_specs=pl.BlockSpec((tm, D_MODEL), lambda i: (i, 0)),
        out_shape=jax.ShapeDtypeStruct((T, D_MODEL), F32),
        compiler_params=_cparams("parallel"),
        name="final_residual",
    )(y, x, mod, final_g.reshape(1, D_MODEL))


def _moe(h, route, wg, wu, wd, layer, tm=256):
    pos, n_slots, lo, hi, n_rows = _dispatch_plan(route, tm)
    xs = _scatter_rows(h, pos, n_slots)
    ws = _scatter_rows(route, pos, n_slots)
    ys = _moe_sorted(xs, ws, lo + layer * N_EXPERTS, hi + layer * N_EXPERTS, n_rows, wg, wu, wd, tm)
    return _gather_rows(ys, pos)


def kernel(x_prompt, x_sample, cache_k, cache_v, state_rglru, c, c_ctx, w_ada, b_ada, norm1_g, norm2_g, w_in, w_out, hy_short_w, hy_short_b, hy_w1, hy_b1, hy_w2, hy_b2, hy_w3, hy_freq, hy_bias, rg_conv_w, rg_conv_b, rg_wa, rg_ba, rg_wx, rg_bx, rg_lambda, da_lambda, da_subln, w_router, b_router, moe_wg, moe_wu, moe_wd, final_g):
    Bp, Lp, D = x_prompt.shape
    Bs, Ls, _ = x_sample.shape
    assert Bs <= CTX_ROW
    cond = jnp.zeros((N_COND, D), F32).at[:Bs].set(c).at[CTX_ROW].set(c_ctx)
    mods = _ada_table(cond, w_ada, b_ada)

    dft = {L: tuple(jnp.asarray(m).astype(BF16) for m in _dft_mats(L)) for L in (Lp, Ls)}
    streams = [
        dict(B=Bp, L=Lp, ctx=True, x=x_prompt.reshape(Bp * Lp, D)),
        dict(B=Bs, L=Ls, ctx=False, x=x_sample.reshape(Bs * Ls, D)),
    ]
    w_in_b, w_out_b = w_in.astype(BF16), w_out.astype(BF16)
    wg, wu, wd = (w.astype(BF16).reshape((DEPTH * N_EXPERTS,) + w.shape[2:]) for w in (moe_wg, moe_wu, moe_wd))
    new_kv, ss = None, []
    for l in range(DEPTH):
        lam_init = 0.8 - 0.6 * math.exp(-0.3 * l)
        for st in streams:
            B, L, ctx = st["B"], st["L"], st["ctx"]
            cmat, smat = dft[L]
            outs = _norm_proj(st["x"], mods[l], norm1_g[l], w_in_b, l, B, L, ctx, kv_prev=new_kv if ctx else None,
                              pending=(st["y"], mods[l - 1]) if l else None)
            p_hy, p_g, p_x, q, k, v = outs[:6]
            if l:
                st["x"] = outs[6]
            kre, kim = _hy_spectra(L, cmat, smat, hy_w1[l], hy_b1[l], hy_w2[l], hy_b2[l], hy_w3[l], hy_freq[l])
            y_hy = _hyena(p_hy, B, L, cmat, smat, kre, kim, hy_short_w[l], hy_short_b[l], hy_bias[l])
            rg_args = (rg_conv_w[l], rg_conv_b[l], rg_wa[l], rg_ba[l], rg_wx[l], rg_bx[l], rg_lambda[l])
            if ctx:
                y_rg, s_l = _rglru(p_g, p_x, B, L, *rg_args, None)
                o = _attention(q, k, v, l, None, da_lambda[l], da_subln[l], lam_init, B, L)
                new_kv = (k, v)
                ss.append(s_l)
            else:
                y_rg = _rglru(p_g, p_x, B, L, *rg_args, state_rglru[:, l])
                o = _attention(q, k, v, l, (cache_k, cache_v), da_lambda[l], da_subln[l], lam_init, B, L)
            st["x"], h2, route = _out_proj(y_hy, y_rg, o, w_out_b, l, st["x"], mods[l], norm2_g[l],
                                           w_router, b_router, B, L, ctx)
            st["y"] = _moe(h2, route, wg, wu, wd, l)
    y_prompt, y_sample = (
        _final_residual(st["y"], st["x"], mods[DEPTH - 1], final_g, st["B"], st["L"], st["ctx"]).reshape(shape)
        for st, shape in zip(streams, (x_prompt.shape, x_sample.shape)))
    return (y_prompt, y_sample, new_kv[0], new_kv[1], jnp.stack(ss, axis=1))
```

```python
import functools
import math

import numpy as np
import jax
import jax.numpy as jnp
from jax import lax
from jax.experimental import pallas as pl
from jax.experimental.pallas import tpu as pltpu
from jax.experimental.pallas import tpu_sc as plsc

F32 = jnp.float32
BF16 = jnp.bfloat16

D_MODEL = 1024
DEPTH = 2
GRID_W = 64
D_HY = 256
HY_EMB = 33
HY_BANDS = (HY_EMB - 1) // 2
HY_FFN = 64
HY_MIN_DECAY = math.log(1e-2) / 1.5
HY_MAX_DECAY = math.log(1e-2) / 0.3
D_RG = 256
N_RG_HEADS = 4
RG_C = 8.0
N_DA_HEADS = 4
DA_HEAD = 64
DA_VDIM = 2 * DA_HEAD
D_DA = N_DA_HEADS * DA_VDIM
D_MIX = D_HY + D_RG + D_DA
D_IN = 3 * D_HY + 2 * D_RG + 3 * D_DA
ROPE_PAIRS = DA_HEAD // 4
ROPE_THETA = 10000.0
N_EXPERTS = 16
N_GROUPS = 4
EXP_PER_GROUP = N_EXPERTS // N_GROUPS
D_EXPERT = 512
PAIRS_PER_GROUP = EXP_PER_GROUP * (EXP_PER_GROUP - 1) // 2
N_CLASSES = N_GROUPS * PAIRS_PER_GROUP
EPS = 1e-6
N_COND = 16
CTX_ROW = 8
LANES = 128
SUBLANES = 8
VMEM_LIMIT = 56 * 1024 * 1024


def _cparams(*sem):
    return pltpu.CompilerParams(dimension_semantics=sem, vmem_limit_bytes=VMEM_LIMIT)


def _split(x):
    hi = x.astype(BF16)
    lo = (x - hi.astype(F32)).astype(BF16)
    return hi, lo


def _dot(a, b):
    return jnp.dot(a, b, preferred_element_type=F32)


def _dot3(a, b):
    ah, al = _split(a)
    bh, bl = _split(b)
    return _dot(ah, bh) + _dot(al, bh) + _dot(ah, bl)


def _dot_nt(a, b):
    return lax.dot_general(a, b, (((1,), (1,)), ((), ())), preferred_element_type=F32)


def _sigmoid(x):
    return 1.0 / (1.0 + jnp.exp(-x))


def _const_spec(shape):
    n = len(shape)
    return pl.BlockSpec(shape, lambda *_: (0,) * n)


def _ada_kernel(c_ref, w_ref, b_ref, o_ref):
    c = c_ref[...]
    s = c * _sigmoid(c)
    o_ref[...] = _dot3(s, w_ref[...]) + b_ref[...]


def _ada_table(cond, w_ada, b_ada):
    D = D_MODEL
    out = pl.pallas_call(
        _ada_kernel,
        grid=(DEPTH, 6),
        in_specs=[
            pl.BlockSpec((N_COND, D), lambda l, j: (0, 0)),
            pl.BlockSpec((None, D, D), lambda l, j: (l, 0, j)),
            pl.BlockSpec((None, None, 1, D), lambda l, j: (l, j, 0, 0)),
        ],
        out_specs=pl.BlockSpec((None, None, N_COND, D), lambda l, j: (l, j, 0, 0)),
        out_shape=jax.ShapeDtypeStruct((DEPTH, 6, N_COND, D), F32),
        compiler_params=_cparams("parallel", "parallel"),
        name="ada_table",
    )(cond, w_ada, b_ada.reshape(DEPTH, 6, 1, D))
    return out.transpose(0, 2, 1, 3)


def _rope_tables(L):
    t = np.arange(L)
    j = np.arange(LANES)
    jj = j % DA_HEAD
    is_col = (jj // (DA_HEAD // 2)) == 1
    pair = jj % ROPE_PAIRS
    second = (jj % (DA_HEAD // 2)) >= ROPE_PAIRS
    inv = ROPE_THETA ** (-np.arange(ROPE_PAIRS, dtype=np.float64) / ROPE_PAIRS)
    pos = np.where(is_col[None, :], (t % GRID_W)[:, None], (t // GRID_W)[:, None]).astype(np.float64)
    ang = pos * inv[pair][None, :]
    cos = np.cos(ang).astype(np.float32)
    sin = np.sin(ang).astype(np.float32)
    sin_a = np.where(second[None, :], 0.0, -sin).astype(np.float32)
    sin_b = np.where(second[None, :], sin, 0.0).astype(np.float32)
    return cos, sin_a, sin_b


def _rope(x, cos, sin_a, sin_b):
    nxt = pltpu.roll(x, LANES - ROPE_PAIRS, axis=1)
    prv = pltpu.roll(x, ROPE_PAIRS, axis=1)
    return x * cos + nxt * sin_a + prv * sin_b


def _norm_proj_kernel(rope, kv_dtype, pending, kv_layer, x_ref, mod_ref, g_ref, w_ref, *rest):
    x = x_ref[...]
    if pending:
        y_ref, modp_ref, xnew_ref = rest[0], rest[1], rest[-1]
        x = x + modp_ref[5:6, :] * _unpack_pairs(y_ref[...])
        xnew_ref[...] = x
        rest = rest[2:-1]
    if rope:
        cos_ref, sa_ref, sb_ref = rest[:3]
    phy_ref, pg_ref, px_ref, q_ref, k_ref, v_ref = rest[-6:]
    ms = jnp.mean(x * x, axis=-1, keepdims=True)
    y = x * lax.rsqrt(ms + EPS) * g_ref[...]
    h = (y * (1.0 + mod_ref[1:2, :]) + mod_ref[0:1, :]).astype(BF16)
    o = 3 * D_HY
    phy_ref[...] = _dot(h, w_ref[:, 0:o]).astype(BF16)
    pg_ref[...] = _dot(h, w_ref[:, o:o + D_RG])
    px_ref[...] = _dot(h, w_ref[:, o + D_RG:o + 2 * D_RG])
    o += 2 * D_RG
    q = _dot(h, w_ref[:, o:o + D_DA]) * (DA_HEAD ** -0.5 * math.log2(math.e))
    k = _dot(h, w_ref[:, o + D_DA:o + 2 * D_DA])
    v = _dot(h, w_ref[:, o + 2 * D_DA:o + 3 * D_DA])
    if rope:
        cos, sa, sb = cos_ref[...], sa_ref[...], sb_ref[...]
    for hd in range(N_DA_HEADS):
        sl = slice(hd * DA_VDIM, (hd + 1) * DA_VDIM)
        qh, kh = q[:, sl], k[:, sl]
        if rope:
            qh = _rope(qh, cos, sa, sb)
            kh = _rope(kh, cos, sa, sb)
        q_ref[hd] = qh.astype(BF16)
        if kv_layer is None:
            k_ref[hd] = kh.astype(kv_dtype)
            v_ref[hd] = v[:, sl].astype(kv_dtype)
        else:
            for l in range(DEPTH):
                k_ref[l, hd] = kh.astype(kv_dtype) if l == kv_layer else jnp.zeros_like(kh, dtype=kv_dtype)
                v_ref[l, hd] = v[:, sl].astype(kv_dtype) if l == kv_layer else jnp.zeros_like(kh, dtype=kv_dtype)


def _norm_proj(x, mod, g, w_in, layer, B, L, ctx, kv_prev=None, pending=None, tm=512):
    T = B * L
    tm = min(tm, L)
    nl = L // tm
    rope, kv_dtype = not ctx, (F32 if ctx else BF16)
    row = (lambda i: CTX_ROW) if ctx else (lambda i: i // nl)
    mod_spec = pl.BlockSpec((None, 6, D_MODEL), lambda i: (row(i), 0, 0))
    in_specs = [
        pl.BlockSpec((tm, D_MODEL), lambda i: (i, 0)),
        mod_spec,
        _const_spec((1, D_MODEL)),
        pl.BlockSpec((None, D_MODEL, D_IN), lambda i: (layer, 0, 0)),
    ]
    args = [x, mod, g.reshape(1, D_MODEL), w_in]
    if pending is not None:
        in_specs += [pl.BlockSpec((tm, D_MODEL // 2), lambda i: (i, 0)), mod_spec]
        args += list(pending)
    if rope:
        tabs = _rope_tables(L)
        in_specs += [pl.BlockSpec((tm, LANES), lambda i: (i % nl, 0))] * 3
        args += [jnp.asarray(t) for t in tabs]
    head_spec = pl.BlockSpec((None, N_DA_HEADS, tm, DA_VDIM), lambda i: (i // nl, 0, i % nl, 0))
    head_shape = (B, N_DA_HEADS, L, DA_VDIM)
    kv_spec, kv_shape, aliases, kv_layer = head_spec, head_shape, {}, None
    if ctx:
        kv_shape = (B, DEPTH, N_DA_HEADS, L, DA_VDIM)
        if kv_prev is not None:
            kv_spec = pl.BlockSpec((None, None, N_DA_HEADS, tm, DA_VDIM), lambda i: (i // nl, layer, 0, i % nl, 0))
            aliases = {len(args): 4, len(args) + 1: 5}
            in_specs += [pl.BlockSpec(memory_space=pl.ANY)] * 2
            args += list(kv_prev)
        else:
            kv_spec = pl.BlockSpec((None, DEPTH, N_DA_HEADS, tm, DA_VDIM), lambda i: (i // nl, 0, 0, i % nl, 0))
            kv_layer = layer
    out_specs = [
        pl.BlockSpec((tm, 3 * D_HY), lambda i: (i, 0)),
        pl.BlockSpec((tm, D_RG), lambda i: (i, 0)),
        pl.BlockSpec((tm, D_RG), lambda i: (i, 0)),
        head_spec, kv_spec, kv_spec,
    ]
    out_shape = [
        jax.ShapeDtypeStruct((T, 3 * D_HY), BF16),
        jax.ShapeDtypeStruct((T, D_RG), F32),
        jax.ShapeDtypeStruct((T, D_RG), F32),
        jax.ShapeDtypeStruct(head_shape, BF16),
        jax.ShapeDtypeStruct(kv_shape, kv_dtype),
        jax.ShapeDtypeStruct(kv_shape, kv_dtype),
    ]
    if pending is not None:
        out_specs.append(pl.BlockSpec((tm, D_MODEL), lambda i: (i, 0)))
        out_shape.append(jax.ShapeDtypeStruct((T, D_MODEL), F32))
    return pl.pallas_call(
        functools.partial(_norm_proj_kernel, rope, kv_dtype, pending is not None, kv_layer),
        grid=(T // tm,),
        in_specs=in_specs,
        out_specs=out_specs,
        out_shape=out_shape,
        input_output_aliases=aliases,
        compiler_params=_cparams("parallel"),
        name="norm_proj_rope" if rope else "norm_proj",
    )(*args)


def _dft_mats(L):
    n = 2 * L - 1
    fs = (np.arange(L, dtype=np.int64)[:, None] * np.arange(L, dtype=np.int64)[None, :]) % n
    ang = fs.astype(np.float64) * (2.0 * np.pi / n)
    return np.cos(ang).astype(np.float32), np.sin(ang).astype(np.float32)


def _hy_features(L):
    t = np.linspace(0.0, 1.0, L, dtype=np.float64)[:, None]
    ang = ((2.0 * math.pi / L) * np.arange(L, dtype=np.float64))[:, None]
    bands = np.linspace(1e-4, HY_BANDS - 1, HY_BANDS, dtype=np.float64)[None, :]
    ba = bands * ang
    z = np.concatenate([t, np.cos(ba), -np.sin(ba)], axis=-1).astype(np.float32)
    return np.pad(z, ((0, 0), (0, LANES - HY_EMB)))


def _hy_filter_kernel(z_ref, w1_ref, b1_ref, w2_ref, b2_ref, w3_ref, fr_ref, rc_ref, rs_ref):
    tr = z_ref.shape[0]
    z = z_ref[...]
    h = jnp.sin(fr_ref[0:1, :] * (_dot3(z, w1_ref[...]) + b1_ref[...]))
    h = jnp.sin(fr_ref[1:2, :] * (_dot3(h, w2_ref[...]) + b2_ref[...]))
    h = _dot3(h, w3_ref[...])
    t = z[:, 0:1]
    step = (HY_MAX_DECAY - HY_MIN_DECAY) / (D_HY - 1)
    deltas = HY_MIN_DECAY + step * lax.broadcasted_iota(jnp.int32, (1, D_HY), 1).astype(F32)
    window = jnp.exp(-t * jnp.abs(deltas))
    not_first = pl.program_id(0) * tr + lax.broadcasted_iota(jnp.int32, (tr, 1), 0) > 0
    for o in range(2):
        hf = h[:, (2 * o) * D_HY:(2 * o + 1) * D_HY] * window
        hb = jnp.where(not_first, h[:, (2 * o + 1) * D_HY:(2 * o + 2) * D_HY] * window, 0.0)
        rc_ref[:, o * D_HY:(o + 1) * D_HY] = (hf + hb).astype(BF16)
        rs_ref[:, o * D_HY:(o + 1) * D_HY] = (hb - hf).astype(BF16)


def _hy_spectrum_kernel(c_ref, s_ref, rc_ref, rs_ref, w_ref, kre_ref, kim_ref):
    w = w_ref[...]
    kre_ref[...] = _dot(c_ref[...], rc_ref[...]) * w
    kim_ref[...] = _dot(s_ref[...], rs_ref[...]) * w


def _hy_spectra(L, cmat, smat, w1, b1, w2, b2, w3, freq):
    z = jnp.asarray(_hy_features(L))
    w1p = jnp.pad(w1, ((0, LANES - HY_EMB), (0, 0)))
    nw = 2 * D_HY
    tr = min(L, 256)
    rc, rs = pl.pallas_call(
        _hy_filter_kernel,
        grid=(L // tr,),
        in_specs=[
            pl.BlockSpec((tr, LANES), lambda i: (i, 0)),
            _const_spec((LANES, HY_FFN)), _const_spec((1, HY_FFN)),
            _const_spec((HY_FFN, HY_FFN)), _const_spec((1, HY_FFN)),
            _const_spec((HY_FFN, 2 * nw)), _const_spec((2, HY_FFN)),
        ],
        out_specs=[pl.BlockSpec((tr, nw), lambda i: (i, 0))] * 2,
        out_shape=[jax.ShapeDtypeStruct((L, nw), BF16)] * 2,
        compiler_params=_cparams("parallel"),
        name="hy_filter",
    )(z, w1p, b1.reshape(1, HY_FFN), w2, b2.reshape(1, HY_FFN), w3, freq)
    n = 2 * L - 1
    wsc = np.full((L, 1), 2.0 / n, np.float32)
    wsc[0, 0] = 1.0 / n
    return pl.pallas_call(
        _hy_spectrum_kernel,
        grid=(L // tr,),
        in_specs=[
            pl.BlockSpec((tr, L), lambda i: (i, 0)),
            pl.BlockSpec((tr, L), lambda i: (i, 0)),
            _const_spec((L, nw)),
            _const_spec((L, nw)),
            pl.BlockSpec((tr, 1), lambda i: (i, 0)),
        ],
        out_specs=[pl.BlockSpec((tr, nw), lambda i: (i, 0))] * 2,
        out_shape=[jax.ShapeDtypeStruct((L, nw), F32)] * 2,
        compiler_params=_cparams("parallel"),
        name="hy_spectrum",
    )(cmat, smat, rc, rs, jnp.asarray(wsc))


def _hyena_kernel(L, tr, p_ref, sw_ref, sb_ref, bias_ref, c_ref, s_ref, kre_ref, kim_ref, o_ref,
                  pad_ref, u_ref, sig_ref, sig16_ref, zre_ref, zim_ref):
    C3 = 3 * D_HY
    zeros = jnp.zeros((8, C3), F32)
    pad_ref[0:8, :] = zeros
    pad_ref[8 + L:16 + L, :] = zeros
    chunks = [slice(r0, r0 + tr) for r0 in range(0, L, tr)]
    for c in chunks:
        pad_ref[8 + c.start:8 + c.stop, :] = p_ref[c, :].astype(F32)
    for c in chunks:
        u = sb_ref[...]
        for j in range(3):
            u = u + pad_ref[7 + j + c.start:7 + j + c.stop, :] * sw_ref[j:j + 1, :]
        u_ref[c, :] = u[:, D_HY:C3]
        sig_ref[c, :] = u[:, 0:D_HY]
        sig16_ref[c, :] = u[:, 0:D_HY].astype(BF16)

    for o in range(2):
        ko = slice(o * D_HY, (o + 1) * D_HY)
        for c in chunks:
            ure = _dot(c_ref[c, :], sig16_ref[...])
            us = _dot(s_ref[c, :], sig16_ref[...])
            kre, kim = kre_ref[c, ko], kim_ref[c, ko]
            zre_ref[c, :] = (ure * kre + us * kim).astype(BF16)
            zim_ref[c, :] = (ure * kim - us * kre).astype(BF16)
        gate = slice(o * D_HY, (o + 1) * D_HY)
        for c in chunks:
            y = _dot(c_ref[c, :], zre_ref[...]) - _dot(s_ref[c, :], zim_ref[...])
            z = u_ref[c, gate] * (y + sig_ref[c, :] * bias_ref[o:o + 1, :])
            if o == 0:
                sig_ref[c, :] = z
                sig16_ref[c, :] = z.astype(BF16)
            else:
                o_ref[c, :] = z.astype(o_ref.dtype)


def _hyena(p_hy, B, L, cmat, smat, kre, kim, short_w, short_b, bias, tr=512):
    C3 = 3 * D_HY
    tr = min(tr, L)
    once = pl.Buffered(1)
    return pl.pallas_call(
        functools.partial(_hyena_kernel, L, tr),
        grid=(B,),
        in_specs=[
            pl.BlockSpec((L, C3), lambda b: (b, 0)),
            _const_spec((3, C3)),
            _const_spec((1, C3)),
            _const_spec((2, D_HY)),
            pl.BlockSpec((L, L), lambda b: (0, 0), pipeline_mode=once),
            pl.BlockSpec((L, L), lambda b: (0, 0), pipeline_mode=once),
            pl.BlockSpec((L, 2 * D_HY), lambda b: (0, 0), pipeline_mode=once),
            pl.BlockSpec((L, 2 * D_HY), lambda b: (0, 0), pipeline_mode=once),
        ],
        out_specs=pl.BlockSpec((L, D_HY), lambda b: (b, 0)),
        out_shape=jax.ShapeDtypeStruct((B * L, D_HY), BF16),
        scratch_shapes=[
            pltpu.VMEM((L + 16, C3), F32),
            pltpu.VMEM((L, 2 * D_HY), F32),
            pltpu.VMEM((L, D_HY), F32),
            pltpu.VMEM((L, D_HY), BF16),
            pltpu.VMEM((L, D_HY), BF16),
            pltpu.VMEM((L, D_HY), BF16),
        ],
        compiler_params=_cparams("parallel"),
        name="hyena",
    )(p_hy, short_w, short_b.reshape(1, C3), bias, cmat, smat, kre, kim)


def _softplus(z):
    return jnp.maximum(z, 0.0) + jnp.log1p(jnp.exp(-jnp.abs(z)))


def _sigmoid_tanh(x):
    return 0.5 + 0.5 * jnp.tanh(0.5 * x)


def _gelu_tanh(x):
    return 0.5 * x * (1.0 + jnp.tanh(math.sqrt(2.0 / math.pi) * (x + 0.044715 * x * x * x)))


def _rglru_kernel(L, has_state, pg_ref, px_ref, cw_ref, cb_ref, w3_ref, gb_ref, lam_ref, *rest):
    if has_state:
        st_ref, y_ref, pad_ref, a_ref, b_ref, h_ref = rest
    else:
        y_ref, st_out_ref, pad_ref, a_ref, b_ref, h_ref = rest
    C = D_RG
    zeros = jnp.zeros((8, C), F32)
    pad_ref[0:8, :] = zeros
    pad_ref[8 + L:16 + L, :] = zeros
    pad_ref[8:8 + L, :] = px_ref[...]
    sp = _softplus(-lam_ref[...])
    tr = min(L, 256)
    for r0 in range(0, L, tr):
        xr = cb_ref[...]
        for j in range(4):
            xr = xr + pad_ref[6 + j + r0:6 + j + r0 + tr, :] * cw_ref[j:j + 1, :]
        xh, xl = _split(xr)
        x3 = jnp.concatenate([xh, xl, xh], axis=1)
        for d in range(2):
            g = []
            for m in range(2):
                cols = slice((2 * d + m) * C, (2 * d + m + 1) * C)
                g.append(_sigmoid_tanh(_dot(x3, w3_ref[:, cols]) + gb_ref[:, cols]))
            log_a = -RG_C * g[0] * sp[d:d + 1, :]
            a = jnp.exp(log_a)
            a_ref[d, r0:r0 + tr, :] = a
            b_ref[d, r0:r0 + tr, :] = jnp.sqrt(-jnp.tanh(log_a) * (1.0 + a * a)) * (g[1] * xr)

    if has_state:
        h0f, h0b = st_ref[0:1, :], st_ref[1:2, :]
    else:
        h0f = h0b = jnp.zeros((1, C), F32)

    row = lax.broadcasted_iota(jnp.int32, (SUBLANES, 1), 0)

    def tile_scan(a, b, reverse):
        for d in (1, 2, 4):
            shift = SUBLANES - d if reverse else d
            valid = (row < SUBLANES - d) if reverse else (row >= d)
            a_s, b_s = pltpu.roll(a, shift, axis=0), pltpu.roll(b, shift, axis=0)
            b = jnp.where(valid, a * b_s + b, b)
            a = jnp.where(valid, a * a_s, a)
        return a, b

    def step(i, carry):
        hf, hb = carry
        t0 = pl.multiple_of(i * SUBLANES, SUBLANES)
        tb0 = pl.multiple_of(L - SUBLANES - i * SUBLANES, SUBLANES)
        af, bf = tile_scan(a_ref[0, pl.ds(t0, SUBLANES), :], b_ref[0, pl.ds(t0, SUBLANES), :], False)
        ab, bb = tile_scan(a_ref[1, pl.ds(tb0, SUBLANES), :], b_ref[1, pl.ds(tb0, SUBLANES), :], True)
        hf_tile = af * hf + bf
        hb_tile = ab * hb + bb
        h_ref[0, pl.ds(t0, SUBLANES), :] = hf_tile
        h_ref[1, pl.ds(tb0, SUBLANES), :] = hb_tile
        return hf_tile[SUBLANES - 1:SUBLANES], hb_tile[0:1]

    lax.fori_loop(0, L // SUBLANES, step, (h0f, h0b), unroll=2)
    y_ref[...] = ((h_ref[0] + h_ref[1]) * _gelu_tanh(pg_ref[...])).astype(y_ref.dtype)
    if not has_state:
        st_out_ref[0:1, :] = h_ref[0, L - 1:L, :]
        st_out_ref[1:2, :] = h_ref[1, 0:1, :]


def _block_diag(w):
    H, d, _ = w.shape
    eye = jnp.eye(H, dtype=w.dtype)
    return (eye[:, None, :, None] * w[:, :, None, :]).reshape(H * d, H * d)


def _rglru(p_g, p_x, B, L, conv_w, conv_b, wa, ba, wx, bx, lam, state):
    C = D_RG
    wcat = jnp.concatenate([_block_diag(wa[0]), _block_diag(wx[0]), _block_diag(wa[1]), _block_diag(wx[1])], axis=1)
    wh = wcat.astype(BF16)
    wl = (wcat - wh.astype(F32)).astype(BF16)
    w3 = jnp.concatenate([wh, wh, wl], axis=0)
    gb = jnp.concatenate([ba[0], bx[0], ba[1], bx[1]]).reshape(1, 4 * C)
    has_state = state is not None
    in_specs = [
        pl.BlockSpec((L, C), lambda b: (b, 0)),
        pl.BlockSpec((L, C), lambda b: (b, 0)),
        _const_spec((4, C)),
        _const_spec((1, C)),
        _const_spec((3 * C, 4 * C)),
        _const_spec((1, 4 * C)),
        _const_spec((2, C)),
    ]
    args = [p_g, p_x, conv_w, conv_b.reshape(1, C), w3, gb, lam]
    y_spec = pl.BlockSpec((L, C), lambda b: (b, 0))
    y_shape = jax.ShapeDtypeStruct((B * L, C), BF16)
    if has_state:
        in_specs.append(pl.BlockSpec((None, 2, C), lambda b: (b, 0, 0)))
        args.append(state)
        out_specs, out_shape = y_spec, y_shape
    else:
        out_specs = [y_spec, pl.BlockSpec((None, 2, C), lambda b: (b, 0, 0))]
        out_shape = [y_shape, jax.ShapeDtypeStruct((B, 2, C), F32)]
    return pl.pallas_call(
        functools.partial(_rglru_kernel, L, has_state),
        grid=(B,),
        in_specs=in_specs,
        out_specs=out_specs,
        out_shape=out_shape,
        scratch_shapes=[
            pltpu.VMEM((L + 16, C), F32),
            pltpu.VMEM((2, L, C), F32),
            pltpu.VMEM((2, L, C), F32),
            pltpu.VMEM((2, L, C), F32),
        ],
        compiler_params=_cparams("parallel"),
        name="rglru_state" if has_state else "rglru",
    )(*args)


def _attn_kernel(L, P, tq, unroll, lam_init, q_ref, k_ref, v_ref, *rest):
    if P:
        ck_ref, cv_ref, dal_ref, sub_ref, o_ref, kk_ref, vv_ref, s_ref = rest
    else:
        dal_ref, sub_ref, o_ref, kk_ref, vv_ref, s_ref = rest
    lv = dal_ref[...]
    s01 = jnp.sum(lv[0:1, :] * lv[1:2, :], axis=-1, keepdims=True)
    s23 = jnp.sum(lv[2:3, :] * lv[3:4, :], axis=-1, keepdims=True)
    lam = jnp.exp(s01) - jnp.exp(s23) + lam_init
    first_half = lax.broadcasted_iota(jnp.int32, (1, DA_VDIM), 1) < DA_HEAD
    sub = sub_ref[...] * (1.0 - lam_init)
    for hd in range(N_DA_HEADS):
        if P:
            kk_ref[0:P, :] = ck_ref[hd].astype(BF16)
            vv_ref[0:P, :] = cv_ref[hd].astype(BF16)
        kk_ref[P:P + L, :] = k_ref[hd].astype(BF16)
        vv_ref[P:P + L, :] = v_ref[hd].astype(BF16)

        def scores(i, buf):
            q = q_ref[hd, pl.ds(pl.multiple_of(i * tq, tq), tq), :]
            zero = jnp.zeros_like(q)
            qs = jnp.concatenate([jnp.where(first_half, q, zero), jnp.where(first_half, zero, q)], axis=0)
            s_ref[buf] = _dot_nt(qs, kk_ref[...])

        def finish(i, buf):
            s = s_ref[buf]
            p = jnp.exp2(s - jnp.max(s, axis=-1, keepdims=True))
            rinv = 1.0 / jnp.sum(p, axis=-1, keepdims=True)
            acc = _dot(p.astype(BF16), vv_ref[...])
            o = acc[0:tq] * rinv[0:tq] - acc[tq:2 * tq] * (lam * rinv[tq:2 * tq])
            o = o * lax.rsqrt(jnp.mean(o * o, axis=-1, keepdims=True) + EPS) * sub
            r0 = pl.multiple_of(i * tq, tq)
            o_ref[pl.ds(r0, tq), hd * DA_VDIM:(hd + 1) * DA_VDIM] = o.astype(o_ref.dtype)

        def pair(j, carry):
            i = 2 * j
            scores(i + 1, 1)
            finish(i, 0)
            scores(i + 2, 0)
            finish(i + 1, 1)
            return carry

        n = L // tq
        scores(0, 0)
        lax.fori_loop(0, n // 2 - 1, pair, 0, unroll=unroll)
        scores(n - 1, 1)
        finish(n - 2, 0)
        finish(n - 1, 1)


def _attention(q, k, v, layer, cache, dal, subln, lam_init, B, L, tq=128, unroll=2):
    H, dv = N_DA_HEADS, DA_VDIM
    hspec = pl.BlockSpec((None, H, L, dv), lambda b: (b, 0, 0, 0))
    kvspec = hspec if k.ndim == 4 else pl.BlockSpec((None, None, H, L, dv), lambda b: (b, layer, 0, 0, 0))
    in_specs = [hspec, kvspec, kvspec]
    args = [q, k, v]
    P = 0
    if cache is not None:
        ck, cv = cache
        P = ck.shape[3]
        cspec = pl.BlockSpec((None, None, H, P, dv), lambda b: (b, layer, 0, 0, 0))
        in_specs += [cspec, cspec]
        args += [ck, cv]
    assert L % tq == 0
    in_specs += [_const_spec((4, DA_HEAD)), _const_spec((1, dv))]
    args += [dal, subln.reshape(1, dv)]
    return pl.pallas_call(
        functools.partial(_attn_kernel, L, P, tq, min(unroll, max(1, L // tq // 2 - 1)), lam_init),
        grid=(B,),
        in_specs=in_specs,
        out_specs=pl.BlockSpec((L, H * dv), lambda b: (b, 0)),
        out_shape=jax.ShapeDtypeStruct((B * L, H * dv), BF16),
        scratch_shapes=[pltpu.VMEM((P + L, dv), BF16), pltpu.VMEM((P + L, dv), BF16),
                        pltpu.VMEM((2, 2 * tq, P + L), F32)],
        compiler_params=_cparams("parallel"),
        name="diff_attn_cache" if P else "diff_attn",
    )(*args)


def _route(logits):
    m = logits[0]
    for e in range(1, N_EXPERTS):
        m = jnp.maximum(m, logits[e])
    ex = [jnp.exp(l - m) for l in logits]
    tot = ex[0]
    for e in range(1, N_EXPERTS):
        tot = tot + ex[e]
    inv = 1.0 / tot
    p = [e_ * inv for e_ in ex]
    G = EXP_PER_GROUP
    best, gsel = None, None
    for g in range(N_GROUPS):
        a = p[g * G:(g + 1) * G]
        sc = None
        for i in range(G):
            for j in range(i + 1, G):
                pair = a[i] + a[j]
                sc = pair if sc is None else jnp.maximum(sc, pair)
        if g == 0:
            best, gsel = sc, jnp.zeros_like(sc, dtype=jnp.int32)
        else:
            upd = sc > best
            best = jnp.where(upd, sc, best)
            gsel = jnp.where(upd, g, gsel)
    vals = []
    for j in range(G):
        vj = p[j]
        for g in range(1, N_GROUPS):
            vj = jnp.where(gsel == g, p[g * G + j], vj)
        vals.append(vj)
    p1, i1 = vals[0], jnp.zeros_like(gsel)
    for j in range(1, G):
        upd = vals[j] > p1
        p1 = jnp.where(upd, vals[j], p1)
        i1 = jnp.where(upd, j, i1)
    p2, i2 = None, None
    for j in range(G):
        cand = jnp.where(i1 == j, -1.0, vals[j])
        if p2 is None:
            p2, i2 = cand, jnp.zeros_like(gsel)
        else:
            upd = cand > p2
            p2 = jnp.where(upd, cand, p2)
            i2 = jnp.where(upd, j, i2)
    den = 1.0 / (p1 + p2)
    w1, w2 = p1 * den, p2 * den
    swap = i2 < i1
    a, b = jnp.where(swap, i2, i1), jnp.where(swap, i1, i2)
    w_lo, w_hi = jnp.where(swap, w2, w1), jnp.where(swap, w1, w2)
    pair = jnp.where(a == 0, b - 1, jnp.where(a == 1, b + 1, 5))
    cls = gsel * PAIRS_PER_GROUP + pair
    return cls.astype(F32), w_lo, w_hi


def _pack_pairs(x):
    n = x.shape[1] // 2
    b = pltpu.bitcast(x, jnp.uint32)
    w = (b[:, :n] >> 16) | (b[:, n:] & jnp.uint32(0xFFFF0000))
    return pltpu.bitcast(w, jnp.int32)


def _unpack_pairs(w):
    u = pltpu.bitcast(w, jnp.uint32)
    lo = pltpu.bitcast(u << 16, F32)
    hi = pltpu.bitcast(u & jnp.uint32(0xFFFF0000), F32)
    return jnp.concatenate([lo, hi], axis=1)


def _out_proj_kernel(yh_ref, yr_ref, o_ref, w_ref, x_ref, mod_ref, g_ref, wrh_ref, wrl_ref, br_ref,
                     xo_ref, h_ref, route_ref, y_ref):
    n_sub = y_ref.shape[0]
    sub = x_ref.shape[0] // n_sub

    def project(j):
        r = slice(j * sub, (j + 1) * sub)
        y_ref[j] = (_dot(yh_ref[r, :], w_ref[0:D_HY, :]) + _dot(yr_ref[r, :], w_ref[D_HY:D_HY + D_RG, :])
                    + _dot(o_ref[r, :], w_ref[D_HY + D_RG:D_MIX, :]))

    def finish(j):
        r = slice(j * sub, (j + 1) * sub)
        x = x_ref[r, :] + mod_ref[2:3, :] * y_ref[j]
        xo_ref[r, :] = x
        ms = jnp.mean(x * x, axis=-1, keepdims=True)
        h = (x * lax.rsqrt(ms + EPS) * g_ref[...]) * (1.0 + mod_ref[4:5, :]) + mod_ref[3:4, :]
        hh, hl = _split(h)
        h_ref[r, :] = _pack_pairs(hh.astype(F32))
        lg = _dot_nt(wrh_ref[...], hh) + _dot_nt(wrh_ref[...], hl) + _dot_nt(wrl_ref[...], hh) + br_ref[...]
        info = _route([lg[e:e + 1, :] for e in range(N_EXPERTS)])
        rt = jnp.concatenate(list(info) + [jnp.zeros((LANES - len(info), sub), F32)], axis=0)
        route_ref[r, :] = rt.T

    project(0)
    for j in range(n_sub):
        if j + 1 < n_sub:
            project(j + 1)
        finish(j)


def _out_proj(y_hy, y_rg, o, w_out, layer, x, mod, g2, w_router, b_router, B, L, ctx_rows, tm=1024, n_sub=2):
    T = B * L
    tm = min(tm, T if ctx_rows else L)
    assert T % tm == 0 and tm % n_sub == 0
    nl = max(L // tm, 1)
    row = (lambda i: CTX_ROW) if ctx_rows else (lambda i: i // nl)
    wrt = w_router.T
    wrh = wrt.astype(BF16)
    wrl = (wrt - wrh.astype(F32)).astype(BF16)
    rows = lambda w: pl.BlockSpec((tm, w), lambda i: (i, 0))
    return pl.pallas_call(
        _out_proj_kernel,
        grid=(T // tm,),
        in_specs=[
            rows(D_HY), rows(D_RG), rows(D_DA),
            pl.BlockSpec((None, D_MIX, D_MODEL), lambda i: (layer, 0, 0)),
            rows(D_MODEL),
            pl.BlockSpec((None, 6, D_MODEL), lambda i: (row(i), 0, 0)),
            _const_spec((1, D_MODEL)),
            _const_spec((N_EXPERTS, D_MODEL)),
            _const_spec((N_EXPERTS, D_MODEL)),
            _const_spec((N_EXPERTS, 1)),
        ],
        out_specs=[rows(D_MODEL), rows(D_MODEL // 2), rows(LANES)],
        out_shape=[
            jax.ShapeDtypeStruct((T, D_MODEL), F32),
            jax.ShapeDtypeStruct((T, D_MODEL // 2), jnp.int32),
            jax.ShapeDtypeStruct((T, LANES), F32),
        ],
        scratch_shapes=[pltpu.VMEM((n_sub, tm // n_sub, D_MODEL), F32)],
        compiler_params=_cparams("parallel"),
        name="out_proj_route",
    )(y_hy, y_rg, o, w_out, x, mod, g2.reshape(1, D_MODEL), wrh, wrl, b_router.reshape(N_EXPERTS, 1))


def _gather_rows(table, idx, rows_per_step=64, n_buf=2):
    info = plsc.get_sparse_core_info()
    n_workers = info.num_cores * info.num_subcores
    n, width = idx.shape[0], table.shape[1]
    per_worker = n // n_workers
    n_steps = per_worker // rows_per_step
    assert per_worker * n_workers == n and n_steps * rows_per_step == per_worker and n_steps >= n_buf
    mesh = plsc.VectorSubcoreMesh(core_axis_name="c", subcore_axis_name="s")

    @functools.partial(
        pl.kernel, mesh=mesh,
        out_type=jax.ShapeDtypeStruct((n, width), table.dtype),
        scratch_types=[
            pltpu.VMEM((per_worker,), jnp.int32),
            pltpu.VMEM((n_buf, rows_per_step, width), table.dtype),
            pltpu.SemaphoreType.DMA((n_buf,)),
            pltpu.SemaphoreType.DMA((n_buf,)),
        ],
    )
    def gather(table_hbm, idx_hbm, out_hbm, idx_v, rows_v, sem_in, sem_out):
        worker = lax.axis_index("s") * info.num_cores + lax.axis_index("c")
        base = pl.multiple_of(worker * per_worker, per_worker)
        pltpu.sync_copy(idx_hbm.at[pl.ds(base, per_worker)], idx_v)

        def read(b, step):
            rows = idx_v.at[pl.ds(step * rows_per_step, rows_per_step)]
            return pltpu.make_async_copy(table_hbm.at[rows], rows_v.at[b], sem_in.at[b])

        def write(b, step):
            off = pl.multiple_of(base + step * rows_per_step, rows_per_step)
            return pltpu.make_async_copy(rows_v.at[b], out_hbm.at[pl.ds(off, rows_per_step)], sem_out.at[b])

        for step in range(n_steps + 1):
            if step < n_steps:
                if step >= n_buf:
                    write(step % n_buf, step - n_buf).wait()
                read(step % n_buf, step).start()
            if step >= 1:
                read((step - 1) % n_buf, step - 1).wait()
                write((step - 1) % n_buf, step - 1).start()
        for step in range(n_steps - n_buf, n_steps):
            write(step % n_buf, step).wait()

    return gather(table, idx)


def _scatter_rows(src, pos, n_slots, rows_per_step=64, n_buf=2):
    info = plsc.get_sparse_core_info()
    n_workers = info.num_cores * info.num_subcores
    n, width = src.shape
    per_worker = n // n_workers
    n_steps = per_worker // rows_per_step
    assert per_worker * n_workers == n and n_steps * rows_per_step == per_worker and n_steps >= n_buf
    mesh = plsc.VectorSubcoreMesh(core_axis_name="c", subcore_axis_name="s")

    @functools.partial(
        pl.kernel, mesh=mesh,
        out_type=jax.ShapeDtypeStruct((n_slots, width), src.dtype),
        scratch_types=[
            pltpu.VMEM((n_steps, rows_per_step), jnp.int32),
            pltpu.VMEM((n_buf, rows_per_step, width), src.dtype),
            pltpu.SemaphoreType.DMA((n_buf,)),
            pltpu.SemaphoreType.DMA((n_buf,)),
        ],
    )
    def scatter(src_hbm, idx_hbm, out_hbm, idx_v, rows_v, sem_in, sem_out):
        worker = lax.axis_index("s") * info.num_cores + lax.axis_index("c")
        base = pl.multiple_of(worker * per_worker, per_worker)
        pltpu.sync_copy(idx_hbm.at[worker], idx_v)

        def read(b, step):
            off = pl.multiple_of(base + step * rows_per_step, rows_per_step)
            return pltpu.make_async_copy(src_hbm.at[pl.ds(off, rows_per_step)], rows_v.at[b], sem_in.at[b])

        def write(b, step):
            return pltpu.make_async_copy(rows_v.at[b], out_hbm.at[idx_v.at[step]], sem_out.at[b])

        for step in range(n_steps + 1):
            if step < n_steps:
                if step >= n_buf:
                    write(step % n_buf, step - n_buf).wait()
                read(step % n_buf, step).start()
            if step >= 1:
                read((step - 1) % n_buf, step - 1).wait()
                write((step - 1) % n_buf, step - 1).start()
        for step in range(n_steps - n_buf, n_steps):
            write(step % n_buf, step).wait()

    return scatter(src, pos.reshape(n_workers, n_steps, rows_per_step))


def _dispatch_plan(route, tm):
    T = route.shape[0]
    n_slots = T + N_CLASSES * tm
    cls = route[:, 0].astype(jnp.int32)
    onehot = (cls[:, None] == jnp.arange(N_CLASSES, dtype=jnp.int32)[None, :]).astype(jnp.int32)
    csum = jnp.cumsum(onehot, axis=0)
    rank = jnp.sum(onehot * csum, axis=1) - 1
    counts = csum[-1]
    padded = ((counts + tm - 1) // tm) * tm
    ends = jnp.cumsum(padded)
    starts = ends - padded
    pos = jnp.sum(onehot * starts[None, :], axis=1) + rank
    tile_start = jnp.arange(n_slots // tm, dtype=jnp.int32) * tm
    tile_cls = jnp.minimum(jnp.sum((tile_start[:, None] >= ends[None, :]).astype(jnp.int32), axis=1), N_CLASSES - 1)
    n_rows = jnp.clip(counts[tile_cls] - (tile_start - starts[tile_cls]), 0, tm).astype(jnp.int32)
    pairs = np.array([(a, b) for a in range(EXP_PER_GROUP) for b in range(a + 1, EXP_PER_GROUP)], np.int32)
    group, pair = tile_cls // PAIRS_PER_GROUP, tile_cls % PAIRS_PER_GROUP
    lo = group * EXP_PER_GROUP + jnp.asarray(pairs[:, 0])[pair]
    hi = group * EXP_PER_GROUP + jnp.asarray(pairs[:, 1])[pair]
    return pos, n_slots, lo, hi, n_rows


def _moe_sorted_kernel(lo_ref, hi_ref, rows_ref, xs_ref, ws_ref, wg_lo, wu_lo, wd_lo, wg_hi, wu_hi, wd_hi, o_ref):
    i = pl.program_id(0)

    @pl.when(rows_ref[i] > 0)
    def _():
        real = lax.broadcasted_iota(jnp.int32, (xs_ref.shape[0], 1), 0) < rows_ref[i]
        x = jnp.where(real, _unpack_pairs(xs_ref[...]), 0.0).astype(BF16)
        y = None
        for wg, wu, wd, col in ((wg_lo, wu_lo, wd_lo, 1), (wg_hi, wu_hi, wd_hi, 2)):
            a = _dot(x, wg[...])
            he = (a * _sigmoid(a)) * _dot(x, wu[...]) * jnp.where(real, ws_ref[:, col:col + 1], 0.0)
            part = _dot(he.astype(BF16), wd[...])
            y = part if y is None else y + part
        o_ref[...] = _pack_pairs(y.astype(BF16).astype(F32))

    @pl.when(rows_ref[i] == 0)
    def _():
        o_ref[...] = jnp.zeros_like(o_ref)


def _moe_sorted(xs, ws, lo, hi, n_rows, wg, wu, wd, tm):
    n_slots = xs.shape[0]
    half = D_MODEL // 2
    up = lambda sel: pl.BlockSpec((None, D_MODEL, D_EXPERT), lambda i, lo, hi, v: ((lo, hi)[sel][i], 0, 0))
    down = lambda sel: pl.BlockSpec((None, D_EXPERT, D_MODEL), lambda i, lo, hi, v: ((lo, hi)[sel][i], 0, 0))
    return pl.pallas_call(
        _moe_sorted_kernel,
        grid_spec=pltpu.PrefetchScalarGridSpec(
            num_scalar_prefetch=3,
            grid=(n_slots // tm,),
            in_specs=[
                pl.BlockSpec((tm, half), lambda i, lo, hi, v: (i, 0)),
                pl.BlockSpec((tm, LANES), lambda i, lo, hi, v: (i, 0)),
                up(0), up(0), down(0), up(1), up(1), down(1),
            ],
            out_specs=pl.BlockSpec((tm, half), lambda i, lo, hi, v: (i, 0)),
        ),
        out_shape=jax.ShapeDtypeStruct((n_slots, half), jnp.int32),
        compiler_params=_cparams("arbitrary"),
        name="moe_sorted",
    )(lo, hi, n_rows, xs, ws, wg, wu, wd, wg, wu, wd)


def _final_residual_kernel(y_ref, x_ref, mod_ref, fg_ref, o_ref):
    x = x_ref[...] + mod_ref[5:6, :] * _unpack_pairs(y_ref[...])
    o_ref[...] = x * lax.rsqrt(jnp.mean(x * x, axis=-1, keepdims=True) + EPS) * fg_ref[...]


def _final_residual(y, x, mod, final_g, B, L, ctx_rows, tm=512):
    T = B * L
    tm = min(tm, L)
    nl = L // tm
    row = (lambda i: CTX_ROW) if ctx_rows else (lambda i: i // nl)
    return pl.pallas_call(
        _final_residual_kernel,
        grid=(T // tm,),
        in_specs=[
            pl.BlockSpec((tm, D_MODEL // 2), lambda i: (i, 0)),
            pl.BlockSpec((tm, D_MODEL), lambda i: (i, 0)),
            pl.BlockSpec((None, 6, D_MODEL), lambda i: (row(i), 0, 0)),
            _const_spec((1, D_MODEL)),
        ],
        out_specs=pl.BlockSpec((tm, D_MODEL), lambda i: (i, 0)),
        out_shape=jax.ShapeDtypeStruct((T, D_MODEL), F32),
        compiler_params=_cparams("parallel"),
        name="final_residual",
    )(y, x, mod, final_g.reshape(1, D_MODEL))


def _moe(h, route, wg, wu, wd, layer, tm=256):
    pos, n_slots, lo, hi, n_rows = _dispatch_plan(route, tm)
    xs = _scatter_rows(h, pos, n_slots)
    ws = _scatter_rows(route, pos, n_slots)
    ys = _moe_sorted(xs, ws, lo + layer * N_EXPERTS, hi + layer * N_EXPERTS, n_rows, wg, wu, wd, tm)
    return _gather_rows(ys, pos)


def kernel(x_prompt, x_sample, cache_k, cache_v, state_rglru, c, c_ctx, w_ada, b_ada, norm1_g, norm2_g, w_in, w_out, hy_short_w, hy_short_b, hy_w1, hy_b1, hy_w2, hy_b2, hy_w3, hy_freq, hy_bias, rg_conv_w, rg_conv_b, rg_wa, rg_ba, rg_wx, rg_bx, rg_lambda, da_lambda, da_subln, w_router, b_router, moe_wg, moe_wu, moe_wd, final_g):
    Bp, Lp, D = x_prompt.shape
    Bs, Ls, _ = x_sample.shape
    assert Bs <= CTX_ROW
    cond = jnp.zeros((N_COND, D), F32).at[:Bs].set(c).at[CTX_ROW].set(c_ctx)
    mods = _ada_table(cond, w_ada, b_ada)

    dft = {L: tuple(jnp.asarray(m).astype(BF16) for m in _dft_mats(L)) for L in (Lp, Ls)}
    streams = [
        dict(B=Bp, L=Lp, ctx=True, x=x_prompt.reshape(Bp * Lp, D)),
        dict(B=Bs, L=Ls, ctx=False, x=x_sample.reshape(Bs * Ls, D)),
    ]
    w_in_b, w_out_b = w_in.astype(BF16), w_out.astype(BF16)
    wg, wu, wd = (w.astype(BF16).reshape((DEPTH * N_EXPERTS,) + w.shape[2:]) for w in (moe_wg, moe_wu, moe_wd))
    new_kv, ss = None, []
    for l in range(DEPTH):
        lam_init = 0.8 - 0.6 * math.exp(-0.3 * l)
        for st in streams:
            B, L, ctx = st["B"], st["L"], st["ctx"]
            cmat, smat = dft[L]
            outs = _norm_proj(st["x"], mods[l], norm1_g[l], w_in_b, l, B, L, ctx, kv_prev=new_kv if ctx else None,
                              pending=(st["y"], mods[l - 1]) if l else None)
            p_hy, p_g, p_x, q, k, v = outs[:6]
            if l:
                st["x"] = outs[6]
            kre, kim = _hy_spectra(L, cmat, smat, hy_w1[l], hy_b1[l], hy_w2[l], hy_b2[l], hy_w3[l], hy_freq[l])
            y_hy = _hyena(p_hy, B, L, cmat, smat, kre, kim, hy_short_w[l], hy_short_b[l], hy_bias[l])
            rg_args = (rg_conv_w[l], rg_conv_b[l], rg_wa[l], rg_ba[l], rg_wx[l], rg_bx[l], rg_lambda[l])
            if ctx:
                y_rg, s_l = _rglru(p_g, p_x, B, L, *rg_args, None)
                o = _attention(q, k, v, l, None, da_lambda[l], da_subln[l], lam_init, B, L)
                new_kv = (k, v)
                ss.append(s_l)
            else:
                y_rg = _rglru(p_g, p_x, B, L, *rg_args, state_rglru[:, l])
                o = _attention(q, k, v, l, (cache_k, cache_v), da_lambda[l], da_subln[l], lam_init, B, L)
            st["x"], h2, route = _out_proj(y_hy, y_rg, o, w_out_b, l, st["x"], mods[l], norm2_g[l],
                                           w_router, b_router, B, L, ctx)
            st["y"] = _moe(h2, route, wg, wu, wd, l)
    y_prompt, y_sample = (
        _final_residual(st["y"], st["x"], mods[DEPTH - 1], final_g, st["B"], st["L"], st["ctx"]).reshape(shape)
        for st, shape in zip(streams, (x_prompt.shape, x_sample.shape)))
    return (y_prompt, y_sample, new_kv[0], new_kv[1], jnp.stack(ss, axis=1))
```

```python
import functools
import math

import numpy as np
import jax
import jax.numpy as jnp
from jax import lax
from jax.experimental import pallas as pl
from jax.experimental.pallas import tpu as pltpu
from jax.experimental.pallas import tpu_sc as plsc

F32 = jnp.float32
BF16 = jnp.bfloat16

D_MODEL = 1024
DEPTH = 2
GRID_W = 64
D_HY = 256
HY_EMB = 33
HY_BANDS = (HY_EMB - 1) // 2
HY_FFN = 64
HY_MIN_DECAY = math.log(1e-2) / 1.5
HY_MAX_DECAY = math.log(1e-2) / 0.3
D_RG = 256
N_RG_HEADS = 4
RG_C = 8.0
N_DA_HEADS = 4
DA_HEAD = 64
DA_VDIM = 2 * DA_HEAD
D_DA = N_DA_HEADS * DA_VDIM
D_MIX = D_HY + D_RG + D_DA
D_IN = 3 * D_HY + 2 * D_RG + 3 * D_DA
ROPE_PAIRS = DA_HEAD // 4
ROPE_THETA = 10000.0
N_EXPERTS = 16
N_GROUPS = 4
EXP_PER_GROUP = N_EXPERTS // N_GROUPS
D_EXPERT = 512
PAIRS_PER_GROUP = EXP_PER_GROUP * (EXP_PER_GROUP - 1) // 2
N_CLASSES = N_GROUPS * PAIRS_PER_GROUP
EPS = 1e-6
N_COND = 16
CTX_ROW = 8
LANES = 128
SUBLANES = 8
VMEM_LIMIT = 56 * 1024 * 1024


def _cparams(*sem):
    return pltpu.CompilerParams(dimension_semantics=sem, vmem_limit_bytes=VMEM_LIMIT)


def _split(x):
    hi = x.astype(BF16)
    lo = (x - hi.astype(F32)).astype(BF16)
    return hi, lo


def _dot(a, b):
    return jnp.dot(a, b, preferred_element_type=F32)


def _dot3(a, b):
    ah, al = _split(a)
    bh, bl = _split(b)
    return _dot(ah, bh) + _dot(al, bh) + _dot(ah, bl)


def _dot_nt(a, b):
    return lax.dot_general(a, b, (((1,), (1,)), ((), ())), preferred_element_type=F32)


def _sigmoid(x):
    return 1.0 / (1.0 + jnp.exp(-x))


def _const_spec(shape):
    n = len(shape)
    return pl.BlockSpec(shape, lambda *_: (0,) * n)


def _ada_kernel(c_ref, w_ref, b_ref, o_ref):
    c = c_ref[...]
    s = c * _sigmoid(c)
    o_ref[...] = _dot3(s, w_ref[...]) + b_ref[...]


def _ada_table(cond, w_ada, b_ada):
    D = D_MODEL
    out = pl.pallas_call(
        _ada_kernel,
        grid=(DEPTH, 6),
        in_specs=[
            pl.BlockSpec((N_COND, D), lambda l, j: (0, 0)),
            pl.BlockSpec((None, D, D), lambda l, j: (l, 0, j)),
            pl.BlockSpec((None, None, 1, D), lambda l, j: (l, j, 0, 0)),
        ],
        out_specs=pl.BlockSpec((None, None, N_COND, D), lambda l, j: (l, j, 0, 0)),
        out_shape=jax.ShapeDtypeStruct((DEPTH, 6, N_COND, D), F32),
        compiler_params=_cparams("parallel", "parallel"),
        name="ada_table",
    )(cond, w_ada, b_ada.reshape(DEPTH, 6, 1, D))
    return out.transpose(0, 2, 1, 3)


def _rope_tables(L):
    t = np.arange(L)
    j = np.arange(LANES)
    jj = j % DA_HEAD
    is_col = (jj // (DA_HEAD // 2)) == 1
    pair = jj % ROPE_PAIRS
    second = (jj % (DA_HEAD // 2)) >= ROPE_PAIRS
    inv = ROPE_THETA ** (-np.arange(ROPE_PAIRS, dtype=np.float64) / ROPE_PAIRS)
    pos = np.where(is_col[None, :], (t % GRID_W)[:, None], (t // GRID_W)[:, None]).astype(np.float64)
    ang = pos * inv[pair][None, :]
    cos = np.cos(ang).astype(np.float32)
    sin = np.sin(ang).astype(np.float32)
    sin_a = np.where(second[None, :], 0.0, -sin).astype(np.float32)
    sin_b = np.where(second[None, :], sin, 0.0).astype(np.float32)
    return cos, sin_a, sin_b


def _rope(x, cos, sin_a, sin_b):
    nxt = pltpu.roll(x, LANES - ROPE_PAIRS, axis=1)
    prv = pltpu.roll(x, ROPE_PAIRS, axis=1)
    return x * cos + nxt * sin_a + prv * sin_b


def _norm_proj_kernel(rope, kv_dtype, pending, kv_layer, x_ref, mod_ref, g_ref, w_ref, *rest):
    x = x_ref[...]
    if pending:
        y_ref, modp_ref, xnew_ref = rest[0], rest[1], rest[-1]
        x = x + modp_ref[5:6, :] * _unpack_pairs(y_ref[...])
        xnew_ref[...] = x
        rest = rest[2:-1]
    if rope:
        cos_ref, sa_ref, sb_ref = rest[:3]
    phy_ref, pg_ref, px_ref, q_ref, k_ref, v_ref = rest[-6:]
    ms = jnp.mean(x * x, axis=-1, keepdims=True)
    y = x * lax.rsqrt(ms + EPS) * g_ref[...]
    h = (y * (1.0 + mod_ref[1:2, :]) + mod_ref[0:1, :]).astype(BF16)
    o = 3 * D_HY
    phy_ref[...] = _dot(h, w_ref[:, 0:o]).astype(BF16)
    pg_ref[...] = _dot(h, w_ref[:, o:o + D_RG])
    px_ref[...] = _dot(h, w_ref[:, o + D_RG:o + 2 * D_RG])
    o += 2 * D_RG
    q = _dot(h, w_ref[:, o:o + D_DA]) * (DA_HEAD ** -0.5 * math.log2(math.e))
    k = _dot(h, w_ref[:, o + D_DA:o + 2 * D_DA])
    v = _dot(h, w_ref[:, o + 2 * D_DA:o + 3 * D_DA])
    if rope:
        cos, sa, sb = cos_ref[...], sa_ref[...], sb_ref[...]
    for hd in range(N_DA_HEADS):
        sl = slice(hd * DA_VDIM, (hd + 1) * DA_VDIM)
        qh, kh = q[:, sl], k[:, sl]
        if rope:
            qh = _rope(qh, cos, sa, sb)
            kh = _rope(kh, cos, sa, sb)
        q_ref[hd] = qh.astype(BF16)
        if kv_layer is None:
            k_ref[hd] = kh.astype(kv_dtype)
            v_ref[hd] = v[:, sl].astype(kv_dtype)
        else:
            for l in range(DEPTH):
                k_ref[l, hd] = kh.astype(kv_dtype) if l == kv_layer else jnp.zeros_like(kh, dtype=kv_dtype)
                v_ref[l, hd] = v[:, sl].astype(kv_dtype) if l == kv_layer else jnp.zeros_like(kh, dtype=kv_dtype)


def _norm_proj(x, mod, g, w_in, layer, B, L, ctx, kv_prev=None, pending=None, tm=512):
    T = B * L
    tm = min(tm, L)
    nl = L // tm
    rope, kv_dtype = not ctx, (F32 if ctx else BF16)
    row = (lambda i: CTX_ROW) if ctx else (lambda i: i // nl)
    mod_spec = pl.BlockSpec((None, 6, D_MODEL), lambda i: (row(i), 0, 0))
    in_specs = [
        pl.BlockSpec((tm, D_MODEL), lambda i: (i, 0)),
        mod_spec,
        _const_spec((1, D_MODEL)),
        pl.BlockSpec((None, D_MODEL, D_IN), lambda i: (layer, 0, 0)),
    ]
    args = [x, mod, g.reshape(1, D_MODEL), w_in]
    if pending is not None:
        in_specs += [pl.BlockSpec((tm, D_MODEL // 2), lambda i: (i, 0)), mod_spec]
        args += list(pending)
    if rope:
        tabs = _rope_tables(L)
        in_specs += [pl.BlockSpec((tm, LANES), lambda i: (i % nl, 0))] * 3
        args += [jnp.asarray(t) for t in tabs]
    head_spec = pl.BlockSpec((None, N_DA_HEADS, tm, DA_VDIM), lambda i: (i // nl, 0, i % nl, 0))
    head_shape = (B, N_DA_HEADS, L, DA_VDIM)
    kv_spec, kv_shape, aliases, kv_layer = head_spec, head_shape, {}, None
    if ctx:
        kv_shape = (B, DEPTH, N_DA_HEADS, L, DA_VDIM)
        if kv_prev is not None:
            kv_spec = pl.BlockSpec((None, None, N_DA_HEADS, tm, DA_VDIM), lambda i: (i // nl, layer, 0, i % nl, 0))
            aliases = {len(args): 4, len(args) + 1: 5}
            in_specs += [pl.BlockSpec(memory_space=pl.ANY)] * 2
            args += list(kv_prev)
        else:
            kv_spec = pl.BlockSpec((None, DEPTH, N_DA_HEADS, tm, DA_VDIM), lambda i: (i // nl, 0, 0, i % nl, 0))
            kv_layer = layer
    out_specs = [
        pl.BlockSpec((tm, 3 * D_HY), lambda i: (i, 0)),
        pl.BlockSpec((tm, D_RG), lambda i: (i, 0)),
        pl.BlockSpec((tm, D_RG), lambda i: (i, 0)),
        head_spec, kv_spec, kv_spec,
    ]
    out_shape = [
        jax.ShapeDtypeStruct((T, 3 * D_HY), BF16),
        jax.ShapeDtypeStruct((T, D_RG), F32),
        jax.ShapeDtypeStruct((T, D_RG), F32),
        jax.ShapeDtypeStruct(head_shape, BF16),
        jax.ShapeDtypeStruct(kv_shape, kv_dtype),
        jax.ShapeDtypeStruct(kv_shape, kv_dtype),
    ]
    if pending is not None:
        out_specs.append(pl.BlockSpec((tm, D_MODEL), lambda i: (i, 0)))
        out_shape.append(jax.ShapeDtypeStruct((T, D_MODEL), F32))
    return pl.pallas_call(
        functools.partial(_norm_proj_kernel, rope, kv_dtype, pending is not None, kv_layer),
        grid=(T // tm,),
        in_specs=in_specs,
        out_specs=out_specs,
        out_shape=out_shape,
        input_output_aliases=aliases,
        compiler_params=_cparams("parallel"),
        name="norm_proj_rope" if rope else "norm_proj",
    )(*args)


def _dft_mats(L):
    n = 2 * L - 1
    fs = (np.arange(L, dtype=np.int64)[:, None] * np.arange(L, dtype=np.int64)[None, :]) % n
    ang = fs.astype(np.float64) * (2.0 * np.pi / n)
    return np.cos(ang).astype(np.float32), np.sin(ang).astype(np.float32)


def _hy_features(L):
    t = np.linspace(0.0, 1.0, L, dtype=np.float64)[:, None]
    ang = ((2.0 * math.pi / L) * np.arange(L, dtype=np.float64))[:, None]
    bands = np.linspace(1e-4, HY_BANDS - 1, HY_BANDS, dtype=np.float64)[None, :]
    ba = bands * ang
    z = np.concatenate([t, np.cos(ba), -np.sin(ba)], axis=-1).astype(np.float32)
    return np.pad(z, ((0, 0), (0, LANES - HY_EMB)))


def _hy_filter_kernel(z_ref, w1_ref, b1_ref, w2_ref, b2_ref, w3_ref, fr_ref, rc_ref, rs_ref):
    tr = z_ref.shape[0]
    z = z_ref[...]
    h = jnp.sin(fr_ref[0:1, :] * (_dot3(z, w1_ref[...]) + b1_ref[...]))
    h = jnp.sin(fr_ref[1:2, :] * (_dot3(h, w2_ref[...]) + b2_ref[...]))
    h = _dot3(h, w3_ref[...])
    t = z[:, 0:1]
    step = (HY_MAX_DECAY - HY_MIN_DECAY) / (D_HY - 1)
    deltas = HY_MIN_DECAY + step * lax.broadcasted_iota(jnp.int32, (1, D_HY), 1).astype(F32)
    window = jnp.exp(-t * jnp.abs(deltas))
    not_first = pl.program_id(0) * tr + lax.broadcasted_iota(jnp.int32, (tr, 1), 0) > 0
    for o in range(2):
        hf = h[:, (2 * o) * D_HY:(2 * o + 1) * D_HY] * window
        hb = jnp.where(not_first, h[:, (2 * o + 1) * D_HY:(2 * o + 2) * D_HY] * window, 0.0)
        rc_ref[:, o * D_HY:(o + 1) * D_HY] = (hf + hb).astype(BF16)
        rs_ref[:, o * D_HY:(o + 1) * D_HY] = (hb - hf).astype(BF16)


def _hy_spectrum_kernel(c_ref, s_ref, rc_ref, rs_ref, w_ref, kre_ref, kim_ref):
    w = w_ref[...]
    kre_ref[...] = _dot(c_ref[...], rc_ref[...]) * w
    kim_ref[...] = _dot(s_ref[...], rs_ref[...]) * w


def _hy_spectra(L, cmat, smat, w1, b1, w2, b2, w3, freq):
    z = jnp.asarray(_hy_features(L))
    w1p = jnp.pad(w1, ((0, LANES - HY_EMB), (0, 0)))
    nw = 2 * D_HY
    tr = min(L, 256)
    rc, rs = pl.pallas_call(
        _hy_filter_kernel,
        grid=(L // tr,),
        in_specs=[
            pl.BlockSpec((tr, LANES), lambda i: (i, 0)),
            _const_spec((LANES, HY_FFN)), _const_spec((1, HY_FFN)),
            _const_spec((HY_FFN, HY_FFN)), _const_spec((1, HY_FFN)),
            _const_spec((HY_FFN, 2 * nw)), _const_spec((2, HY_FFN)),
        ],
        out_specs=[pl.BlockSpec((tr, nw), lambda i: (i, 0))] * 2,
        out_shape=[jax.ShapeDtypeStruct((L, nw), BF16)] * 2,
        compiler_params=_cparams("parallel"),
        name="hy_filter",
    )(z, w1p, b1.reshape(1, HY_FFN), w2, b2.reshape(1, HY_FFN), w3, freq)
    n = 2 * L - 1
    wsc = np.full((L, 1), 2.0 / n, np.float32)
    wsc[0, 0] = 1.0 / n
    return pl.pallas_call(
        _hy_spectrum_kernel,
        grid=(L // tr,),
        in_specs=[
            pl.BlockSpec((tr, L), lambda i: (i, 0)),
            pl.BlockSpec((tr, L), lambda i: (i, 0)),
            _const_spec((L, nw)),
            _const_spec((L, nw)),
            pl.BlockSpec((tr, 1), lambda i: (i, 0)),
        ],
        out_specs=[pl.BlockSpec((tr, nw), lambda i: (i, 0))] * 2,
        out_shape=[jax.ShapeDtypeStruct((L, nw), F32)] * 2,
        compiler_params=_cparams("parallel"),
        name="hy_spectrum",
    )(cmat, smat, rc, rs, jnp.asarray(wsc))


def _hyena_kernel(L, tr, p_ref, sw_ref, sb_ref, bias_ref, c_ref, s_ref, kre_ref, kim_ref, o_ref,
                  pad_ref, u_ref, sig_ref, sig16_ref, zre_ref, zim_ref):
    C3 = 3 * D_HY
    zeros = jnp.zeros((8, C3), F32)
    pad_ref[0:8, :] = zeros
    pad_ref[8 + L:16 + L, :] = zeros
    chunks = [slice(r0, r0 + tr) for r0 in range(0, L, tr)]
    for c in chunks:
        pad_ref[8 + c.start:8 + c.stop, :] = p_ref[c, :].astype(F32)
    for c in chunks:
        u = sb_ref[...]
        for j in range(3):
            u = u + pad_ref[7 + j + c.start:7 + j + c.stop, :] * sw_ref[j:j + 1, :]
        u_ref[c, :] = u[:, D_HY:C3]
        sig_ref[c, :] = u[:, 0:D_HY]
        sig16_ref[c, :] = u[:, 0:D_HY].astype(BF16)

    for o in range(2):
        ko = slice(o * D_HY, (o + 1) * D_HY)
        for c in chunks:
            ure = _dot(c_ref[c, :], sig16_ref[...])
            us = _dot(s_ref[c, :], sig16_ref[...])
            kre, kim = kre_ref[c, ko], kim_ref[c, ko]
            zre_ref[c, :] = (ure * kre + us * kim).astype(BF16)
            zim_ref[c, :] = (ure * kim - us * kre).astype(BF16)
        gate = slice(o * D_HY, (o + 1) * D_HY)
        for c in chunks:
            y = _dot(c_ref[c, :], zre_ref[...]) - _dot(s_ref[c, :], zim_ref[...])
            z = u_ref[c, gate] * (y + sig_ref[c, :] * bias_ref[o:o + 1, :])
            if o == 0:
                sig_ref[c, :] = z
                sig16_ref[c, :] = z.astype(BF16)
            else:
                o_ref[c, :] = z.astype(o_ref.dtype)


def _hyena(p_hy, B, L, cmat, smat, kre, kim, short_w, short_b, bias, tr=512):
    C3 = 3 * D_HY
    tr = min(tr, L)
    once = pl.Buffered(1)
    return pl.pallas_call(
        functools.partial(_hyena_kernel, L, tr),
        grid=(B,),
        in_specs=[
            pl.BlockSpec((L, C3), lambda b: (b, 0)),
            _const_spec((3, C3)),
            _const_spec((1, C3)),
            _const_spec((2, D_HY)),
            pl.BlockSpec((L, L), lambda b: (0, 0), pipeline_mode=once),
            pl.BlockSpec((L, L), lambda b: (0, 0), pipeline_mode=once),
            pl.BlockSpec((L, 2 * D_HY), lambda b: (0, 0), pipeline_mode=once),
            pl.BlockSpec((L, 2 * D_HY), lambda b: (0, 0), pipeline_mode=once),
        ],
        out_specs=pl.BlockSpec((L, D_HY), lambda b: (b, 0)),
        out_shape=jax.ShapeDtypeStruct((B * L, D_HY), BF16),
        scratch_shapes=[
            pltpu.VMEM((L + 16, C3), F32),
            pltpu.VMEM((L, 2 * D_HY), F32),
            pltpu.VMEM((L, D_HY), F32),
            pltpu.VMEM((L, D_HY), BF16),
            pltpu.VMEM((L, D_HY), BF16),
            pltpu.VMEM((L, D_HY), BF16),
        ],
        compiler_params=_cparams("parallel"),
        name="hyena",
    )(p_hy, short_w, short_b.reshape(1, C3), bias, cmat, smat, kre, kim)


def _softplus(z):
    return jnp.maximum(z, 0.0) + jnp.log1p(jnp.exp(-jnp.abs(z)))


def _gelu_tanh(x):
    return 0.5 * x * (1.0 + jnp.tanh(math.sqrt(2.0 / math.pi) * (x + 0.044715 * x * x * x)))


def _rglru_kernel(L, has_state, pg_ref, px_ref, cw_ref, cb_ref, w3_ref, gb_ref, lam_ref, *rest):
    if has_state:
        st_ref, y_ref, pad_ref, a_ref, b_ref, h_ref = rest
    else:
        y_ref, st_out_ref, pad_ref, a_ref, b_ref, h_ref = rest
    C = D_RG
    zeros = jnp.zeros((8, C), F32)
    pad_ref[0:8, :] = zeros
    pad_ref[8 + L:16 + L, :] = zeros
    pad_ref[8:8 + L, :] = px_ref[...]
    half_c = (-0.5 * RG_C) * _softplus(-lam_ref[...])
    tr = min(L, 256)
    for r0 in range(0, L, tr):
        xr = cb_ref[...]
        for j in range(4):
            xr = xr + pad_ref[6 + j + r0:6 + j + r0 + tr, :] * cw_ref[j:j + 1, :]
        xh, xl = _split(xr)
        x3 = jnp.concatenate([xh, xl, xh], axis=1)
        for d in range(2):
            t = []
            for m in range(2):
                cols = slice((2 * d + m) * C, (2 * d + m + 1) * C)
                t.append(jnp.tanh(_dot(x3, w3_ref[:, cols]) + gb_ref[:, cols]))
            log_a = half_c[d:d + 1, :] + half_c[d:d + 1, :] * t[0]
            a = jnp.exp(log_a)
            a_ref[d, r0:r0 + tr, :] = a
            gate_x = 0.5 * xr
            b_ref[d, r0:r0 + tr, :] = jnp.sqrt(jnp.tanh(log_a) * (-1.0 - a * a)) * (gate_x + gate_x * t[1])

    if has_state:
        h0f, h0b = st_ref[0:1, :], st_ref[1:2, :]
    else:
        h0f = h0b = jnp.zeros((1, C), F32)

    row = lax.broadcasted_iota(jnp.int32, (SUBLANES, 1), 0)

    def tile_scan(a, b, reverse):
        for d in (1, 2, 4):
            shift = SUBLANES - d if reverse else d
            valid = (row < SUBLANES - d) if reverse else (row >= d)
            a_s, b_s = pltpu.roll(a, shift, axis=0), pltpu.roll(b, shift, axis=0)
            b = jnp.where(valid, a * b_s + b, b)
            a = jnp.where(valid, a * a_s, a)
        return a, b

    def step(i, carry):
        hf, hb = carry
        t0 = pl.multiple_of(i * SUBLANES, SUBLANES)
        tb0 = pl.multiple_of(L - SUBLANES - i * SUBLANES, SUBLANES)
        af, bf = tile_scan(a_ref[0, pl.ds(t0, SUBLANES), :], b_ref[0, pl.ds(t0, SUBLANES), :], False)
        ab, bb = tile_scan(a_ref[1, pl.ds(tb0, SUBLANES), :], b_ref[1, pl.ds(tb0, SUBLANES), :], True)
        hf_tile = af * hf + bf
        hb_tile = ab * hb + bb
        h_ref[0, pl.ds(t0, SUBLANES), :] = hf_tile
        h_ref[1, pl.ds(tb0, SUBLANES), :] = hb_tile
        return hf_tile[SUBLANES - 1:SUBLANES], hb_tile[0:1]

    lax.fori_loop(0, L // SUBLANES, step, (h0f, h0b), unroll=4)
    y_ref[...] = ((h_ref[0] + h_ref[1]) * _gelu_tanh(pg_ref[...])).astype(y_ref.dtype)
    if not has_state:
        st_out_ref[0:1, :] = h_ref[0, L - 1:L, :]
        st_out_ref[1:2, :] = h_ref[1, 0:1, :]


def _block_diag(w):
    H, d, _ = w.shape
    eye = jnp.eye(H, dtype=w.dtype)
    return (eye[:, None, :, None] * w[:, :, None, :]).reshape(H * d, H * d)


def _rglru(p_g, p_x, B, L, conv_w, conv_b, wa, ba, wx, bx, lam, state):
    C = D_RG
    wcat = 0.5 * jnp.concatenate(
        [_block_diag(wa[0]), _block_diag(wx[0]), _block_diag(wa[1]), _block_diag(wx[1])], axis=1)
    wh = wcat.astype(BF16)
    wl = (wcat - wh.astype(F32)).astype(BF16)
    w3 = jnp.concatenate([wh, wh, wl], axis=0)
    gb = 0.5 * jnp.concatenate([ba[0], bx[0], ba[1], bx[1]]).reshape(1, 4 * C)
    has_state = state is not None
    in_specs = [
        pl.BlockSpec((L, C), lambda b: (b, 0)),
        pl.BlockSpec((L, C), lambda b: (b, 0)),
        _const_spec((4, C)),
        _const_spec((1, C)),
        _const_spec((3 * C, 4 * C)),
        _const_spec((1, 4 * C)),
        _const_spec((2, C)),
    ]
    args = [p_g, p_x, conv_w, conv_b.reshape(1, C), w3, gb, lam]
    y_spec = pl.BlockSpec((L, C), lambda b: (b, 0))
    y_shape = jax.ShapeDtypeStruct((B * L, C), BF16)
    if has_state:
        in_specs.append(pl.BlockSpec((None, 2, C), lambda b: (b, 0, 0)))
        args.append(state)
        out_specs, out_shape = y_spec, y_shape
    else:
        out_specs = [y_spec, pl.BlockSpec((None, 2, C), lambda b: (b, 0, 0))]
        out_shape = [y_shape, jax.ShapeDtypeStruct((B, 2, C), F32)]
    return pl.pallas_call(
        functools.partial(_rglru_kernel, L, has_state),
        grid=(B,),
        in_specs=in_specs,
        out_specs=out_specs,
        out_shape=out_shape,
        scratch_shapes=[
            pltpu.VMEM((L + 16, C), F32),
            pltpu.VMEM((2, L, C), F32),
            pltpu.VMEM((2, L, C), F32),
            pltpu.VMEM((2, L, C), F32),
        ],
        compiler_params=_cparams("parallel"),
        name="rglru_state" if has_state else "rglru",
    )(*args)


def _attn_kernel(L, P, tq, unroll, lam_init, q_ref, k_ref, v_ref, *rest):
    if P:
        ck_ref, cv_ref, dal_ref, sub_ref, o_ref, kk_ref, vv_ref, s_ref = rest
    else:
        dal_ref, sub_ref, o_ref, kk_ref, vv_ref, s_ref = rest
    lv = dal_ref[...]
    s01 = jnp.sum(lv[0:1, :] * lv[1:2, :], axis=-1, keepdims=True)
    s23 = jnp.sum(lv[2:3, :] * lv[3:4, :], axis=-1, keepdims=True)
    lam = jnp.exp(s01) - jnp.exp(s23) + lam_init
    first_half = lax.broadcasted_iota(jnp.int32, (1, DA_VDIM), 1) < DA_HEAD
    sub = sub_ref[...] * (1.0 - lam_init)
    for hd in range(N_DA_HEADS):
        if P:
            kk_ref[0:P, :] = ck_ref[hd].astype(BF16)
            vv_ref[0:P, :] = cv_ref[hd].astype(BF16)
        kk_ref[P:P + L, :] = k_ref[hd].astype(BF16)
        vv_ref[P:P + L, :] = v_ref[hd].astype(BF16)

        def scores(i, buf):
            q = q_ref[hd, pl.ds(pl.multiple_of(i * tq, tq), tq), :]
            zero = jnp.zeros_like(q)
            qs = jnp.concatenate([jnp.where(first_half, q, zero), jnp.where(first_half, zero, q)], axis=0)
            s_ref[buf] = _dot_nt(qs, kk_ref[...])

        def finish(i, buf):
            s = s_ref[buf]
            p = jnp.exp2(s - jnp.max(s, axis=-1, keepdims=True))
            rinv = 1.0 / jnp.sum(p, axis=-1, keepdims=True)
            acc = _dot(p.astype(BF16), vv_ref[...])
            o = acc[0:tq] * rinv[0:tq] - acc[tq:2 * tq] * (lam * rinv[tq:2 * tq])
            o = o * lax.rsqrt(jnp.mean(o * o, axis=-1, keepdims=True) + EPS) * sub
            r0 = pl.multiple_of(i * tq, tq)
            o_ref[pl.ds(r0, tq), hd * DA_VDIM:(hd + 1) * DA_VDIM] = o.astype(o_ref.dtype)

        def pair(j, carry):
            i = 2 * j
            scores(i + 1, 1)
            finish(i, 0)
            scores(i + 2, 0)
            finish(i + 1, 1)
            return carry

        n = L // tq
        scores(0, 0)
        lax.fori_loop(0, n // 2 - 1, pair, 0, unroll=unroll)
        scores(n - 1, 1)
        finish(n - 2, 0)
        finish(n - 1, 1)


def _attention(q, k, v, layer, cache, dal, subln, lam_init, B, L, tq=128, unroll=2):
    H, dv = N_DA_HEADS, DA_VDIM
    hspec = pl.BlockSpec((None, H, L, dv), lambda b: (b, 0, 0, 0))
    kvspec = hspec if k.ndim == 4 else pl.BlockSpec((None, None, H, L, dv), lambda b: (b, layer, 0, 0, 0))
    in_specs = [hspec, kvspec, kvspec]
    args = [q, k, v]
    P = 0
    if cache is not None:
        ck, cv = cache
        P = ck.shape[3]
        cspec = pl.BlockSpec((None, None, H, P, dv), lambda b: (b, layer, 0, 0, 0))
        in_specs += [cspec, cspec]
        args += [ck, cv]
    assert L % tq == 0
    in_specs += [_const_spec((4, DA_HEAD)), _const_spec((1, dv))]
    args += [dal, subln.reshape(1, dv)]
    return pl.pallas_call(
        functools.partial(_attn_kernel, L, P, tq, min(unroll, max(1, L // tq // 2 - 1)), lam_init),
        grid=(B,),
        in_specs=in_specs,
        out_specs=pl.BlockSpec((L, H * dv), lambda b: (b, 0)),
        out_shape=jax.ShapeDtypeStruct((B * L, H * dv), BF16),
        scratch_shapes=[pltpu.VMEM((P + L, dv), BF16), pltpu.VMEM((P + L, dv), BF16),
                        pltpu.VMEM((2, 2 * tq, P + L), F32)],
        compiler_params=_cparams("parallel"),
        name="diff_attn_cache" if P else "diff_attn",
    )(*args)


def _route(logits):
    m = logits[0]
    for e in range(1, N_EXPERTS):
        m = jnp.maximum(m, logits[e])
    ex = [jnp.exp(l - m) for l in logits]
    tot = ex[0]
    for e in range(1, N_EXPERTS):
        tot = tot + ex[e]
    inv = 1.0 / tot
    p = [e_ * inv for e_ in ex]
    G = EXP_PER_GROUP
    best, gsel = None, None
    for g in range(N_GROUPS):
        a = p[g * G:(g + 1) * G]
        sc = None
        for i in range(G):
            for j in range(i + 1, G):
                pair = a[i] + a[j]
                sc = pair if sc is None else jnp.maximum(sc, pair)
        if g == 0:
            best, gsel = sc, jnp.zeros_like(sc, dtype=jnp.int32)
        else:
            upd = sc > best
            best = jnp.where(upd, sc, best)
            gsel = jnp.where(upd, g, gsel)
    vals = []
    for j in range(G):
        vj = p[j]
        for g in range(1, N_GROUPS):
            vj = jnp.where(gsel == g, p[g * G + j], vj)
        vals.append(vj)
    p1, i1 = vals[0], jnp.zeros_like(gsel)
    for j in range(1, G):
        upd = vals[j] > p1
        p1 = jnp.where(upd, vals[j], p1)
        i1 = jnp.where(upd, j, i1)
    p2, i2 = None, None
    for j in range(G):
        cand = jnp.where(i1 == j, -1.0, vals[j])
        if p2 is None:
            p2, i2 = cand, jnp.zeros_like(gsel)
        else:
            upd = cand > p2
            p2 = jnp.where(upd, cand, p2)
            i2 = jnp.where(upd, j, i2)
    den = 1.0 / (p1 + p2)
    w1, w2 = p1 * den, p2 * den
    swap = i2 < i1
    a, b = jnp.where(swap, i2, i1), jnp.where(swap, i1, i2)
    w_lo, w_hi = jnp.where(swap, w2, w1), jnp.where(swap, w1, w2)
    pair = jnp.where(a == 0, b - 1, jnp.where(a == 1, b + 1, 5))
    cls = gsel * PAIRS_PER_GROUP + pair
    return cls.astype(F32), w_lo, w_hi


def _pack_pairs(x):
    n = x.shape[1] // 2
    b = pltpu.bitcast(x, jnp.uint32)
    w = (b[:, :n] >> 16) | (b[:, n:] & jnp.uint32(0xFFFF0000))
    return pltpu.bitcast(w, jnp.int32)


def _unpack_pairs(w):
    u = pltpu.bitcast(w, jnp.uint32)
    lo = pltpu.bitcast(u << 16, F32)
    hi = pltpu.bitcast(u & jnp.uint32(0xFFFF0000), F32)
    return jnp.concatenate([lo, hi], axis=1)


def _out_proj_kernel(yh_ref, yr_ref, o_ref, w_ref, x_ref, mod_ref, g_ref, wrh_ref, wrl_ref, br_ref,
                     xo_ref, h_ref, route_ref, y_ref):
    n_sub = y_ref.shape[0]
    sub = x_ref.shape[0] // n_sub

    def project(j):
        r = slice(j * sub, (j + 1) * sub)
        y_ref[j] = (_dot(yh_ref[r, :], w_ref[0:D_HY, :]) + _dot(yr_ref[r, :], w_ref[D_HY:D_HY + D_RG, :])
                    + _dot(o_ref[r, :], w_ref[D_HY + D_RG:D_MIX, :]))

    def finish(j):
        r = slice(j * sub, (j + 1) * sub)
        x = x_ref[r, :] + mod_ref[2:3, :] * y_ref[j]
        xo_ref[r, :] = x
        ms = jnp.mean(x * x, axis=-1, keepdims=True)
        h = (x * lax.rsqrt(ms + EPS) * g_ref[...]) * (1.0 + mod_ref[4:5, :]) + mod_ref[3:4, :]
        hh, hl = _split(h)
        h_ref[r, :] = _pack_pairs(hh.astype(F32))
        lg = _dot_nt(wrh_ref[...], hh) + _dot_nt(wrh_ref[...], hl) + _dot_nt(wrl_ref[...], hh) + br_ref[...]
        info = _route([lg[e:e + 1, :] for e in range(N_EXPERTS)])
        rt = jnp.concatenate(list(info) + [jnp.zeros((LANES - len(info), sub), F32)], axis=0)
        route_ref[r, :] = rt.T

    project(0)
    for j in range(n_sub):
        if j + 1 < n_sub:
            project(j + 1)
        finish(j)


def _out_proj(y_hy, y_rg, o, w_out, layer, x, mod, g2, w_router, b_router, B, L, ctx_rows, tm=1024, n_sub=2):
    T = B * L
    tm = min(tm, T if ctx_rows else L)
    assert T % tm == 0 and tm % n_sub == 0
    nl = max(L // tm, 1)
    row = (lambda i: CTX_ROW) if ctx_rows else (lambda i: i // nl)
    wrt = w_router.T
    wrh = wrt.astype(BF16)
    wrl = (wrt - wrh.astype(F32)).astype(BF16)
    rows = lambda w: pl.BlockSpec((tm, w), lambda i: (i, 0))
    return pl.pallas_call(
        _out_proj_kernel,
        grid=(T // tm,),
        in_specs=[
            rows(D_HY), rows(D_RG), rows(D_DA),
            pl.BlockSpec((None, D_MIX, D_MODEL), lambda i: (layer, 0, 0)),
            rows(D_MODEL),
            pl.BlockSpec((None, 6, D_MODEL), lambda i: (row(i), 0, 0)),
            _const_spec((1, D_MODEL)),
            _const_spec((N_EXPERTS, D_MODEL)),
            _const_spec((N_EXPERTS, D_MODEL)),
            _const_spec((N_EXPERTS, 1)),
        ],
        out_specs=[rows(D_MODEL), rows(D_MODEL // 2), rows(LANES)],
        out_shape=[
            jax.ShapeDtypeStruct((T, D_MODEL), F32),
            jax.ShapeDtypeStruct((T, D_MODEL // 2), jnp.int32),
            jax.ShapeDtypeStruct((T, LANES), F32),
        ],
        scratch_shapes=[pltpu.VMEM((n_sub, tm // n_sub, D_MODEL), F32)],
        compiler_params=_cparams("parallel"),
        name="out_proj_route",
    )(y_hy, y_rg, o, w_out, x, mod, g2.reshape(1, D_MODEL), wrh, wrl, b_router.reshape(N_EXPERTS, 1))


def _gather_rows(table, idx, rows_per_step=64, n_buf=2):
    info = plsc.get_sparse_core_info()
    n_workers = info.num_cores * info.num_subcores
    n, width = idx.shape[0], table.shape[1]
    per_worker = n // n_workers
    n_steps = per_worker // rows_per_step
    assert per_worker * n_workers == n and n_steps * rows_per_step == per_worker and n_steps >= n_buf
    mesh = plsc.VectorSubcoreMesh(core_axis_name="c", subcore_axis_name="s")

    @functools.partial(
        pl.kernel, mesh=mesh,
        out_type=jax.ShapeDtypeStruct((n, width), table.dtype),
        scratch_types=[
            pltpu.VMEM((per_worker,), jnp.int32),
            pltpu.VMEM((n_buf, rows_per_step, width), table.dtype),
            pltpu.SemaphoreType.DMA((n_buf,)),
            pltpu.SemaphoreType.DMA((n_buf,)),
        ],
    )
    def gather(table_hbm, idx_hbm, out_hbm, idx_v, rows_v, sem_in, sem_out):
        worker = lax.axis_index("s") * info.num_cores + lax.axis_index("c")
        base = pl.multiple_of(worker * per_worker, per_worker)
        pltpu.sync_copy(idx_hbm.at[pl.ds(base, per_worker)], idx_v)

        def read(b, step):
            rows = idx_v.at[pl.ds(step * rows_per_step, rows_per_step)]
            return pltpu.make_async_copy(table_hbm.at[rows], rows_v.at[b], sem_in.at[b])

        def write(b, step):
            off = pl.multiple_of(base + step * rows_per_step, rows_per_step)
            return pltpu.make_async_copy(rows_v.at[b], out_hbm.at[pl.ds(off, rows_per_step)], sem_out.at[b])

        for step in range(n_steps + 1):
            if step < n_steps:
                if step >= n_buf:
                    write(step % n_buf, step - n_buf).wait()
                read(step % n_buf, step).start()
            if step >= 1:
                read((step - 1) % n_buf, step - 1).wait()
                write((step - 1) % n_buf, step - 1).start()
        for step in range(n_steps - n_buf, n_steps):
            write(step % n_buf, step).wait()

    return gather(table, idx)


def _scatter_rows(src, pos, n_slots, rows_per_step=64, n_buf=2):
    info = plsc.get_sparse_core_info()
    n_workers = info.num_cores * info.num_subcores
    n, width = src.shape
    per_worker = n // n_workers
    n_steps = per_worker // rows_per_step
    assert per_worker * n_workers == n and n_steps * rows_per_step == per_worker and n_steps >= n_buf
    mesh = plsc.VectorSubcoreMesh(core_axis_name="c", subcore_axis_name="s")

    @functools.partial(
        pl.kernel, mesh=mesh,
        out_type=jax.ShapeDtypeStruct((n_slots, width), src.dtype),
        scratch_types=[
            pltpu.VMEM((n_steps, rows_per_step), jnp.int32),
            pltpu.VMEM((n_buf, rows_per_step, width), src.dtype),
            pltpu.SemaphoreType.DMA((n_buf,)),
            pltpu.SemaphoreType.DMA((n_buf,)),
        ],
    )
    def scatter(src_hbm, idx_hbm, out_hbm, idx_v, rows_v, sem_in, sem_out):
        worker = lax.axis_index("s") * info.num_cores + lax.axis_index("c")
        base = pl.multiple_of(worker * per_worker, per_worker)
        pltpu.sync_copy(idx_hbm.at[worker], idx_v)

        def read(b, step):
            off = pl.multiple_of(base + step * rows_per_step, rows_per_step)
            return pltpu.make_async_copy(src_hbm.at[pl.ds(off, rows_per_step)], rows_v.at[b], sem_in.at[b])

        def write(b, step):
            return pltpu.make_async_copy(rows_v.at[b], out_hbm.at[idx_v.at[step]], sem_out.at[b])

        for step in range(n_steps + 1):
            if step < n_steps:
                if step >= n_buf:
                    write(step % n_buf, step - n_buf).wait()
                read(step % n_buf, step).start()
            if step >= 1:
                read((step - 1) % n_buf, step - 1).wait()
                write((step - 1) % n_buf, step - 1).start()
        for step in range(n_steps - n_buf, n_steps):
            write(step % n_buf, step).wait()

    return scatter(src, pos.reshape(n_workers, n_steps, rows_per_step))


def _dispatch_plan(route, tm):
    T = route.shape[0]
    n_slots = T + N_CLASSES * tm
    cls = route[:, 0].astype(jnp.int32)
    onehot = (cls[:, None] == jnp.arange(N_CLASSES, dtype=jnp.int32)[None, :]).astype(jnp.int32)
    csum = jnp.cumsum(onehot, axis=0)
    rank = jnp.sum(onehot * csum, axis=1) - 1
    counts = csum[-1]
    padded = ((counts + tm - 1) // tm) * tm
    ends = jnp.cumsum(padded)
    starts = ends - padded
    pos = jnp.sum(onehot * starts[None, :], axis=1) + rank
    tile_start = jnp.arange(n_slots // tm, dtype=jnp.int32) * tm
    tile_cls = jnp.minimum(jnp.sum((tile_start[:, None] >= ends[None, :]).astype(jnp.int32), axis=1), N_CLASSES - 1)
    n_rows = jnp.clip(counts[tile_cls] - (tile_start - starts[tile_cls]), 0, tm).astype(jnp.int32)
    pairs = np.array([(a, b) for a in range(EXP_PER_GROUP) for b in range(a + 1, EXP_PER_GROUP)], np.int32)
    group, pair = tile_cls // PAIRS_PER_GROUP, tile_cls % PAIRS_PER_GROUP
    lo = group * EXP_PER_GROUP + jnp.asarray(pairs[:, 0])[pair]
    hi = group * EXP_PER_GROUP + jnp.asarray(pairs[:, 1])[pair]
    return pos, n_slots, lo, hi, n_rows


def _moe_sorted_kernel(lo_ref, hi_ref, rows_ref, xs_ref, ws_ref, wg_lo, wu_lo, wd_lo, wg_hi, wu_hi, wd_hi, o_ref):
    i = pl.program_id(0)

    @pl.when(rows_ref[i] > 0)
    def _():
        real = lax.broadcasted_iota(jnp.int32, (xs_ref.shape[0], 1), 0) < rows_ref[i]
        x = jnp.where(real, _unpack_pairs(xs_ref[...]), 0.0).astype(BF16)
        y = None
        for wg, wu, wd, col in ((wg_lo, wu_lo, wd_lo, 1), (wg_hi, wu_hi, wd_hi, 2)):
            a = _dot(x, wg[...])
            he = (a * _sigmoid(a)) * _dot(x, wu[...]) * jnp.where(real, ws_ref[:, col:col + 1], 0.0)
            part = _dot(he.astype(BF16), wd[...])
            y = part if y is None else y + part
        o_ref[...] = _pack_pairs(y.astype(BF16).astype(F32))

    @pl.when(rows_ref[i] == 0)
    def _():
        o_ref[...] = jnp.zeros_like(o_ref)


def _moe_sorted(xs, ws, lo, hi, n_rows, wg, wu, wd, tm):
    n_slots = xs.shape[0]
    half = D_MODEL // 2
    up = lambda sel: pl.BlockSpec((None, D_MODEL, D_EXPERT), lambda i, lo, hi, v: ((lo, hi)[sel][i], 0, 0))
    down = lambda sel: pl.BlockSpec((None, D_EXPERT, D_MODEL), lambda i, lo, hi, v: ((lo, hi)[sel][i], 0, 0))
    return pl.pallas_call(
        _moe_sorted_kernel,
        grid_spec=pltpu.PrefetchScalarGridSpec(
            num_scalar_prefetch=3,
            grid=(n_slots // tm,),
            in_specs=[
                pl.BlockSpec((tm, half), lambda i, lo, hi, v: (i, 0)),
                pl.BlockSpec((tm, LANES), lambda i, lo, hi, v: (i, 0)),
                up(0), up(0), down(0), up(1), up(1), down(1),
            ],
            out_specs=pl.BlockSpec((tm, half), lambda i, lo, hi, v: (i, 0)),
        ),
        out_shape=jax.ShapeDtypeStruct((n_slots, half), jnp.int32),
        compiler_params=_cparams("arbitrary"),
        name="moe_sorted",
    )(lo, hi, n_rows, xs, ws, wg, wu, wd, wg, wu, wd)


def _final_residual_kernel(y_ref, x_ref, mod_ref, fg_ref, o_ref):
    x = x_ref[...] + mod_ref[5:6, :] * _unpack_pairs(y_ref[...])
    o_ref[...] = x * lax.rsqrt(jnp.mean(x * x, axis=-1, keepdims=True) + EPS) * fg_ref[...]


def _final_residual(y, x, mod, final_g, B, L, ctx_rows, tm=1024):
    T = B * L
    tm = min(tm, T if ctx_rows else L)
    assert T % tm == 0
    nl = max(L // tm, 1)
    row = (lambda i: CTX_ROW) if ctx_rows else (lambda i: i // nl)
    return pl.pallas_call(
        _final_residual_kernel,
        grid=(T // tm,),
        in_specs=[
            pl.BlockSpec((tm, D_MODEL // 2), lambda i: (i, 0)),
            pl.BlockSpec((tm, D_MODEL), lambda i: (i, 0)),
            pl.BlockSpec((None, 6, D_MODEL), lambda i: (row(i), 0, 0)),
            _const_spec((1, D_MODEL)),
        ],
        out_specs=pl.BlockSpec((tm, D_MODEL), lambda i: (i, 0)),
        out_shape=jax.ShapeDtypeStruct((T, D_MODEL), F32),
        compiler_params=_cparams("parallel"),
        name="final_residual",
    )(y, x, mod, final_g.reshape(1, D_MODEL))


def _moe(h, route, wg, wu, wd, layer, tm=256):
    pos, n_slots, lo, hi, n_rows = _dispatch_plan(route, tm)
    xs = _scatter_rows(h, pos, n_slots)
    ws = _scatter_rows(route, pos, n_slots)
    ys = _moe_sorted(xs, ws, lo + layer * N_EXPERTS, hi + layer * N_EXPERTS, n_rows, wg, wu, wd, tm)
    return _gather_rows(ys, pos)


def kernel(x_prompt, x_sample, cache_k, cache_v, state_rglru, c, c_ctx, w_ada, b_ada, norm1_g, norm2_g, w_in, w_out, hy_short_w, hy_short_b, hy_w1, hy_b1, hy_w2, hy_b2, hy_w3, hy_freq, hy_bias, rg_conv_w, rg_conv_b, rg_wa, rg_ba, rg_wx, rg_bx, rg_lambda, da_lambda, da_subln, w_router, b_router, moe_wg, moe_wu, moe_wd, final_g):
    Bp, Lp, D = x_prompt.shape
    Bs, Ls, _ = x_sample.shape
    assert Bs <= CTX_ROW
    cond = jnp.zeros((N_COND, D), F32).at[:Bs].set(c).at[CTX_ROW].set(c_ctx)
    mods = _ada_table(cond, w_ada, b_ada)

    dft = {L: tuple(jnp.asarray(m).astype(BF16) for m in _dft_mats(L)) for L in (Lp, Ls)}
    streams = [
        dict(B=Bp, L=Lp, ctx=True, x=x_prompt.reshape(Bp * Lp, D)),
        dict(B=Bs, L=Ls, ctx=False, x=x_sample.reshape(Bs * Ls, D)),
    ]
    w_in_b, w_out_b = w_in.astype(BF16), w_out.astype(BF16)
    wg, wu, wd = (w.astype(BF16).reshape((DEPTH * N_EXPERTS,) + w.shape[2:]) for w in (moe_wg, moe_wu, moe_wd))
    new_kv, ss = None, []
    for l in range(DEPTH):
        lam_init = 0.8 - 0.6 * math.exp(-0.3 * l)
        for st in streams:
            B, L, ctx = st["B"], st["L"], st["ctx"]
            cmat, smat = dft[L]
            outs = _norm_proj(st["x"], mods[l], norm1_g[l], w_in_b, l, B, L, ctx, kv_prev=new_kv if ctx else None,
                              pending=(st["y"], mods[l - 1]) if l else None)
            p_hy, p_g, p_x, q, k, v = outs[:6]
            if l:
                st["x"] = outs[6]
            kre, kim = _hy_spectra(L, cmat, smat, hy_w1[l], hy_b1[l], hy_w2[l], hy_b2[l], hy_w3[l], hy_freq[l])
            y_hy = _hyena(p_hy, B, L, cmat, smat, kre, kim, hy_short_w[l], hy_short_b[l], hy_bias[l])
            rg_args = (rg_conv_w[l], rg_conv_b[l], rg_wa[l], rg_ba[l], rg_wx[l], rg_bx[l], rg_lambda[l])
            if ctx:
                y_rg, s_l = _rglru(p_g, p_x, B, L, *rg_args, None)
                o = _attention(q, k, v, l, None, da_lambda[l], da_subln[l], lam_init, B, L)
                new_kv = (k, v)
                ss.append(s_l)
            else:
                y_rg = _rglru(p_g, p_x, B, L, *rg_args, state_rglru[:, l])
                o = _attention(q, k, v, l, (cache_k, cache_v), da_lambda[l], da_subln[l], lam_init, B, L)
            st["x"], h2, route = _out_proj(y_hy, y_rg, o, w_out_b, l, st["x"], mods[l], norm2_g[l],
                                           w_router, b_router, B, L, ctx)
            st["y"] = _moe(h2, route, wg, wu, wd, l)
    y_prompt, y_sample = (
        _final_residual(st["y"], st["x"], mods[DEPTH - 1], final_g, st["B"], st["L"], st["ctx"]).reshape(shape)
        for st, shape in zip(streams, (x_prompt.shape, x_sample.shape)))
    return (y_prompt, y_sample, new_kv[0], new_kv[1], jnp.stack(ss, axis=1))
```

```python
import functools
import math

import numpy as np
import jax
import jax.numpy as jnp
from jax import lax
from jax.experimental import pallas as pl
from jax.experimental.pallas import tpu as pltpu
from jax.experimental.pallas import tpu_sc as plsc

F32 = jnp.float32
BF16 = jnp.bfloat16

D_MODEL = 1024
DEPTH = 2
GRID_W = 64
D_HY = 256
HY_EMB = 33
HY_BANDS = (HY_EMB - 1) // 2
HY_FFN = 64
HY_MIN_DECAY = math.log(1e-2) / 1.5
HY_MAX_DECAY = math.log(1e-2) / 0.3
D_RG = 256
N_RG_HEADS = 4
RG_C = 8.0
N_DA_HEADS = 4
DA_HEAD = 64
DA_VDIM = 2 * DA_HEAD
D_DA = N_DA_HEADS * DA_VDIM
D_MIX = D_HY + D_RG + D_DA
D_IN = 3 * D_HY + 2 * D_RG + 3 * D_DA
ROPE_PAIRS = DA_HEAD // 4
ROPE_THETA = 10000.0
N_EXPERTS = 16
N_GROUPS = 4
EXP_PER_GROUP = N_EXPERTS // N_GROUPS
D_EXPERT = 512
PAIRS_PER_GROUP = EXP_PER_GROUP * (EXP_PER_GROUP - 1) // 2
N_CLASSES = N_GROUPS * PAIRS_PER_GROUP
EPS = 1e-6
N_COND = 16
CTX_ROW = 8
LANES = 128
SUBLANES = 8
VMEM_LIMIT = 56 * 1024 * 1024


def _cparams(*sem):
    return pltpu.CompilerParams(dimension_semantics=sem, vmem_limit_bytes=VMEM_LIMIT)


def _split(x):
    hi = x.astype(BF16)
    lo = (x - hi.astype(F32)).astype(BF16)
    return hi, lo


def _dot(a, b):
    return jnp.dot(a, b, preferred_element_type=F32)


def _dot3(a, b):
    ah, al = _split(a)
    bh, bl = _split(b)
    return _dot(ah, bh) + _dot(al, bh) + _dot(ah, bl)


def _dot_nt(a, b):
    return lax.dot_general(a, b, (((1,), (1,)), ((), ())), preferred_element_type=F32)


def _sigmoid(x):
    return 1.0 / (1.0 + jnp.exp(-x))


def _const_spec(shape):
    n = len(shape)
    return pl.BlockSpec(shape, lambda *_: (0,) * n)


def _ada_kernel(c_ref, w_ref, b_ref, o_ref):
    c = c_ref[...]
    s = c * _sigmoid(c)
    o_ref[...] = _dot3(s, w_ref[...]) + b_ref[...]


def _ada_table(cond, w_ada, b_ada):
    D = D_MODEL
    out = pl.pallas_call(
        _ada_kernel,
        grid=(DEPTH, 6),
        in_specs=[
            pl.BlockSpec((N_COND, D), lambda l, j: (0, 0)),
            pl.BlockSpec((None, D, D), lambda l, j: (l, 0, j)),
            pl.BlockSpec((None, None, 1, D), lambda l, j: (l, j, 0, 0)),
        ],
        out_specs=pl.BlockSpec((None, None, N_COND, D), lambda l, j: (l, j, 0, 0)),
        out_shape=jax.ShapeDtypeStruct((DEPTH, 6, N_COND, D), F32),
        compiler_params=_cparams("parallel", "parallel"),
        name="ada_table",
    )(cond, w_ada, b_ada.reshape(DEPTH, 6, 1, D))
    return out.transpose(0, 2, 1, 3)


def _rope_tables(L):
    t = np.arange(L)
    j = np.arange(LANES)
    jj = j % DA_HEAD
    is_col = (jj // (DA_HEAD // 2)) == 1
    pair = jj % ROPE_PAIRS
    second = (jj % (DA_HEAD // 2)) >= ROPE_PAIRS
    inv = ROPE_THETA ** (-np.arange(ROPE_PAIRS, dtype=np.float64) / ROPE_PAIRS)
    pos = np.where(is_col[None, :], (t % GRID_W)[:, None], (t // GRID_W)[:, None]).astype(np.float64)
    ang = pos * inv[pair][None, :]
    cos = np.cos(ang).astype(np.float32)
    sin = np.sin(ang).astype(np.float32)
    sin_a = np.where(second[None, :], 0.0, -sin).astype(np.float32)
    sin_b = np.where(second[None, :], sin, 0.0).astype(np.float32)
    return cos, sin_a, sin_b


def _rope(x, cos, sin_a, sin_b):
    nxt = pltpu.roll(x, LANES - ROPE_PAIRS, axis=1)
    prv = pltpu.roll(x, ROPE_PAIRS, axis=1)
    return x * cos + nxt * sin_a + prv * sin_b


def _norm_proj_kernel(rope, kv_dtype, pending, kv_layer, x_ref, mod_ref, g_ref, w_ref, *rest):
    x = x_ref[...]
    if pending:
        y_ref, modp_ref, xnew_ref = rest[0], rest[1], rest[-1]
        x = x + modp_ref[5:6, :] * _unpack_pairs(y_ref[...])
        xnew_ref[...] = x
        rest = rest[2:-1]
    if rope:
        cos_ref, sa_ref, sb_ref = rest[:3]
    phy_ref, pg_ref, px_ref, q_ref, k_ref, v_ref = rest[-6:]
    ms = jnp.mean(x * x, axis=-1, keepdims=True)
    y = x * lax.rsqrt(ms + EPS) * g_ref[...]
    h = (y * (1.0 + mod_ref[1:2, :]) + mod_ref[0:1, :]).astype(BF16)
    o = 3 * D_HY
    phy_ref[...] = _dot(h, w_ref[:, 0:o]).astype(BF16)
    pg_ref[...] = _dot(h, w_ref[:, o:o + D_RG])
    px_ref[...] = _dot(h, w_ref[:, o + D_RG:o + 2 * D_RG])
    o += 2 * D_RG
    q = _dot(h, w_ref[:, o:o + D_DA]) * (DA_HEAD ** -0.5 * math.log2(math.e))
    k = _dot(h, w_ref[:, o + D_DA:o + 2 * D_DA])
    v = _dot(h, w_ref[:, o + 2 * D_DA:o + 3 * D_DA])
    if rope:
        cos, sa, sb = cos_ref[...], sa_ref[...], sb_ref[...]
    for hd in range(N_DA_HEADS):
        sl = slice(hd * DA_VDIM, (hd + 1) * DA_VDIM)
        qh, kh = q[:, sl], k[:, sl]
        if rope:
            qh = _rope(qh, cos, sa, sb)
            kh = _rope(kh, cos, sa, sb)
        q_ref[hd] = qh.astype(BF16)
        if kv_layer is None:
            k_ref[hd] = kh.astype(kv_dtype)
            v_ref[hd] = v[:, sl].astype(kv_dtype)
        else:
            for l in range(DEPTH):
                k_ref[l, hd] = kh.astype(kv_dtype) if l == kv_layer else jnp.zeros_like(kh, dtype=kv_dtype)
                v_ref[l, hd] = v[:, sl].astype(kv_dtype) if l == kv_layer else jnp.zeros_like(kh, dtype=kv_dtype)


def _norm_proj(x, mod, g, w_in, layer, B, L, ctx, kv_prev=None, pending=None, tm=512):
    T = B * L
    tm = min(tm, L)
    nl = L // tm
    rope, kv_dtype = not ctx, (F32 if ctx else BF16)
    row = (lambda i: CTX_ROW) if ctx else (lambda i: i // nl)
    mod_spec = pl.BlockSpec((None, 6, D_MODEL), lambda i: (row(i), 0, 0))
    in_specs = [
        pl.BlockSpec((tm, D_MODEL), lambda i: (i, 0)),
        mod_spec,
        _const_spec((1, D_MODEL)),
        pl.BlockSpec((None, D_MODEL, D_IN), lambda i: (layer, 0, 0)),
    ]
    args = [x, mod, g.reshape(1, D_MODEL), w_in]
    if pending is not None:
        in_specs += [pl.BlockSpec((tm, D_MODEL // 2), lambda i: (i, 0)), mod_spec]
        args += list(pending)
    if rope:
        tabs = _rope_tables(L)
        in_specs += [pl.BlockSpec((tm, LANES), lambda i: (i % nl, 0))] * 3
        args += [jnp.asarray(t) for t in tabs]
    head_spec = pl.BlockSpec((None, N_DA_HEADS, tm, DA_VDIM), lambda i: (i // nl, 0, i % nl, 0))
    head_shape = (B, N_DA_HEADS, L, DA_VDIM)
    kv_spec, kv_shape, aliases, kv_layer = head_spec, head_shape, {}, None
    if ctx:
        kv_shape = (B, DEPTH, N_DA_HEADS, L, DA_VDIM)
        if kv_prev is not None:
            kv_spec = pl.BlockSpec((None, None, N_DA_HEADS, tm, DA_VDIM), lambda i: (i // nl, layer, 0, i % nl, 0))
            aliases = {len(args): 4, len(args) + 1: 5}
            in_specs += [pl.BlockSpec(memory_space=pl.ANY)] * 2
            args += list(kv_prev)
        else:
            kv_spec = pl.BlockSpec((None, DEPTH, N_DA_HEADS, tm, DA_VDIM), lambda i: (i // nl, 0, 0, i % nl, 0))
            kv_layer = layer
    out_specs = [
        pl.BlockSpec((tm, 3 * D_HY), lambda i: (i, 0)),
        pl.BlockSpec((tm, D_RG), lambda i: (i, 0)),
        pl.BlockSpec((tm, D_RG), lambda i: (i, 0)),
        head_spec, kv_spec, kv_spec,
    ]
    out_shape = [
        jax.ShapeDtypeStruct((T, 3 * D_HY), BF16),
        jax.ShapeDtypeStruct((T, D_RG), F32),
        jax.ShapeDtypeStruct((T, D_RG), F32),
        jax.ShapeDtypeStruct(head_shape, BF16),
        jax.ShapeDtypeStruct(kv_shape, kv_dtype),
        jax.ShapeDtypeStruct(kv_shape, kv_dtype),
    ]
    if pending is not None:
        out_specs.append(pl.BlockSpec((tm, D_MODEL), lambda i: (i, 0)))
        out_shape.append(jax.ShapeDtypeStruct((T, D_MODEL), F32))
    return pl.pallas_call(
        functools.partial(_norm_proj_kernel, rope, kv_dtype, pending is not None, kv_layer),
        grid=(T // tm,),
        in_specs=in_specs,
        out_specs=out_specs,
        out_shape=out_shape,
        input_output_aliases=aliases,
        compiler_params=_cparams("parallel"),
        name="norm_proj_rope" if rope else "norm_proj",
    )(*args)


def _dft_mats(L):
    n = 2 * L - 1
    fs = (np.arange(L, dtype=np.int64)[:, None] * np.arange(L, dtype=np.int64)[None, :]) % n
    ang = fs.astype(np.float64) * (2.0 * np.pi / n)
    return np.cos(ang).astype(np.float32), np.sin(ang).astype(np.float32)


def _hy_features(L):
    t = np.linspace(0.0, 1.0, L, dtype=np.float64)[:, None]
    ang = ((2.0 * math.pi / L) * np.arange(L, dtype=np.float64))[:, None]
    bands = np.linspace(1e-4, HY_BANDS - 1, HY_BANDS, dtype=np.float64)[None, :]
    ba = bands * ang
    z = np.concatenate([t, np.cos(ba), -np.sin(ba)], axis=-1).astype(np.float32)
    return np.pad(z, ((0, 0), (0, LANES - HY_EMB)))


def _hy_filter_kernel(z_ref, w1_ref, b1_ref, w2_ref, b2_ref, w3_ref, fr_ref, rc_ref, rs_ref):
    tr = z_ref.shape[0]
    z = z_ref[...]
    h = jnp.sin(fr_ref[0:1, :] * (_dot3(z, w1_ref[...]) + b1_ref[...]))
    h = jnp.sin(fr_ref[1:2, :] * (_dot3(h, w2_ref[...]) + b2_ref[...]))
    h = _dot3(h, w3_ref[...])
    t = z[:, 0:1]
    step = (HY_MAX_DECAY - HY_MIN_DECAY) / (D_HY - 1)
    deltas = HY_MIN_DECAY + step * lax.broadcasted_iota(jnp.int32, (1, D_HY), 1).astype(F32)
    window = jnp.exp(-t * jnp.abs(deltas))
    not_first = pl.program_id(0) * tr + lax.broadcasted_iota(jnp.int32, (tr, 1), 0) > 0
    for o in range(2):
        hf = h[:, (2 * o) * D_HY:(2 * o + 1) * D_HY] * window
        hb = jnp.where(not_first, h[:, (2 * o + 1) * D_HY:(2 * o + 2) * D_HY] * window, 0.0)
        rc_ref[:, o * D_HY:(o + 1) * D_HY] = (hf + hb).astype(BF16)
        rs_ref[:, o * D_HY:(o + 1) * D_HY] = (hb - hf).astype(BF16)


def _hy_spectrum_kernel(c_ref, s_ref, rc_ref, rs_ref, w_ref, kre_ref, kim_ref):
    w = w_ref[...]
    kre_ref[...] = _dot(c_ref[...], rc_ref[...]) * w
    kim_ref[...] = _dot(s_ref[...], rs_ref[...]) * w


def _hy_spectra(L, cmat, smat, w1, b1, w2, b2, w3, freq):
    z = jnp.asarray(_hy_features(L))
    w1p = jnp.pad(w1, ((0, LANES - HY_EMB), (0, 0)))
    nw = 2 * D_HY
    tr = min(L, 256)
    rc, rs = pl.pallas_call(
        _hy_filter_kernel,
        grid=(L // tr,),
        in_specs=[
            pl.BlockSpec((tr, LANES), lambda i: (i, 0)),
            _const_spec((LANES, HY_FFN)), _const_spec((1, HY_FFN)),
            _const_spec((HY_FFN, HY_FFN)), _const_spec((1, HY_FFN)),
            _const_spec((HY_FFN, 2 * nw)), _const_spec((2, HY_FFN)),
        ],
        out_specs=[pl.BlockSpec((tr, nw), lambda i: (i, 0))] * 2,
        out_shape=[jax.ShapeDtypeStruct((L, nw), BF16)] * 2,
        compiler_params=_cparams("parallel"),
        name="hy_filter",
    )(z, w1p, b1.reshape(1, HY_FFN), w2, b2.reshape(1, HY_FFN), w3, freq)
    n = 2 * L - 1
    wsc = np.full((L, 1), 2.0 / n, np.float32)
    wsc[0, 0] = 1.0 / n
    return pl.pallas_call(
        _hy_spectrum_kernel,
        grid=(L // tr,),
        in_specs=[
            pl.BlockSpec((tr, L), lambda i: (i, 0)),
            pl.BlockSpec((tr, L), lambda i: (i, 0)),
            _const_spec((L, nw)),
            _const_spec((L, nw)),
            pl.BlockSpec((tr, 1), lambda i: (i, 0)),
        ],
        out_specs=[pl.BlockSpec((tr, nw), lambda i: (i, 0))] * 2,
        out_shape=[jax.ShapeDtypeStruct((L, nw), F32)] * 2,
        compiler_params=_cparams("parallel"),
        name="hy_spectrum",
    )(cmat, smat, rc, rs, jnp.asarray(wsc))


def _hyena_kernel(L, tr, p_ref, sw_ref, sb_ref, bias_ref, c_ref, s_ref, kre_ref, kim_ref, o_ref,
                  pad_ref, u_ref, sig_ref, sig16_ref, zre_ref, zim_ref):
    C3 = 3 * D_HY
    zeros = jnp.zeros((8, C3), F32)
    pad_ref[0:8, :] = zeros
    pad_ref[8 + L:16 + L, :] = zeros
    chunks = [slice(r0, r0 + tr) for r0 in range(0, L, tr)]
    for c in chunks:
        pad_ref[8 + c.start:8 + c.stop, :] = p_ref[c, :].astype(F32)
    for c in chunks:
        u = sb_ref[...]
        for j in range(3):
            u = u + pad_ref[7 + j + c.start:7 + j + c.stop, :] * sw_ref[j:j + 1, :]
        u_ref[c, :] = u[:, D_HY:C3]
        sig_ref[c, :] = u[:, 0:D_HY]
        sig16_ref[c, :] = u[:, 0:D_HY].astype(BF16)

    for o in range(2):
        ko = slice(o * D_HY, (o + 1) * D_HY)
        for c in chunks:
            ure = _dot(c_ref[c, :], sig16_ref[...])
            us = _dot(s_ref[c, :], sig16_ref[...])
            kre, kim = kre_ref[c, ko], kim_ref[c, ko]
            zre_ref[c, :] = (ure * kre + us * kim).astype(BF16)
            zim_ref[c, :] = (ure * kim - us * kre).astype(BF16)
        gate = slice(o * D_HY, (o + 1) * D_HY)
        for c in chunks:
            y = _dot(c_ref[c, :], zre_ref[...]) - _dot(s_ref[c, :], zim_ref[...])
            z = u_ref[c, gate] * (y + sig_ref[c, :] * bias_ref[o:o + 1, :])
            if o == 0:
                sig_ref[c, :] = z
                sig16_ref[c, :] = z.astype(BF16)
            else:
                o_ref[c, :] = z.astype(o_ref.dtype)


def _hyena(p_hy, B, L, cmat, smat, kre, kim, short_w, short_b, bias, tr=512):
    C3 = 3 * D_HY
    tr = min(tr, L)
    once = pl.Buffered(1)
    return pl.pallas_call(
        functools.partial(_hyena_kernel, L, tr),
        grid=(B,),
        in_specs=[
            pl.BlockSpec((L, C3), lambda b: (b, 0)),
            _const_spec((3, C3)),
            _const_spec((1, C3)),
            _const_spec((2, D_HY)),
            pl.BlockSpec((L, L), lambda b: (0, 0), pipeline_mode=once),
            pl.BlockSpec((L, L), lambda b: (0, 0), pipeline_mode=once),
            pl.BlockSpec((L, 2 * D_HY), lambda b: (0, 0), pipeline_mode=once),
            pl.BlockSpec((L, 2 * D_HY), lambda b: (0, 0), pipeline_mode=once),
        ],
        out_specs=pl.BlockSpec((L, D_HY), lambda b: (b, 0)),
        out_shape=jax.ShapeDtypeStruct((B * L, D_HY), BF16),
        scratch_shapes=[
            pltpu.VMEM((L + 16, C3), F32),
            pltpu.VMEM((L, 2 * D_HY), F32),
            pltpu.VMEM((L, D_HY), F32),
            pltpu.VMEM((L, D_HY), BF16),
            pltpu.VMEM((L, D_HY), BF16),
            pltpu.VMEM((L, D_HY), BF16),
        ],
        compiler_params=_cparams("parallel"),
        name="hyena",
    )(p_hy, short_w, short_b.reshape(1, C3), bias, cmat, smat, kre, kim)


def _softplus(z):
    return jnp.maximum(z, 0.0) + jnp.log1p(jnp.exp(-jnp.abs(z)))


def _gelu_tanh(x):
    return 0.5 * x * (1.0 + jnp.tanh(math.sqrt(2.0 / math.pi) * (x + 0.044715 * x * x * x)))


def _rglru_kernel(L, has_state, pg_ref, px_ref, cw_ref, cb_ref, w3_ref, gb_ref, lam_ref, *rest):
    if has_state:
        st_ref, y_ref, pad_ref, a_ref, b_ref, h_ref = rest
    else:
        y_ref, st_out_ref, pad_ref, a_ref, b_ref, h_ref = rest
    C = D_RG
    zeros = jnp.zeros((8, C), F32)
    pad_ref[0:8, :] = zeros
    pad_ref[8 + L:16 + L, :] = zeros
    pad_ref[8:8 + L, :] = px_ref[...]
    half_c = (-0.5 * RG_C) * _softplus(-lam_ref[...])
    tr = min(L, 256)
    for r0 in range(0, L, tr):
        xr = cb_ref[...]
        for j in range(4):
            xr = xr + pad_ref[6 + j + r0:6 + j + r0 + tr, :] * cw_ref[j:j + 1, :]
        xh, xl = _split(xr)
        x3 = jnp.concatenate([xh, xl, xh], axis=1)
        for d in range(2):
            t = []
            for m in range(2):
                cols = slice((2 * d + m) * C, (2 * d + m + 1) * C)
                t.append(jnp.tanh(_dot(x3, w3_ref[:, cols]) + gb_ref[:, cols]))
            log_a = half_c[d:d + 1, :] + half_c[d:d + 1, :] * t[0]
            a = jnp.exp(log_a)
            a_ref[d, r0:r0 + tr, :] = a
            gate_x = 0.5 * xr
            b_ref[d, r0:r0 + tr, :] = jnp.sqrt(jnp.tanh(log_a) * (-1.0 - a * a)) * (gate_x + gate_x * t[1])

    if has_state:
        h0f, h0b = st_ref[0:1, :], st_ref[1:2, :]
    else:
        h0f = h0b = jnp.zeros((1, C), F32)

    row = lax.broadcasted_iota(jnp.int32, (SUBLANES, 1), 0)

    def tile_scan(a, b, reverse):
        for d in (1, 2, 4):
            shift = SUBLANES - d if reverse else d
            valid = (row < SUBLANES - d) if reverse else (row >= d)
            a_s, b_s = pltpu.roll(a, shift, axis=0), pltpu.roll(b, shift, axis=0)
            b = jnp.where(valid, a * b_s + b, b)
            a = jnp.where(valid, a * a_s, a)
        return a, b

    def step(i, carry):
        hf, hb = carry
        t0 = pl.multiple_of(i * SUBLANES, SUBLANES)
        tb0 = pl.multiple_of(L - SUBLANES - i * SUBLANES, SUBLANES)
        af, bf = tile_scan(a_ref[0, pl.ds(t0, SUBLANES), :], b_ref[0, pl.ds(t0, SUBLANES), :], False)
        ab, bb = tile_scan(a_ref[1, pl.ds(tb0, SUBLANES), :], b_ref[1, pl.ds(tb0, SUBLANES), :], True)
        hf_tile = af * hf + bf
        hb_tile = ab * hb + bb
        h_ref[0, pl.ds(t0, SUBLANES), :] = hf_tile
        h_ref[1, pl.ds(tb0, SUBLANES), :] = hb_tile
        return hf_tile[SUBLANES - 1:SUBLANES], hb_tile[0:1]

    lax.fori_loop(0, L // SUBLANES, step, (h0f, h0b), unroll=4)
    y_ref[...] = ((h_ref[0] + h_ref[1]) * _gelu_tanh(pg_ref[...])).astype(y_ref.dtype)
    if not has_state:
        st_out_ref[0:1, :] = h_ref[0, L - 1:L, :]
        st_out_ref[1:2, :] = h_ref[1, 0:1, :]


def _block_diag(w):
    H, d, _ = w.shape
    eye = jnp.eye(H, dtype=w.dtype)
    return (eye[:, None, :, None] * w[:, :, None, :]).reshape(H * d, H * d)


def _rglru(p_g, p_x, B, L, conv_w, conv_b, wa, ba, wx, bx, lam, state):
    C = D_RG
    wcat = 0.5 * jnp.concatenate(
        [_block_diag(wa[0]), _block_diag(wx[0]), _block_diag(wa[1]), _block_diag(wx[1])], axis=1)
    wh = wcat.astype(BF16)
    wl = (wcat - wh.astype(F32)).astype(BF16)
    w3 = jnp.concatenate([wh, wh, wl], axis=0)
    gb = 0.5 * jnp.concatenate([ba[0], bx[0], ba[1], bx[1]]).reshape(1, 4 * C)
    has_state = state is not None
    in_specs = [
        pl.BlockSpec((L, C), lambda b: (b, 0)),
        pl.BlockSpec((L, C), lambda b: (b, 0)),
        _const_spec((4, C)),
        _const_spec((1, C)),
        _const_spec((3 * C, 4 * C)),
        _const_spec((1, 4 * C)),
        _const_spec((2, C)),
    ]
    args = [p_g, p_x, conv_w, conv_b.reshape(1, C), w3, gb, lam]
    y_spec = pl.BlockSpec((L, C), lambda b: (b, 0))
    y_shape = jax.ShapeDtypeStruct((B * L, C), BF16)
    if has_state:
        in_specs.append(pl.BlockSpec((None, 2, C), lambda b: (b, 0, 0)))
        args.append(state)
        out_specs, out_shape = y_spec, y_shape
    else:
        out_specs = [y_spec, pl.BlockSpec((None, 2, C), lambda b: (b, 0, 0))]
        out_shape = [y_shape, jax.ShapeDtypeStruct((B, 2, C), F32)]
    return pl.pallas_call(
        functools.partial(_rglru_kernel, L, has_state),
        grid=(B,),
        in_specs=in_specs,
        out_specs=out_specs,
        out_shape=out_shape,
        scratch_shapes=[
            pltpu.VMEM((L + 16, C), F32),
            pltpu.VMEM((2, L, C), F32),
            pltpu.VMEM((2, L, C), F32),
            pltpu.VMEM((2, L, C), F32),
        ],
        compiler_params=_cparams("parallel"),
        name="rglru_state" if has_state else "rglru",
    )(*args)


def _attn_kernel(L, P, tq, unroll, lam_init, q_ref, k_ref, v_ref, *rest):
    if P:
        ck_ref, cv_ref, dal_ref, sub_ref, o_ref, kk_ref, vv_ref, s_ref = rest
    else:
        dal_ref, sub_ref, o_ref, kk_ref, vv_ref, s_ref = rest
    lv = dal_ref[...]
    s01 = jnp.sum(lv[0:1, :] * lv[1:2, :], axis=-1, keepdims=True)
    s23 = jnp.sum(lv[2:3, :] * lv[3:4, :], axis=-1, keepdims=True)
    lam = jnp.exp(s01) - jnp.exp(s23) + lam_init
    first_half = lax.broadcasted_iota(jnp.int32, (1, DA_VDIM), 1) < DA_HEAD
    sub = sub_ref[...] * (1.0 - lam_init)
    for hd in range(N_DA_HEADS):
        if P:
            kk_ref[0:P, :] = ck_ref[hd].astype(BF16)
            vv_ref[0:P, :] = cv_ref[hd].astype(BF16)
        kk_ref[P:P + L, :] = k_ref[hd].astype(BF16)
        vv_ref[P:P + L, :] = v_ref[hd].astype(BF16)

        def scores(i, buf):
            q = q_ref[hd, pl.ds(pl.multiple_of(i * tq, tq), tq), :]
            zero = jnp.zeros_like(q)
            qs = jnp.concatenate([jnp.where(first_half, q, zero), jnp.where(first_half, zero, q)], axis=0)
            s_ref[buf] = _dot_nt(qs, kk_ref[...])

        def finish(i, buf):
            s = s_ref[buf]
            p = jnp.exp2(s - jnp.max(s, axis=-1, keepdims=True))
            rinv = 1.0 / jnp.sum(p, axis=-1, keepdims=True)
            acc = _dot(p.astype(BF16), vv_ref[...])
            o = acc[0:tq] * rinv[0:tq] - acc[tq:2 * tq] * (lam * rinv[tq:2 * tq])
            o = o * lax.rsqrt(jnp.mean(o * o, axis=-1, keepdims=True) + EPS) * sub
            r0 = pl.multiple_of(i * tq, tq)
            o_ref[pl.ds(r0, tq), hd * DA_VDIM:(hd + 1) * DA_VDIM] = o.astype(o_ref.dtype)

        def pair(j, carry):
            i = 2 * j
            scores(i + 1, 1)
            finish(i, 0)
            scores(i + 2, 0)
            finish(i + 1, 1)
            return carry

        n = L // tq
        scores(0, 0)
        lax.fori_loop(0, n // 2 - 1, pair, 0, unroll=unroll)
        scores(n - 1, 1)
        finish(n - 2, 0)
        finish(n - 1, 1)


def _attention(q, k, v, layer, cache, dal, subln, lam_init, B, L, tq=128, unroll=2):
    H, dv = N_DA_HEADS, DA_VDIM
    hspec = pl.BlockSpec((None, H, L, dv), lambda b: (b, 0, 0, 0))
    kvspec = hspec if k.ndim == 4 else pl.BlockSpec((None, None, H, L, dv), lambda b: (b, layer, 0, 0, 0))
    in_specs = [hspec, kvspec, kvspec]
    args = [q, k, v]
    P = 0
    if cache is not None:
        ck, cv = cache
        P = ck.shape[3]
        cspec = pl.BlockSpec((None, None, H, P, dv), lambda b: (b, layer, 0, 0, 0))
        in_specs += [cspec, cspec]
        args += [ck, cv]
    assert L % tq == 0
    in_specs += [_const_spec((4, DA_HEAD)), _const_spec((1, dv))]
    args += [dal, subln.reshape(1, dv)]
    return pl.pallas_call(
        functools.partial(_attn_kernel, L, P, tq, min(unroll, max(1, L // tq // 2 - 1)), lam_init),
        grid=(B,),
        in_specs=in_specs,
        out_specs=pl.BlockSpec((L, H * dv), lambda b: (b, 0)),
        out_shape=jax.ShapeDtypeStruct((B * L, H * dv), BF16),
        scratch_shapes=[pltpu.VMEM((P + L, dv), BF16), pltpu.VMEM((P + L, dv), BF16),
                        pltpu.VMEM((2, 2 * tq, P + L), F32)],
        compiler_params=_cparams("parallel"),
        name="diff_attn_cache" if P else "diff_attn",
    )(*args)


def _route(logits):
    m = logits[0]
    for e in range(1, N_EXPERTS):
        m = jnp.maximum(m, logits[e])
    ex = [jnp.exp(l - m) for l in logits]
    tot = ex[0]
    for e in range(1, N_EXPERTS):
        tot = tot + ex[e]
    inv = 1.0 / tot
    p = [e_ * inv for e_ in ex]
    G = EXP_PER_GROUP
    best, gsel = None, None
    for g in range(N_GROUPS):
        a = p[g * G:(g + 1) * G]
        sc = None
        for i in range(G):
            for j in range(i + 1, G):
                pair = a[i] + a[j]
                sc = pair if sc is None else jnp.maximum(sc, pair)
        if g == 0:
            best, gsel = sc, jnp.zeros_like(sc, dtype=jnp.int32)
        else:
            upd = sc > best
            best = jnp.where(upd, sc, best)
            gsel = jnp.where(upd, g, gsel)
    vals = []
    for j in range(G):
        vj = p[j]
        for g in range(1, N_GROUPS):
            vj = jnp.where(gsel == g, p[g * G + j], vj)
        vals.append(vj)
    p1, i1 = vals[0], jnp.zeros_like(gsel)
    for j in range(1, G):
        upd = vals[j] > p1
        p1 = jnp.where(upd, vals[j], p1)
        i1 = jnp.where(upd, j, i1)
    p2, i2 = None, None
    for j in range(G):
        cand = jnp.where(i1 == j, -1.0, vals[j])
        if p2 is None:
            p2, i2 = cand, jnp.zeros_like(gsel)
        else:
            upd = cand > p2
            p2 = jnp.where(upd, cand, p2)
            i2 = jnp.where(upd, j, i2)
    den = 1.0 / (p1 + p2)
    w1, w2 = p1 * den, p2 * den
    swap = i2 < i1
    a, b = jnp.where(swap, i2, i1), jnp.where(swap, i1, i2)
    w_lo, w_hi = jnp.where(swap, w2, w1), jnp.where(swap, w1, w2)
    pair = jnp.where(a == 0, b - 1, jnp.where(a == 1, b + 1, 5))
    cls = gsel * PAIRS_PER_GROUP + pair
    return cls.astype(F32), w_lo, w_hi


def _pack_pairs(x):
    n = x.shape[1] // 2
    b = pltpu.bitcast(x, jnp.uint32)
    w = (b[:, :n] >> 16) | (b[:, n:] & jnp.uint32(0xFFFF0000))
    return pltpu.bitcast(w, jnp.int32)


def _unpack_pairs(w):
    u = pltpu.bitcast(w, jnp.uint32)
    lo = pltpu.bitcast(u << 16, F32)
    hi = pltpu.bitcast(u & jnp.uint32(0xFFFF0000), F32)
    return jnp.concatenate([lo, hi], axis=1)


def _out_proj_kernel(yh_ref, yr_ref, o_ref, w_ref, x_ref, mod_ref, g_ref, wrh_ref, wrl_ref, br_ref,
                     xo_ref, h_ref, route_ref, y_ref):
    n_sub = y_ref.shape[0]
    sub = x_ref.shape[0] // n_sub

    def project(j):
        r = slice(j * sub, (j + 1) * sub)
        y_ref[j] = (_dot(yh_ref[r, :], w_ref[0:D_HY, :]) + _dot(yr_ref[r, :], w_ref[D_HY:D_HY + D_RG, :])
                    + _dot(o_ref[r, :], w_ref[D_HY + D_RG:D_MIX, :]))

    def finish(j):
        r = slice(j * sub, (j + 1) * sub)
        x = x_ref[r, :] + mod_ref[2:3, :] * y_ref[j]
        xo_ref[r, :] = x
        ms = jnp.mean(x * x, axis=-1, keepdims=True)
        h = (x * lax.rsqrt(ms + EPS) * g_ref[...]) * (1.0 + mod_ref[4:5, :]) + mod_ref[3:4, :]
        hh, hl = _split(h)
        h_ref[r, :] = _pack_pairs(hh.astype(F32))
        lg = _dot_nt(wrh_ref[...], hh) + _dot_nt(wrh_ref[...], hl) + _dot_nt(wrl_ref[...], hh) + br_ref[...]
        info = _route([lg[e:e + 1, :] for e in range(N_EXPERTS)])
        rt = jnp.concatenate(list(info) + [jnp.zeros((LANES - len(info), sub), F32)], axis=0)
        route_ref[r, :] = rt.T

    project(0)
    for j in range(n_sub):
        if j + 1 < n_sub:
            project(j + 1)
        finish(j)


def _out_proj(y_hy, y_rg, o, w_out, layer, x, mod, g2, w_router, b_router, B, L, ctx_rows, tm=1024, n_sub=2):
    T = B * L
    tm = min(tm, T if ctx_rows else L)
    assert T % tm == 0 and tm % n_sub == 0
    nl = max(L // tm, 1)
    row = (lambda i: CTX_ROW) if ctx_rows else (lambda i: i // nl)
    wrt = w_router.T
    wrh = wrt.astype(BF16)
    wrl = (wrt - wrh.astype(F32)).astype(BF16)
    rows = lambda w: pl.BlockSpec((tm, w), lambda i: (i, 0))
    return pl.pallas_call(
        _out_proj_kernel,
        grid=(T // tm,),
        in_specs=[
            rows(D_HY), rows(D_RG), rows(D_DA),
            pl.BlockSpec((None, D_MIX, D_MODEL), lambda i: (layer, 0, 0)),
            rows(D_MODEL),
            pl.BlockSpec((None, 6, D_MODEL), lambda i: (row(i), 0, 0)),
            _const_spec((1, D_MODEL)),
            _const_spec((N_EXPERTS, D_MODEL)),
            _const_spec((N_EXPERTS, D_MODEL)),
            _const_spec((N_EXPERTS, 1)),
        ],
        out_specs=[rows(D_MODEL), rows(D_MODEL // 2), rows(LANES)],
        out_shape=[
            jax.ShapeDtypeStruct((T, D_MODEL), F32),
            jax.ShapeDtypeStruct((T, D_MODEL // 2), jnp.int32),
            jax.ShapeDtypeStruct((T, LANES), F32),
        ],
        scratch_shapes=[pltpu.VMEM((n_sub, tm // n_sub, D_MODEL), F32)],
        compiler_params=_cparams("parallel"),
        name="out_proj_route",
    )(y_hy, y_rg, o, w_out, x, mod, g2.reshape(1, D_MODEL), wrh, wrl, b_router.reshape(N_EXPERTS, 1))


def _gather_rows(table, idx, rows_per_step=64, n_buf=2):
    info = plsc.get_sparse_core_info()
    n_workers = info.num_cores * info.num_subcores
    n, width = idx.shape[0], table.shape[1]
    per_worker = n // n_workers
    n_steps = per_worker // rows_per_step
    assert per_worker * n_workers == n and n_steps * rows_per_step == per_worker and n_steps >= n_buf
    mesh = plsc.VectorSubcoreMesh(core_axis_name="c", subcore_axis_name="s")

    @functools.partial(
        pl.kernel, mesh=mesh,
        out_type=jax.ShapeDtypeStruct((n, width), table.dtype),
        scratch_types=[
            pltpu.VMEM((per_worker,), jnp.int32),
            pltpu.VMEM((n_buf, rows_per_step, width), table.dtype),
            pltpu.SemaphoreType.DMA((n_buf,)),
            pltpu.SemaphoreType.DMA((n_buf,)),
        ],
    )
    def gather(table_hbm, idx_hbm, out_hbm, idx_v, rows_v, sem_in, sem_out):
        worker = lax.axis_index("s") * info.num_cores + lax.axis_index("c")
        base = pl.multiple_of(worker * per_worker, per_worker)
        pltpu.sync_copy(idx_hbm.at[pl.ds(base, per_worker)], idx_v)

        def read(b, step):
            rows = idx_v.at[pl.ds(step * rows_per_step, rows_per_step)]
            return pltpu.make_async_copy(table_hbm.at[rows], rows_v.at[b], sem_in.at[b])

        def write(b, step):
            off = pl.multiple_of(base + step * rows_per_step, rows_per_step)
            return pltpu.make_async_copy(rows_v.at[b], out_hbm.at[pl.ds(off, rows_per_step)], sem_out.at[b])

        for step in range(n_steps + 1):
            if step < n_steps:
                if step >= n_buf:
                    write(step % n_buf, step - n_buf).wait()
                read(step % n_buf, step).start()
            if step >= 1:
                read((step - 1) % n_buf, step - 1).wait()
                write((step - 1) % n_buf, step - 1).start()
        for step in range(n_steps - n_buf, n_steps):
            write(step % n_buf, step).wait()

    return gather(table, idx)


def _scatter_rows(src, pos, n_slots, rows_per_step=64, n_buf=2):
    info = plsc.get_sparse_core_info()
    n_workers = info.num_cores * info.num_subcores
    n, width = src.shape
    per_worker = n // n_workers
    n_steps = per_worker // rows_per_step
    assert per_worker * n_workers == n and n_steps * rows_per_step == per_worker and n_steps >= n_buf
    mesh = plsc.VectorSubcoreMesh(core_axis_name="c", subcore_axis_name="s")

    @functools.partial(
        pl.kernel, mesh=mesh,
        out_type=jax.ShapeDtypeStruct((n_slots, width), src.dtype),
        scratch_types=[
            pltpu.VMEM((n_steps, rows_per_step), jnp.int32),
            pltpu.VMEM((n_buf, rows_per_step, width), src.dtype),
            pltpu.SemaphoreType.DMA((n_buf,)),
            pltpu.SemaphoreType.DMA((n_buf,)),
        ],
    )
    def scatter(src_hbm, idx_hbm, out_hbm, idx_v, rows_v, sem_in, sem_out):
        worker = lax.axis_index("s") * info.num_cores + lax.axis_index("c")
        base = pl.multiple_of(worker * per_worker, per_worker)
        pltpu.sync_copy(idx_hbm.at[worker], idx_v)

        def read(b, step):
            off = pl.multiple_of(base + step * rows_per_step, rows_per_step)
            return pltpu.make_async_copy(src_hbm.at[pl.ds(off, rows_per_step)], rows_v.at[b], sem_in.at[b])

        def write(b, step):
            return pltpu.make_async_copy(rows_v.at[b], out_hbm.at[idx_v.at[step]], sem_out.at[b])

        for step in range(n_steps + 1):
            if step < n_steps:
                if step >= n_buf:
                    write(step % n_buf, step - n_buf).wait()
                read(step % n_buf, step).start()
            if step >= 1:
                read((step - 1) % n_buf, step - 1).wait()
                write((step - 1) % n_buf, step - 1).start()
        for step in range(n_steps - n_buf, n_steps):
            write(step % n_buf, step).wait()

    return scatter(src, pos.reshape(n_workers, n_steps, rows_per_step))


def _dispatch_plan(route, tm):
    T = route.shape[0]
    n_slots = T + N_CLASSES * tm
    cls = route[:, 0].astype(jnp.int32)
    onehot = (cls[:, None] == jnp.arange(N_CLASSES, dtype=jnp.int32)[None, :]).astype(jnp.int32)
    csum = jnp.cumsum(onehot, axis=0)
    rank = jnp.sum(onehot * csum, axis=1) - 1
    counts = csum[-1]
    padded = ((counts + tm - 1) // tm) * tm
    ends = jnp.cumsum(padded)
    starts = ends - padded
    pos = jnp.sum(onehot * starts[None, :], axis=1) + rank
    tile_start = jnp.arange(n_slots // tm, dtype=jnp.int32) * tm
    tile_cls = jnp.minimum(jnp.sum((tile_start[:, None] >= ends[None, :]).astype(jnp.int32), axis=1), N_CLASSES - 1)
    n_rows = jnp.clip(counts[tile_cls] - (tile_start - starts[tile_cls]), 0, tm).astype(jnp.int32)
    pairs = np.array([(a, b) for a in range(EXP_PER_GROUP) for b in range(a + 1, EXP_PER_GROUP)], np.int32)
    group, pair = tile_cls // PAIRS_PER_GROUP, tile_cls % PAIRS_PER_GROUP
    lo = group * EXP_PER_GROUP + jnp.asarray(pairs[:, 0])[pair]
    hi = group * EXP_PER_GROUP + jnp.asarray(pairs[:, 1])[pair]
    return pos, n_slots, lo, hi, n_rows


def _moe_sorted_kernel(lo_ref, hi_ref, rows_ref, xs_ref, ws_ref, wg_lo, wu_lo, wd_lo, wg_hi, wu_hi, wd_hi, o_ref):
    i = pl.program_id(0)

    @pl.when(rows_ref[i] > 0)
    def _():
        real = lax.broadcasted_iota(jnp.int32, (xs_ref.shape[0], 1), 0) < rows_ref[i]
        x = jnp.where(real, _unpack_pairs(xs_ref[...]), 0.0).astype(BF16)
        y = None
        for wg, wu, wd, col in ((wg_lo, wu_lo, wd_lo, 1), (wg_hi, wu_hi, wd_hi, 2)):
            a = _dot(x, wg[...])
            he = (a * _sigmoid(a)) * _dot(x, wu[...]) * jnp.where(real, ws_ref[:, col:col + 1], 0.0)
            part = _dot(he.astype(BF16), wd[...])
            y = part if y is None else y + part
        o_ref[...] = _pack_pairs(y.astype(BF16).astype(F32))

    @pl.when(rows_ref[i] == 0)
    def _():
        o_ref[...] = jnp.zeros_like(o_ref)


def _moe_sorted(xs, ws, lo, hi, n_rows, wg, wu, wd, tm):
    n_slots = xs.shape[0]
    half = D_MODEL // 2
    up = lambda sel: pl.BlockSpec((None, D_MODEL, D_EXPERT), lambda i, lo, hi, v: ((lo, hi)[sel][i], 0, 0))
    down = lambda sel: pl.BlockSpec((None, D_EXPERT, D_MODEL), lambda i, lo, hi, v: ((lo, hi)[sel][i], 0, 0))
    return pl.pallas_call(
        _moe_sorted_kernel,
        grid_spec=pltpu.PrefetchScalarGridSpec(
            num_scalar_prefetch=3,
            grid=(n_slots // tm,),
            in_specs=[
                pl.BlockSpec((tm, half), lambda i, lo, hi, v: (i, 0)),
                pl.BlockSpec((tm, LANES), lambda i, lo, hi, v: (i, 0)),
                up(0), up(0), down(0), up(1), up(1), down(1),
            ],
            out_specs=pl.BlockSpec((tm, half), lambda i, lo, hi, v: (i, 0)),
        ),
        out_shape=jax.ShapeDtypeStruct((n_slots, half), jnp.int32),
        compiler_params=_cparams("arbitrary"),
        name="moe_sorted",
    )(lo, hi, n_rows, xs, ws, wg, wu, wd, wg, wu, wd)


def _final_residual_kernel(y_ref, x_ref, mod_ref, fg_ref, o_ref):
    x = x_ref[...] + mod_ref[5:6, :] * _unpack_pairs(y_ref[...])
    o_ref[...] = x * lax.rsqrt(jnp.mean(x * x, axis=-1, keepdims=True) + EPS) * fg_ref[...]


def _final_residual(y, x, mod, final_g, B, L, ctx_rows, tm=2048):
    T = B * L
    tm = min(tm, T if ctx_rows else L)
    assert T % tm == 0
    nl = max(L // tm, 1)
    row = (lambda i: CTX_ROW) if ctx_rows else (lambda i: i // nl)
    return pl.pallas_call(
        _final_residual_kernel,
        grid=(T // tm,),
        in_specs=[
            pl.BlockSpec((tm, D_MODEL // 2), lambda i: (i, 0)),
            pl.BlockSpec((tm, D_MODEL), lambda i: (i, 0)),
            pl.BlockSpec((None, 6, D_MODEL), lambda i: (row(i), 0, 0)),
            _const_spec((1, D_MODEL)),
        ],
        out_specs=pl.BlockSpec((tm, D_MODEL), lambda i: (i, 0)),
        out_shape=jax.ShapeDtypeStruct((T, D_MODEL), F32),
        compiler_params=_cparams("parallel"),
        name="final_residual",
    )(y, x, mod, final_g.reshape(1, D_MODEL))


def _moe(h, route, wg, wu, wd, layer, tm=256):
    pos, n_slots, lo, hi, n_rows = _dispatch_plan(route, tm)
    xs = _scatter_rows(h, pos, n_slots)
    ws = _scatter_rows(route, pos, n_slots)
    ys = _moe_sorted(xs, ws, lo + layer * N_EXPERTS, hi + layer * N_EXPERTS, n_rows, wg, wu, wd, tm)
    return _gather_rows(ys, pos)


def kernel(x_prompt, x_sample, cache_k, cache_v, state_rglru, c, c_ctx, w_ada, b_ada, norm1_g, norm2_g, w_in, w_out, hy_short_w, hy_short_b, hy_w1, hy_b1, hy_w2, hy_b2, hy_w3, hy_freq, hy_bias, rg_conv_w, rg_conv_b, rg_wa, rg_ba, rg_wx, rg_bx, rg_lambda, da_lambda, da_subln, w_router, b_router, moe_wg, moe_wu, moe_wd, final_g):
    Bp, Lp, D = x_prompt.shape
    Bs, Ls, _ = x_sample.shape
    assert Bs <= CTX_ROW
    cond = jnp.zeros((N_COND, D), F32).at[:Bs].set(c).at[CTX_ROW].set(c_ctx)
    mods = _ada_table(cond, w_ada, b_ada)

    dft = {L: tuple(jnp.asarray(m).astype(BF16) for m in _dft_mats(L)) for L in (Lp, Ls)}
    streams = [
        dict(B=Bp, L=Lp, ctx=True, x=x_prompt.reshape(Bp * Lp, D)),
        dict(B=Bs, L=Ls, ctx=False, x=x_sample.reshape(Bs * Ls, D)),
    ]
    w_in_b, w_out_b = w_in.astype(BF16), w_out.astype(BF16)
    wg, wu, wd = (w.astype(BF16).reshape((DEPTH * N_EXPERTS,) + w.shape[2:]) for w in (moe_wg, moe_wu, moe_wd))
    new_kv, ss = None, []
    for l in range(DEPTH):
        lam_init = 0.8 - 0.6 * math.exp(-0.3 * l)
        for st in streams:
            B, L, ctx = st["B"], st["L"], st["ctx"]
            cmat, smat = dft[L]
            outs = _norm_proj(st["x"], mods[l], norm1_g[l], w_in_b, l, B, L, ctx, kv_prev=new_kv if ctx else None,
                              pending=(st["y"], mods[l - 1]) if l else None)
            p_hy, p_g, p_x, q, k, v = outs[:6]
            if l:
                st["x"] = outs[6]
            kre, kim = _hy_spectra(L, cmat, smat, hy_w1[l], hy_b1[l], hy_w2[l], hy_b2[l], hy_w3[l], hy_freq[l])
            y_hy = _hyena(p_hy, B, L, cmat, smat, kre, kim, hy_short_w[l], hy_short_b[l], hy_bias[l])
            rg_args = (rg_conv_w[l], rg_conv_b[l], rg_wa[l], rg_ba[l], rg_wx[l], rg_bx[l], rg_lambda[l])
            if ctx:
                y_rg, s_l = _rglru(p_g, p_x, B, L, *rg_args, None)
                o = _attention(q, k, v, l, None, da_lambda[l], da_subln[l], lam_init, B, L)
                new_kv = (k, v)
                ss.append(s_l)
            else:
                y_rg = _rglru(p_g, p_x, B, L, *rg_args, state_rglru[:, l])
                o = _attention(q, k, v, l, (cache_k, cache_v), da_lambda[l], da_subln[l], lam_init, B, L)
            st["x"], h2, route = _out_proj(y_hy, y_rg, o, w_out_b, l, st["x"], mods[l], norm2_g[l],
                                           w_router, b_router, B, L, ctx)
            st["y"] = _moe(h2, route, wg, wu, wd, l)
    y_prompt, y_sample = (
        _final_residual(st["y"], st["x"], mods[DEPTH - 1], final_g, st["B"], st["L"], st["ctx"]).reshape(shape)
        for st, shape in zip(streams, (x_prompt.shape, x_sample.shape)))
    return (y_prompt, y_sample, new_kv[0], new_kv[1], jnp.stack(ss, axis=1))
```

```python
import functools
import math

import numpy as np
import jax
import jax.numpy as jnp
from jax import lax
from jax.experimental import pallas as pl
from jax.experimental.pallas import tpu as pltpu
from jax.experimental.pallas import tpu_sc as plsc

F32 = jnp.float32
BF16 = jnp.bfloat16

D_MODEL = 1024
DEPTH = 2
GRID_W = 64
D_HY = 256
HY_EMB = 33
HY_BANDS = (HY_EMB - 1) // 2
HY_FFN = 64
HY_MIN_DECAY = math.log(1e-2) / 1.5
HY_MAX_DECAY = math.log(1e-2) / 0.3
D_RG = 256
N_RG_HEADS = 4
RG_C = 8.0
N_DA_HEADS = 4
DA_HEAD = 64
DA_VDIM = 2 * DA_HEAD
D_DA = N_DA_HEADS * DA_VDIM
D_MIX = D_HY + D_RG + D_DA
D_IN = 3 * D_HY + 2 * D_RG + 3 * D_DA
ROPE_PAIRS = DA_HEAD // 4
ROPE_THETA = 10000.0
N_EXPERTS = 16
N_GROUPS = 4
EXP_PER_GROUP = N_EXPERTS // N_GROUPS
D_EXPERT = 512
PAIRS_PER_GROUP = EXP_PER_GROUP * (EXP_PER_GROUP - 1) // 2
N_CLASSES = N_GROUPS * PAIRS_PER_GROUP
EPS = 1e-6
N_COND = 16
CTX_ROW = 8
LANES = 128
SUBLANES = 8
VMEM_LIMIT = 56 * 1024 * 1024


def _cparams(*sem):
    return pltpu.CompilerParams(dimension_semantics=sem, vmem_limit_bytes=VMEM_LIMIT)


def _split(x):
    hi = x.astype(BF16)
    lo = (x - hi.astype(F32)).astype(BF16)
    return hi, lo


def _dot(a, b):
    return jnp.dot(a, b, preferred_element_type=F32)


def _dot3(a, b):
    ah, al = _split(a)
    bh, bl = _split(b)
    return _dot(ah, bh) + _dot(al, bh) + _dot(ah, bl)


def _dot_nt(a, b):
    return lax.dot_general(a, b, (((1,), (1,)), ((), ())), preferred_element_type=F32)


def _sigmoid(x):
    return 1.0 / (1.0 + jnp.exp(-x))


def _const_spec(shape):
    n = len(shape)
    return pl.BlockSpec(shape, lambda *_: (0,) * n)


def _ada_kernel(c_ref, w_ref, b_ref, o_ref):
    c = c_ref[...]
    s = c * _sigmoid(c)
    o_ref[...] = _dot3(s, w_ref[...]) + b_ref[...]


def _ada_table(cond, w_ada, b_ada):
    D = D_MODEL
    out = pl.pallas_call(
        _ada_kernel,
        grid=(DEPTH, 6),
        in_specs=[
            pl.BlockSpec((N_COND, D), lambda l, j: (0, 0)),
            pl.BlockSpec((None, D, D), lambda l, j: (l, 0, j)),
            pl.BlockSpec((None, None, 1, D), lambda l, j: (l, j, 0, 0)),
        ],
        out_specs=pl.BlockSpec((None, None, N_COND, D), lambda l, j: (l, j, 0, 0)),
        out_shape=jax.ShapeDtypeStruct((DEPTH, 6, N_COND, D), F32),
        compiler_params=_cparams("parallel", "parallel"),
        name="ada_table",
    )(cond, w_ada, b_ada.reshape(DEPTH, 6, 1, D))
    return out.transpose(0, 2, 1, 3)


def _rope_tables(L):
    t = np.arange(L)
    j = np.arange(LANES)
    jj = j % DA_HEAD
    is_col = (jj // (DA_HEAD // 2)) == 1
    pair = jj % ROPE_PAIRS
    second = (jj % (DA_HEAD // 2)) >= ROPE_PAIRS
    inv = ROPE_THETA ** (-np.arange(ROPE_PAIRS, dtype=np.float64) / ROPE_PAIRS)
    pos = np.where(is_col[None, :], (t % GRID_W)[:, None], (t // GRID_W)[:, None]).astype(np.float64)
    ang = pos * inv[pair][None, :]
    cos = np.cos(ang).astype(np.float32)
    sin = np.sin(ang).astype(np.float32)
    sin_a = np.where(second[None, :], 0.0, -sin).astype(np.float32)
    sin_b = np.where(second[None, :], sin, 0.0).astype(np.float32)
    return cos, sin_a, sin_b


def _rope(x, cos, sin_a, sin_b):
    nxt = pltpu.roll(x, LANES - ROPE_PAIRS, axis=1)
    prv = pltpu.roll(x, ROPE_PAIRS, axis=1)
    return x * cos + nxt * sin_a + prv * sin_b


def _norm_proj_kernel(rope, kv_dtype, pending, kv_layer, x_ref, mod_ref, g_ref, w_ref, *rest):
    x = x_ref[...]
    if pending:
        y_ref, modp_ref, xnew_ref = rest[0], rest[1], rest[-1]
        x = x + modp_ref[5:6, :] * _unpack_pairs(y_ref[...])
        xnew_ref[...] = x
        rest = rest[2:-1]
    if rope:
        cos_ref, sa_ref, sb_ref = rest[:3]
    phy_ref, pg_ref, px_ref, q_ref, k_ref, v_ref = rest[-6:]
    ms = jnp.mean(x * x, axis=-1, keepdims=True)
    y = x * lax.rsqrt(ms + EPS) * g_ref[...]
    h = (y * (1.0 + mod_ref[1:2, :]) + mod_ref[0:1, :]).astype(BF16)
    o = 3 * D_HY
    phy_ref[...] = _dot(h, w_ref[:, 0:o]).astype(BF16)
    pg_ref[...] = _dot(h, w_ref[:, o:o + D_RG])
    px_ref[...] = _dot(h, w_ref[:, o + D_RG:o + 2 * D_RG])
    o += 2 * D_RG
    q = _dot(h, w_ref[:, o:o + D_DA]) * (DA_HEAD ** -0.5 * math.log2(math.e))
    k = _dot(h, w_ref[:, o + D_DA:o + 2 * D_DA])
    v = _dot(h, w_ref[:, o + 2 * D_DA:o + 3 * D_DA])
    if rope:
        cos, sa, sb = cos_ref[...], sa_ref[...], sb_ref[...]
    for hd in range(N_DA_HEADS):
        sl = slice(hd * DA_VDIM, (hd + 1) * DA_VDIM)
        qh, kh = q[:, sl], k[:, sl]
        if rope:
            qh = _rope(qh, cos, sa, sb)
            kh = _rope(kh, cos, sa, sb)
        q_ref[hd] = qh.astype(BF16)
        if kv_layer is None:
            k_ref[hd] = kh.astype(kv_dtype)
            v_ref[hd] = v[:, sl].astype(kv_dtype)
        else:
            for l in range(DEPTH):
                k_ref[l, hd] = kh.astype(kv_dtype) if l == kv_layer else jnp.zeros_like(kh, dtype=kv_dtype)
                v_ref[l, hd] = v[:, sl].astype(kv_dtype) if l == kv_layer else jnp.zeros_like(kh, dtype=kv_dtype)


def _norm_proj(x, mod, g, w_in, layer, B, L, ctx, kv_prev=None, pending=None, tm=512):
    T = B * L
    tm = min(tm, L)
    nl = L // tm
    rope, kv_dtype = not ctx, (F32 if ctx else BF16)
    row = (lambda i: CTX_ROW) if ctx else (lambda i: i // nl)
    mod_spec = pl.BlockSpec((None, 6, D_MODEL), lambda i: (row(i), 0, 0))
    in_specs = [
        pl.BlockSpec((tm, D_MODEL), lambda i: (i, 0)),
        mod_spec,
        _const_spec((1, D_MODEL)),
        pl.BlockSpec((None, D_MODEL, D_IN), lambda i: (layer, 0, 0)),
    ]
    args = [x, mod, g.reshape(1, D_MODEL), w_in]
    if pending is not None:
        in_specs += [pl.BlockSpec((tm, D_MODEL // 2), lambda i: (i, 0)), mod_spec]
        args += list(pending)
    if rope:
        tabs = _rope_tables(L)
        in_specs += [pl.BlockSpec((tm, LANES), lambda i: (i % nl, 0))] * 3
        args += [jnp.asarray(t) for t in tabs]
    head_spec = pl.BlockSpec((None, N_DA_HEADS, tm, DA_VDIM), lambda i: (i // nl, 0, i % nl, 0))
    head_shape = (B, N_DA_HEADS, L, DA_VDIM)
    kv_spec, kv_shape, aliases, kv_layer = head_spec, head_shape, {}, None
    if ctx:
        kv_shape = (B, DEPTH, N_DA_HEADS, L, DA_VDIM)
        if kv_prev is not None:
            kv_spec = pl.BlockSpec((None, None, N_DA_HEADS, tm, DA_VDIM), lambda i: (i // nl, layer, 0, i % nl, 0))
            aliases = {len(args): 4, len(args) + 1: 5}
            in_specs += [pl.BlockSpec(memory_space=pl.ANY)] * 2
            args += list(kv_prev)
        else:
            kv_spec = pl.BlockSpec((None, DEPTH, N_DA_HEADS, tm, DA_VDIM), lambda i: (i // nl, 0, 0, i % nl, 0))
            kv_layer = layer
    out_specs = [
        pl.BlockSpec((tm, 3 * D_HY), lambda i: (i, 0)),
        pl.BlockSpec((tm, D_RG), lambda i: (i, 0)),
        pl.BlockSpec((tm, D_RG), lambda i: (i, 0)),
        head_spec, kv_spec, kv_spec,
    ]
    out_shape = [
        jax.ShapeDtypeStruct((T, 3 * D_HY), BF16),
        jax.ShapeDtypeStruct((T, D_RG), F32),
        jax.ShapeDtypeStruct((T, D_RG), F32),
        jax.ShapeDtypeStruct(head_shape, BF16),
        jax.ShapeDtypeStruct(kv_shape, kv_dtype),
        jax.ShapeDtypeStruct(kv_shape, kv_dtype),
    ]
    if pending is not None:
        out_specs.append(pl.BlockSpec((tm, D_MODEL), lambda i: (i, 0)))
        out_shape.append(jax.ShapeDtypeStruct((T, D_MODEL), F32))
    return pl.pallas_call(
        functools.partial(_norm_proj_kernel, rope, kv_dtype, pending is not None, kv_layer),
        grid=(T // tm,),
        in_specs=in_specs,
        out_specs=out_specs,
        out_shape=out_shape,
        input_output_aliases=aliases,
        compiler_params=_cparams("parallel"),
        name="norm_proj_rope" if rope else "norm_proj",
    )(*args)


def _dft_mats(L):
    n = 2 * L - 1
    fs = (np.arange(L, dtype=np.int64)[:, None] * np.arange(L, dtype=np.int64)[None, :]) % n
    ang = fs.astype(np.float64) * (2.0 * np.pi / n)
    return np.cos(ang).astype(np.float32), np.sin(ang).astype(np.float32)


def _hy_features(L):
    t = np.linspace(0.0, 1.0, L, dtype=np.float64)[:, None]
    ang = ((2.0 * math.pi / L) * np.arange(L, dtype=np.float64))[:, None]
    bands = np.linspace(1e-4, HY_BANDS - 1, HY_BANDS, dtype=np.float64)[None, :]
    ba = bands * ang
    z = np.concatenate([t, np.cos(ba), -np.sin(ba)], axis=-1).astype(np.float32)
    return np.pad(z, ((0, 0), (0, LANES - HY_EMB)))


def _hy_filter_kernel(z_ref, w1_ref, b1_ref, w2_ref, b2_ref, w3_ref, fr_ref, rc_ref, rs_ref):
    tr = z_ref.shape[0]
    z = z_ref[...]
    h = jnp.sin(fr_ref[0:1, :] * (_dot3(z, w1_ref[...]) + b1_ref[...]))
    h = jnp.sin(fr_ref[1:2, :] * (_dot3(h, w2_ref[...]) + b2_ref[...]))
    h = _dot3(h, w3_ref[...])
    t = z[:, 0:1]
    step = (HY_MAX_DECAY - HY_MIN_DECAY) / (D_HY - 1)
    deltas = HY_MIN_DECAY + step * lax.broadcasted_iota(jnp.int32, (1, D_HY), 1).astype(F32)
    window = jnp.exp(-t * jnp.abs(deltas))
    not_first = pl.program_id(0) * tr + lax.broadcasted_iota(jnp.int32, (tr, 1), 0) > 0
    for o in range(2):
        hf = h[:, (2 * o) * D_HY:(2 * o + 1) * D_HY] * window
        hb = jnp.where(not_first, h[:, (2 * o + 1) * D_HY:(2 * o + 2) * D_HY] * window, 0.0)
        rc_ref[:, o * D_HY:(o + 1) * D_HY] = (hf + hb).astype(BF16)
        rs_ref[:, o * D_HY:(o + 1) * D_HY] = (hb - hf).astype(BF16)


def _hy_spectrum_kernel(c_ref, s_ref, rc_ref, rs_ref, w_ref, kre_ref, kim_ref):
    w = w_ref[...]
    kre_ref[...] = _dot(c_ref[...], rc_ref[...]) * w
    kim_ref[...] = _dot(s_ref[...], rs_ref[...]) * w


def _hy_spectra(L, cmat, smat, w1, b1, w2, b2, w3, freq):
    z = jnp.asarray(_hy_features(L))
    w1p = jnp.pad(w1, ((0, LANES - HY_EMB), (0, 0)))
    nw = 2 * D_HY
    tr = min(L, 256)
    rc, rs = pl.pallas_call(
        _hy_filter_kernel,
        grid=(L // tr,),
        in_specs=[
            pl.BlockSpec((tr, LANES), lambda i: (i, 0)),
            _const_spec((LANES, HY_FFN)), _const_spec((1, HY_FFN)),
            _const_spec((HY_FFN, HY_FFN)), _const_spec((1, HY_FFN)),
            _const_spec((HY_FFN, 2 * nw)), _const_spec((2, HY_FFN)),
        ],
        out_specs=[pl.BlockSpec((tr, nw), lambda i: (i, 0))] * 2,
        out_shape=[jax.ShapeDtypeStruct((L, nw), BF16)] * 2,
        compiler_params=_cparams("parallel"),
        name="hy_filter",
    )(z, w1p, b1.reshape(1, HY_FFN), w2, b2.reshape(1, HY_FFN), w3, freq)
    n = 2 * L - 1
    wsc = np.full((L, 1), 2.0 / n, np.float32)
    wsc[0, 0] = 1.0 / n
    return pl.pallas_call(
        _hy_spectrum_kernel,
        grid=(L // tr,),
        in_specs=[
            pl.BlockSpec((tr, L), lambda i: (i, 0)),
            pl.BlockSpec((tr, L), lambda i: (i, 0)),
            _const_spec((L, nw)),
            _const_spec((L, nw)),
            pl.BlockSpec((tr, 1), lambda i: (i, 0)),
        ],
        out_specs=[pl.BlockSpec((tr, nw), lambda i: (i, 0))] * 2,
        out_shape=[jax.ShapeDtypeStruct((L, nw), F32)] * 2,
        compiler_params=_cparams("parallel"),
        name="hy_spectrum",
    )(cmat, smat, rc, rs, jnp.asarray(wsc))


def _hyena_kernel(L, tr, nb, p_ref, sw_ref, sb_ref, bias_ref, c_ref, s_ref, kre_ref, kim_ref, o_ref,
                  pad_ref, u_ref, sig_ref, sig16_ref, zre_ref, zim_ref):
    C3, D = 3 * D_HY, D_HY
    zeros = jnp.zeros((8, C3), F32)
    pad_ref[0:8, :] = zeros
    pad_ref[8 + L:16 + L, :] = zeros
    chunks = [slice(r0, r0 + tr) for r0 in range(0, L, tr)]
    wide = (lambda a: jnp.concatenate([a] * nb, axis=1)) if nb > 1 else (lambda a: a)
    for bb in range(nb):
        for c in chunks:
            pad_ref[8 + c.start:8 + c.stop, :] = p_ref[bb * L + c.start:bb * L + c.stop, :].astype(F32)
        for c in chunks:
            u = sb_ref[...]
            for j in range(3):
                u = u + pad_ref[7 + j + c.start:7 + j + c.stop, :] * sw_ref[j:j + 1, :]
            u_ref[c, bb * 2 * D:(bb + 1) * 2 * D] = u[:, D:C3]
            sig_ref[c, bb * D:(bb + 1) * D] = u[:, 0:D]
            sig16_ref[c, bb * D:(bb + 1) * D] = u[:, 0:D].astype(BF16)

    for o in range(2):
        ko = slice(o * D, (o + 1) * D)
        for c in chunks:
            ure = _dot(c_ref[c, :], sig16_ref[...])
            us = _dot(s_ref[c, :], sig16_ref[...])
            kre, kim = wide(kre_ref[c, ko]), wide(kim_ref[c, ko])
            zre_ref[c, :] = (ure * kre + us * kim).astype(BF16)
            zim_ref[c, :] = (ure * kim - us * kre).astype(BF16)
        for c in chunks:
            y = _dot(c_ref[c, :], zre_ref[...]) - _dot(s_ref[c, :], zim_ref[...])
            y = y + sig_ref[c, :] * wide(bias_ref[o:o + 1, :])
            for bb in range(nb):
                cols = slice(bb * D, (bb + 1) * D)
                z = u_ref[c, bb * 2 * D + o * D:bb * 2 * D + (o + 1) * D] * y[:, cols]
                if o == 0:
                    sig_ref[c, cols] = z
                    sig16_ref[c, cols] = z.astype(BF16)
                else:
                    o_ref[bb * L + c.start:bb * L + c.stop, :] = z.astype(o_ref.dtype)


def _hyena(p_hy, B, L, cmat, smat, kre, kim, short_w, short_b, bias, tr=512, max_rows=1024):
    C3 = 3 * D_HY
    tr = min(tr, L)
    nb = math.gcd(B, max_rows // L) if L <= max_rows else 1
    once = pl.Buffered(1)
    return pl.pallas_call(
        functools.partial(_hyena_kernel, L, tr, nb),
        grid=(B // nb,),
        in_specs=[
            pl.BlockSpec((nb * L, C3), lambda b: (b, 0)),
            _const_spec((3, C3)),
            _const_spec((1, C3)),
            _const_spec((2, D_HY)),
            pl.BlockSpec((L, L), lambda b: (0, 0), pipeline_mode=once),
            pl.BlockSpec((L, L), lambda b: (0, 0), pipeline_mode=once),
            pl.BlockSpec((L, 2 * D_HY), lambda b: (0, 0), pipeline_mode=once),
            pl.BlockSpec((L, 2 * D_HY), lambda b: (0, 0), pipeline_mode=once),
        ],
        out_specs=pl.BlockSpec((nb * L, D_HY), lambda b: (b, 0)),
        out_shape=jax.ShapeDtypeStruct((B * L, D_HY), BF16),
        scratch_shapes=[
            pltpu.VMEM((L + 16, C3), F32),
            pltpu.VMEM((L, nb * 2 * D_HY), F32),
            pltpu.VMEM((L, nb * D_HY), F32),
            pltpu.VMEM((L, nb * D_HY), BF16),
            pltpu.VMEM((L, nb * D_HY), BF16),
            pltpu.VMEM((L, nb * D_HY), BF16),
        ],
        compiler_params=_cparams("parallel"),
        name="hyena",
    )(p_hy, short_w, short_b.reshape(1, C3), bias, cmat, smat, kre, kim)


def _softplus(z):
    return jnp.maximum(z, 0.0) + jnp.log1p(jnp.exp(-jnp.abs(z)))


def _gelu_tanh(x):
    return 0.5 * x * (1.0 + jnp.tanh(math.sqrt(2.0 / math.pi) * (x + 0.044715 * x * x * x)))


def _rglru_kernel(L, has_state, pg_ref, px_ref, cw_ref, cb_ref, w3_ref, gb_ref, lam_ref, *rest):
    if has_state:
        st_ref, y_ref, pad_ref, a_ref, b_ref, h_ref = rest
    else:
        y_ref, st_out_ref, pad_ref, a_ref, b_ref, h_ref = rest
    C = D_RG
    zeros = jnp.zeros((8, C), F32)
    pad_ref[0:8, :] = zeros
    pad_ref[8 + L:16 + L, :] = zeros
    pad_ref[8:8 + L, :] = px_ref[...]
    half_c = (-0.5 * RG_C) * _softplus(-lam_ref[...])
    tr = min(L, 256)
    for r0 in range(0, L, tr):
        xr = cb_ref[...]
        for j in range(4):
            xr = xr + pad_ref[6 + j + r0:6 + j + r0 + tr, :] * cw_ref[j:j + 1, :]
        xh, xl = _split(xr)
        x3 = jnp.concatenate([xh, xl, xh], axis=1)
        for d in range(2):
            t = []
            for m in range(2):
                cols = slice((2 * d + m) * C, (2 * d + m + 1) * C)
                t.append(jnp.tanh(_dot(x3, w3_ref[:, cols]) + gb_ref[:, cols]))
            log_a = half_c[d:d + 1, :] + half_c[d:d + 1, :] * t[0]
            a = jnp.exp(log_a)
            a_ref[d, r0:r0 + tr, :] = a
            gate_x = 0.5 * xr
            b_ref[d, r0:r0 + tr, :] = jnp.sqrt(jnp.tanh(log_a) * (-1.0 - a * a)) * (gate_x + gate_x * t[1])

    if has_state:
        h0f, h0b = st_ref[0:1, :], st_ref[1:2, :]
    else:
        h0f = h0b = jnp.zeros((1, C), F32)

    row = lax.broadcasted_iota(jnp.int32, (SUBLANES, 1), 0)

    def tile_scan(a, b, reverse):
        for d in (1, 2, 4):
            shift = SUBLANES - d if reverse else d
            valid = (row < SUBLANES - d) if reverse else (row >= d)
            a_s, b_s = pltpu.roll(a, shift, axis=0), pltpu.roll(b, shift, axis=0)
            b = jnp.where(valid, a * b_s + b, b)
            a = jnp.where(valid, a * a_s, a)
        return a, b

    def step(i, carry):
        hf, hb = carry
        t0 = pl.multiple_of(i * SUBLANES, SUBLANES)
        tb0 = pl.multiple_of(L - SUBLANES - i * SUBLANES, SUBLANES)
        af, bf = tile_scan(a_ref[0, pl.ds(t0, SUBLANES), :], b_ref[0, pl.ds(t0, SUBLANES), :], False)
        ab, bb = tile_scan(a_ref[1, pl.ds(tb0, SUBLANES), :], b_ref[1, pl.ds(tb0, SUBLANES), :], True)
        hf_tile = af * hf + bf
        hb_tile = ab * hb + bb
        h_ref[0, pl.ds(t0, SUBLANES), :] = hf_tile
        h_ref[1, pl.ds(tb0, SUBLANES), :] = hb_tile
        return hf_tile[SUBLANES - 1:SUBLANES], hb_tile[0:1]

    lax.fori_loop(0, L // SUBLANES, step, (h0f, h0b), unroll=4)
    y_ref[...] = ((h_ref[0] + h_ref[1]) * _gelu_tanh(pg_ref[...])).astype(y_ref.dtype)
    if not has_state:
        st_out_ref[0:1, :] = h_ref[0, L - 1:L, :]
        st_out_ref[1:2, :] = h_ref[1, 0:1, :]


def _block_diag(w):
    H, d, _ = w.shape
    eye = jnp.eye(H, dtype=w.dtype)
    return (eye[:, None, :, None] * w[:, :, None, :]).reshape(H * d, H * d)


def _rglru(p_g, p_x, B, L, conv_w, conv_b, wa, ba, wx, bx, lam, state):
    C = D_RG
    wcat = 0.5 * jnp.concatenate(
        [_block_diag(wa[0]), _block_diag(wx[0]), _block_diag(wa[1]), _block_diag(wx[1])], axis=1)
    wh = wcat.astype(BF16)
    wl = (wcat - wh.astype(F32)).astype(BF16)
    w3 = jnp.concatenate([wh, wh, wl], axis=0)
    gb = 0.5 * jnp.concatenate([ba[0], bx[0], ba[1], bx[1]]).reshape(1, 4 * C)
    has_state = state is not None
    in_specs = [
        pl.BlockSpec((L, C), lambda b: (b, 0)),
        pl.BlockSpec((L, C), lambda b: (b, 0)),
        _const_spec((4, C)),
        _const_spec((1, C)),
        _const_spec((3 * C, 4 * C)),
        _const_spec((1, 4 * C)),
        _const_spec((2, C)),
    ]
    args = [p_g, p_x, conv_w, conv_b.reshape(1, C), w3, gb, lam]
    y_spec = pl.BlockSpec((L, C), lambda b: (b, 0))
    y_shape = jax.ShapeDtypeStruct((B * L, C), BF16)
    if has_state:
        in_specs.append(pl.BlockSpec((None, 2, C), lambda b: (b, 0, 0)))
        args.append(state)
        out_specs, out_shape = y_spec, y_shape
    else:
        out_specs = [y_spec, pl.BlockSpec((None, 2, C), lambda b: (b, 0, 0))]
        out_shape = [y_shape, jax.ShapeDtypeStruct((B, 2, C), F32)]
    return pl.pallas_call(
        functools.partial(_rglru_kernel, L, has_state),
        grid=(B,),
        in_specs=in_specs,
        out_specs=out_specs,
        out_shape=out_shape,
        scratch_shapes=[
            pltpu.VMEM((L + 16, C), F32),
            pltpu.VMEM((2, L, C), F32),
            pltpu.VMEM((2, L, C), F32),
            pltpu.VMEM((2, L, C), F32),
        ],
        compiler_params=_cparams("parallel"),
        name="rglru_state" if has_state else "rglru",
    )(*args)


def _attn_kernel(L, P, tq, unroll, lam_init, q_ref, k_ref, v_ref, *rest):
    if P:
        ck_ref, cv_ref, dal_ref, sub_ref, o_ref, kk_ref, vv_ref, s_ref = rest
    else:
        dal_ref, sub_ref, o_ref, kk_ref, vv_ref, s_ref = rest
    lv = dal_ref[...]
    s01 = jnp.sum(lv[0:1, :] * lv[1:2, :], axis=-1, keepdims=True)
    s23 = jnp.sum(lv[2:3, :] * lv[3:4, :], axis=-1, keepdims=True)
    lam = jnp.exp(s01) - jnp.exp(s23) + lam_init
    first_half = lax.broadcasted_iota(jnp.int32, (1, DA_VDIM), 1) < DA_HEAD
    sub = sub_ref[...] * (1.0 - lam_init)
    for hd in range(N_DA_HEADS):
        if P:
            kk_ref[0:P, :] = ck_ref[hd].astype(BF16)
            vv_ref[0:P, :] = cv_ref[hd].astype(BF16)
        kk_ref[P:P + L, :] = k_ref[hd].astype(BF16)
        vv_ref[P:P + L, :] = v_ref[hd].astype(BF16)

        def scores(i, buf):
            q = q_ref[hd, pl.ds(pl.multiple_of(i * tq, tq), tq), :]
            zero = jnp.zeros_like(q)
            qs = jnp.concatenate([jnp.where(first_half, q, zero), jnp.where(first_half, zero, q)], axis=0)
            s_ref[buf] = _dot_nt(qs, kk_ref[...])

        def finish(i, buf):
            s = s_ref[buf]
            p = jnp.exp2(s - jnp.max(s, axis=-1, keepdims=True))
            rinv = 1.0 / jnp.sum(p, axis=-1, keepdims=True)
            acc = _dot(p.astype(BF16), vv_ref[...])
            o = acc[0:tq] * rinv[0:tq] - acc[tq:2 * tq] * (lam * rinv[tq:2 * tq])
            o = o * lax.rsqrt(jnp.mean(o * o, axis=-1, keepdims=True) + EPS) * sub
            r0 = pl.multiple_of(i * tq, tq)
            o_ref[pl.ds(r0, tq), hd * DA_VDIM:(hd + 1) * DA_VDIM] = o.astype(o_ref.dtype)

        def pair(j, carry):
            i = 2 * j
            scores(i + 1, 1)
            finish(i, 0)
            scores(i + 2, 0)
            finish(i + 1, 1)
            return carry

        n = L // tq
        scores(0, 0)
        lax.fori_loop(0, n // 2 - 1, pair, 0, unroll=unroll)
        scores(n - 1, 1)
        finish(n - 2, 0)
        finish(n - 1, 1)


def _attention(q, k, v, layer, cache, dal, subln, lam_init, B, L, tq=128, unroll=2):
    H, dv = N_DA_HEADS, DA_VDIM
    hspec = pl.BlockSpec((None, H, L, dv), lambda b: (b, 0, 0, 0))
    kvspec = hspec if k.ndim == 4 else pl.BlockSpec((None, None, H, L, dv), lambda b: (b, layer, 0, 0, 0))
    in_specs = [hspec, kvspec, kvspec]
    args = [q, k, v]
    P = 0
    if cache is not None:
        ck, cv = cache
        P = ck.shape[3]
        cspec = pl.BlockSpec((None, None, H, P, dv), lambda b: (b, layer, 0, 0, 0))
        in_specs += [cspec, cspec]
        args += [ck, cv]
    assert L % tq == 0
    in_specs += [_const_spec((4, DA_HEAD)), _const_spec((1, dv))]
    args += [dal, subln.reshape(1, dv)]
    return pl.pallas_call(
        functools.partial(_attn_kernel, L, P, tq, min(unroll, max(1, L // tq // 2 - 1)), lam_init),
        grid=(B,),
        in_specs=in_specs,
        out_specs=pl.BlockSpec((L, H * dv), lambda b: (b, 0)),
        out_shape=jax.ShapeDtypeStruct((B * L, H * dv), BF16),
        scratch_shapes=[pltpu.VMEM((P + L, dv), BF16), pltpu.VMEM((P + L, dv), BF16),
                        pltpu.VMEM((2, 2 * tq, P + L), F32)],
        compiler_params=_cparams("parallel"),
        name="diff_attn_cache" if P else "diff_attn",
    )(*args)


def _route(logits):
    m = logits[0]
    for e in range(1, N_EXPERTS):
        m = jnp.maximum(m, logits[e])
    ex = [jnp.exp(l - m) for l in logits]
    tot = ex[0]
    for e in range(1, N_EXPERTS):
        tot = tot + ex[e]
    inv = 1.0 / tot
    p = [e_ * inv for e_ in ex]
    G = EXP_PER_GROUP
    best, gsel = None, None
    for g in range(N_GROUPS):
        a = p[g * G:(g + 1) * G]
        sc = None
        for i in range(G):
            for j in range(i + 1, G):
                pair = a[i] + a[j]
                sc = pair if sc is None else jnp.maximum(sc, pair)
        if g == 0:
            best, gsel = sc, jnp.zeros_like(sc, dtype=jnp.int32)
        else:
            upd = sc > best
            best = jnp.where(upd, sc, best)
            gsel = jnp.where(upd, g, gsel)
    vals = []
    for j in range(G):
        vj = p[j]
        for g in range(1, N_GROUPS):
            vj = jnp.where(gsel == g, p[g * G + j], vj)
        vals.append(vj)
    p1, i1 = vals[0], jnp.zeros_like(gsel)
    for j in range(1, G):
        upd = vals[j] > p1
        p1 = jnp.where(upd, vals[j], p1)
        i1 = jnp.where(upd, j, i1)
    p2, i2 = None, None
    for j in range(G):
        cand = jnp.where(i1 == j, -1.0, vals[j])
        if p2 is None:
            p2, i2 = cand, jnp.zeros_like(gsel)
        else:
            upd = cand > p2
            p2 = jnp.where(upd, cand, p2)
            i2 = jnp.where(upd, j, i2)
    den = 1.0 / (p1 + p2)
    w1, w2 = p1 * den, p2 * den
    swap = i2 < i1
    a, b = jnp.where(swap, i2, i1), jnp.where(swap, i1, i2)
    w_lo, w_hi = jnp.where(swap, w2, w1), jnp.where(swap, w1, w2)
    pair = jnp.where(a == 0, b - 1, jnp.where(a == 1, b + 1, 5))
    cls = gsel * PAIRS_PER_GROUP + pair
    return cls.astype(F32), w_lo, w_hi


def _pack_pairs(x):
    n = x.shape[1] // 2
    b = pltpu.bitcast(x, jnp.uint32)
    w = (b[:, :n] >> 16) | (b[:, n:] & jnp.uint32(0xFFFF0000))
    return pltpu.bitcast(w, jnp.int32)


def _unpack_pairs(w):
    u = pltpu.bitcast(w, jnp.uint32)
    lo = pltpu.bitcast(u << 16, F32)
    hi = pltpu.bitcast(u & jnp.uint32(0xFFFF0000), F32)
    return jnp.concatenate([lo, hi], axis=1)


def _out_proj_kernel(yh_ref, yr_ref, o_ref, w_ref, x_ref, mod_ref, g_ref, wrh_ref, wrl_ref, br_ref,
                     xo_ref, h_ref, route_ref, y_ref):
    n_sub = y_ref.shape[0]
    sub = x_ref.shape[0] // n_sub

    def project(j):
        r = slice(j * sub, (j + 1) * sub)
        y_ref[j] = (_dot(yh_ref[r, :], w_ref[0:D_HY, :]) + _dot(yr_ref[r, :], w_ref[D_HY:D_HY + D_RG, :])
                    + _dot(o_ref[r, :], w_ref[D_HY + D_RG:D_MIX, :]))

    def finish(j):
        r = slice(j * sub, (j + 1) * sub)
        x = x_ref[r, :] + mod_ref[2:3, :] * y_ref[j]
        xo_ref[r, :] = x
        ms = jnp.mean(x * x, axis=-1, keepdims=True)
        h = (x * lax.rsqrt(ms + EPS) * g_ref[...]) * (1.0 + mod_ref[4:5, :]) + mod_ref[3:4, :]
        hh, hl = _split(h)
        h_ref[r, :] = _pack_pairs(hh.astype(F32))
        lg = _dot_nt(wrh_ref[...], hh) + _dot_nt(wrh_ref[...], hl) + _dot_nt(wrl_ref[...], hh) + br_ref[...]
        info = _route([lg[e:e + 1, :] for e in range(N_EXPERTS)])
        rt = jnp.concatenate(list(info) + [jnp.zeros((LANES - len(info), sub), F32)], axis=0)
        route_ref[r, :] = rt.T

    project(0)
    for j in range(n_sub):
        if j + 1 < n_sub:
            project(j + 1)
        finish(j)


def _out_proj(y_hy, y_rg, o, w_out, layer, x, mod, g2, w_router, b_router, B, L, ctx_rows, tm=1024, n_sub=2):
    T = B * L
    tm = min(tm, T if ctx_rows else L)
    assert T % tm == 0 and tm % n_sub == 0
    nl = max(L // tm, 1)
    row = (lambda i: CTX_ROW) if ctx_rows else (lambda i: i // nl)
    wrt = w_router.T
    wrh = wrt.astype(BF16)
    wrl = (wrt - wrh.astype(F32)).astype(BF16)
    rows = lambda w: pl.BlockSpec((tm, w), lambda i: (i, 0))
    return pl.pallas_call(
        _out_proj_kernel,
        grid=(T // tm,),
        in_specs=[
            rows(D_HY), rows(D_RG), rows(D_DA),
            pl.BlockSpec((None, D_MIX, D_MODEL), lambda i: (layer, 0, 0)),
            rows(D_MODEL),
            pl.BlockSpec((None, 6, D_MODEL), lambda i: (row(i), 0, 0)),
            _const_spec((1, D_MODEL)),
            _const_spec((N_EXPERTS, D_MODEL)),
            _const_spec((N_EXPERTS, D_MODEL)),
            _const_spec((N_EXPERTS, 1)),
        ],
        out_specs=[rows(D_MODEL), rows(D_MODEL // 2), rows(LANES)],
        out_shape=[
            jax.ShapeDtypeStruct((T, D_MODEL), F32),
            jax.ShapeDtypeStruct((T, D_MODEL // 2), jnp.int32),
            jax.ShapeDtypeStruct((T, LANES), F32),
        ],
        scratch_shapes=[pltpu.VMEM((n_sub, tm // n_sub, D_MODEL), F32)],
        compiler_params=_cparams("parallel"),
        name="out_proj_route",
    )(y_hy, y_rg, o, w_out, x, mod, g2.reshape(1, D_MODEL), wrh, wrl, b_router.reshape(N_EXPERTS, 1))


def _gather_rows(table, idx, rows_per_step=64, n_buf=2):
    info = plsc.get_sparse_core_info()
    n_workers = info.num_cores * info.num_subcores
    n, width = idx.shape[0], table.shape[1]
    per_worker = n // n_workers
    n_steps = per_worker // rows_per_step
    assert per_worker * n_workers == n and n_steps * rows_per_step == per_worker and n_steps >= n_buf
    mesh = plsc.VectorSubcoreMesh(core_axis_name="c", subcore_axis_name="s")

    @functools.partial(
        pl.kernel, mesh=mesh,
        out_type=jax.ShapeDtypeStruct((n, width), table.dtype),
        scratch_types=[
            pltpu.VMEM((per_worker,), jnp.int32),
            pltpu.VMEM((n_buf, rows_per_step, width), table.dtype),
            pltpu.SemaphoreType.DMA((n_buf,)),
            pltpu.SemaphoreType.DMA((n_buf,)),
        ],
    )
    def gather(table_hbm, idx_hbm, out_hbm, idx_v, rows_v, sem_in, sem_out):
        worker = lax.axis_index("s") * info.num_cores + lax.axis_index("c")
        base = pl.multiple_of(worker * per_worker, per_worker)
        pltpu.sync_copy(idx_hbm.at[pl.ds(base, per_worker)], idx_v)

        def read(b, step):
            rows = idx_v.at[pl.ds(step * rows_per_step, rows_per_step)]
            return pltpu.make_async_copy(table_hbm.at[rows], rows_v.at[b], sem_in.at[b])

        def write(b, step):
            off = pl.multiple_of(base + step * rows_per_step, rows_per_step)
            return pltpu.make_async_copy(rows_v.at[b], out_hbm.at[pl.ds(off, rows_per_step)], sem_out.at[b])

        for step in range(n_steps + 1):
            if step < n_steps:
                if step >= n_buf:
                    write(step % n_buf, step - n_buf).wait()
                read(step % n_buf, step).start()
            if step >= 1:
                read((step - 1) % n_buf, step - 1).wait()
                write((step - 1) % n_buf, step - 1).start()
        for step in range(n_steps - n_buf, n_steps):
            write(step % n_buf, step).wait()

    return gather(table, idx)


def _scatter_rows(src, pos, n_slots, rows_per_step=64, n_buf=2):
    info = plsc.get_sparse_core_info()
    n_workers = info.num_cores * info.num_subcores
    n, width = src.shape
    per_worker = n // n_workers
    n_steps = per_worker // rows_per_step
    assert per_worker * n_workers == n and n_steps * rows_per_step == per_worker and n_steps >= n_buf
    mesh = plsc.VectorSubcoreMesh(core_axis_name="c", subcore_axis_name="s")

    @functools.partial(
        pl.kernel, mesh=mesh,
        out_type=jax.ShapeDtypeStruct((n_slots, width), src.dtype),
        scratch_types=[
            pltpu.VMEM((n_steps, rows_per_step), jnp.int32),
            pltpu.VMEM((n_buf, rows_per_step, width), src.dtype),
            pltpu.SemaphoreType.DMA((n_buf,)),
            pltpu.SemaphoreType.DMA((n_buf,)),
        ],
    )
    def scatter(src_hbm, idx_hbm, out_hbm, idx_v, rows_v, sem_in, sem_out):
        worker = lax.axis_index("s") * info.num_cores + lax.axis_index("c")
        base = pl.multiple_of(worker * per_worker, per_worker)
        pltpu.sync_copy(idx_hbm.at[worker], idx_v)

        def read(b, step):
            off = pl.multiple_of(base + step * rows_per_step, rows_per_step)
            return pltpu.make_async_copy(src_hbm.at[pl.ds(off, rows_per_step)], rows_v.at[b], sem_in.at[b])

        def write(b, step):
            return pltpu.make_async_copy(rows_v.at[b], out_hbm.at[idx_v.at[step]], sem_out.at[b])

        for step in range(n_steps + 1):
            if step < n_steps:
                if step >= n_buf:
                    write(step % n_buf, step - n_buf).wait()
                read(step % n_buf, step).start()
            if step >= 1:
                read((step - 1) % n_buf, step - 1).wait()
                write((step - 1) % n_buf, step - 1).start()
        for step in range(n_steps - n_buf, n_steps):
            write(step % n_buf, step).wait()

    return scatter(src, pos.reshape(n_workers, n_steps, rows_per_step))


def _dispatch_plan(route, tm):
    T = route.shape[0]
    n_slots = T + N_CLASSES * tm
    cls = route[:, 0].astype(jnp.int32)
    onehot = (cls[:, None] == jnp.arange(N_CLASSES, dtype=jnp.int32)[None, :]).astype(jnp.int32)
    csum = jnp.cumsum(onehot, axis=0)
    rank = jnp.sum(onehot * csum, axis=1) - 1
    counts = csum[-1]
    padded = ((counts + tm - 1) // tm) * tm
    ends = jnp.cumsum(padded)
    starts = ends - padded
    pos = jnp.sum(onehot * starts[None, :], axis=1) + rank
    tile_start = jnp.arange(n_slots // tm, dtype=jnp.int32) * tm
    tile_cls = jnp.minimum(jnp.sum((tile_start[:, None] >= ends[None, :]).astype(jnp.int32), axis=1), N_CLASSES - 1)
    n_rows = jnp.clip(counts[tile_cls] - (tile_start - starts[tile_cls]), 0, tm).astype(jnp.int32)
    pairs = np.array([(a, b) for a in range(EXP_PER_GROUP) for b in range(a + 1, EXP_PER_GROUP)], np.int32)
    group, pair = tile_cls // PAIRS_PER_GROUP, tile_cls % PAIRS_PER_GROUP
    lo = group * EXP_PER_GROUP + jnp.asarray(pairs[:, 0])[pair]
    hi = group * EXP_PER_GROUP + jnp.asarray(pairs[:, 1])[pair]
    return pos, n_slots, lo, hi, n_rows


def _moe_sorted_kernel(lo_ref, hi_ref, rows_ref, xs_ref, ws_ref, wg_lo, wu_lo, wd_lo, wg_hi, wu_hi, wd_hi, o_ref):
    i = pl.program_id(0)

    @pl.when(rows_ref[i] > 0)
    def _():
        real = lax.broadcasted_iota(jnp.int32, (xs_ref.shape[0], 1), 0) < rows_ref[i]
        x = jnp.where(real, _unpack_pairs(xs_ref[...]), 0.0).astype(BF16)
        y = None
        for wg, wu, wd, col in ((wg_lo, wu_lo, wd_lo, 1), (wg_hi, wu_hi, wd_hi, 2)):
            a = _dot(x, wg[...])
            he = (a * _sigmoid(a)) * _dot(x, wu[...]) * jnp.where(real, ws_ref[:, col:col + 1], 0.0)
            part = _dot(he.astype(BF16), wd[...])
            y = part if y is None else y + part
        o_ref[...] = _pack_pairs(y.astype(BF16).astype(F32))

    @pl.when(rows_ref[i] == 0)
    def _():
        o_ref[...] = jnp.zeros_like(o_ref)


def _moe_sorted(xs, ws, lo, hi, n_rows, wg, wu, wd, tm):
    n_slots = xs.shape[0]
    half = D_MODEL // 2
    up = lambda sel: pl.BlockSpec((None, D_MODEL, D_EXPERT), lambda i, lo, hi, v: ((lo, hi)[sel][i], 0, 0))
    down = lambda sel: pl.BlockSpec((None, D_EXPERT, D_MODEL), lambda i, lo, hi, v: ((lo, hi)[sel][i], 0, 0))
    return pl.pallas_call(
        _moe_sorted_kernel,
        grid_spec=pltpu.PrefetchScalarGridSpec(
            num_scalar_prefetch=3,
            grid=(n_slots // tm,),
            in_specs=[
                pl.BlockSpec((tm, half), lambda i, lo, hi, v: (i, 0)),
                pl.BlockSpec((tm, LANES), lambda i, lo, hi, v: (i, 0)),
                up(0), up(0), down(0), up(1), up(1), down(1),
            ],
            out_specs=pl.BlockSpec((tm, half), lambda i, lo, hi, v: (i, 0)),
        ),
        out_shape=jax.ShapeDtypeStruct((n_slots, half), jnp.int32),
        compiler_params=_cparams("arbitrary"),
        name="moe_sorted",
    )(lo, hi, n_rows, xs, ws, wg, wu, wd, wg, wu, wd)


def _final_residual_kernel(y_ref, x_ref, mod_ref, fg_ref, o_ref):
    x = x_ref[...] + mod_ref[5:6, :] * _unpack_pairs(y_ref[...])
    o_ref[...] = x * lax.rsqrt(jnp.mean(x * x, axis=-1, keepdims=True) + EPS) * fg_ref[...]


def _final_residual(y, x, mod, final_g, B, L, ctx_rows, tm=2048):
    T = B * L
    tm = min(tm, T if ctx_rows else L)
    assert T % tm == 0
    nl = max(L // tm, 1)
    row = (lambda i: CTX_ROW) if ctx_rows else (lambda i: i // nl)
    return pl.pallas_call(
        _final_residual_kernel,
        grid=(T // tm,),
        in_specs=[
            pl.BlockSpec((tm, D_MODEL // 2), lambda i: (i, 0)),
            pl.BlockSpec((tm, D_MODEL), lambda i: (i, 0)),
            pl.BlockSpec((None, 6, D_MODEL), lambda i: (row(i), 0, 0)),
            _const_spec((1, D_MODEL)),
        ],
        out_specs=pl.BlockSpec((tm, D_MODEL), lambda i: (i, 0)),
        out_shape=jax.ShapeDtypeStruct((T, D_MODEL), F32),
        compiler_params=_cparams("parallel"),
        name="final_residual",
    )(y, x, mod, final_g.reshape(1, D_MODEL))


def _moe(h, route, wg, wu, wd, layer, tm=256):
    pos, n_slots, lo, hi, n_rows = _dispatch_plan(route, tm)
    xs = _scatter_rows(h, pos, n_slots)
    ws = _scatter_rows(route, pos, n_slots)
    ys = _moe_sorted(xs, ws, lo + layer * N_EXPERTS, hi + layer * N_EXPERTS, n_rows, wg, wu, wd, tm)
    return _gather_rows(ys, pos)


def kernel(x_prompt, x_sample, cache_k, cache_v, state_rglru, c, c_ctx, w_ada, b_ada, norm1_g, norm2_g, w_in, w_out, hy_short_w, hy_short_b, hy_w1, hy_b1, hy_w2, hy_b2, hy_w3, hy_freq, hy_bias, rg_conv_w, rg_conv_b, rg_wa, rg_ba, rg_wx, rg_bx, rg_lambda, da_lambda, da_subln, w_router, b_router, moe_wg, moe_wu, moe_wd, final_g):
    Bp, Lp, D = x_prompt.shape
    Bs, Ls, _ = x_sample.shape
    assert Bs <= CTX_ROW
    cond = jnp.zeros((N_COND, D), F32).at[:Bs].set(c).at[CTX_ROW].set(c_ctx)
    mods = _ada_table(cond, w_ada, b_ada)

    dft = {L: tuple(jnp.asarray(m).astype(BF16) for m in _dft_mats(L)) for L in (Lp, Ls)}
    streams = [
        dict(B=Bp, L=Lp, ctx=True, x=x_prompt.reshape(Bp * Lp, D)),
        dict(B=Bs, L=Ls, ctx=False, x=x_sample.reshape(Bs * Ls, D)),
    ]
    w_in_b, w_out_b = w_in.astype(BF16), w_out.astype(BF16)
    wg, wu, wd = (w.astype(BF16).reshape((DEPTH * N_EXPERTS,) + w.shape[2:]) for w in (moe_wg, moe_wu, moe_wd))
    new_kv, ss = None, []
    for l in range(DEPTH):
        lam_init = 0.8 - 0.6 * math.exp(-0.3 * l)
        for st in streams:
            B, L, ctx = st["B"], st["L"], st["ctx"]
            cmat, smat = dft[L]
            outs = _norm_proj(st["x"], mods[l], norm1_g[l], w_in_b, l, B, L, ctx, kv_prev=new_kv if ctx else None,
                              pending=(st["y"], mods[l - 1]) if l else None)
            p_hy, p_g, p_x, q, k, v = outs[:6]
            if l:
                st["x"] = outs[6]
            kre, kim = _hy_spectra(L, cmat, smat, hy_w1[l], hy_b1[l], hy_w2[l], hy_b2[l], hy_w3[l], hy_freq[l])
            y_hy = _hyena(p_hy, B, L, cmat, smat, kre, kim, hy_short_w[l], hy_short_b[l], hy_bias[l])
            rg_args = (rg_conv_w[l], rg_conv_b[l], rg_wa[l], rg_ba[l], rg_wx[l], rg_bx[l], rg_lambda[l])
            if ctx:
                y_rg, s_l = _rglru(p_g, p_x, B, L, *rg_args, None)
                o = _attention(q, k, v, l, None, da_lambda[l], da_subln[l], lam_init, B, L)
                new_kv = (k, v)
                ss.append(s_l)
            else:
                y_rg = _rglru(p_g, p_x, B, L, *rg_args, state_rglru[:, l])
                o = _attention(q, k, v, l, (cache_k, cache_v), da_lambda[l], da_subln[l], lam_init, B, L)
            st["x"], h2, route = _out_proj(y_hy, y_rg, o, w_out_b, l, st["x"], mods[l], norm2_g[l],
                                           w_router, b_router, B, L, ctx)
            st["y"] = _moe(h2, route, wg, wu, wd, l)
    y_prompt, y_sample = (
        _final_residual(st["y"], st["x"], mods[DEPTH - 1], final_g, st["B"], st["L"], st["ctx"]).reshape(shape)
        for st, shape in zip(streams, (x_prompt.shape, x_sample.shape)))
    return (y_prompt, y_sample, new_kv[0], new_kv[1], jnp.stack(ss, axis=1))
```

```python
import functools
import math

import numpy as np
import jax
import jax.numpy as jnp
from jax import lax
from jax.experimental import pallas as pl
from jax.experimental.pallas import tpu as pltpu
from jax.experimental.pallas import tpu_sc as plsc

F32 = jnp.float32
BF16 = jnp.bfloat16

D_MODEL = 1024
DEPTH = 2
GRID_W = 64
D_HY = 256
HY_EMB = 33
HY_BANDS = (HY_EMB - 1) // 2
HY_FFN = 64
HY_MIN_DECAY = math.log(1e-2) / 1.5
HY_MAX_DECAY = math.log(1e-2) / 0.3
D_RG = 256
N_RG_HEADS = 4
RG_C = 8.0
N_DA_HEADS = 4
DA_HEAD = 64
DA_VDIM = 2 * DA_HEAD
D_DA = N_DA_HEADS * DA_VDIM
D_MIX = D_HY + D_RG + D_DA
D_IN = 3 * D_HY + 2 * D_RG + 3 * D_DA
ROPE_PAIRS = DA_HEAD // 4
ROPE_THETA = 10000.0
N_EXPERTS = 16
N_GROUPS = 4
EXP_PER_GROUP = N_EXPERTS // N_GROUPS
D_EXPERT = 512
PAIRS_PER_GROUP = EXP_PER_GROUP * (EXP_PER_GROUP - 1) // 2
N_CLASSES = N_GROUPS * PAIRS_PER_GROUP
EPS = 1e-6
N_COND = 16
CTX_ROW = 8
LANES = 128
SUBLANES = 8
VMEM_LIMIT = 56 * 1024 * 1024


def _cparams(*sem):
    return pltpu.CompilerParams(dimension_semantics=sem, vmem_limit_bytes=VMEM_LIMIT)


def _split(x):
    hi = x.astype(BF16)
    lo = (x - hi.astype(F32)).astype(BF16)
    return hi, lo


def _dot(a, b):
    return jnp.dot(a, b, preferred_element_type=F32)


def _dot3(a, b):
    ah, al = _split(a)
    bh, bl = _split(b)
    return _dot(ah, bh) + _dot(al, bh) + _dot(ah, bl)


def _dot_nt(a, b):
    return lax.dot_general(a, b, (((1,), (1,)), ((), ())), preferred_element_type=F32)


def _sigmoid(x):
    return 1.0 / (1.0 + jnp.exp(-x))


def _const_spec(shape):
    n = len(shape)
    return pl.BlockSpec(shape, lambda *_: (0,) * n)


def _ada_kernel(c_ref, w_ref, b_ref, o_ref):
    c = c_ref[...]
    s = c * _sigmoid(c)
    o_ref[...] = _dot3(s, w_ref[...]) + b_ref[...]


def _ada_table(cond, w_ada, b_ada):
    D = D_MODEL
    out = pl.pallas_call(
        _ada_kernel,
        grid=(DEPTH, 6),
        in_specs=[
            pl.BlockSpec((N_COND, D), lambda l, j: (0, 0)),
            pl.BlockSpec((None, D, D), lambda l, j: (l, 0, j)),
            pl.BlockSpec((None, None, 1, D), lambda l, j: (l, j, 0, 0)),
        ],
        out_specs=pl.BlockSpec((None, None, N_COND, D), lambda l, j: (l, j, 0, 0)),
        out_shape=jax.ShapeDtypeStruct((DEPTH, 6, N_COND, D), F32),
        compiler_params=_cparams("parallel", "parallel"),
        name="ada_table",
    )(cond, w_ada, b_ada.reshape(DEPTH, 6, 1, D))
    return out.transpose(0, 2, 1, 3)


def _rope_tables(L):
    t = np.arange(L)
    j = np.arange(LANES)
    jj = j % DA_HEAD
    is_col = (jj // (DA_HEAD // 2)) == 1
    pair = jj % ROPE_PAIRS
    second = (jj % (DA_HEAD // 2)) >= ROPE_PAIRS
    inv = ROPE_THETA ** (-np.arange(ROPE_PAIRS, dtype=np.float64) / ROPE_PAIRS)
    pos = np.where(is_col[None, :], (t % GRID_W)[:, None], (t // GRID_W)[:, None]).astype(np.float64)
    ang = pos * inv[pair][None, :]
    cos = np.cos(ang).astype(np.float32)
    sin = np.sin(ang).astype(np.float32)
    sin_a = np.where(second[None, :], 0.0, -sin).astype(np.float32)
    sin_b = np.where(second[None, :], sin, 0.0).astype(np.float32)
    return cos, sin_a, sin_b


def _rope(x, cos, sin_a, sin_b):
    nxt = pltpu.roll(x, LANES - ROPE_PAIRS, axis=1)
    prv = pltpu.roll(x, ROPE_PAIRS, axis=1)
    return x * cos + nxt * sin_a + prv * sin_b


def _norm_proj_kernel(rope, kv_dtype, pending, kv_layer, x_ref, mod_ref, g_ref, w_ref, *rest):
    x = x_ref[...]
    if pending:
        y_ref, modp_ref, xnew_ref = rest[0], rest[1], rest[-1]
        x = x + modp_ref[5:6, :] * _unpack_pairs(y_ref[...])
        xnew_ref[...] = x
        rest = rest[2:-1]
    if rope:
        cos_ref, sa_ref, sb_ref = rest[:3]
    phy_ref, pg_ref, px_ref, q_ref, k_ref, v_ref = rest[-6:]
    ms = jnp.mean(x * x, axis=-1, keepdims=True)
    y = x * lax.rsqrt(ms + EPS) * g_ref[...]
    h = (y * (1.0 + mod_ref[1:2, :]) + mod_ref[0:1, :]).astype(BF16)
    o = 3 * D_HY
    phy_ref[...] = _dot(h, w_ref[:, 0:o]).astype(BF16)
    pg_ref[...] = _dot(h, w_ref[:, o:o + D_RG])
    px_ref[...] = _dot(h, w_ref[:, o + D_RG:o + 2 * D_RG])
    o += 2 * D_RG
    q = _dot(h, w_ref[:, o:o + D_DA]) * (DA_HEAD ** -0.5 * math.log2(math.e))
    k = _dot(h, w_ref[:, o + D_DA:o + 2 * D_DA])
    v = _dot(h, w_ref[:, o + 2 * D_DA:o + 3 * D_DA])
    if rope:
        cos, sa, sb = cos_ref[...], sa_ref[...], sb_ref[...]
    for hd in range(N_DA_HEADS):
        sl = slice(hd * DA_VDIM, (hd + 1) * DA_VDIM)
        qh, kh = q[:, sl], k[:, sl]
        if rope:
            qh = _rope(qh, cos, sa, sb)
            kh = _rope(kh, cos, sa, sb)
        q_ref[hd] = qh.astype(BF16)
        if kv_layer is None:
            k_ref[hd] = kh.astype(kv_dtype)
            v_ref[hd] = v[:, sl].astype(kv_dtype)
        else:
            for l in range(DEPTH):
                k_ref[l, hd] = kh.astype(kv_dtype) if l == kv_layer else jnp.zeros_like(kh, dtype=kv_dtype)
                v_ref[l, hd] = v[:, sl].astype(kv_dtype) if l == kv_layer else jnp.zeros_like(kh, dtype=kv_dtype)


def _norm_proj(x, mod, g, w_in, layer, B, L, ctx, kv_prev=None, pending=None, tm=512):
    T = B * L
    tm = min(tm, L)
    nl = L // tm
    rope, kv_dtype = not ctx, (F32 if ctx else BF16)
    row = (lambda i: CTX_ROW) if ctx else (lambda i: i // nl)
    mod_spec = pl.BlockSpec((None, 6, D_MODEL), lambda i: (row(i), 0, 0))
    in_specs = [
        pl.BlockSpec((tm, D_MODEL), lambda i: (i, 0)),
        mod_spec,
        _const_spec((1, D_MODEL)),
        pl.BlockSpec((None, D_MODEL, D_IN), lambda i: (layer, 0, 0)),
    ]
    args = [x, mod, g.reshape(1, D_MODEL), w_in]
    if pending is not None:
        in_specs += [pl.BlockSpec((tm, D_MODEL // 2), lambda i: (i, 0)), mod_spec]
        args += list(pending)
    if rope:
        tabs = _rope_tables(L)
        in_specs += [pl.BlockSpec((tm, LANES), lambda i: (i % nl, 0))] * 3
        args += [jnp.asarray(t) for t in tabs]
    head_spec = pl.BlockSpec((None, N_DA_HEADS, tm, DA_VDIM), lambda i: (i // nl, 0, i % nl, 0))
    head_shape = (B, N_DA_HEADS, L, DA_VDIM)
    kv_spec, kv_shape, aliases, kv_layer = head_spec, head_shape, {}, None
    if ctx:
        kv_shape = (B, DEPTH, N_DA_HEADS, L, DA_VDIM)
        if kv_prev is not None:
            kv_spec = pl.BlockSpec((None, None, N_DA_HEADS, tm, DA_VDIM), lambda i: (i // nl, layer, 0, i % nl, 0))
            aliases = {len(args): 4, len(args) + 1: 5}
            in_specs += [pl.BlockSpec(memory_space=pl.ANY)] * 2
            args += list(kv_prev)
        else:
            kv_spec = pl.BlockSpec((None, DEPTH, N_DA_HEADS, tm, DA_VDIM), lambda i: (i // nl, 0, 0, i % nl, 0))
            kv_layer = layer
    out_specs = [
        pl.BlockSpec((tm, 3 * D_HY), lambda i: (i, 0)),
        pl.BlockSpec((tm, D_RG), lambda i: (i, 0)),
        pl.BlockSpec((tm, D_RG), lambda i: (i, 0)),
        head_spec, kv_spec, kv_spec,
    ]
    out_shape = [
        jax.ShapeDtypeStruct((T, 3 * D_HY), BF16),
        jax.ShapeDtypeStruct((T, D_RG), F32),
        jax.ShapeDtypeStruct((T, D_RG), F32),
        jax.ShapeDtypeStruct(head_shape, BF16),
        jax.ShapeDtypeStruct(kv_shape, kv_dtype),
        jax.ShapeDtypeStruct(kv_shape, kv_dtype),
    ]
    if pending is not None:
        out_specs.append(pl.BlockSpec((tm, D_MODEL), lambda i: (i, 0)))
        out_shape.append(jax.ShapeDtypeStruct((T, D_MODEL), F32))
    return pl.pallas_call(
        functools.partial(_norm_proj_kernel, rope, kv_dtype, pending is not None, kv_layer),
        grid=(T // tm,),
        in_specs=in_specs,
        out_specs=out_specs,
        out_shape=out_shape,
        input_output_aliases=aliases,
        compiler_params=_cparams("parallel"),
        name="norm_proj_rope" if rope else "norm_proj",
    )(*args)


def _dft_mats(L):
    n = 2 * L - 1
    fs = (np.arange(L, dtype=np.int64)[:, None] * np.arange(L, dtype=np.int64)[None, :]) % n
    ang = fs.astype(np.float64) * (2.0 * np.pi / n)
    return np.cos(ang).astype(np.float32), np.sin(ang).astype(np.float32)


def _hy_features(L):
    t = np.linspace(0.0, 1.0, L, dtype=np.float64)[:, None]
    ang = ((2.0 * math.pi / L) * np.arange(L, dtype=np.float64))[:, None]
    bands = np.linspace(1e-4, HY_BANDS - 1, HY_BANDS, dtype=np.float64)[None, :]
    ba = bands * ang
    z = np.concatenate([t, np.cos(ba), -np.sin(ba)], axis=-1).astype(np.float32)
    return np.pad(z, ((0, 0), (0, LANES - HY_EMB)))


def _hy_filter_kernel(z_ref, w1_ref, b1_ref, w2_ref, b2_ref, w3_ref, fr_ref, rc_ref, rs_ref):
    tr = z_ref.shape[0]
    z = z_ref[...]
    h = jnp.sin(fr_ref[0:1, :] * (_dot3(z, w1_ref[...]) + b1_ref[...]))
    h = jnp.sin(fr_ref[1:2, :] * (_dot3(h, w2_ref[...]) + b2_ref[...]))
    h = _dot3(h, w3_ref[...])
    t = z[:, 0:1]
    step = (HY_MAX_DECAY - HY_MIN_DECAY) / (D_HY - 1)
    deltas = HY_MIN_DECAY + step * lax.broadcasted_iota(jnp.int32, (1, D_HY), 1).astype(F32)
    window = jnp.exp(-t * jnp.abs(deltas))
    not_first = pl.program_id(0) * tr + lax.broadcasted_iota(jnp.int32, (tr, 1), 0) > 0
    for o in range(2):
        hf = h[:, (2 * o) * D_HY:(2 * o + 1) * D_HY] * window
        hb = jnp.where(not_first, h[:, (2 * o + 1) * D_HY:(2 * o + 2) * D_HY] * window, 0.0)
        rc_ref[:, o * D_HY:(o + 1) * D_HY] = (hf + hb).astype(BF16)
        rs_ref[:, o * D_HY:(o + 1) * D_HY] = (hb - hf).astype(BF16)


def _hy_spectrum_kernel(c_ref, s_ref, rc_ref, rs_ref, w_ref, kre_ref, kim_ref):
    w = w_ref[...]
    kre_ref[...] = _dot(c_ref[...], rc_ref[...]) * w
    kim_ref[...] = _dot(s_ref[...], rs_ref[...]) * w


def _hy_spectra(L, cmat, smat, w1, b1, w2, b2, w3, freq):
    z = jnp.asarray(_hy_features(L))
    w1p = jnp.pad(w1, ((0, LANES - HY_EMB), (0, 0)))
    nw = 2 * D_HY
    tr = min(L, 256)
    rc, rs = pl.pallas_call(
        _hy_filter_kernel,
        grid=(L // tr,),
        in_specs=[
            pl.BlockSpec((tr, LANES), lambda i: (i, 0)),
            _const_spec((LANES, HY_FFN)), _const_spec((1, HY_FFN)),
            _const_spec((HY_FFN, HY_FFN)), _const_spec((1, HY_FFN)),
            _const_spec((HY_FFN, 2 * nw)), _const_spec((2, HY_FFN)),
        ],
        out_specs=[pl.BlockSpec((tr, nw), lambda i: (i, 0))] * 2,
        out_shape=[jax.ShapeDtypeStruct((L, nw), BF16)] * 2,
        compiler_params=_cparams("parallel"),
        name="hy_filter",
    )(z, w1p, b1.reshape(1, HY_FFN), w2, b2.reshape(1, HY_FFN), w3, freq)
    n = 2 * L - 1
    wsc = np.full((L, 1), 2.0 / n, np.float32)
    wsc[0, 0] = 1.0 / n
    return pl.pallas_call(
        _hy_spectrum_kernel,
        grid=(L // tr,),
        in_specs=[
            pl.BlockSpec((tr, L), lambda i: (i, 0)),
            pl.BlockSpec((tr, L), lambda i: (i, 0)),
            _const_spec((L, nw)),
            _const_spec((L, nw)),
            pl.BlockSpec((tr, 1), lambda i: (i, 0)),
        ],
        out_specs=[pl.BlockSpec((tr, nw), lambda i: (i, 0))] * 2,
        out_shape=[jax.ShapeDtypeStruct((L, nw), F32)] * 2,
        compiler_params=_cparams("parallel"),
        name="hy_spectrum",
    )(cmat, smat, rc, rs, jnp.asarray(wsc))


def _hyena_kernel(L, tr, nb, p_ref, sw_ref, sb_ref, bias_ref, c_ref, s_ref, kre_ref, kim_ref, o_ref,
                  pad_ref, u_ref, sig_ref, sig16_ref, zre_ref, zim_ref):
    C3, D = 3 * D_HY, D_HY
    zeros = jnp.zeros((8, C3), F32)
    pad_ref[0:8, :] = zeros
    pad_ref[8 + L:16 + L, :] = zeros
    chunks = [slice(r0, r0 + tr) for r0 in range(0, L, tr)]
    wide = (lambda a: jnp.concatenate([a] * nb, axis=1)) if nb > 1 else (lambda a: a)
    for bb in range(nb):
        for c in chunks:
            pad_ref[8 + c.start:8 + c.stop, :] = p_ref[bb * L + c.start:bb * L + c.stop, :].astype(F32)
        for c in chunks:
            u = sb_ref[...]
            for j in range(3):
                u = u + pad_ref[7 + j + c.start:7 + j + c.stop, :] * sw_ref[j:j + 1, :]
            u_ref[c, bb * 2 * D:(bb + 1) * 2 * D] = u[:, D:C3]
            sig_ref[c, bb * D:(bb + 1) * D] = u[:, 0:D]
            sig16_ref[c, bb * D:(bb + 1) * D] = u[:, 0:D].astype(BF16)

    for o in range(2):
        ko = slice(o * D, (o + 1) * D)
        for c in chunks:
            ure = _dot(c_ref[c, :], sig16_ref[...])
            us = _dot(s_ref[c, :], sig16_ref[...])
            kre, kim = wide(kre_ref[c, ko]), wide(kim_ref[c, ko])
            zre_ref[c, :] = (ure * kre + us * kim).astype(BF16)
            zim_ref[c, :] = (ure * kim - us * kre).astype(BF16)
        for c in chunks:
            y = _dot(c_ref[c, :], zre_ref[...]) - _dot(s_ref[c, :], zim_ref[...])
            y = y + sig_ref[c, :] * wide(bias_ref[o:o + 1, :])
            for bb in range(nb):
                cols = slice(bb * D, (bb + 1) * D)
                z = u_ref[c, bb * 2 * D + o * D:bb * 2 * D + (o + 1) * D] * y[:, cols]
                if o == 0:
                    sig_ref[c, cols] = z
                    sig16_ref[c, cols] = z.astype(BF16)
                else:
                    o_ref[bb * L + c.start:bb * L + c.stop, :] = z.astype(o_ref.dtype)


def _hyena(p_hy, B, L, cmat, smat, kre, kim, short_w, short_b, bias, tr=512, max_rows=1024):
    C3 = 3 * D_HY
    tr = min(tr, L)
    nb = math.gcd(B, max_rows // L) if L <= max_rows else 1
    once = pl.Buffered(1)
    return pl.pallas_call(
        functools.partial(_hyena_kernel, L, tr, nb),
        grid=(B // nb,),
        in_specs=[
            pl.BlockSpec((nb * L, C3), lambda b: (b, 0)),
            _const_spec((3, C3)),
            _const_spec((1, C3)),
            _const_spec((2, D_HY)),
            pl.BlockSpec((L, L), lambda b: (0, 0), pipeline_mode=once),
            pl.BlockSpec((L, L), lambda b: (0, 0), pipeline_mode=once),
            pl.BlockSpec((L, 2 * D_HY), lambda b: (0, 0), pipeline_mode=once),
            pl.BlockSpec((L, 2 * D_HY), lambda b: (0, 0), pipeline_mode=once),
        ],
        out_specs=pl.BlockSpec((nb * L, D_HY), lambda b: (b, 0)),
        out_shape=jax.ShapeDtypeStruct((B * L, D_HY), BF16),
        scratch_shapes=[
            pltpu.VMEM((L + 16, C3), F32),
            pltpu.VMEM((L, nb * 2 * D_HY), F32),
            pltpu.VMEM((L, nb * D_HY), F32),
            pltpu.VMEM((L, nb * D_HY), BF16),
            pltpu.VMEM((L, nb * D_HY), BF16),
            pltpu.VMEM((L, nb * D_HY), BF16),
        ],
        compiler_params=_cparams("parallel"),
        name="hyena",
    )(p_hy, short_w, short_b.reshape(1, C3), bias, cmat, smat, kre, kim)


def _softplus(z):
    return jnp.maximum(z, 0.0) + jnp.log1p(jnp.exp(-jnp.abs(z)))


def _gelu_tanh(x):
    return 0.5 * x * (1.0 + jnp.tanh(math.sqrt(2.0 / math.pi) * (x + 0.044715 * x * x * x)))


def _rglru_kernel(L, has_state, pg_ref, px_ref, cw_ref, cb_ref, w3_ref, gb_ref, lam_ref, *rest):
    if has_state:
        st_ref, y_ref, pad_ref, a_ref, b_ref, h_ref = rest
    else:
        y_ref, st_out_ref, pad_ref, a_ref, b_ref, h_ref = rest
    C = D_RG
    zeros = jnp.zeros((8, C), F32)
    pad_ref[0:8, :] = zeros
    pad_ref[8 + L:16 + L, :] = zeros
    pad_ref[8:8 + L, :] = px_ref[...]
    half_c = (-0.5 * RG_C) * _softplus(-lam_ref[...])
    tr = min(L, 256)
    for r0 in range(0, L, tr):
        xr = cb_ref[...]
        for j in range(4):
            xr = xr + pad_ref[6 + j + r0:6 + j + r0 + tr, :] * cw_ref[j:j + 1, :]
        xh, xl = _split(xr)
        x3 = jnp.concatenate([xh, xl, xh], axis=1)
        for d in range(2):
            t = []
            for m in range(2):
                cols = slice((2 * d + m) * C, (2 * d + m + 1) * C)
                t.append(jnp.tanh(_dot(x3, w3_ref[:, cols]) + gb_ref[:, cols]))
            log_a = half_c[d:d + 1, :] + half_c[d:d + 1, :] * t[0]
            a = jnp.exp(log_a)
            a_ref[d, r0:r0 + tr, :] = a
            gate_x = 0.5 * xr
            b_ref[d, r0:r0 + tr, :] = jnp.sqrt(jnp.tanh(log_a) * (-1.0 - a * a)) * (gate_x + gate_x * t[1])

    if has_state:
        h0f, h0b = st_ref[0:1, :], st_ref[1:2, :]
    else:
        h0f = h0b = jnp.zeros((1, C), F32)

    row = lax.broadcasted_iota(jnp.int32, (SUBLANES, 1), 0)

    def tile_scan(a, b, reverse):
        for d in (1, 2, 4):
            shift = SUBLANES - d if reverse else d
            valid = (row < SUBLANES - d) if reverse else (row >= d)
            a_s, b_s = pltpu.roll(a, shift, axis=0), pltpu.roll(b, shift, axis=0)
            b = jnp.where(valid, a * b_s + b, b)
            a = jnp.where(valid, a * a_s, a)
        return a, b

    def step(i, carry):
        hf, hb = carry
        t0 = pl.multiple_of(i * SUBLANES, SUBLANES)
        tb0 = pl.multiple_of(L - SUBLANES - i * SUBLANES, SUBLANES)
        af, bf = tile_scan(a_ref[0, pl.ds(t0, SUBLANES), :], b_ref[0, pl.ds(t0, SUBLANES), :], False)
        ab, bb = tile_scan(a_ref[1, pl.ds(tb0, SUBLANES), :], b_ref[1, pl.ds(tb0, SUBLANES), :], True)
        hf_tile = af * hf + bf
        hb_tile = ab * hb + bb
        h_ref[0, pl.ds(t0, SUBLANES), :] = hf_tile
        h_ref[1, pl.ds(tb0, SUBLANES), :] = hb_tile
        return hf_tile[SUBLANES - 1:SUBLANES], hb_tile[0:1]

    lax.fori_loop(0, L // SUBLANES, step, (h0f, h0b), unroll=4)
    y_ref[...] = ((h_ref[0] + h_ref[1]) * _gelu_tanh(pg_ref[...])).astype(y_ref.dtype)
    if not has_state:
        st_out_ref[0:1, :] = h_ref[0, L - 1:L, :]
        st_out_ref[1:2, :] = h_ref[1, 0:1, :]


def _block_diag(w):
    H, d, _ = w.shape
    eye = jnp.eye(H, dtype=w.dtype)
    return (eye[:, None, :, None] * w[:, :, None, :]).reshape(H * d, H * d)


def _rglru(p_g, p_x, B, L, conv_w, conv_b, wa, ba, wx, bx, lam, state):
    C = D_RG
    wcat = 0.5 * jnp.concatenate(
        [_block_diag(wa[0]), _block_diag(wx[0]), _block_diag(wa[1]), _block_diag(wx[1])], axis=1)
    wh = wcat.astype(BF16)
    wl = (wcat - wh.astype(F32)).astype(BF16)
    w3 = jnp.concatenate([wh, wh, wl], axis=0)
    gb = 0.5 * jnp.concatenate([ba[0], bx[0], ba[1], bx[1]]).reshape(1, 4 * C)
    has_state = state is not None
    in_specs = [
        pl.BlockSpec((L, C), lambda b: (b, 0)),
        pl.BlockSpec((L, C), lambda b: (b, 0)),
        _const_spec((4, C)),
        _const_spec((1, C)),
        _const_spec((3 * C, 4 * C)),
        _const_spec((1, 4 * C)),
        _const_spec((2, C)),
    ]
    args = [p_g, p_x, conv_w, conv_b.reshape(1, C), w3, gb, lam]
    y_spec = pl.BlockSpec((L, C), lambda b: (b, 0))
    y_shape = jax.ShapeDtypeStruct((B * L, C), BF16)
    if has_state:
        in_specs.append(pl.BlockSpec((None, 2, C), lambda b: (b, 0, 0)))
        args.append(state)
        out_specs, out_shape = y_spec, y_shape
    else:
        out_specs = [y_spec, pl.BlockSpec((None, 2, C), lambda b: (b, 0, 0))]
        out_shape = [y_shape, jax.ShapeDtypeStruct((B, 2, C), F32)]
    return pl.pallas_call(
        functools.partial(_rglru_kernel, L, has_state),
        grid=(B,),
        in_specs=in_specs,
        out_specs=out_specs,
        out_shape=out_shape,
        scratch_shapes=[
            pltpu.VMEM((L + 16, C), F32),
            pltpu.VMEM((2, L, C), F32),
            pltpu.VMEM((2, L, C), F32),
            pltpu.VMEM((2, L, C), F32),
        ],
        compiler_params=_cparams("parallel"),
        name="rglru_state" if has_state else "rglru",
    )(*args)


def _attn_kernel(L, P, tq, unroll, lam_init, q_ref, k_ref, v_ref, *rest):
    if P:
        ck_ref, cv_ref, dal_ref, sub_ref, o_ref, kk_ref, vv_ref, s_ref = rest
    else:
        dal_ref, sub_ref, o_ref, kk_ref, vv_ref, s_ref = rest
    lv = dal_ref[...]
    s01 = jnp.sum(lv[0:1, :] * lv[1:2, :], axis=-1, keepdims=True)
    s23 = jnp.sum(lv[2:3, :] * lv[3:4, :], axis=-1, keepdims=True)
    lam = jnp.exp(s01) - jnp.exp(s23) + lam_init
    first_half = lax.broadcasted_iota(jnp.int32, (1, DA_VDIM), 1) < DA_HEAD
    sub = sub_ref[...] * (1.0 - lam_init)
    for hd in range(N_DA_HEADS):
        if P:
            kk_ref[0:P, :] = ck_ref[hd].astype(BF16)
            vv_ref[0:P, :] = cv_ref[hd].astype(BF16)
        kk_ref[P:P + L, :] = k_ref[hd].astype(BF16)
        vv_ref[P:P + L, :] = v_ref[hd].astype(BF16)

        def scores(i, buf):
            q = q_ref[hd, pl.ds(pl.multiple_of(i * tq, tq), tq), :]
            zero = jnp.zeros_like(q)
            qs = jnp.concatenate([jnp.where(first_half, q, zero), jnp.where(first_half, zero, q)], axis=0)
            s_ref[buf] = _dot_nt(qs, kk_ref[...])

        def finish(i, buf):
            s = s_ref[buf]
            p = jnp.exp2(s - jnp.max(s, axis=-1, keepdims=True))
            rinv = 1.0 / jnp.sum(p, axis=-1, keepdims=True)
            acc = _dot(p.astype(BF16), vv_ref[...])
            o = acc[0:tq] * rinv[0:tq] - acc[tq:2 * tq] * (lam * rinv[tq:2 * tq])
            o = o * lax.rsqrt(jnp.mean(o * o, axis=-1, keepdims=True) + EPS) * sub
            r0 = pl.multiple_of(i * tq, tq)
            o_ref[pl.ds(r0, tq), hd * DA_VDIM:(hd + 1) * DA_VDIM] = o.astype(o_ref.dtype)

        def pair(j, carry):
            i = 2 * j
            scores(i + 1, 1)
            finish(i, 0)
            scores(i + 2, 0)
            finish(i + 1, 1)
            return carry

        n = L // tq
        scores(0, 0)
        lax.fori_loop(0, n // 2 - 1, pair, 0, unroll=unroll)
        scores(n - 1, 1)
        finish(n - 2, 0)
        finish(n - 1, 1)


def _attention(q, k, v, layer, cache, dal, subln, lam_init, B, L, tq=128, unroll=2):
    H, dv = N_DA_HEADS, DA_VDIM
    hspec = pl.BlockSpec((None, H, L, dv), lambda b: (b, 0, 0, 0))
    kvspec = hspec if k.ndim == 4 else pl.BlockSpec((None, None, H, L, dv), lambda b: (b, layer, 0, 0, 0))
    in_specs = [hspec, kvspec, kvspec]
    args = [q, k, v]
    P = 0
    if cache is not None:
        ck, cv = cache
        P = ck.shape[3]
        cspec = pl.BlockSpec((None, None, H, P, dv), lambda b: (b, layer, 0, 0, 0))
        in_specs += [cspec, cspec]
        args += [ck, cv]
    assert L % tq == 0
    in_specs += [_const_spec((4, DA_HEAD)), _const_spec((1, dv))]
    args += [dal, subln.reshape(1, dv)]
    return pl.pallas_call(
        functools.partial(_attn_kernel, L, P, tq, min(unroll, max(1, L // tq // 2 - 1)), lam_init),
        grid=(B,),
        in_specs=in_specs,
        out_specs=pl.BlockSpec((L, H * dv), lambda b: (b, 0)),
        out_shape=jax.ShapeDtypeStruct((B * L, H * dv), BF16),
        scratch_shapes=[pltpu.VMEM((P + L, dv), BF16), pltpu.VMEM((P + L, dv), BF16),
                        pltpu.VMEM((2, 2 * tq, P + L), F32)],
        compiler_params=_cparams("parallel"),
        name="diff_attn_cache" if P else "diff_attn",
    )(*args)


def _route(logits):
    m = logits[0]
    for e in range(1, N_EXPERTS):
        m = jnp.maximum(m, logits[e])
    ex = [jnp.exp(l - m) for l in logits]
    tot = ex[0]
    for e in range(1, N_EXPERTS):
        tot = tot + ex[e]
    inv = 1.0 / tot
    p = [e_ * inv for e_ in ex]
    G = EXP_PER_GROUP
    best, gsel = None, None
    for g in range(N_GROUPS):
        a = p[g * G:(g + 1) * G]
        sc = None
        for i in range(G):
            for j in range(i + 1, G):
                pair = a[i] + a[j]
                sc = pair if sc is None else jnp.maximum(sc, pair)
        if g == 0:
            best, gsel = sc, jnp.zeros_like(sc, dtype=jnp.int32)
        else:
            upd = sc > best
            best = jnp.where(upd, sc, best)
            gsel = jnp.where(upd, g, gsel)
    vals = []
    for j in range(G):
        vj = p[j]
        for g in range(1, N_GROUPS):
            vj = jnp.where(gsel == g, p[g * G + j], vj)
        vals.append(vj)
    p1, i1 = vals[0], jnp.zeros_like(gsel)
    for j in range(1, G):
        upd = vals[j] > p1
        p1 = jnp.where(upd, vals[j], p1)
        i1 = jnp.where(upd, j, i1)
    p2, i2 = None, None
    for j in range(G):
        cand = jnp.where(i1 == j, -1.0, vals[j])
        if p2 is None:
            p2, i2 = cand, jnp.zeros_like(gsel)
        else:
            upd = cand > p2
            p2 = jnp.where(upd, cand, p2)
            i2 = jnp.where(upd, j, i2)
    den = 1.0 / (p1 + p2)
    w1, w2 = p1 * den, p2 * den
    swap = i2 < i1
    a, b = jnp.where(swap, i2, i1), jnp.where(swap, i1, i2)
    w_lo, w_hi = jnp.where(swap, w2, w1), jnp.where(swap, w1, w2)
    pair = jnp.where(a == 0, b - 1, jnp.where(a == 1, b + 1, 5))
    cls = gsel * PAIRS_PER_GROUP + pair
    return cls.astype(F32), w_lo, w_hi


def _pack_pairs(x):
    n = x.shape[1] // 2
    b = pltpu.bitcast(x, jnp.uint32)
    w = (b[:, :n] >> 16) | (b[:, n:] & jnp.uint32(0xFFFF0000))
    return pltpu.bitcast(w, jnp.int32)


def _unpack_pairs(w):
    u = pltpu.bitcast(w, jnp.uint32)
    lo = pltpu.bitcast(u << 16, F32)
    hi = pltpu.bitcast(u & jnp.uint32(0xFFFF0000), F32)
    return jnp.concatenate([lo, hi], axis=1)


def _out_proj_kernel(yh_ref, yr_ref, o_ref, w_ref, x_ref, mod_ref, g_ref, wrh_ref, wrl_ref, br_ref,
                     xo_ref, h_ref, route_ref, y_ref):
    n_sub = y_ref.shape[0]
    sub = x_ref.shape[0] // n_sub

    def project(j):
        r = slice(j * sub, (j + 1) * sub)
        y_ref[j] = (_dot(yh_ref[r, :], w_ref[0:D_HY, :]) + _dot(yr_ref[r, :], w_ref[D_HY:D_HY + D_RG, :])
                    + _dot(o_ref[r, :], w_ref[D_HY + D_RG:D_MIX, :]))

    def finish(j):
        r = slice(j * sub, (j + 1) * sub)
        x = x_ref[r, :] + mod_ref[2:3, :] * y_ref[j]
        xo_ref[r, :] = x
        ms = jnp.mean(x * x, axis=-1, keepdims=True)
        h = (x * lax.rsqrt(ms + EPS) * g_ref[...]) * (1.0 + mod_ref[4:5, :]) + mod_ref[3:4, :]
        hh, hl = _split(h)
        h_ref[r, :] = _pack_pairs(hh.astype(F32))
        lg = _dot_nt(wrh_ref[...], hh) + _dot_nt(wrh_ref[...], hl) + _dot_nt(wrl_ref[...], hh) + br_ref[...]
        info = _route([lg[e:e + 1, :] for e in range(N_EXPERTS)])
        rt = jnp.concatenate(list(info) + [jnp.zeros((LANES - len(info), sub), F32)], axis=0)
        route_ref[r, :] = rt.T

    project(0)
    for j in range(n_sub):
        if j + 1 < n_sub:
            project(j + 1)
        finish(j)


def _out_proj(y_hy, y_rg, o, w_out, layer, x, mod, g2, w_router, b_router, B, L, ctx_rows, tm=1024, n_sub=2):
    T = B * L
    tm = min(tm, T if ctx_rows else L)
    assert T % tm == 0 and tm % n_sub == 0
    nl = max(L // tm, 1)
    row = (lambda i: CTX_ROW) if ctx_rows else (lambda i: i // nl)
    wrt = w_router.T
    wrh = wrt.astype(BF16)
    wrl = (wrt - wrh.astype(F32)).astype(BF16)
    rows = lambda w: pl.BlockSpec((tm, w), lambda i: (i, 0))
    return pl.pallas_call(
        _out_proj_kernel,
        grid=(T // tm,),
        in_specs=[
            rows(D_HY), rows(D_RG), rows(D_DA),
            pl.BlockSpec((None, D_MIX, D_MODEL), lambda i: (layer, 0, 0)),
            rows(D_MODEL),
            pl.BlockSpec((None, 6, D_MODEL), lambda i: (row(i), 0, 0)),
            _const_spec((1, D_MODEL)),
            _const_spec((N_EXPERTS, D_MODEL)),
            _const_spec((N_EXPERTS, D_MODEL)),
            _const_spec((N_EXPERTS, 1)),
        ],
        out_specs=[rows(D_MODEL), rows(D_MODEL // 2), rows(LANES)],
        out_shape=[
            jax.ShapeDtypeStruct((T, D_MODEL), F32),
            jax.ShapeDtypeStruct((T, D_MODEL // 2), jnp.int32),
            jax.ShapeDtypeStruct((T, LANES), F32),
        ],
        scratch_shapes=[pltpu.VMEM((n_sub, tm // n_sub, D_MODEL), F32)],
        compiler_params=_cparams("parallel"),
        name="out_proj_route",
    )(y_hy, y_rg, o, w_out, x, mod, g2.reshape(1, D_MODEL), wrh, wrl, b_router.reshape(N_EXPERTS, 1))


def _gather_rows(table, idx, rows_per_step=64, n_buf=2):
    info = plsc.get_sparse_core_info()
    n_workers = info.num_cores * info.num_subcores
    n, width = idx.shape[0], table.shape[1]
    per_worker = n // n_workers
    n_steps = per_worker // rows_per_step
    assert per_worker * n_workers == n and n_steps * rows_per_step == per_worker and n_steps >= n_buf
    mesh = plsc.VectorSubcoreMesh(core_axis_name="c", subcore_axis_name="s")

    @functools.partial(
        pl.kernel, mesh=mesh,
        out_type=jax.ShapeDtypeStruct((n, width), table.dtype),
        scratch_types=[
            pltpu.VMEM((per_worker,), jnp.int32),
            pltpu.VMEM((n_buf, rows_per_step, width), table.dtype),
            pltpu.SemaphoreType.DMA((n_buf,)),
            pltpu.SemaphoreType.DMA((n_buf,)),
        ],
    )
    def gather(table_hbm, idx_hbm, out_hbm, idx_v, rows_v, sem_in, sem_out):
        worker = lax.axis_index("s") * info.num_cores + lax.axis_index("c")
        base = pl.multiple_of(worker * per_worker, per_worker)
        pltpu.sync_copy(idx_hbm.at[pl.ds(base, per_worker)], idx_v)

        def read(b, step):
            rows = idx_v.at[pl.ds(step * rows_per_step, rows_per_step)]
            return pltpu.make_async_copy(table_hbm.at[rows], rows_v.at[b], sem_in.at[b])

        def write(b, step):
            off = pl.multiple_of(base + step * rows_per_step, rows_per_step)
            return pltpu.make_async_copy(rows_v.at[b], out_hbm.at[pl.ds(off, rows_per_step)], sem_out.at[b])

        for step in range(n_steps + 1):
            if step < n_steps:
                if step >= n_buf:
                    write(step % n_buf, step - n_buf).wait()
                read(step % n_buf, step).start()
            if step >= 1:
                read((step - 1) % n_buf, step - 1).wait()
                write((step - 1) % n_buf, step - 1).start()
        for step in range(n_steps - n_buf, n_steps):
            write(step % n_buf, step).wait()

    return gather(table, idx)


def _scatter_rows(src, pos, n_slots, rows_per_step=64, n_buf=2):
    info = plsc.get_sparse_core_info()
    n_workers = info.num_cores * info.num_subcores
    n_tab = len(src)
    n = src[0].shape[0]
    per_worker = n // n_workers
    n_steps = per_worker // rows_per_step
    assert per_worker * n_workers == n and n_steps * rows_per_step == per_worker and n_steps >= n_buf
    mesh = plsc.VectorSubcoreMesh(core_axis_name="c", subcore_axis_name="s")

    @functools.partial(
        pl.kernel, mesh=mesh,
        out_type=tuple(jax.ShapeDtypeStruct((n_slots, t.shape[1]), t.dtype) for t in src),
        scratch_types=[pltpu.VMEM((n_steps, rows_per_step), jnp.int32)]
        + [pltpu.VMEM((n_buf, rows_per_step, t.shape[1]), t.dtype) for t in src]
        + [pltpu.SemaphoreType.DMA((n_tab, n_buf)), pltpu.SemaphoreType.DMA((n_tab, n_buf))],
    )
    def scatter(*refs):
        src_hbm, idx_hbm = refs[:n_tab], refs[n_tab]
        out_hbm = refs[n_tab + 1:2 * n_tab + 1]
        idx_v = refs[2 * n_tab + 1]
        rows_v = refs[2 * n_tab + 2:3 * n_tab + 2]
        sem_in, sem_out = refs[3 * n_tab + 2:]
        worker = lax.axis_index("s") * info.num_cores + lax.axis_index("c")
        base = pl.multiple_of(worker * per_worker, per_worker)
        pltpu.sync_copy(idx_hbm.at[worker], idx_v)

        def read(t, b, step):
            off = pl.multiple_of(base + step * rows_per_step, rows_per_step)
            return pltpu.make_async_copy(src_hbm[t].at[pl.ds(off, rows_per_step)], rows_v[t].at[b], sem_in.at[t, b])

        def write(t, b, step):
            return pltpu.make_async_copy(rows_v[t].at[b], out_hbm[t].at[idx_v.at[step]], sem_out.at[t, b])

        for step in range(n_steps + 1):
            for t in range(n_tab):
                if step < n_steps:
                    if step >= n_buf:
                        write(t, step % n_buf, step - n_buf).wait()
                    read(t, step % n_buf, step).start()
                if step >= 1:
                    read(t, (step - 1) % n_buf, step - 1).wait()
                    write(t, (step - 1) % n_buf, step - 1).start()
        for step in range(n_steps - n_buf, n_steps):
            for t in range(n_tab):
                write(t, step % n_buf, step).wait()

    return scatter(*src, pos.reshape(n_workers, n_steps, rows_per_step))


def _dispatch_plan(route, tm):
    T = route.shape[0]
    n_slots = T + N_CLASSES * tm
    cls = route[:, 0].astype(jnp.int32)
    onehot = (cls[:, None] == jnp.arange(N_CLASSES, dtype=jnp.int32)[None, :]).astype(jnp.int32)
    csum = jnp.cumsum(onehot, axis=0)
    rank = jnp.sum(onehot * csum, axis=1) - 1
    counts = csum[-1]
    padded = ((counts + tm - 1) // tm) * tm
    ends = jnp.cumsum(padded)
    starts = ends - padded
    pos = jnp.sum(onehot * starts[None, :], axis=1) + rank
    tile_start = jnp.arange(n_slots // tm, dtype=jnp.int32) * tm
    tile_cls = jnp.minimum(jnp.sum((tile_start[:, None] >= ends[None, :]).astype(jnp.int32), axis=1), N_CLASSES - 1)
    n_rows = jnp.clip(counts[tile_cls] - (tile_start - starts[tile_cls]), 0, tm).astype(jnp.int32)
    pairs = np.array([(a, b) for a in range(EXP_PER_GROUP) for b in range(a + 1, EXP_PER_GROUP)], np.int32)
    group, pair = tile_cls // PAIRS_PER_GROUP, tile_cls % PAIRS_PER_GROUP
    lo = group * EXP_PER_GROUP + jnp.asarray(pairs[:, 0])[pair]
    hi = group * EXP_PER_GROUP + jnp.asarray(pairs[:, 1])[pair]
    return pos, n_slots, lo, hi, n_rows


def _moe_sorted_kernel(lo_ref, hi_ref, rows_ref, xs_ref, ws_ref, wg_lo, wu_lo, wd_lo, wg_hi, wu_hi, wd_hi, o_ref):
    i = pl.program_id(0)

    @pl.when(rows_ref[i] > 0)
    def _():
        real = lax.broadcasted_iota(jnp.int32, (xs_ref.shape[0], 1), 0) < rows_ref[i]
        x = jnp.where(real, _unpack_pairs(xs_ref[...]), 0.0).astype(BF16)
        y = None
        for wg, wu, wd, col in ((wg_lo, wu_lo, wd_lo, 1), (wg_hi, wu_hi, wd_hi, 2)):
            a = _dot(x, wg[...])
            he = (a * _sigmoid(a)) * _dot(x, wu[...]) * jnp.where(real, ws_ref[:, col:col + 1], 0.0)
            part = _dot(he.astype(BF16), wd[...])
            y = part if y is None else y + part
        o_ref[...] = _pack_pairs(y.astype(BF16).astype(F32))

    @pl.when(rows_ref[i] == 0)
    def _():
        o_ref[...] = jnp.zeros_like(o_ref)


def _moe_sorted(xs, ws, lo, hi, n_rows, wg, wu, wd, tm):
    n_slots = xs.shape[0]
    half = D_MODEL // 2
    up = lambda sel: pl.BlockSpec((None, D_MODEL, D_EXPERT), lambda i, lo, hi, v: ((lo, hi)[sel][i], 0, 0))
    down = lambda sel: pl.BlockSpec((None, D_EXPERT, D_MODEL), lambda i, lo, hi, v: ((lo, hi)[sel][i], 0, 0))
    return pl.pallas_call(
        _moe_sorted_kernel,
        grid_spec=pltpu.PrefetchScalarGridSpec(
            num_scalar_prefetch=3,
            grid=(n_slots // tm,),
            in_specs=[
                pl.BlockSpec((tm, half), lambda i, lo, hi, v: (i, 0)),
                pl.BlockSpec((tm, LANES), lambda i, lo, hi, v: (i, 0)),
                up(0), up(0), down(0), up(1), up(1), down(1),
            ],
            out_specs=pl.BlockSpec((tm, half), lambda i, lo, hi, v: (i, 0)),
        ),
        out_shape=jax.ShapeDtypeStruct((n_slots, half), jnp.int32),
        compiler_params=_cparams("arbitrary"),
        name="moe_sorted",
    )(lo, hi, n_rows, xs, ws, wg, wu, wd, wg, wu, wd)


def _final_residual_kernel(y_ref, x_ref, mod_ref, fg_ref, o_ref):
    x = x_ref[...] + mod_ref[5:6, :] * _unpack_pairs(y_ref[...])
    o_ref[...] = x * lax.rsqrt(jnp.mean(x * x, axis=-1, keepdims=True) + EPS) * fg_ref[...]


def _final_residual(y, x, mod, final_g, B, L, ctx_rows, tm=2048):
    T = B * L
    tm = min(tm, T if ctx_rows else L)
    assert T % tm == 0
    nl = max(L // tm, 1)
    row = (lambda i: CTX_ROW) if ctx_rows else (lambda i: i // nl)
    return pl.pallas_call(
        _final_residual_kernel,
        grid=(T // tm,),
        in_specs=[
            pl.BlockSpec((tm, D_MODEL // 2), lambda i: (i, 0)),
            pl.BlockSpec((tm, D_MODEL), lambda i: (i, 0)),
            pl.BlockSpec((None, 6, D_MODEL), lambda i: (row(i), 0, 0)),
            _const_spec((1, D_MODEL)),
        ],
        out_specs=pl.BlockSpec((tm, D_MODEL), lambda i: (i, 0)),
        out_shape=jax.ShapeDtypeStruct((T, D_MODEL), F32),
        compiler_params=_cparams("parallel"),
        name="final_residual",
    )(y, x, mod, final_g.reshape(1, D_MODEL))


def _moe(h, route, wg, wu, wd, layer, tm=256):
    pos, n_slots, lo, hi, n_rows = _dispatch_plan(route, tm)
    xs, ws = _scatter_rows((h, route), pos, n_slots)
    ys = _moe_sorted(xs, ws, lo + layer * N_EXPERTS, hi + layer * N_EXPERTS, n_rows, wg, wu, wd, tm)
    return _gather_rows(ys, pos)


def kernel(x_prompt, x_sample, cache_k, cache_v, state_rglru, c, c_ctx, w_ada, b_ada, norm1_g, norm2_g, w_in, w_out, hy_short_w, hy_short_b, hy_w1, hy_b1, hy_w2, hy_b2, hy_w3, hy_freq, hy_bias, rg_conv_w, rg_conv_b, rg_wa, rg_ba, rg_wx, rg_bx, rg_lambda, da_lambda, da_subln, w_router, b_router, moe_wg, moe_wu, moe_wd, final_g):
    Bp, Lp, D = x_prompt.shape
    Bs, Ls, _ = x_sample.shape
    assert Bs <= CTX_ROW
    cond = jnp.zeros((N_COND, D), F32).at[:Bs].set(c).at[CTX_ROW].set(c_ctx)
    mods = _ada_table(cond, w_ada, b_ada)

    dft = {L: tuple(jnp.asarray(m).astype(BF16) for m in _dft_mats(L)) for L in (Lp, Ls)}
    streams = [
        dict(B=Bp, L=Lp, ctx=True, x=x_prompt.reshape(Bp * Lp, D)),
        dict(B=Bs, L=Ls, ctx=False, x=x_sample.reshape(Bs * Ls, D)),
    ]
    w_in_b, w_out_b = w_in.astype(BF16), w_out.astype(BF16)
    wg, wu, wd = (w.astype(BF16).reshape((DEPTH * N_EXPERTS,) + w.shape[2:]) for w in (moe_wg, moe_wu, moe_wd))
    new_kv, ss = None, []
    for l in range(DEPTH):
        lam_init = 0.8 - 0.6 * math.exp(-0.3 * l)
        for st in streams:
            B, L, ctx = st["B"], st["L"], st["ctx"]
            cmat, smat = dft[L]
            outs = _norm_proj(st["x"], mods[l], norm1_g[l], w_in_b, l, B, L, ctx, kv_prev=new_kv if ctx else None,
                              pending=(st["y"], mods[l - 1]) if l else None)
            p_hy, p_g, p_x, q, k, v = outs[:6]
            if l:
                st["x"] = outs[6]
            kre, kim = _hy_spectra(L, cmat, smat, hy_w1[l], hy_b1[l], hy_w2[l], hy_b2[l], hy_w3[l], hy_freq[l])
            y_hy = _hyena(p_hy, B, L, cmat, smat, kre, kim, hy_short_w[l], hy_short_b[l], hy_bias[l])
            rg_args = (rg_conv_w[l], rg_conv_b[l], rg_wa[l], rg_ba[l], rg_wx[l], rg_bx[l], rg_lambda[l])
            if ctx:
                y_rg, s_l = _rglru(p_g, p_x, B, L, *rg_args, None)
                o = _attention(q, k, v, l, None, da_lambda[l], da_subln[l], lam_init, B, L)
                new_kv = (k, v)
                ss.append(s_l)
            else:
                y_rg = _rglru(p_g, p_x, B, L, *rg_args, state_rglru[:, l])
                o = _attention(q, k, v, l, (cache_k, cache_v), da_lambda[l], da_subln[l], lam_init, B, L)
            st["x"], h2, route = _out_proj(y_hy, y_rg, o, w_out_b, l, st["x"], mods[l], norm2_g[l],
                                           w_router, b_router, B, L, ctx)
            st["y"] = _moe(h2, route, wg, wu, wd, l)
    y_prompt, y_sample = (
        _final_residual(st["y"], st["x"], mods[DEPTH - 1], final_g, st["B"], st["L"], st["ctx"]).reshape(shape)
        for st, shape in zip(streams, (x_prompt.shape, x_sample.shape)))
    return (y_prompt, y_sample, new_kv[0], new_kv[1], jnp.stack(ss, axis=1))
```
